```python
import jax, jax.numpy as jnp
from jax import lax
import numpy as np

D_MODEL = 1024
BATCH = 4
SEQ = 8192
DEPTH = 2

EPS = 1e-6
HEAD_DIM = 64
ATTN_Q_HEADS = 8
ATTN_KV_HEADS = 2
ATTN_GROUP = ATTN_Q_HEADS // ATTN_KV_HEADS
ATTN_WIDTH = ATTN_Q_HEADS * HEAD_DIM
KV_WIDTH = ATTN_KV_HEADS * HEAD_DIM
WINDOW = 128
ATTN_BLOCK = WINDOW
ROPE_THETA = 10000.0
SSM_HEADS = 8
SSM_HEAD_DIM = 64
SSM_WIDTH = SSM_HEADS * SSM_HEAD_DIM
SSM_GROUPS = 2
SSM_HEADS_PER_GROUP = SSM_HEADS // SSM_GROUPS
SSM_STATE = 128
CONV_WIDTH = 4
CONV_CH = SSM_WIDTH + 2 * SSM_GROUPS * SSM_STATE
CHUNK = 128
MIX_WIDTH = ATTN_WIDTH + SSM_WIDTH
IN_WIDTH = ATTN_WIDTH + 2 * KV_WIDTH + SSM_WIDTH + CONV_CH + SSM_HEADS
IN_SPLITS = [ATTN_WIDTH,
             ATTN_WIDTH + KV_WIDTH,
             ATTN_WIDTH + 2 * KV_WIDTH,
             ATTN_WIDTH + 2 * KV_WIDTH + SSM_WIDTH,
             ATTN_WIDTH + 2 * KV_WIDTH + SSM_WIDTH + CONV_CH]
FFN_DIM = 2816
N_EXPERTS = 8
TOP_K = 2
EXPERT_DIM = 3584
MOE_BLOCK = 256
N_DENSE = (DEPTH + 1) // 2
N_MOE = DEPTH // 2

kernel_name = "hymba_swa_sink_ssd_moe_adaln"


def rms_norm(x, g):
    xf = x.astype(jnp.float32)
    y = xf * lax.rsqrt(jnp.mean(xf * xf, axis=-1, keepdims=True) + EPS)
    return (y * g.astype(jnp.float32)).astype(x.dtype)


def modulate(h, shift, scale):
    return h * (1 + scale[:, None, :]) + shift[:, None, :]


def apply_rope(t, cos, sin):
    tf = t.astype(jnp.float32)
    t1, t2 = jnp.split(tf, 2, axis=-1)
    return jnp.concatenate([t1 * cos - t2 * sin, t2 * cos + t1 * sin], axis=-1).astype(t.dtype)


def sliding_window_attention(q, k, v, sinks):
    b, s = q.shape[:2]
    nb = s // ATTN_BLOCK
    qb = q.reshape(b, nb, ATTN_BLOCK, ATTN_KV_HEADS, ATTN_GROUP, HEAD_DIM)
    kb = k.reshape(b, nb, ATTN_BLOCK, ATTN_KV_HEADS, HEAD_DIM)
    vb = v.reshape(b, nb, ATTN_BLOCK, ATTN_KV_HEADS, HEAD_DIM)

    def with_prev(t):
        prev = jnp.pad(t[:, :-1], ((0, 0), (1, 0), (0, 0), (0, 0), (0, 0)))
        return jnp.concatenate([prev, t], axis=2)

    kk, vv = with_prev(kb), with_prev(vb)
    scores = jnp.einsum('bnqhgd,bnkhd->bhgnqk', qb, kk).astype(jnp.float32) * (HEAD_DIM ** -0.5)
    qi = jnp.arange(ATTN_BLOCK)[:, None]
    kj = jnp.arange(2 * ATTN_BLOCK)[None, :]
    delta = ATTN_BLOCK + qi - kj
    band = (delta >= 0) & (delta < WINDOW)
    key_valid = (jnp.arange(nb)[:, None] * ATTN_BLOCK - ATTN_BLOCK + kj) >= 0
    mask = band[None] & key_valid[:, None, :]
    scores = jnp.where(mask, scores, -jnp.inf)
    sink = sinks.astype(jnp.float32).reshape(ATTN_KV_HEADS, ATTN_GROUP)[None, :, :, None, None, None]
    m = jnp.maximum(jnp.max(scores, axis=-1, keepdims=True), sink)
    p = jnp.exp(scores - m)
    denom = jnp.sum(p, axis=-1, keepdims=True) + jnp.exp(sink - m)
    out = jnp.einsum('bhgnqk,bnkhd->bnqhgd', p / denom, vv.astype(jnp.float32))
    return out.reshape(b, s, ATTN_WIDTH).astype(q.dtype)


def causal_depthwise_conv(u, w, bias):
    out = lax.conv_general_dilated(
        u, w[:, None, :].astype(u.dtype), window_strides=(1,),
        padding=[(CONV_WIDTH - 1, 0)], dimension_numbers=('NWC', 'WIO', 'NWC'),
        feature_group_count=CONV_CH)
    return out + bias.astype(u.dtype)


def ssd_chunked(x, dt, a, b_mat, c_mat):
    bsz, s = x.shape[:2]
    nc = s // CHUNK
    g, r, p_, n = SSM_GROUPS, SSM_HEADS_PER_GROUP, SSM_HEAD_DIM, SSM_STATE
    xd = (x.astype(jnp.float32) * dt[..., None]).reshape(bsz, nc, CHUNK, g, r, p_)
    a_cs = jnp.cumsum((dt * a).reshape(bsz, nc, CHUNK, g, r), axis=2)
    bc = b_mat.astype(jnp.float32).reshape(bsz, nc, CHUNK, g, n)
    cc = c_mat.astype(jnp.float32).reshape(bsz, nc, CHUNK, g, n)
    seg = a_cs[:, :, :, None] - a_cs[:, :, None, :]
    causal = jnp.tril(jnp.ones((CHUNK, CHUNK), dtype=bool))[:, :, None, None]
    decay = jnp.exp(jnp.where(causal, seg, -jnp.inf))
    cb = jnp.einsum('bclgn,bcsgn->bclsg', cc, bc)
    y_diag = jnp.einsum('bclsgr,bcsgrp->bclgrp', cb[..., None] * decay, xd)
    decay_to_end = jnp.exp(a_cs[:, :, -1:] - a_cs)
    chunk_states = jnp.einsum('bclgn,bclgrp->bcgrpn', bc, xd * decay_to_end[..., None])
    chunk_decay = jnp.exp(a_cs[:, :, -1])

    def step(h, inp):
        st, dec = inp
        return dec[..., None, None] * h + st, h

    h0 = jnp.zeros((bsz, g, r, p_, n), jnp.float32)
    _, h_prev = lax.scan(step, h0, (jnp.moveaxis(chunk_states, 1, 0), jnp.moveaxis(chunk_decay, 1, 0)))
    h_prev = jnp.moveaxis(h_prev, 0, 1)
    y_off = jnp.einsum('bclgn,bcgrpn->bclgrp', cc, h_prev) * jnp.exp(a_cs)[..., None]
    return (y_diag + y_off).reshape(bsz, s, SSM_HEADS, SSM_HEAD_DIM)


def swiglu(h, w_gate, w_up, w_down):
    return (jax.nn.silu(h @ w_gate) * (h @ w_up)) @ w_down


def moe_swiglu(h, router_w, w_gate, w_up, w_down):
    b, s, d = h.shape
    t = b * s
    hf = h.reshape(t, d)
    logits = (hf @ router_w).astype(jnp.float32)
    top_val, top_idx = lax.top_k(logits, TOP_K)
    gates = jax.nn.softmax(top_val, axis=-1)
    tk = t * TOP_K
    e_flat = top_idx.reshape(tk).astype(jnp.int32)
    g_flat = gates.reshape(tk)
    e_s, order = lax.sort((e_flat, jnp.arange(tk, dtype=jnp.int32)), num_keys=1, is_stable=True)
    tok_s = order // TOP_K
    g_s = g_flat[order]
    counts = jnp.bincount(e_flat, length=N_EXPERTS).astype(jnp.int32)
    padded = (counts + MOE_BLOCK - 1) // MOE_BLOCK * MOE_BLOCK
    pad_end = jnp.cumsum(padded)
    pad_start = pad_end - padded
    start = jnp.cumsum(counts) - counts
    dest = pad_start[e_s] + jnp.arange(tk, dtype=jnp.int32) - start[e_s]
    n_rows = (tk + MOE_BLOCK - 1) // MOE_BLOCK * MOE_BLOCK + N_EXPERTS * MOE_BLOCK
    n_blocks = n_rows // MOE_BLOCK
    row_tok = jnp.full((n_rows,), t, jnp.int32).at[dest].set(tok_s)
    row_gate = jnp.zeros((n_rows,), jnp.float32).at[dest].set(g_s)
    block_expert = jnp.minimum(
        jnp.searchsorted(pad_end, jnp.arange(n_blocks, dtype=jnp.int32) * MOE_BLOCK, side='right'),
        N_EXPERTS - 1)
    h_pad = jnp.concatenate([hf, jnp.zeros((1, d), hf.dtype)], axis=0)
    xb = h_pad[row_tok].reshape(n_blocks, MOE_BLOCK, d)

    def expert_block(args):
        xblk, e = args
        return swiglu(xblk, w_gate[e], w_up[e], w_down[e])

    yb = lax.map(expert_block, (xb, block_expert)).reshape(n_rows, d)
    out = jnp.zeros((t + 1, d), h.dtype).at[row_tok].add((yb * row_gate[:, None]).astype(h.dtype))
    return out[:t].reshape(b, s, d)


def setup_inputs(seed: int = 0) -> dict:
    key = jax.random.key(seed)
    ks = jax.random.split(key, 32)
    f32 = jnp.float32
    nrm = lambda k, shape, scale: jax.random.normal(k, shape, f32) * scale
    x = jax.random.normal(ks[0], (BATCH, SEQ, D_MODEL), f32)
    c = jax.random.normal(ks[1], (BATCH, D_MODEL), f32)
    offsets = jax.random.randint(ks[2], (BATCH, 1), 0, 4096, dtype=jnp.int32)
    positions = offsets + jnp.arange(SEQ, dtype=jnp.int32)[None, :]
    dt0 = jnp.exp(jax.random.uniform(ks[3], (DEPTH, SSM_HEADS), f32) * (np.log(0.1) - np.log(0.001)) + np.log(0.001))
    return {
        "x": x,
        "c": c,
        "positions": positions,
        "ada_w": nrm(ks[4], (DEPTH, D_MODEL, 6 * D_MODEL), 0.5 * D_MODEL ** -0.5),
        "ada_b": nrm(ks[5], (DEPTH, 6 * D_MODEL), 0.02),
        "norm_mix_g": 1.0 + nrm(ks[6], (DEPTH, D_MODEL), 0.02),
        "norm_ffn_g": 1.0 + nrm(ks[7], (DEPTH, D_MODEL), 0.02),
        "w_in": nrm(ks[8], (DEPTH, D_MODEL, IN_WIDTH), D_MODEL ** -0.5),
        "w_out": nrm(ks[9], (DEPTH, MIX_WIDTH, D_MODEL), MIX_WIDTH ** -0.5),
        "attn_sinks": nrm(ks[10], (DEPTH, ATTN_Q_HEADS), 0.5),
        "conv_w": nrm(ks[11], (DEPTH, CONV_WIDTH, CONV_CH), CONV_WIDTH ** -0.5),
        "conv_b": nrm(ks[12], (DEPTH, CONV_CH), 0.02),
        "dt_bias": dt0 + jnp.log(-jnp.expm1(-dt0)),
        "a_log": jnp.log(jax.random.uniform(ks[13], (DEPTH, SSM_HEADS), f32, 1.0, 16.0)),
        "d_skip": 1.0 + nrm(ks[14], (DEPTH, SSM_HEADS), 0.1),
        "ssm_norm_g": 1.0 + nrm(ks[15], (DEPTH, SSM_WIDTH), 0.02),
        "ffn_w_gate": nrm(ks[16], (N_DENSE, D_MODEL, FFN_DIM), D_MODEL ** -0.5),
        "ffn_w_up": nrm(ks[17], (N_DENSE, D_MODEL, FFN_DIM), D_MODEL ** -0.5),
        "ffn_w_down": nrm(ks[18], (N_DENSE, FFN_DIM, D_MODEL), FFN_DIM ** -0.5),
        "router_w": nrm(ks[19], (N_MOE, D_MODEL, N_EXPERTS), D_MODEL ** -0.5),
        "moe_w_gate": nrm(ks[20], (N_MOE, N_EXPERTS, D_MODEL, EXPERT_DIM), D_MODEL ** -0.5),
        "moe_w_up": nrm(ks[21], (N_MOE, N_EXPERTS, D_MODEL, EXPERT_DIM), D_MODEL ** -0.5),
        "moe_w_down": nrm(ks[22], (N_MOE, N_EXPERTS, EXPERT_DIM, D_MODEL), EXPERT_DIM ** -0.5),
        "final_norm_g": 1.0 + nrm(ks[23], (D_MODEL,), 0.02),
    }


def reference(x, c, positions, ada_w, ada_b, norm_mix_g, norm_ffn_g, w_in, w_out,
              attn_sinks, conv_w, conv_b, dt_bias, a_log, d_skip, ssm_norm_g,
              ffn_w_gate, ffn_w_up, ffn_w_down, router_w, moe_w_gate, moe_w_up,
              moe_w_down, final_norm_g):
    f32 = jnp.float32
    b, s, _ = x.shape
    inv_freq = ROPE_THETA ** (-jnp.arange(0, HEAD_DIM, 2, dtype=f32) / HEAD_DIM)
    ang = positions.astype(f32)[..., None] * inv_freq
    cos = jnp.cos(ang)[:, :, None, :]
    sin = jnp.sin(ang)[:, :, None, :]
    c_act = jax.nn.silu(c)
    for l in range(DEPTH):
        mod = c_act @ ada_w[l] + ada_b[l]
        sh_m, sc_m, g_m, sh_f, sc_f, g_f = jnp.split(mod, 6, axis=-1)
        h = modulate(rms_norm(x, norm_mix_g[l]), sh_m, sc_m)
        proj = h @ w_in[l]
        q, k, v, z, xbc, dt_raw = jnp.split(proj, IN_SPLITS, axis=-1)
        q = apply_rope(q.reshape(b, s, ATTN_Q_HEADS, HEAD_DIM), cos, sin)
        k = apply_rope(k.reshape(b, s, ATTN_KV_HEADS, HEAD_DIM), cos, sin)
        v = v.reshape(b, s, ATTN_KV_HEADS, HEAD_DIM)
        attn_out = sliding_window_attention(q, k, v, attn_sinks[l])
        xbc = jax.nn.silu(causal_depthwise_conv(xbc, conv_w[l], conv_b[l]))
        xs, bm, cm = jnp.split(xbc, [SSM_WIDTH, SSM_WIDTH + SSM_GROUPS * SSM_STATE], axis=-1)
        xs = xs.reshape(b, s, SSM_HEADS, SSM_HEAD_DIM)
        dt = jax.nn.softplus(dt_raw.astype(f32) + dt_bias[l].astype(f32))
        a = -jnp.exp(a_log[l].astype(f32))
        y = ssd_chunked(xs, dt, a, bm.reshape(b, s, SSM_GROUPS, SSM_STATE),
                        cm.reshape(b, s, SSM_GROUPS, SSM_STATE))
        y = y + d_skip[l].astype(f32)[:, None] * xs.astype(f32)
        y = y.reshape(b, s, SSM_WIDTH) * jax.nn.silu(z.astype(f32))
        yg = y.reshape(b, s, SSM_GROUPS, SSM_WIDTH // SSM_GROUPS)
        yg = yg * lax.rsqrt(jnp.mean(yg * yg, axis=-1, keepdims=True) + EPS)
        ssm_out = (yg.reshape(b, s, SSM_WIDTH) * ssm_norm_g[l].astype(f32)).astype(x.dtype)
        mixed = jnp.concatenate([attn_out, ssm_out], axis=-1) @ w_out[l]
        x = x + g_m[:, None, :] * mixed
        h = modulate(rms_norm(x, norm_ffn_g[l]), sh_f, sc_f)
        if l % 2 == 0:
            f = swiglu(h, ffn_w_gate[l // 2], ffn_w_up[l // 2], ffn_w_down[l // 2])
        else:
            f = moe_swiglu(h, router_w[l // 2], moe_w_gate[l // 2], moe_w_up[l // 2], moe_w_down[l // 2])
        x = x + g_f[:, None, :] * f
    return rms_norm(x, final_norm_g)
```

```python
import functools

import numpy as np
import jax
import jax.numpy as jnp
from jax import lax
from jax.experimental import pallas as pl
from jax.experimental.pallas import tpu as pltpu

F32 = jnp.float32
BF16 = jnp.bfloat16

LANES = 128
SUBLANES = 8
VMEM_LIMIT = 48 * 1024 * 1024

EPS = 1e-6
HEAD_DIM = 64
Q_HEADS = 8
KV_HEADS = 2
GROUP = Q_HEADS // KV_HEADS
ATTN_WIDTH = Q_HEADS * HEAD_DIM
KV_WIDTH = KV_HEADS * HEAD_DIM
WINDOW = 128
ROPE_THETA = 10000.0
SSM_HEADS = 8
SSM_HEAD_DIM = 64
SSM_WIDTH = SSM_HEADS * SSM_HEAD_DIM
SSM_GROUPS = 2
SSM_STATE = 128
GROUP_WIDTH = SSM_WIDTH // SSM_GROUPS
CONV_WIDTH = 4
CONV_CH = SSM_WIDTH + 2 * SSM_GROUPS * SSM_STATE
CHUNK = 128
N_EXPERTS = 8
TOP_K = 2

Q_TILES = ATTN_WIDTH // LANES
C_Q = 0
C_K = C_Q + ATTN_WIDTH
C_V = C_K + KV_WIDTH
C_Z = C_V + KV_WIDTH
C_X = C_Z + SSM_WIDTH
C_DT = C_X + CONV_CH
C_END = C_DT + LANES

ROW_TILE = 512
ATTN_TILE = 512
MOE_ROWS = 1024
MOE_FT = 512
DISPATCH_TOKENS = 2048
COMBINE_TOKENS = 256


def _cparams(*sem):
    return pltpu.CompilerParams(dimension_semantics=sem, vmem_limit_bytes=VMEM_LIMIT)


def _silu(v):
    return v * (1.0 / (1.0 + jnp.exp(-v)))


def _softplus(v):
    return jnp.maximum(v, 0.0) + jnp.log1p(jnp.exp(-jnp.abs(v)))


def _rms_mod(x, g, scale, shift):
    ms = jnp.mean(x * x, axis=-1, keepdims=True)
    return (x * lax.rsqrt(ms + EPS) * g) * (1.0 + scale) + shift


def _ada_kernel(c_ref, w_ref, b_ref, o_ref):
    c = c_ref[...]
    o_ref[0] = jnp.dot(_silu(c), w_ref[0], preferred_element_type=F32,
                       precision=lax.Precision.HIGHEST) + b_ref[0]


def _ada_mod(c8, ada_w, ada_b):
    depth, d, n = ada_w.shape
    tn = 1536
    return pl.pallas_call(
        _ada_kernel,
        grid=(depth, n // tn),
        in_specs=[pl.BlockSpec((SUBLANES, d), lambda l, j: (0, 0)),
                  pl.BlockSpec((1, d, tn), lambda l, j: (l, 0, j)),
                  pl.BlockSpec((1, 1, tn), lambda l, j: (l, 0, j))],
        out_specs=pl.BlockSpec((1, SUBLANES, tn), lambda l, j: (l, 0, j)),
        out_shape=jax.ShapeDtypeStruct((depth, SUBLANES, n), F32),
        compiler_params=_cparams("parallel", "parallel"),
    )(c8, ada_w, ada_b.reshape(depth, 1, n))


def _rope_kernel(pos_ref, inv_ref, cos_ref, sin_ref):
    ang = pos_ref[...].astype(F32) * inv_ref[...]
    cos_ref[...] = jnp.cos(ang)
    sin_ref[...] = jnp.sin(ang)


def _rope_tables(positions):
    t = positions.size
    per_row = LANES // (HEAD_DIM // 2)
    pos_dense = jnp.repeat(positions.reshape(t // per_row, per_row), HEAD_DIM // 2, axis=1)
    inv_freq = ROPE_THETA ** (-jnp.arange(0, HEAD_DIM, 2, dtype=F32) / HEAD_DIM)
    inv_dense = jnp.tile(inv_freq, per_row).reshape(1, LANES)
    rows = t // per_row
    tr = min(1024, rows)
    cos_d, sin_d = pl.pallas_call(
        _rope_kernel,
        grid=(rows // tr,),
        in_specs=[pl.BlockSpec((tr, LANES), lambda i: (i, 0)),
                  pl.BlockSpec((1, LANES), lambda i: (0, 0))],
        out_specs=[pl.BlockSpec((tr, LANES), lambda i: (i, 0))] * 2,
        out_shape=[jax.ShapeDtypeStruct((rows, LANES), F32)] * 2,
        compiler_params=_cparams("parallel"),
    )(pos_dense, inv_dense)
    cos = cos_d.reshape(t, HEAD_DIM // 2)
    sin = sin_d.reshape(t, HEAD_DIM // 2)
    return jnp.tile(cos, (1, 4)), jnp.concatenate([-sin, -sin, sin, sin], axis=1)


def _inproj_kernel(x_ref, g_ref, sc_ref, sh_ref, cos_ref, sin_ref, w_ref, dtb_ref,
                   q_ref, kv_ref, z_ref, xbc_ref, dt_ref):
    h = _rms_mod(x_ref[...], g_ref[...], sc_ref[0], sh_ref[0]).astype(BF16)
    cos = cos_ref[...]
    sin = sin_ref[...]

    def rope(t):
        return t * cos + pltpu.roll(t, LANES // 2, axis=1) * sin

    qkv = jnp.dot(h, w_ref[:, C_Q:C_Z], preferred_element_type=F32)
    for i in range(Q_TILES):
        q_ref[:, i * LANES:(i + 1) * LANES] = (
            rope(qkv[:, i * LANES:(i + 1) * LANES]) * (HEAD_DIM ** -0.5)).astype(BF16)
    kv_ref[:, 0:LANES] = rope(qkv[:, C_K:C_V]).astype(BF16)
    kv_ref[:, LANES:2 * LANES] = qkv[:, C_V:C_Z].astype(BF16)
    z_ref[...] = jnp.dot(h, w_ref[:, C_Z:C_X], preferred_element_type=F32)
    xbc_ref[...] = jnp.dot(h, w_ref[:, C_X:C_DT], preferred_element_type=F32)
    dt_raw = jnp.dot(h, w_ref[:, C_DT:C_END], preferred_element_type=F32)
    dt_ref[...] = _softplus(dt_raw + dtb_ref[...])


def _inproj(x2, g, sc, sh, cos, sin, w, dtb, tiles_per_batch):
    t, d = x2.shape
    tm = ROW_TILE
    row = lambda i: (i, 0)
    const = lambda i: (0, 0)
    per_b = lambda i: (i // tiles_per_batch, 0, 0)
    return pl.pallas_call(
        _inproj_kernel,
        grid=(t // tm,),
        in_specs=[pl.BlockSpec((tm, d), row),
                  pl.BlockSpec((1, d), const),
                  pl.BlockSpec((1, 1, d), per_b),
                  pl.BlockSpec((1, 1, d), per_b),
                  pl.BlockSpec((tm, LANES), row),
                  pl.BlockSpec((tm, LANES), row),
                  pl.BlockSpec((d, C_END), const),
                  pl.BlockSpec((1, LANES), const)],
        out_specs=[pl.BlockSpec((tm, ATTN_WIDTH), row),
                   pl.BlockSpec((tm, 2 * KV_WIDTH), row),
                   pl.BlockSpec((tm, SSM_WIDTH), row),
                   pl.BlockSpec((tm, CONV_CH), row),
                   pl.BlockSpec((tm, LANES), row)],
        out_shape=[jax.ShapeDtypeStruct((t, ATTN_WIDTH), BF16),
                   jax.ShapeDtypeStruct((t, 2 * KV_WIDTH), BF16),
                   jax.ShapeDtypeStruct((t, SSM_WIDTH), F32),
                   jax.ShapeDtypeStruct((t, CONV_CH), F32),
                   jax.ShapeDtypeStruct((t, LANES), F32)],
        compiler_params=_cparams("parallel"),
    )(x2, g, sc, sh, cos, sin, w, dtb)


def _attn_kernel(sink_ref, q_ref, kv_ref, kvp_ref, o_ref, *, tiles_per_seq):
    first = (pl.program_id(0) % tiles_per_seq) == 0
    blk = WINDOW
    nsub = ATTN_TILE // blk
    lane = lax.broadcasted_iota(jnp.int32, (1, LANES), 1)
    k_lo_mask = (lane % HEAD_DIM) < (HEAD_DIM // 2)
    v_lo_mask = lane < HEAD_DIM
    zero = jnp.zeros((), BF16)

    k_all = jnp.concatenate([kvp_ref[:, 0:LANES], kv_ref[:, 0:LANES]], axis=0)
    v_all = jnp.concatenate([kvp_ref[:, LANES:2 * LANES], kv_ref[:, LANES:2 * LANES]], axis=0)
    k_sel = (jnp.where(k_lo_mask, k_all, zero), jnp.where(k_lo_mask, zero, k_all))
    v_sel = (jnp.where(v_lo_mask, v_all, zero), jnp.where(v_lo_mask, zero, v_all))

    qi = lax.broadcasted_iota(jnp.int32, (Q_TILES * blk, 2 * blk), 0) % blk
    kj = lax.broadcasted_iota(jnp.int32, (Q_TILES * blk, 2 * blk), 1)
    band = (kj <= blk + qi) & (kj > qi)
    band_first = band & (kj >= jnp.where(first, blk, 0))
    sink_col = [jnp.concatenate([jnp.full((blk, 1), sink_ref[hk * GROUP + i], F32)
                                 for i in range(GROUP)], axis=0) for hk in range(KV_HEADS)]

    for n in range(nsub):
        q_st = jnp.concatenate([q_ref[n * blk:(n + 1) * blk, i * LANES:(i + 1) * LANES]
                                for i in range(Q_TILES)], axis=0)
        mask = band_first if n == 0 else band
        out = None
        for hk in range(KV_HEADS):
            k_n = k_sel[hk][n * blk:(n + 2) * blk]
            v_n = v_sel[hk][n * blk:(n + 2) * blk]
            s = lax.dot_general(q_st, k_n, (((1,), (1,)), ((), ())), preferred_element_type=F32)
            s = jnp.where(mask, s, -jnp.inf)
            m = jnp.maximum(jnp.max(s, axis=-1, keepdims=True), sink_col[hk])
            p = jnp.exp(s - m)
            denom = jnp.sum(p, axis=-1, keepdims=True) + jnp.exp(sink_col[hk] - m)
            o = jnp.dot(p.astype(BF16), v_n, preferred_element_type=F32) / denom
            out = o if out is None else out + o
        for i in range(Q_TILES):
            o_ref[n * blk:(n + 1) * blk, i * LANES:(i + 1) * LANES] = out[i * blk:(i + 1) * blk].astype(BF16)


def _attention(q, kv, sinks, seq):
    t = q.shape[0]
    tq = ATTN_TILE
    r = tq // WINDOW
    kern = functools.partial(_attn_kernel, tiles_per_seq=seq // tq)
    return pl.pallas_call(
        kern,
        grid=(t // tq,),
        in_specs=[pl.BlockSpec(memory_space=pltpu.SMEM),
                  pl.BlockSpec((tq, ATTN_WIDTH), lambda i: (i, 0)),
                  pl.BlockSpec((tq, 2 * KV_WIDTH), lambda i: (i, 0)),
                  pl.BlockSpec((WINDOW, 2 * KV_WIDTH), lambda i: (jnp.maximum(i * r - 1, 0), 0))],
        out_specs=pl.BlockSpec((tq, ATTN_WIDTH), lambda i: (i, 0)),
        out_shape=jax.ShapeDtypeStruct((t, ATTN_WIDTH), BF16),
        compiler_params=_cparams("parallel"),
    )(sinks, q, kv, kv)


def _ssd_kernel(xbc_ref, z_ref, dt_ref, cw_ref, cb_ref, alog_ref, dskip_ref, ng_ref,
                o_ref, ext_ref, st_ref):
    L = CHUNK
    halo = SUBLANES

    @pl.when(pl.program_id(1) == 0)
    def _():
        ext_ref[0:halo, :] = jnp.zeros((halo, CONV_CH), F32)
        st_ref[...] = jnp.zeros(st_ref.shape, F32)

    ext_ref[halo:halo + L, :] = xbc_ref[...]
    acc = cb_ref[...] + cw_ref[CONV_WIDTH - 1:CONV_WIDTH, :] * ext_ref[halo:halo + L, :]
    for k in range(CONV_WIDTH - 1):
        off = halo - (CONV_WIDTH - 1) + k
        acc = acc + cw_ref[k:k + 1, :] * ext_ref[off:off + L, :]
    ext_ref[0:halo, :] = ext_ref[L:L + halo, :]
    u = _silu(acc)
    xs = u[:, 0:SSM_WIDTH]
    bm = u[:, SSM_WIDTH:SSM_WIDTH + SSM_GROUPS * SSM_STATE]
    cm = u[:, SSM_WIDTH + SSM_GROUPS * SSM_STATE:]

    lane = lax.broadcasted_iota(jnp.int32, (1, LANES), 1)
    a = jnp.where(lane < SSM_HEADS, -jnp.exp(alog_ref[...]), 0.0)
    dt = dt_ref[...]
    dta = dt * a
    row = lax.broadcasted_iota(jnp.int32, (L, L), 0)
    col = lax.broadcasted_iota(jnp.int32, (L, L), 1)
    causal = row >= col
    tri = jnp.where(causal, 1.0, 0.0).astype(BF16)
    p0 = dta.astype(BF16)
    r1 = dta - p0.astype(F32)
    p1 = r1.astype(BF16)
    p2 = (r1 - p1.astype(F32)).astype(BF16)
    acs = (jnp.dot(tri, p0, preferred_element_type=F32) + jnp.dot(tri, p1, preferred_element_type=F32)
           + jnp.dot(tri, p2, preferred_element_type=F32))
    acs_t = acs.T

    head_of_lane = lax.broadcasted_iota(jnp.int32, (1, SSM_WIDTH), 1) // SSM_HEAD_DIM

    def expand(m):
        out = jnp.zeros((L, SSM_WIDTH), F32)
        for h in range(SSM_HEADS):
            out = jnp.where(head_of_lane == h, m[:, h:h + 1], out)
        return out

    dt_e = expand(dt)
    acs_e = expand(acs)
    last = acs_e[L - 1:L, :]
    xd = xs * dt_e
    xd_b = xd.astype(BF16)
    xdw_b = (xd * jnp.exp(last - acs_e)).astype(BF16)
    e_acs = jnp.exp(acs_e)
    c_dec = jnp.exp(last)

    r_heads = SSM_HEADS // SSM_GROUPS
    glane = lax.broadcasted_iota(jnp.int32, (1, GROUP_WIDTH), 1) // SSM_HEAD_DIM
    zero = jnp.zeros((), BF16)
    ys = []
    for g in range(SSM_GROUPS):
        gs = slice(g * GROUP_WIDTH, (g + 1) * GROUP_WIDTH)
        b_g = bm[:, g * SSM_STATE:(g + 1) * SSM_STATE]
        c_b = cm[:, g * SSM_STATE:(g + 1) * SSM_STATE].astype(BF16)
        cb = lax.dot_general(c_b, b_g.astype(BF16), (((1,), (1,)), ((), ())),
                             preferred_element_type=F32)
        st = st_ref[g]
        y_g = jnp.dot(c_b, st.astype(BF16), preferred_element_type=F32) * e_acs[:, gs]
        xd_g = xd_b[:, gs]
        for r in range(r_heads):
            h = g * r_heads + r
            seg = acs[:, h:h + 1] - acs_t[h:h + 1, :]
            m_h = (cb * jnp.exp(jnp.where(causal, seg, -jnp.inf))).astype(BF16)
            y_g = y_g + jnp.dot(m_h, jnp.where(glane == r, xd_g, zero), preferred_element_type=F32)
        new = jnp.dot(b_g.T.astype(BF16), xdw_b[:, gs], preferred_element_type=F32)
        st_ref[g] = c_dec[:, gs] * st + new
        ys.append(y_g)

    y = jnp.concatenate(ys, axis=1) + dskip_ref[...] * xs
    y = y * _silu(z_ref[...])
    outs = []
    for g in range(SSM_GROUPS):
        yg = y[:, g * GROUP_WIDTH:(g + 1) * GROUP_WIDTH]
        outs.append(yg * lax.rsqrt(jnp.mean(yg * yg, axis=-1, keepdims=True) + EPS))
    o_ref[...] = (jnp.concatenate(outs, axis=1) * ng_ref[...]).astype(BF16)


def _ssd(xbc, z, dt, cw, cb, alog, dskip, ng, batch, seq):
    t = xbc.shape[0]
    nc = seq // CHUNK
    row = lambda b, c: (b * nc + c, 0)
    const = lambda b, c: (0, 0)
    return pl.pallas_call(
        _ssd_kernel,
        grid=(batch, nc),
        in_specs=[pl.BlockSpec((CHUNK, CONV_CH), row),
                  pl.BlockSpec((CHUNK, SSM_WIDTH), row),
                  pl.BlockSpec((CHUNK, LANES), row),
                  pl.BlockSpec((CONV_WIDTH, CONV_CH), const),
                  pl.BlockSpec((1, CONV_CH), const),
                  pl.BlockSpec((1, LANES), const),
                  pl.BlockSpec((1, SSM_WIDTH), const),
                  pl.BlockSpec((1, SSM_WIDTH), const)],
        out_specs=pl.BlockSpec((CHUNK, SSM_WIDTH), row),
        out_shape=jax.ShapeDtypeStruct((t, SSM_WIDTH), BF16),
        scratch_shapes=[pltpu.VMEM((SUBLANES + CHUNK, CONV_CH), F32),
                        pltpu.VMEM((SSM_GROUPS, SSM_STATE, GROUP_WIDTH), F32)],
        compiler_params=_cparams("parallel", "arbitrary"),
    )(xbc, z, dt, cw, cb, alog, dskip, ng)


def _outproj_core(a_ref, s_ref, x_ref, w_ref, gm_ref, g_ref, sc_ref, sh_ref):
    half = a_ref.shape[1]
    mixed = (jnp.dot(a_ref[...], w_ref[0:half, :], preferred_element_type=F32)
             + jnp.dot(s_ref[...], w_ref[half:, :], preferred_element_type=F32))
    x_new = x_ref[...] + gm_ref[0] * mixed
    return x_new, _rms_mod(x_new, g_ref[...], sc_ref[0], sh_ref[0])


def _outproj_kernel(a_ref, s_ref, x_ref, w_ref, gm_ref, g_ref, sc_ref, sh_ref, xo_ref, h_ref):
    x_new, h = _outproj_core(a_ref, s_ref, x_ref, w_ref, gm_ref, g_ref, sc_ref, sh_ref)
    xo_ref[...] = x_new
    h_ref[...] = h.astype(BF16)


def _outproj_router_kernel(a_ref, s_ref, x_ref, w_ref, gm_ref, g_ref, sc_ref, sh_ref, rw_hi_ref, rw_lo_ref,
                           xo_ref, h_ref, route_ref, gate_ref, cnt_ref, carry_ref):
    x_new, h = _outproj_core(a_ref, s_ref, x_ref, w_ref, gm_ref, g_ref, sc_ref, sh_ref)
    xo_ref[...] = x_new
    tm = h.shape[0]
    for s in range(h.shape[1] // LANES):
        h_ref[pl.ds(s, tm, stride=SUBLANES), :] = h[:, s * LANES:(s + 1) * LANES]

    h_hi = h.astype(BF16)
    h_lo = (h - h_hi.astype(F32)).astype(BF16)
    logits = (jnp.dot(h_hi, rw_hi_ref[...], preferred_element_type=F32)
              + jnp.dot(h_hi, rw_lo_ref[...], preferred_element_type=F32)
              + jnp.dot(h_lo, rw_hi_ref[...], preferred_element_type=F32))
    lane = lax.broadcasted_iota(jnp.int32, (tm, LANES), 1).astype(F32)
    logits = jnp.where(lane < N_EXPERTS, logits, -jnp.inf)
    m0 = jnp.max(logits, axis=-1, keepdims=True)
    i0 = jnp.min(jnp.where(logits == m0, lane, float(LANES)), axis=-1, keepdims=True)
    rest = jnp.where(lane == i0, -jnp.inf, logits)
    m1 = jnp.max(rest, axis=-1, keepdims=True)
    i1 = jnp.min(jnp.where(rest == m1, lane, float(LANES)), axis=-1, keepdims=True)
    e = jnp.exp(m1 - m0)
    g0 = 1.0 / (1.0 + e)
    g1 = e / (1.0 + e)

    @pl.when(pl.program_id(0) == 0)
    def _():
        carry_ref[...] = jnp.zeros(carry_ref.shape, F32)

    sel0 = lane == i0
    sel1 = lane == i1
    member = jnp.where(sel0 | sel1, 1.0, 0.0)
    r_i = lax.broadcasted_iota(jnp.int32, (tm, tm), 0)
    c_i = lax.broadcasted_iota(jnp.int32, (tm, tm), 1)
    strict = jnp.where(r_i > c_i, 1.0, 0.0).astype(BF16)
    rank = carry_ref[...] + jnp.dot(strict, member.astype(BF16), preferred_element_type=F32)
    r0 = jnp.sum(jnp.where(sel0, rank, 0.0), axis=-1, keepdims=True)
    r1 = jnp.sum(jnp.where(sel1, rank, 0.0), axis=-1, keepdims=True)
    carry_ref[...] = carry_ref[...] + jnp.sum(member, axis=0, keepdims=True)
    cnt_ref[...] = carry_ref[...]

    route = jnp.where(lane == 0.0, i0, jnp.where(lane == 1.0, i1, jnp.where(lane == 2.0, r0, r1)))
    route_ref[...] = route[:, 0:SUBLANES].astype(jnp.int32)
    gate_ref[...] = jnp.where(lane == 0.0, g0, g1)[:, 0:SUBLANES]


def _outproj(attn, ssm, x2, w, gm, g, sc, sh, tiles_per_batch, router=None):
    t, d = x2.shape
    tm = ROW_TILE
    row = lambda i: (i, 0)
    const = lambda i: (0, 0)
    per_b = lambda i: (i // tiles_per_batch, 0, 0)
    half = attn.shape[1]
    in_specs = [pl.BlockSpec((tm, half), row),
                pl.BlockSpec((tm, half), row),
                pl.BlockSpec((tm, d), row),
                pl.BlockSpec((2 * half, d), const),
                pl.BlockSpec((1, 1, d), per_b),
                pl.BlockSpec((1, d), const),
                pl.BlockSpec((1, 1, d), per_b),
                pl.BlockSpec((1, 1, d), per_b)]
    if router is None:
        return pl.pallas_call(
            _outproj_kernel,
            grid=(t // tm,),
            in_specs=in_specs,
            out_specs=[pl.BlockSpec((tm, d), row), pl.BlockSpec((tm, d), row)],
            out_shape=[jax.ShapeDtypeStruct((t, d), F32), jax.ShapeDtypeStruct((t, d), BF16)],
            compiler_params=_cparams("parallel"),
        )(attn, ssm, x2, w, gm, g, sc, sh)
    rw_hi, rw_lo = router
    return pl.pallas_call(
        _outproj_router_kernel,
        grid=(t // tm,),
        in_specs=in_specs + [pl.BlockSpec((d, LANES), const), pl.BlockSpec((d, LANES), const)],
        out_specs=[pl.BlockSpec((tm, d), row),
                   pl.BlockSpec((tm * SUBLANES, LANES), row),
                   pl.BlockSpec((tm, SUBLANES), row),
                   pl.BlockSpec((tm, SUBLANES), row),
                   pl.BlockSpec((1, LANES), const)],
        out_shape=[jax.ShapeDtypeStruct((t, d), F32),
                   jax.ShapeDtypeStruct((t * SUBLANES, LANES), F32),
                   jax.ShapeDtypeStruct((t, SUBLANES), jnp.int32),
                   jax.ShapeDtypeStruct((t, SUBLANES), F32),
                   jax.ShapeDtypeStruct((1, LANES), F32)],
        scratch_shapes=[pltpu.VMEM((1, LANES), F32)],
        compiler_params=_cparams("arbitrary"),
    )(attn, ssm, x2, w, gm, g, sc, sh, rw_hi, rw_lo)


def _ffn_kernel(h_ref, x_ref, wg_ref, wu_ref, wd_ref, gf_ref, o_ref, acc_ref):
    j = pl.program_id(1)
    h = h_ref[...]
    gate = jnp.dot(h, wg_ref[...], preferred_element_type=F32)
    up = jnp.dot(h, wu_ref[...], preferred_element_type=F32)
    part = jnp.dot((_silu(gate) * up).astype(BF16), wd_ref[...], preferred_element_type=F32)

    @pl.when(j == 0)
    def _():
        acc_ref[...] = part

    @pl.when(j > 0)
    def _():
        acc_ref[...] += part

    @pl.when(j == pl.num_programs(1) - 1)
    def _():
        o_ref[...] = x_ref[...] + gf_ref[0] * acc_ref[...]


def _ffn(h, x2, wg, wu, wd, gf, tiles_per_batch):
    t, d = x2.shape
    f = wg.shape[1]
    tm = ROW_TILE
    nf = 2
    tf = f // nf
    row = lambda i, j: (i, 0)
    return pl.pallas_call(
        _ffn_kernel,
        grid=(t // tm, nf),
        in_specs=[pl.BlockSpec((tm, d), row),
                  pl.BlockSpec((tm, d), row),
                  pl.BlockSpec((d, tf), lambda i, j: (0, j)),
                  pl.BlockSpec((d, tf), lambda i, j: (0, j)),
                  pl.BlockSpec((tf, d), lambda i, j: (j, 0)),
                  pl.BlockSpec((1, 1, d), lambda i, j: (i // tiles_per_batch, 0, 0))],
        out_specs=pl.BlockSpec((tm, d), row),
        out_shape=jax.ShapeDtypeStruct((t, d), F32),
        scratch_shapes=[pltpu.VMEM((tm, d), F32)],
        compiler_params=_cparams("parallel", "arbitrary"),
    )(h, x2, wg, wu, wd, gf)


def _row_copy(src, src_row, dst, dst_row, sem):
    return pltpu.make_async_copy(
        src.at[pl.ds(pl.multiple_of(src_row * SUBLANES, SUBLANES), SUBLANES)],
        dst.at[pl.ds(pl.multiple_of(dst_row * SUBLANES, SUBLANES), SUBLANES)], sem)


def _dispatch_kernel(zs_ref, ze_ref, nvalid_ref, dest_ref, h_hbm, xb_hbm, zero_ref, sem, zsem):
    i = pl.program_id(0)
    n_tok = dest_ref.shape[0] // TOP_K
    blk_rows = zero_ref.shape[0]
    n_blocks = xb_hbm.shape[0] // blk_rows

    def zero_row(r):
        return pltpu.make_async_copy(
            zero_ref.at[pl.ds(0, SUBLANES)],
            xb_hbm.at[pl.ds(pl.multiple_of(r * SUBLANES, SUBLANES), SUBLANES)], zsem)

    def zero_block(b):
        return pltpu.make_async_copy(
            zero_ref, xb_hbm.at[pl.ds(pl.multiple_of(b * blk_rows, blk_rows), blk_rows)], zsem)

    @pl.when(i == 0)
    def _():
        zero_ref[...] = jnp.zeros(zero_ref.shape, F32)
        for start in (True, False):
            for e in range(N_EXPERTS):
                def rows_body(r, c):
                    zero_row(r).start() if start else zero_row(r).wait()
                    return c
                lax.fori_loop(zs_ref[e], ze_ref[e], rows_body, 0)

            def blocks_body(b, c):
                zero_block(b).start() if start else zero_block(b).wait()
                return c
            lax.fori_loop(nvalid_ref[0], n_blocks, blocks_body, 0)

    def issue(t, c):
        tok = i * n_tok + t
        for k in range(TOP_K):
            _row_copy(h_hbm, tok, xb_hbm, dest_ref[t * TOP_K + k], sem).start()
        return c

    lax.fori_loop(0, n_tok, issue, 0)
    rows = n_tok * TOP_K * SUBLANES
    pltpu.make_async_copy(xb_hbm.at[pl.ds(0, rows)], xb_hbm.at[pl.ds(0, rows)], sem).wait()


def _dispatch(h_rows, dest, zs, ze, nvalid, n_blocks, rows):
    t = h_rows.shape[0] // SUBLANES
    td = min(DISPATCH_TOKENS, t)
    grid_spec = pltpu.PrefetchScalarGridSpec(
        num_scalar_prefetch=3,
        grid=(t // td,),
        in_specs=[pl.BlockSpec((td * TOP_K,), lambda i, zs, ze, nv: (i,), memory_space=pltpu.SMEM),
                  pl.BlockSpec(memory_space=pl.ANY)],
        out_specs=pl.BlockSpec(memory_space=pl.ANY),
        scratch_shapes=[pltpu.VMEM((rows * SUBLANES, LANES), F32),
                        pltpu.SemaphoreType.DMA(()), pltpu.SemaphoreType.DMA(())],
    )
    return pl.pallas_call(
        _dispatch_kernel,
        grid_spec=grid_spec,
        out_shape=jax.ShapeDtypeStruct((n_blocks * rows * SUBLANES, LANES), F32),
        compiler_params=pltpu.CompilerParams(dimension_semantics=("arbitrary",), has_side_effects=True,
                                             vmem_limit_bytes=VMEM_LIMIT),
    )(zs, ze, nvalid, dest, h_rows)


def _moe_kernel(bexp_ref, nvalid_ref, xb_ref, wg_ref, wu_ref, wd_ref, y_ref, x_scr, acc_ref):
    b = pl.program_id(0)
    j = pl.program_id(1)
    nf = pl.num_programs(1)
    rows = x_scr.shape[0]
    nsl = x_scr.shape[1] // LANES

    @pl.when(b < nvalid_ref[0])
    def _():
        @pl.when(j == 0)
        def _():
            for s in range(nsl):
                x_scr[:, s * LANES:(s + 1) * LANES] = xb_ref[pl.ds(s, rows, stride=SUBLANES), :].astype(BF16)

        x = x_scr[...]
        gate = jnp.dot(x, wg_ref[0].astype(BF16), preferred_element_type=F32)
        up = jnp.dot(x, wu_ref[0].astype(BF16), preferred_element_type=F32)
        part = jnp.dot((_silu(gate) * up).astype(BF16), wd_ref[0].astype(BF16), preferred_element_type=F32)

        @pl.when(j == 0)
        def _():
            acc_ref[...] = part

        @pl.when(j > 0)
        def _():
            acc_ref[...] += part

        @pl.when(j == nf - 1)
        def _():
            for s in range(nsl):
                y_ref[pl.ds(s, rows, stride=SUBLANES), :] = acc_ref[:, s * LANES:(s + 1) * LANES]

    @pl.when((b >= nvalid_ref[0]) & (j == nf - 1))
    def _():
        y_ref[...] = jnp.zeros(y_ref.shape, F32)


def _moe_experts(xb, bexp, nvalid, wg, wu, wd, n_blocks, rows):
    d = wg.shape[1]
    f = wg.shape[2]
    nf = f // MOE_FT

    def blk(b, j, bexp, nvalid):
        return (jnp.minimum(b, nvalid[0] - 1), 0)

    def fidx(b, j, nvalid):
        return jnp.where(b < nvalid[0], j, nf - 1)

    grid_spec = pltpu.PrefetchScalarGridSpec(
        num_scalar_prefetch=2,
        grid=(n_blocks, nf),
        in_specs=[pl.BlockSpec((rows * SUBLANES, LANES), blk),
                  pl.BlockSpec((1, d, MOE_FT), lambda b, j, bexp, nvalid: (bexp[b], 0, fidx(b, j, nvalid))),
                  pl.BlockSpec((1, d, MOE_FT), lambda b, j, bexp, nvalid: (bexp[b], 0, fidx(b, j, nvalid))),
                  pl.BlockSpec((1, MOE_FT, d), lambda b, j, bexp, nvalid: (bexp[b], fidx(b, j, nvalid), 0))],
        out_specs=pl.BlockSpec((rows * SUBLANES, LANES), lambda b, j, bexp, nvalid: (b, 0)),
        scratch_shapes=[pltpu.VMEM((rows, d), BF16), pltpu.VMEM((rows, d), F32)],
    )
    return pl.pallas_call(
        _moe_kernel,
        grid_spec=grid_spec,
        out_shape=jax.ShapeDtypeStruct(xb.shape, F32),
        compiler_params=_cparams("arbitrary", "arbitrary"),
    )(bexp, nvalid, xb, wg, wu, wd)


def _combine_kernel(dest_ref, y_hbm, x_ref, gate_ref, gf_ref, fg_ref, o_ref, buf_ref, sem):
    tm = x_ref.shape[0]
    nsl = x_ref.shape[1] // LANES

    def issue(t, c):
        for k in range(TOP_K):
            _row_copy(y_hbm, dest_ref[t * TOP_K + k], buf_ref, k * tm + t, sem).start()
        return c

    lax.fori_loop(0, tm, issue, 0)
    rows = tm * TOP_K * SUBLANES
    pltpu.make_async_copy(y_hbm.at[pl.ds(0, rows)], buf_ref, sem).wait()

    g0 = gate_ref[:, 0:1]
    g1 = gate_ref[:, 1:2]
    parts = []
    for s in range(nsl):
        y0 = buf_ref[pl.ds(s, tm, stride=SUBLANES), :]
        y1 = buf_ref[pl.ds(tm * SUBLANES + s, tm, stride=SUBLANES), :]
        parts.append(g0 * y0 + g1 * y1)
    x = x_ref[...] + gf_ref[0] * jnp.concatenate(parts, axis=1)
    ms = jnp.mean(x * x, axis=-1, keepdims=True)
    o_ref[...] = x * lax.rsqrt(ms + EPS) * fg_ref[...]


def _combine(y, dest, x2, gates, gf, fg, tiles_per_batch):
    t, d = x2.shape
    tm = COMBINE_TOKENS
    return pl.pallas_call(
        _combine_kernel,
        grid=(t // tm,),
        in_specs=[pl.BlockSpec((tm * TOP_K,), lambda i: (i,), memory_space=pltpu.SMEM),
                  pl.BlockSpec(memory_space=pl.ANY),
                  pl.BlockSpec((tm, d), lambda i: (i, 0)),
                  pl.BlockSpec((tm, SUBLANES), lambda i: (i, 0)),
                  pl.BlockSpec((1, 1, d), lambda i: (i // tiles_per_batch, 0, 0)),
                  pl.BlockSpec((1, d), lambda i: (0, 0))],
        out_specs=pl.BlockSpec((tm, d), lambda i: (i, 0)),
        out_shape=jax.ShapeDtypeStruct((t, d), F32),
        scratch_shapes=[pltpu.VMEM((tm * TOP_K * SUBLANES, LANES), F32), pltpu.SemaphoreType.DMA(())],
        compiler_params=_cparams("arbitrary"),
    )(dest, y, x2, gates, gf, fg)


def _final_norm_kernel(x_ref, g_ref, o_ref):
    x = x_ref[...]
    o_ref[...] = x * lax.rsqrt(jnp.mean(x * x, axis=-1, keepdims=True) + EPS) * g_ref[...]


def _final_norm(x2, g):
    t, d = x2.shape
    tm = ROW_TILE
    return pl.pallas_call(
        _final_norm_kernel,
        grid=(t // tm,),
        in_specs=[pl.BlockSpec((tm, d), lambda i: (i, 0)), pl.BlockSpec((1, d), lambda i: (0, 0))],
        out_specs=pl.BlockSpec((tm, d), lambda i: (i, 0)),
        out_shape=jax.ShapeDtypeStruct((t, d), F32),
        compiler_params=_cparams("parallel"),
    )(x2, g)


def _qk_column_perm():
    half = HEAD_DIM // 2
    lane = np.arange(LANES)
    pair = (lane % HEAD_DIM) // half
    dim = lane % half + half * (lane // HEAD_DIM)
    q = np.concatenate([(i + Q_TILES * pair) * HEAD_DIM + dim for i in range(Q_TILES)])
    k = ATTN_WIDTH + pair * HEAD_DIM + dim
    return q, k


def _attn_out_row_perm():
    lane = np.arange(LANES)
    return np.concatenate([(i + Q_TILES * (lane // HEAD_DIM)) * HEAD_DIM + lane % HEAD_DIM
                           for i in range(Q_TILES)])


def _pad_lanes(v):
    return jnp.pad(v.astype(F32), (0, LANES - v.shape[0])).reshape(1, LANES)


def _moe_route_tables(route, counts, rows, n_blocks):
    e0, e1, r0, r1 = route[:, 0], route[:, 1], route[:, 2], route[:, 3]
    cnt = counts[0, :N_EXPERTS].astype(jnp.int32)
    padded = (cnt + rows - 1) // rows * rows
    pad_end = jnp.cumsum(padded)
    pad_start = pad_end - padded
    dest = jnp.stack([pad_start[e0] + r0, pad_start[e1] + r1], axis=1).reshape(-1)
    bexp = jnp.minimum(jnp.searchsorted(pad_end, jnp.arange(n_blocks, dtype=jnp.int32) * rows, side='right'),
                       N_EXPERTS - 1).astype(jnp.int32)
    nvalid = (pad_end[-1:] // rows).astype(jnp.int32)
    return dest.astype(jnp.int32), bexp, nvalid, (pad_start + cnt).astype(jnp.int32), pad_end.astype(jnp.int32)


def kernel(x, c, positions, ada_w, ada_b, norm_mix_g, norm_ffn_g, w_in, w_out, attn_sinks, conv_w, conv_b,
           dt_bias, a_log, d_skip, ssm_norm_g, ffn_w_gate, ffn_w_up, ffn_w_down, router_w, moe_w_gate,
           moe_w_up, moe_w_down, final_norm_g):
    batch, seq, d = x.shape
    depth = w_in.shape[0]
    t = batch * seq
    tiles_per_batch = seq // ROW_TILE
    x2 = x.reshape(t, d)

    c8 = jnp.pad(c, ((0, SUBLANES - batch), (0, 0)))
    mod = _ada_mod(c8, ada_w, ada_b)[:, :batch].reshape(depth, batch, 6, 1, d)
    cos, sin = _rope_tables(positions)

    q_perm, k_perm = _qk_column_perm()
    o_perm = _attn_out_row_perm()
    for l in range(depth):
        sh_m, sc_m, g_m, sh_f, sc_f, g_f = (mod[l, :, k] for k in range(6))
        wl = w_in[l]
        w_cat = jnp.concatenate(
            [wl[:, q_perm], wl[:, k_perm], wl[:, C_V:C_DT],
             jnp.pad(wl[:, C_DT:], ((0, 0), (0, LANES - SSM_HEADS)))], axis=1).astype(BF16)
        q, kv, z, xbc, dt = _inproj(x2, norm_mix_g[l].reshape(1, d), sc_m, sh_m, cos, sin, w_cat,
                                    _pad_lanes(dt_bias[l]), tiles_per_batch)
        attn = _attention(q, kv, attn_sinks[l].astype(F32), seq)
        ssm = _ssd(xbc, z, dt, conv_w[l], conv_b[l].reshape(1, CONV_CH), _pad_lanes(a_log[l]),
                   jnp.repeat(d_skip[l].astype(F32), SSM_HEAD_DIM).reshape(1, SSM_WIDTH),
                   ssm_norm_g[l].reshape(1, SSM_WIDTH), batch, seq)
        wo = jnp.concatenate([w_out[l][o_perm], w_out[l][ATTN_WIDTH:]], axis=0).astype(BF16)
        ffn_g = norm_ffn_g[l].reshape(1, d)
        if l % 2 == 0:
            x_new, h = _outproj(attn, ssm, x2, wo, g_m, ffn_g, sc_f, sh_f, tiles_per_batch)
            x2 = _ffn(h, x_new, ffn_w_gate[l // 2].astype(BF16), ffn_w_up[l // 2].astype(BF16),
                      ffn_w_down[l // 2].astype(BF16), g_f, tiles_per_batch)
            if l == depth - 1:
                x2 = _final_norm(x2, final_norm_g.reshape(1, d))
        else:
            if l != depth - 1:
                raise NotImplementedError("the expert layer fuses the final norm and must be last")
            rw = jnp.pad(router_w[l // 2].astype(F32), ((0, 0), (0, LANES - N_EXPERTS)))
            rw_hi = rw.astype(BF16)
            rw_lo = (rw - rw_hi.astype(F32)).astype(BF16)
            x_new, h_rows, route, gates, counts = _outproj(
                attn, ssm, x2, wo, g_m, ffn_g, sc_f, sh_f, tiles_per_batch, router=(rw_hi, rw_lo))
            rows = min(MOE_ROWS, t)
            n_blocks = (t * TOP_K) // rows + N_EXPERTS
            dest, bexp, nvalid, zs, ze = _moe_route_tables(route, counts, rows, n_blocks)
            xb = _dispatch(h_rows, dest, zs, ze, nvalid, n_blocks, rows)
            y = _moe_experts(xb, bexp, nvalid, moe_w_gate[l // 2], moe_w_up[l // 2], moe_w_down[l // 2],
                             n_blocks, rows)
            x2 = _combine(y, dest, x_new, gates, g_f, final_norm_g.reshape(1, d),
                          seq // COMBINE_TOKENS)
    return x2.reshape(batch, seq, d)
```

```python
import functools

import numpy as np
import jax
import jax.numpy as jnp
from jax import lax
from jax.experimental import pallas as pl
from jax.experimental.pallas import tpu as pltpu

F32 = jnp.float32
BF16 = jnp.bfloat16

LANES = 128
SUBLANES = 8
VMEM_LIMIT = 48 * 1024 * 1024

EPS = 1e-6
HEAD_DIM = 64
Q_HEADS = 8
KV_HEADS = 2
GROUP = Q_HEADS // KV_HEADS
ATTN_WIDTH = Q_HEADS * HEAD_DIM
KV_WIDTH = KV_HEADS * HEAD_DIM
WINDOW = 128
ROPE_THETA = 10000.0
SSM_HEADS = 8
SSM_HEAD_DIM = 64
SSM_WIDTH = SSM_HEADS * SSM_HEAD_DIM
SSM_GROUPS = 2
SSM_STATE = 128
GROUP_WIDTH = SSM_WIDTH // SSM_GROUPS
CONV_WIDTH = 4
CONV_CH = SSM_WIDTH + 2 * SSM_GROUPS * SSM_STATE
CHUNK = 128
N_EXPERTS = 8
TOP_K = 2

Q_TILES = ATTN_WIDTH // LANES
C_Q = 0
C_K = C_Q + ATTN_WIDTH
C_V = C_K + KV_WIDTH
C_Z = C_V + KV_WIDTH
C_X = C_Z + SSM_WIDTH
C_DT = C_X + CONV_CH
C_END = C_DT + LANES

ROW_TILE = 512
ATTN_TILE = 512
MOE_ROWS = 1024
MOE_FT = 512
DISPATCH_TOKENS = 1024
COMBINE_TOKENS = 256


def _cparams(*sem):
    return pltpu.CompilerParams(dimension_semantics=sem, vmem_limit_bytes=VMEM_LIMIT)


def _silu(v):
    return v * (1.0 / (1.0 + jnp.exp(-v)))


def _softplus(v):
    return jnp.maximum(v, 0.0) + jnp.log1p(jnp.exp(-jnp.abs(v)))


def _rms_mod(x, g, scale, shift):
    ms = jnp.mean(x * x, axis=-1, keepdims=True)
    return (x * lax.rsqrt(ms + EPS) * g) * (1.0 + scale) + shift


def _ada_kernel(c_ref, w_ref, b_ref, o_ref):
    c = c_ref[...]
    o_ref[0] = jnp.dot(_silu(c), w_ref[0], preferred_element_type=F32,
                       precision=lax.Precision.HIGHEST) + b_ref[0]


def _ada_mod(c8, ada_w, ada_b):
    depth, d, n = ada_w.shape
    tn = 1536
    return pl.pallas_call(
        _ada_kernel,
        grid=(depth, n // tn),
        in_specs=[pl.BlockSpec((SUBLANES, d), lambda l, j: (0, 0)),
                  pl.BlockSpec((1, d, tn), lambda l, j: (l, 0, j)),
                  pl.BlockSpec((1, 1, tn), lambda l, j: (l, 0, j))],
        out_specs=pl.BlockSpec((1, SUBLANES, tn), lambda l, j: (l, 0, j)),
        out_shape=jax.ShapeDtypeStruct((depth, SUBLANES, n), F32),
        compiler_params=_cparams("parallel", "parallel"),
    )(c8, ada_w, ada_b.reshape(depth, 1, n))


def _rope_kernel(pos_ref, inv_ref, cos_ref, sin_ref):
    ang = pos_ref[...].astype(F32) * inv_ref[...]
    cos_ref[...] = jnp.cos(ang)
    sin_ref[...] = jnp.sin(ang)


def _rope_tables(positions):
    t = positions.size
    per_row = LANES // (HEAD_DIM // 2)
    pos_dense = jnp.repeat(positions.reshape(t // per_row, per_row), HEAD_DIM // 2, axis=1)
    inv_freq = ROPE_THETA ** (-jnp.arange(0, HEAD_DIM, 2, dtype=F32) / HEAD_DIM)
    inv_dense = jnp.tile(inv_freq, per_row).reshape(1, LANES)
    rows = t // per_row
    tr = min(1024, rows)
    cos_d, sin_d = pl.pallas_call(
        _rope_kernel,
        grid=(rows // tr,),
        in_specs=[pl.BlockSpec((tr, LANES), lambda i: (i, 0)),
                  pl.BlockSpec((1, LANES), lambda i: (0, 0))],
        out_specs=[pl.BlockSpec((tr, LANES), lambda i: (i, 0))] * 2,
        out_shape=[jax.ShapeDtypeStruct((rows, LANES), F32)] * 2,
        compiler_params=_cparams("parallel"),
    )(pos_dense, inv_dense)
    cos = cos_d.reshape(t, HEAD_DIM // 2)
    sin = sin_d.reshape(t, HEAD_DIM // 2)
    return jnp.tile(cos, (1, 4)), jnp.concatenate([-sin, -sin, sin, sin], axis=1)


def _inproj_kernel(x_ref, g_ref, sc_ref, sh_ref, cos_ref, sin_ref, w_ref, dtb_ref,
                   q_ref, kv_ref, z_ref, xbc_ref, dt_ref):
    h = _rms_mod(x_ref[...], g_ref[...], sc_ref[0], sh_ref[0]).astype(BF16)
    cos = cos_ref[...]
    sin = sin_ref[...]

    def rope(t):
        return t * cos + pltpu.roll(t, LANES // 2, axis=1) * sin

    qkv = jnp.dot(h, w_ref[:, C_Q:C_Z], preferred_element_type=F32)
    for i in range(Q_TILES):
        q_ref[:, i * LANES:(i + 1) * LANES] = (
            rope(qkv[:, i * LANES:(i + 1) * LANES]) * (HEAD_DIM ** -0.5)).astype(BF16)
    kv_ref[:, 0:LANES] = rope(qkv[:, C_K:C_V]).astype(BF16)
    kv_ref[:, LANES:2 * LANES] = qkv[:, C_V:C_Z].astype(BF16)
    z_ref[...] = jnp.dot(h, w_ref[:, C_Z:C_X], preferred_element_type=F32)
    xbc_ref[...] = jnp.dot(h, w_ref[:, C_X:C_DT], preferred_element_type=F32)
    dt_raw = jnp.dot(h, w_ref[:, C_DT:C_END], preferred_element_type=F32)
    dt_ref[...] = _softplus(dt_raw + dtb_ref[...])


def _inproj(x2, g, sc, sh, cos, sin, w, dtb, tiles_per_batch):
    t, d = x2.shape
    tm = ROW_TILE
    row = lambda i: (i, 0)
    const = lambda i: (0, 0)
    per_b = lambda i: (i // tiles_per_batch, 0, 0)
    return pl.pallas_call(
        _inproj_kernel,
        grid=(t // tm,),
        in_specs=[pl.BlockSpec((tm, d), row),
                  pl.BlockSpec((1, d), const),
                  pl.BlockSpec((1, 1, d), per_b),
                  pl.BlockSpec((1, 1, d), per_b),
                  pl.BlockSpec((tm, LANES), row),
                  pl.BlockSpec((tm, LANES), row),
                  pl.BlockSpec((d, C_END), const),
                  pl.BlockSpec((1, LANES), const)],
        out_specs=[pl.BlockSpec((tm, ATTN_WIDTH), row),
                   pl.BlockSpec((tm, 2 * KV_WIDTH), row),
                   pl.BlockSpec((tm, SSM_WIDTH), row),
                   pl.BlockSpec((tm, CONV_CH), row),
                   pl.BlockSpec((tm, LANES), row)],
        out_shape=[jax.ShapeDtypeStruct((t, ATTN_WIDTH), BF16),
                   jax.ShapeDtypeStruct((t, 2 * KV_WIDTH), BF16),
                   jax.ShapeDtypeStruct((t, SSM_WIDTH), F32),
                   jax.ShapeDtypeStruct((t, CONV_CH), F32),
                   jax.ShapeDtypeStruct((t, LANES), F32)],
        compiler_params=_cparams("parallel"),
    )(x2, g, sc, sh, cos, sin, w, dtb)


def _attn_kernel(sink_ref, q_ref, kv_ref, kvp_ref, o_ref, *, tiles_per_seq):
    first = (pl.program_id(0) % tiles_per_seq) == 0
    blk = WINDOW
    nsub = ATTN_TILE // blk
    lane = lax.broadcasted_iota(jnp.int32, (1, LANES), 1)
    k_lo_mask = (lane % HEAD_DIM) < (HEAD_DIM // 2)
    v_lo_mask = lane < HEAD_DIM
    zero = jnp.zeros((), BF16)

    k_all = jnp.concatenate([kvp_ref[:, 0:LANES], kv_ref[:, 0:LANES]], axis=0)
    v_all = jnp.concatenate([kvp_ref[:, LANES:2 * LANES], kv_ref[:, LANES:2 * LANES]], axis=0)
    k_sel = (jnp.where(k_lo_mask, k_all, zero), jnp.where(k_lo_mask, zero, k_all))
    v_sel = (jnp.where(v_lo_mask, v_all, zero), jnp.where(v_lo_mask, zero, v_all))

    qi = lax.broadcasted_iota(jnp.int32, (Q_TILES * blk, 2 * blk), 0) % blk
    kj = lax.broadcasted_iota(jnp.int32, (Q_TILES * blk, 2 * blk), 1)
    band = (kj <= blk + qi) & (kj > qi)
    band_first = band & (kj >= jnp.where(first, blk, 0))
    sink_col = [jnp.concatenate([jnp.full((blk, 1), sink_ref[hk * GROUP + i], F32)
                                 for i in range(GROUP)], axis=0) for hk in range(KV_HEADS)]

    for n in range(nsub):
        q_st = jnp.concatenate([q_ref[n * blk:(n + 1) * blk, i * LANES:(i + 1) * LANES]
                                for i in range(Q_TILES)], axis=0)
        mask = band_first if n == 0 else band
        out = None
        for hk in range(KV_HEADS):
            k_n = k_sel[hk][n * blk:(n + 2) * blk]
            v_n = v_sel[hk][n * blk:(n + 2) * blk]
            s = lax.dot_general(q_st, k_n, (((1,), (1,)), ((), ())), preferred_element_type=F32)
            s = jnp.where(mask, s, -jnp.inf)
            m = jnp.maximum(jnp.max(s, axis=-1, keepdims=True), sink_col[hk])
            p = jnp.exp(s - m)
            denom = jnp.sum(p, axis=-1, keepdims=True) + jnp.exp(sink_col[hk] - m)
            o = jnp.dot(p.astype(BF16), v_n, preferred_element_type=F32) / denom
            out = o if out is None else out + o
        for i in range(Q_TILES):
            o_ref[n * blk:(n + 1) * blk, i * LANES:(i + 1) * LANES] = out[i * blk:(i + 1) * blk].astype(BF16)


def _attention(q, kv, sinks, seq):
    t = q.shape[0]
    tq = ATTN_TILE
    r = tq // WINDOW
    kern = functools.partial(_attn_kernel, tiles_per_seq=seq // tq)
    return pl.pallas_call(
        kern,
        grid=(t // tq,),
        in_specs=[pl.BlockSpec(memory_space=pltpu.SMEM),
                  pl.BlockSpec((tq, ATTN_WIDTH), lambda i: (i, 0)),
                  pl.BlockSpec((tq, 2 * KV_WIDTH), lambda i: (i, 0)),
                  pl.BlockSpec((WINDOW, 2 * KV_WIDTH), lambda i: (jnp.maximum(i * r - 1, 0), 0))],
        out_specs=pl.BlockSpec((tq, ATTN_WIDTH), lambda i: (i, 0)),
        out_shape=jax.ShapeDtypeStruct((t, ATTN_WIDTH), BF16),
        compiler_params=_cparams("parallel"),
    )(sinks, q, kv, kv)


def _ssd_kernel(xbc_ref, z_ref, dt_ref, cw_ref, cb_ref, alog_ref, dskip_ref, ng_ref,
                o_ref, ext_ref, st_ref):
    L = CHUNK
    halo = SUBLANES

    @pl.when(pl.program_id(1) == 0)
    def _():
        ext_ref[0:halo, :] = jnp.zeros((halo, CONV_CH), F32)
        st_ref[...] = jnp.zeros(st_ref.shape, F32)

    ext_ref[halo:halo + L, :] = xbc_ref[...]
    acc = cb_ref[...] + cw_ref[CONV_WIDTH - 1:CONV_WIDTH, :] * ext_ref[halo:halo + L, :]
    for k in range(CONV_WIDTH - 1):
        off = halo - (CONV_WIDTH - 1) + k
        acc = acc + cw_ref[k:k + 1, :] * ext_ref[off:off + L, :]
    ext_ref[0:halo, :] = ext_ref[L:L + halo, :]
    u = _silu(acc)
    xs = u[:, 0:SSM_WIDTH]
    bm = u[:, SSM_WIDTH:SSM_WIDTH + SSM_GROUPS * SSM_STATE]
    cm = u[:, SSM_WIDTH + SSM_GROUPS * SSM_STATE:]

    lane = lax.broadcasted_iota(jnp.int32, (1, LANES), 1)
    a = jnp.where(lane < SSM_HEADS, -jnp.exp(alog_ref[...]), 0.0)
    dt = dt_ref[...]
    dta = dt * a
    row = lax.broadcasted_iota(jnp.int32, (L, L), 0)
    col = lax.broadcasted_iota(jnp.int32, (L, L), 1)
    causal = row >= col
    tri = jnp.where(causal, 1.0, 0.0).astype(BF16)
    p0 = dta.astype(BF16)
    r1 = dta - p0.astype(F32)
    p1 = r1.astype(BF16)
    p2 = (r1 - p1.astype(F32)).astype(BF16)
    acs = (jnp.dot(tri, p0, preferred_element_type=F32) + jnp.dot(tri, p1, preferred_element_type=F32)
           + jnp.dot(tri, p2, preferred_element_type=F32))
    acs_t = acs.T

    head_of_lane = lax.broadcasted_iota(jnp.int32, (1, SSM_WIDTH), 1) // SSM_HEAD_DIM

    def expand(m):
        out = jnp.zeros((L, SSM_WIDTH), F32)
        for h in range(SSM_HEADS):
            out = jnp.where(head_of_lane == h, m[:, h:h + 1], out)
        return out

    dt_e = expand(dt)
    acs_e = expand(acs)
    last = acs_e[L - 1:L, :]
    xd = xs * dt_e
    xd_b = xd.astype(BF16)
    xdw_b = (xd * jnp.exp(last - acs_e)).astype(BF16)
    e_acs = jnp.exp(acs_e)
    c_dec = jnp.exp(last)

    r_heads = SSM_HEADS // SSM_GROUPS
    glane = lax.broadcasted_iota(jnp.int32, (1, GROUP_WIDTH), 1) // SSM_HEAD_DIM
    zero = jnp.zeros((), BF16)
    ys = []
    for g in range(SSM_GROUPS):
        gs = slice(g * GROUP_WIDTH, (g + 1) * GROUP_WIDTH)
        b_g = bm[:, g * SSM_STATE:(g + 1) * SSM_STATE]
        c_b = cm[:, g * SSM_STATE:(g + 1) * SSM_STATE].astype(BF16)
        cb = lax.dot_general(c_b, b_g.astype(BF16), (((1,), (1,)), ((), ())),
                             preferred_element_type=F32)
        st = st_ref[g]
        y_g = jnp.dot(c_b, st.astype(BF16), preferred_element_type=F32) * e_acs[:, gs]
        xd_g = xd_b[:, gs]
        for r in range(r_heads):
            h = g * r_heads + r
            seg = acs[:, h:h + 1] - acs_t[h:h + 1, :]
            m_h = (cb * jnp.exp(jnp.where(causal, seg, -jnp.inf))).astype(BF16)
            y_g = y_g + jnp.dot(m_h, jnp.where(glane == r, xd_g, zero), preferred_element_type=F32)
        new = jnp.dot(b_g.T.astype(BF16), xdw_b[:, gs], preferred_element_type=F32)
        st_ref[g] = c_dec[:, gs] * st + new
        ys.append(y_g)

    y = jnp.concatenate(ys, axis=1) + dskip_ref[...] * xs
    y = y * _silu(z_ref[...])
    outs = []
    for g in range(SSM_GROUPS):
        yg = y[:, g * GROUP_WIDTH:(g + 1) * GROUP_WIDTH]
        outs.append(yg * lax.rsqrt(jnp.mean(yg * yg, axis=-1, keepdims=True) + EPS))
    o_ref[...] = (jnp.concatenate(outs, axis=1) * ng_ref[...]).astype(BF16)


def _ssd(xbc, z, dt, cw, cb, alog, dskip, ng, batch, seq):
    t = xbc.shape[0]
    nc = seq // CHUNK
    row = lambda b, c: (b * nc + c, 0)
    const = lambda b, c: (0, 0)
    return pl.pallas_call(
        _ssd_kernel,
        grid=(batch, nc),
        in_specs=[pl.BlockSpec((CHUNK, CONV_CH), row),
                  pl.BlockSpec((CHUNK, SSM_WIDTH), row),
                  pl.BlockSpec((CHUNK, LANES), row),
                  pl.BlockSpec((CONV_WIDTH, CONV_CH), const),
                  pl.BlockSpec((1, CONV_CH), const),
                  pl.BlockSpec((1, LANES), const),
                  pl.BlockSpec((1, SSM_WIDTH), const),
                  pl.BlockSpec((1, SSM_WIDTH), const)],
        out_specs=pl.BlockSpec((CHUNK, SSM_WIDTH), row),
        out_shape=jax.ShapeDtypeStruct((t, SSM_WIDTH), BF16),
        scratch_shapes=[pltpu.VMEM((SUBLANES + CHUNK, CONV_CH), F32),
                        pltpu.VMEM((SSM_GROUPS, SSM_STATE, GROUP_WIDTH), F32)],
        compiler_params=_cparams("parallel", "arbitrary"),
    )(xbc, z, dt, cw, cb, alog, dskip, ng)


def _outproj_core(a_ref, s_ref, x_ref, w_ref, gm_ref, g_ref, sc_ref, sh_ref):
    half = a_ref.shape[1]
    mixed = (jnp.dot(a_ref[...], w_ref[0:half, :], preferred_element_type=F32)
             + jnp.dot(s_ref[...], w_ref[half:, :], preferred_element_type=F32))
    x_new = x_ref[...] + gm_ref[0] * mixed
    return x_new, _rms_mod(x_new, g_ref[...], sc_ref[0], sh_ref[0])


def _outproj_kernel(a_ref, s_ref, x_ref, w_ref, gm_ref, g_ref, sc_ref, sh_ref, xo_ref, h_ref):
    x_new, h = _outproj_core(a_ref, s_ref, x_ref, w_ref, gm_ref, g_ref, sc_ref, sh_ref)
    xo_ref[...] = x_new
    h_ref[...] = h.astype(BF16)


def _outproj_router_kernel(a_ref, s_ref, x_ref, w_ref, gm_ref, g_ref, sc_ref, sh_ref, rw_hi_ref, rw_lo_ref,
                           xo_ref, h_ref, route_ref, gate_ref, cnt_ref, carry_ref):
    x_new, h = _outproj_core(a_ref, s_ref, x_ref, w_ref, gm_ref, g_ref, sc_ref, sh_ref)
    xo_ref[...] = x_new
    tm = h.shape[0]
    for s in range(h.shape[1] // LANES):
        h_ref[pl.ds(s, tm, stride=SUBLANES), :] = h[:, s * LANES:(s + 1) * LANES]

    h_hi = h.astype(BF16)
    h_lo = (h - h_hi.astype(F32)).astype(BF16)
    logits = (jnp.dot(h_hi, rw_hi_ref[...], preferred_element_type=F32)
              + jnp.dot(h_hi, rw_lo_ref[...], preferred_element_type=F32)
              + jnp.dot(h_lo, rw_hi_ref[...], preferred_element_type=F32))
    lane = lax.broadcasted_iota(jnp.int32, (tm, LANES), 1).astype(F32)
    logits = jnp.where(lane < N_EXPERTS, logits, -jnp.inf)
    m0 = jnp.max(logits, axis=-1, keepdims=True)
    i0 = jnp.min(jnp.where(logits == m0, lane, float(LANES)), axis=-1, keepdims=True)
    rest = jnp.where(lane == i0, -jnp.inf, logits)
    m1 = jnp.max(rest, axis=-1, keepdims=True)
    i1 = jnp.min(jnp.where(rest == m1, lane, float(LANES)), axis=-1, keepdims=True)
    e = jnp.exp(m1 - m0)
    g0 = 1.0 / (1.0 + e)
    g1 = e / (1.0 + e)

    @pl.when(pl.program_id(0) == 0)
    def _():
        carry_ref[...] = jnp.zeros(carry_ref.shape, F32)

    sel0 = lane == i0
    sel1 = lane == i1
    member = jnp.where(sel0 | sel1, 1.0, 0.0)
    r_i = lax.broadcasted_iota(jnp.int32, (tm, tm), 0)
    c_i = lax.broadcasted_iota(jnp.int32, (tm, tm), 1)
    strict = jnp.where(r_i > c_i, 1.0, 0.0).astype(BF16)
    rank = carry_ref[...] + jnp.dot(strict, member.astype(BF16), preferred_element_type=F32)
    r0 = jnp.sum(jnp.where(sel0, rank, 0.0), axis=-1, keepdims=True)
    r1 = jnp.sum(jnp.where(sel1, rank, 0.0), axis=-1, keepdims=True)
    carry_ref[...] = carry_ref[...] + jnp.sum(member, axis=0, keepdims=True)
    cnt_ref[...] = carry_ref[...]

    route = jnp.where(lane == 0.0, i0, jnp.where(lane == 1.0, i1, jnp.where(lane == 2.0, r0, r1)))
    route_ref[...] = route[:, 0:SUBLANES].astype(jnp.int32)
    gate_ref[...] = jnp.where(lane == 0.0, g0, g1)[:, 0:SUBLANES]


def _outproj(attn, ssm, x2, w, gm, g, sc, sh, tiles_per_batch, router=None):
    t, d = x2.shape
    tm = ROW_TILE
    row = lambda i: (i, 0)
    const = lambda i: (0, 0)
    per_b = lambda i: (i // tiles_per_batch, 0, 0)
    half = attn.shape[1]
    in_specs = [pl.BlockSpec((tm, half), row),
                pl.BlockSpec((tm, half), row),
                pl.BlockSpec((tm, d), row),
                pl.BlockSpec((2 * half, d), const),
                pl.BlockSpec((1, 1, d), per_b),
                pl.BlockSpec((1, d), const),
                pl.BlockSpec((1, 1, d), per_b),
                pl.BlockSpec((1, 1, d), per_b)]
    if router is None:
        return pl.pallas_call(
            _outproj_kernel,
            grid=(t // tm,),
            in_specs=in_specs,
            out_specs=[pl.BlockSpec((tm, d), row), pl.BlockSpec((tm, d), row)],
            out_shape=[jax.ShapeDtypeStruct((t, d), F32), jax.ShapeDtypeStruct((t, d), BF16)],
            compiler_params=_cparams("parallel"),
        )(attn, ssm, x2, w, gm, g, sc, sh)
    rw_hi, rw_lo = router
    return pl.pallas_call(
        _outproj_router_kernel,
        grid=(t // tm,),
        in_specs=in_specs + [pl.BlockSpec((d, LANES), const), pl.BlockSpec((d, LANES), const)],
        out_specs=[pl.BlockSpec((tm, d), row),
                   pl.BlockSpec((tm * SUBLANES, LANES), row),
                   pl.BlockSpec((tm, SUBLANES), row),
                   pl.BlockSpec((tm, SUBLANES), row),
                   pl.BlockSpec((1, LANES), const)],
        out_shape=[jax.ShapeDtypeStruct((t, d), F32),
                   jax.ShapeDtypeStruct((t * SUBLANES, LANES), F32),
                   jax.ShapeDtypeStruct((t, SUBLANES), jnp.int32),
                   jax.ShapeDtypeStruct((t, SUBLANES), F32),
                   jax.ShapeDtypeStruct((1, LANES), F32)],
        scratch_shapes=[pltpu.VMEM((1, LANES), F32)],
        compiler_params=_cparams("arbitrary"),
    )(attn, ssm, x2, w, gm, g, sc, sh, rw_hi, rw_lo)


def _ffn_kernel(h_ref, x_ref, wg_ref, wu_ref, wd_ref, gf_ref, o_ref, acc_ref):
    j = pl.program_id(1)
    h = h_ref[...]
    gate = jnp.dot(h, wg_ref[...], preferred_element_type=F32)
    up = jnp.dot(h, wu_ref[...], preferred_element_type=F32)
    part = jnp.dot((_silu(gate) * up).astype(BF16), wd_ref[...], preferred_element_type=F32)

    @pl.when(j == 0)
    def _():
        acc_ref[...] = part

    @pl.when(j > 0)
    def _():
        acc_ref[...] += part

    @pl.when(j == pl.num_programs(1) - 1)
    def _():
        o_ref[...] = x_ref[...] + gf_ref[0] * acc_ref[...]


def _ffn(h, x2, wg, wu, wd, gf, tiles_per_batch):
    t, d = x2.shape
    f = wg.shape[1]
    tm = ROW_TILE
    nf = 2
    tf = f // nf
    row = lambda i, j: (i, 0)
    return pl.pallas_call(
        _ffn_kernel,
        grid=(t // tm, nf),
        in_specs=[pl.BlockSpec((tm, d), row),
                  pl.BlockSpec((tm, d), row),
                  pl.BlockSpec((d, tf), lambda i, j: (0, j)),
                  pl.BlockSpec((d, tf), lambda i, j: (0, j)),
                  pl.BlockSpec((tf, d), lambda i, j: (j, 0)),
                  pl.BlockSpec((1, 1, d), lambda i, j: (i // tiles_per_batch, 0, 0))],
        out_specs=pl.BlockSpec((tm, d), row),
        out_shape=jax.ShapeDtypeStruct((t, d), F32),
        scratch_shapes=[pltpu.VMEM((tm, d), F32)],
        compiler_params=_cparams("parallel", "arbitrary"),
    )(h, x2, wg, wu, wd, gf)


def _row_copy(src, src_row, dst, dst_row, sem):
    return pltpu.make_async_copy(
        src.at[pl.ds(pl.multiple_of(src_row * SUBLANES, SUBLANES), SUBLANES)],
        dst.at[pl.ds(pl.multiple_of(dst_row * SUBLANES, SUBLANES), SUBLANES)], sem)


def _dispatch_kernel(zs_ref, ze_ref, nvalid_ref, dest_ref, h_ref, xb_hbm, zero_ref, sem, zsem):
    i = pl.program_id(0)
    n_tok = dest_ref.shape[0] // TOP_K
    blk_rows = zero_ref.shape[0]
    n_blocks = xb_hbm.shape[0] // blk_rows

    def zero_row(r):
        return pltpu.make_async_copy(
            zero_ref.at[pl.ds(0, SUBLANES)],
            xb_hbm.at[pl.ds(pl.multiple_of(r * SUBLANES, SUBLANES), SUBLANES)], zsem)

    def zero_block(b):
        return pltpu.make_async_copy(
            zero_ref, xb_hbm.at[pl.ds(pl.multiple_of(b * blk_rows, blk_rows), blk_rows)], zsem)

    @pl.when(i == 0)
    def _():
        zero_ref[...] = jnp.zeros(zero_ref.shape, F32)
        for start in (True, False):
            for e in range(N_EXPERTS):
                def rows_body(r, c):
                    zero_row(r).start() if start else zero_row(r).wait()
                    return c
                lax.fori_loop(zs_ref[e], ze_ref[e], rows_body, 0)

            def blocks_body(b, c):
                zero_block(b).start() if start else zero_block(b).wait()
                return c
            lax.fori_loop(nvalid_ref[0], n_blocks, blocks_body, 0)

    def issue(t, c):
        for k in range(TOP_K):
            _row_copy(h_ref, t, xb_hbm, dest_ref[t * TOP_K + k], sem).start()
        return c

    lax.fori_loop(0, n_tok, issue, 0)
    rows = n_tok * TOP_K * SUBLANES
    pltpu.make_async_copy(xb_hbm.at[pl.ds(0, rows)], xb_hbm.at[pl.ds(0, rows)], sem).wait()


def _dispatch(h_rows, dest, zs, ze, nvalid, n_blocks, rows):
    t = h_rows.shape[0] // SUBLANES
    td = min(DISPATCH_TOKENS, t)
    grid_spec = pltpu.PrefetchScalarGridSpec(
        num_scalar_prefetch=3,
        grid=(t // td,),
        in_specs=[pl.BlockSpec((td * TOP_K,), lambda i, zs, ze, nv: (i,), memory_space=pltpu.SMEM),
                  pl.BlockSpec((td * SUBLANES, LANES), lambda i, zs, ze, nv: (i, 0))],
        out_specs=pl.BlockSpec(memory_space=pl.ANY),
        scratch_shapes=[pltpu.VMEM((rows * SUBLANES, LANES), F32),
                        pltpu.SemaphoreType.DMA(()), pltpu.SemaphoreType.DMA(())],
    )
    return pl.pallas_call(
        _dispatch_kernel,
        grid_spec=grid_spec,
        out_shape=jax.ShapeDtypeStruct((n_blocks * rows * SUBLANES, LANES), F32),
        compiler_params=pltpu.CompilerParams(dimension_semantics=("arbitrary",), has_side_effects=True,
                                             vmem_limit_bytes=VMEM_LIMIT),
    )(zs, ze, nvalid, dest, h_rows)


def _moe_kernel(bexp_ref, nvalid_ref, xb_ref, wg_ref, wu_ref, wd_ref, y_ref, x_scr, acc_ref):
    b = pl.program_id(0)
    j = pl.program_id(1)
    nf = pl.num_programs(1)
    rows = x_scr.shape[0]
    nsl = x_scr.shape[1] // LANES

    @pl.when(b < nvalid_ref[0])
    def _():
        @pl.when(j == 0)
        def _():
            for s in range(nsl):
                x_scr[:, s * LANES:(s + 1) * LANES] = xb_ref[pl.ds(s, rows, stride=SUBLANES), :].astype(BF16)

        x = x_scr[...]
        gate = jnp.dot(x, wg_ref[0].astype(BF16), preferred_element_type=F32)
        up = jnp.dot(x, wu_ref[0].astype(BF16), preferred_element_type=F32)
        part = jnp.dot((_silu(gate) * up).astype(BF16), wd_ref[0].astype(BF16), preferred_element_type=F32)

        @pl.when(j == 0)
        def _():
            acc_ref[...] = part

        @pl.when(j > 0)
        def _():
            acc_ref[...] += part

        @pl.when(j == nf - 1)
        def _():
            for s in range(nsl):
                y_ref[pl.ds(s, rows, stride=SUBLANES), :] = acc_ref[:, s * LANES:(s + 1) * LANES]

    @pl.when((b >= nvalid_ref[0]) & (j == nf - 1))
    def _():
        y_ref[...] = jnp.zeros(y_ref.shape, F32)


def _moe_experts(xb, bexp, nvalid, wg, wu, wd, n_blocks, rows):
    d = wg.shape[1]
    f = wg.shape[2]
    nf = f // MOE_FT

    def blk(b, j, bexp, nvalid):
        return (jnp.minimum(b, nvalid[0] - 1), 0)

    def fidx(b, j, nvalid):
        return jnp.where(b < nvalid[0], j, nf - 1)

    grid_spec = pltpu.PrefetchScalarGridSpec(
        num_scalar_prefetch=2,
        grid=(n_blocks, nf),
        in_specs=[pl.BlockSpec((rows * SUBLANES, LANES), blk),
                  pl.BlockSpec((1, d, MOE_FT), lambda b, j, bexp, nvalid: (bexp[b], 0, fidx(b, j, nvalid))),
                  pl.BlockSpec((1, d, MOE_FT), lambda b, j, bexp, nvalid: (bexp[b], 0, fidx(b, j, nvalid))),
                  pl.BlockSpec((1, MOE_FT, d), lambda b, j, bexp, nvalid: (bexp[b], fidx(b, j, nvalid), 0))],
        out_specs=pl.BlockSpec((rows * SUBLANES, LANES), lambda b, j, bexp, nvalid: (b, 0)),
        scratch_shapes=[pltpu.VMEM((rows, d), BF16), pltpu.VMEM((rows, d), F32)],
    )
    return pl.pallas_call(
        _moe_kernel,
        grid_spec=grid_spec,
        out_shape=jax.ShapeDtypeStruct(xb.shape, F32),
        compiler_params=_cparams("arbitrary", "arbitrary"),
    )(bexp, nvalid, xb, wg, wu, wd)


def _combine_kernel(dest_ref, y_hbm, x_ref, gate_ref, gf_ref, fg_ref, o_ref, buf_ref, sem):
    tm = x_ref.shape[0]
    nsl = x_ref.shape[1] // LANES

    def issue(t, c):
        for k in range(TOP_K):
            _row_copy(y_hbm, dest_ref[t * TOP_K + k], buf_ref, k * tm + t, sem).start()
        return c

    lax.fori_loop(0, tm, issue, 0)
    rows = tm * TOP_K * SUBLANES
    pltpu.make_async_copy(y_hbm.at[pl.ds(0, rows)], buf_ref, sem).wait()

    g0 = gate_ref[:, 0:1]
    g1 = gate_ref[:, 1:2]
    parts = []
    for s in range(nsl):
        y0 = buf_ref[pl.ds(s, tm, stride=SUBLANES), :]
        y1 = buf_ref[pl.ds(tm * SUBLANES + s, tm, stride=SUBLANES), :]
        parts.append(g0 * y0 + g1 * y1)
    x = x_ref[...] + gf_ref[0] * jnp.concatenate(parts, axis=1)
    ms = jnp.mean(x * x, axis=-1, keepdims=True)
    o_ref[...] = x * lax.rsqrt(ms + EPS) * fg_ref[...]


def _combine(y, dest, x2, gates, gf, fg, tiles_per_batch):
    t, d = x2.shape
    tm = COMBINE_TOKENS
    return pl.pallas_call(
        _combine_kernel,
        grid=(t // tm,),
        in_specs=[pl.BlockSpec((tm * TOP_K,), lambda i: (i,), memory_space=pltpu.SMEM),
                  pl.BlockSpec(memory_space=pl.ANY),
                  pl.BlockSpec((tm, d), lambda i: (i, 0)),
                  pl.BlockSpec((tm, SUBLANES), lambda i: (i, 0)),
                  pl.BlockSpec((1, 1, d), lambda i: (i // tiles_per_batch, 0, 0)),
                  pl.BlockSpec((1, d), lambda i: (0, 0))],
        out_specs=pl.BlockSpec((tm, d), lambda i: (i, 0)),
        out_shape=jax.ShapeDtypeStruct((t, d), F32),
        scratch_shapes=[pltpu.VMEM((tm * TOP_K * SUBLANES, LANES), F32), pltpu.SemaphoreType.DMA(())],
        compiler_params=_cparams("arbitrary"),
    )(dest, y, x2, gates, gf, fg)


def _final_norm_kernel(x_ref, g_ref, o_ref):
    x = x_ref[...]
    o_ref[...] = x * lax.rsqrt(jnp.mean(x * x, axis=-1, keepdims=True) + EPS) * g_ref[...]


def _final_norm(x2, g):
    t, d = x2.shape
    tm = ROW_TILE
    return pl.pallas_call(
        _final_norm_kernel,
        grid=(t // tm,),
        in_specs=[pl.BlockSpec((tm, d), lambda i: (i, 0)), pl.BlockSpec((1, d), lambda i: (0, 0))],
        out_specs=pl.BlockSpec((tm, d), lambda i: (i, 0)),
        out_shape=jax.ShapeDtypeStruct((t, d), F32),
        compiler_params=_cparams("parallel"),
    )(x2, g)


def _qk_column_perm():
    half = HEAD_DIM // 2
    lane = np.arange(LANES)
    pair = (lane % HEAD_DIM) // half
    dim = lane % half + half * (lane // HEAD_DIM)
    q = np.concatenate([(i + Q_TILES * pair) * HEAD_DIM + dim for i in range(Q_TILES)])
    k = ATTN_WIDTH + pair * HEAD_DIM + dim
    return q, k


def _attn_out_row_perm():
    lane = np.arange(LANES)
    return np.concatenate([(i + Q_TILES * (lane // HEAD_DIM)) * HEAD_DIM + lane % HEAD_DIM
                           for i in range(Q_TILES)])


def _pad_lanes(v):
    return jnp.pad(v.astype(F32), (0, LANES - v.shape[0])).reshape(1, LANES)


def _moe_route_tables(route, counts, rows, n_blocks):
    e0, e1, r0, r1 = route[:, 0], route[:, 1], route[:, 2], route[:, 3]
    cnt = counts[0, :N_EXPERTS].astype(jnp.int32)
    padded = (cnt + rows - 1) // rows * rows
    pad_end = jnp.cumsum(padded)
    pad_start = pad_end - padded
    dest = jnp.stack([pad_start[e0] + r0, pad_start[e1] + r1], axis=1).reshape(-1)
    bexp = jnp.minimum(jnp.searchsorted(pad_end, jnp.arange(n_blocks, dtype=jnp.int32) * rows, side='right'),
                       N_EXPERTS - 1).astype(jnp.int32)
    nvalid = (pad_end[-1:] // rows).astype(jnp.int32)
    return dest.astype(jnp.int32), bexp, nvalid, (pad_start + cnt).astype(jnp.int32), pad_end.astype(jnp.int32)


def kernel(x, c, positions, ada_w, ada_b, norm_mix_g, norm_ffn_g, w_in, w_out, attn_sinks, conv_w, conv_b,
           dt_bias, a_log, d_skip, ssm_norm_g, ffn_w_gate, ffn_w_up, ffn_w_down, router_w, moe_w_gate,
           moe_w_up, moe_w_down, final_norm_g):
    batch, seq, d = x.shape
    depth = w_in.shape[0]
    t = batch * seq
    tiles_per_batch = seq // ROW_TILE
    x2 = x.reshape(t, d)

    c8 = jnp.pad(c, ((0, SUBLANES - batch), (0, 0)))
    mod = _ada_mod(c8, ada_w, ada_b)[:, :batch].reshape(depth, batch, 6, 1, d)
    cos, sin = _rope_tables(positions)

    q_perm, k_perm = _qk_column_perm()
    o_perm = _attn_out_row_perm()
    for l in range(depth):
        sh_m, sc_m, g_m, sh_f, sc_f, g_f = (mod[l, :, k] for k in range(6))
        wl = w_in[l]
        w_cat = jnp.concatenate(
            [wl[:, q_perm], wl[:, k_perm], wl[:, C_V:C_DT],
             jnp.pad(wl[:, C_DT:], ((0, 0), (0, LANES - SSM_HEADS)))], axis=1).astype(BF16)
        q, kv, z, xbc, dt = _inproj(x2, norm_mix_g[l].reshape(1, d), sc_m, sh_m, cos, sin, w_cat,
                                    _pad_lanes(dt_bias[l]), tiles_per_batch)
        attn = _attention(q, kv, attn_sinks[l].astype(F32), seq)
        ssm = _ssd(xbc, z, dt, conv_w[l], conv_b[l].reshape(1, CONV_CH), _pad_lanes(a_log[l]),
                   jnp.repeat(d_skip[l].astype(F32), SSM_HEAD_DIM).reshape(1, SSM_WIDTH),
                   ssm_norm_g[l].reshape(1, SSM_WIDTH), batch, seq)
        wo = jnp.concatenate([w_out[l][o_perm], w_out[l][ATTN_WIDTH:]], axis=0).astype(BF16)
        ffn_g = norm_ffn_g[l].reshape(1, d)
        if l % 2 == 0:
            x_new, h = _outproj(attn, ssm, x2, wo, g_m, ffn_g, sc_f, sh_f, tiles_per_batch)
            x2 = _ffn(h, x_new, ffn_w_gate[l // 2].astype(BF16), ffn_w_up[l // 2].astype(BF16),
                      ffn_w_down[l // 2].astype(BF16), g_f, tiles_per_batch)
            if l == depth - 1:
                x2 = _final_norm(x2, final_norm_g.reshape(1, d))
        else:
            if l != depth - 1:
                raise NotImplementedError("the expert layer fuses the final norm and must be last")
            rw = jnp.pad(router_w[l // 2].astype(F32), ((0, 0), (0, LANES - N_EXPERTS)))
            rw_hi = rw.astype(BF16)
            rw_lo = (rw - rw_hi.astype(F32)).astype(BF16)
            x_new, h_rows, route, gates, counts = _outproj(
                attn, ssm, x2, wo, g_m, ffn_g, sc_f, sh_f, tiles_per_batch, router=(rw_hi, rw_lo))
            rows = min(MOE_ROWS, t)
            n_blocks = (t * TOP_K) // rows + N_EXPERTS
            dest, bexp, nvalid, zs, ze = _moe_route_tables(route, counts, rows, n_blocks)
            xb = _dispatch(h_rows, dest, zs, ze, nvalid, n_blocks, rows)
            y = _moe_experts(xb, bexp, nvalid, moe_w_gate[l // 2], moe_w_up[l // 2], moe_w_down[l // 2],
                             n_blocks, rows)
            x2 = _combine(y, dest, x_new, gates, g_f, final_norm_g.reshape(1, d),
                          seq // COMBINE_TOKENS)
    return x2.reshape(batch, seq, d)
```

```python
import functools

import numpy as np
import jax
import jax.numpy as jnp
from jax import lax
from jax.experimental import pallas as pl
from jax.experimental.pallas import tpu as pltpu

F32 = jnp.float32
BF16 = jnp.bfloat16

LANES = 128
SUBLANES = 8
VMEM_LIMIT = 48 * 1024 * 1024

EPS = 1e-6
HEAD_DIM = 64
Q_HEADS = 8
KV_HEADS = 2
GROUP = Q_HEADS // KV_HEADS
ATTN_WIDTH = Q_HEADS * HEAD_DIM
KV_WIDTH = KV_HEADS * HEAD_DIM
WINDOW = 128
ROPE_THETA = 10000.0
SSM_HEADS = 8
SSM_HEAD_DIM = 64
SSM_WIDTH = SSM_HEADS * SSM_HEAD_DIM
SSM_GROUPS = 2
SSM_STATE = 128
GROUP_WIDTH = SSM_WIDTH // SSM_GROUPS
CONV_WIDTH = 4
CONV_CH = SSM_WIDTH + 2 * SSM_GROUPS * SSM_STATE
CHUNK = 128
N_EXPERTS = 8
TOP_K = 2

Q_TILES = ATTN_WIDTH // LANES
C_Q = 0
C_K = C_Q + ATTN_WIDTH
C_V = C_K + KV_WIDTH
C_Z = C_V + KV_WIDTH
C_X = C_Z + SSM_WIDTH
C_DT = C_X + CONV_CH
C_END = C_DT + LANES

ROW_TILE = 512
ATTN_TILE = 512
MOE_ROWS = 1024
MOE_FT = 512
DISPATCH_TOKENS = 1024
COMBINE_TOKENS = 256
SSD_CHUNKS_PER_STEP = 2
ROW_SPLIT = 2


def _cparams(*sem):
    return pltpu.CompilerParams(dimension_semantics=sem, vmem_limit_bytes=VMEM_LIMIT)


def _silu(v):
    return v * (1.0 / (1.0 + jnp.exp(-v)))


def _softplus(v):
    return jnp.maximum(v, 0.0) + jnp.log1p(jnp.exp(-jnp.abs(v)))


def _rms_mod(x, g, scale, shift):
    ms = jnp.mean(x * x, axis=-1, keepdims=True)
    return (x * lax.rsqrt(ms + EPS) * g) * (1.0 + scale) + shift


def _ada_kernel(c_ref, w_ref, b_ref, o_ref):
    c = c_ref[...]
    o_ref[0] = jnp.dot(_silu(c), w_ref[0], preferred_element_type=F32,
                       precision=lax.Precision.HIGHEST) + b_ref[0]


def _ada_mod(c8, ada_w, ada_b):
    depth, d, n = ada_w.shape
    tn = 1536
    return pl.pallas_call(
        _ada_kernel,
        grid=(depth, n // tn),
        in_specs=[pl.BlockSpec((SUBLANES, d), lambda l, j: (0, 0)),
                  pl.BlockSpec((1, d, tn), lambda l, j: (l, 0, j)),
                  pl.BlockSpec((1, 1, tn), lambda l, j: (l, 0, j))],
        out_specs=pl.BlockSpec((1, SUBLANES, tn), lambda l, j: (l, 0, j)),
        out_shape=jax.ShapeDtypeStruct((depth, SUBLANES, n), F32),
        compiler_params=_cparams("parallel", "parallel"),
    )(c8, ada_w, ada_b.reshape(depth, 1, n))


def _rope_kernel(pos_ref, inv_ref, cos_ref, sin_ref):
    ang = pos_ref[...].astype(F32) * inv_ref[...]
    cos_ref[...] = jnp.cos(ang)
    sin_ref[...] = jnp.sin(ang)


def _rope_tables(positions):
    t = positions.size
    per_row = LANES // (HEAD_DIM // 2)
    pos_dense = jnp.repeat(positions.reshape(t // per_row, per_row), HEAD_DIM // 2, axis=1)
    inv_freq = ROPE_THETA ** (-jnp.arange(0, HEAD_DIM, 2, dtype=F32) / HEAD_DIM)
    inv_dense = jnp.tile(inv_freq, per_row).reshape(1, LANES)
    rows = t // per_row
    tr = min(1024, rows)
    cos_d, sin_d = pl.pallas_call(
        _rope_kernel,
        grid=(rows // tr,),
        in_specs=[pl.BlockSpec((tr, LANES), lambda i: (i, 0)),
                  pl.BlockSpec((1, LANES), lambda i: (0, 0))],
        out_specs=[pl.BlockSpec((tr, LANES), lambda i: (i, 0))] * 2,
        out_shape=[jax.ShapeDtypeStruct((rows, LANES), F32)] * 2,
        compiler_params=_cparams("parallel"),
    )(pos_dense, inv_dense)
    cos = cos_d.reshape(t, HEAD_DIM // 2)
    sin = sin_d.reshape(t, HEAD_DIM // 2)
    return jnp.tile(cos, (1, 4)), jnp.concatenate([-sin, -sin, sin, sin], axis=1)


def _inproj_kernel(x_ref, g_ref, sc_ref, sh_ref, cos_ref, sin_ref, w_ref, dtb_ref,
                   q_ref, kv_ref, z_ref, xbc_ref, dt_ref):
    h = _rms_mod(x_ref[...], g_ref[...], sc_ref[0], sh_ref[0]).astype(BF16)
    cos = cos_ref[...]
    sin = sin_ref[...]

    def rope(t):
        return t * cos + pltpu.roll(t, LANES // 2, axis=1) * sin

    qkv = jnp.dot(h, w_ref[:, C_Q:C_Z], preferred_element_type=F32)
    for i in range(Q_TILES):
        q_ref[:, i * LANES:(i + 1) * LANES] = (
            rope(qkv[:, i * LANES:(i + 1) * LANES]) * (HEAD_DIM ** -0.5)).astype(BF16)
    kv_ref[:, 0:LANES] = rope(qkv[:, C_K:C_V]).astype(BF16)
    kv_ref[:, LANES:2 * LANES] = qkv[:, C_V:C_Z].astype(BF16)
    z_ref[...] = jnp.dot(h, w_ref[:, C_Z:C_X], preferred_element_type=F32)
    xbc_ref[...] = jnp.dot(h, w_ref[:, C_X:C_DT], preferred_element_type=F32)
    dt_raw = jnp.dot(h, w_ref[:, C_DT:C_END], preferred_element_type=F32)
    dt_ref[...] = _softplus(dt_raw + dtb_ref[...])


def _inproj(x2, g, sc, sh, cos, sin, w, dtb, tiles_per_batch):
    t, d = x2.shape
    tm = ROW_TILE
    row = lambda i: (i, 0)
    const = lambda i: (0, 0)
    per_b = lambda i: (i // tiles_per_batch, 0, 0)
    return pl.pallas_call(
        _inproj_kernel,
        grid=(t // tm,),
        in_specs=[pl.BlockSpec((tm, d), row),
                  pl.BlockSpec((1, d), const),
                  pl.BlockSpec((1, 1, d), per_b),
                  pl.BlockSpec((1, 1, d), per_b),
                  pl.BlockSpec((tm, LANES), row),
                  pl.BlockSpec((tm, LANES), row),
                  pl.BlockSpec((d, C_END), const),
                  pl.BlockSpec((1, LANES), const)],
        out_specs=[pl.BlockSpec((tm, ATTN_WIDTH), row),
                   pl.BlockSpec((tm, 2 * KV_WIDTH), row),
                   pl.BlockSpec((tm, SSM_WIDTH), row),
                   pl.BlockSpec((tm, CONV_CH), row),
                   pl.BlockSpec((tm, LANES), row)],
        out_shape=[jax.ShapeDtypeStruct((t, ATTN_WIDTH), BF16),
                   jax.ShapeDtypeStruct((t, 2 * KV_WIDTH), BF16),
                   jax.ShapeDtypeStruct((t, SSM_WIDTH), F32),
                   jax.ShapeDtypeStruct((t, CONV_CH), F32),
                   jax.ShapeDtypeStruct((t, LANES), F32)],
        compiler_params=_cparams("parallel"),
    )(x2, g, sc, sh, cos, sin, w, dtb)


def _attn_kernel(sink_ref, q_ref, kv_ref, kvp_ref, o_ref, *, tiles_per_seq):
    first = (pl.program_id(0) % tiles_per_seq) == 0
    blk = WINDOW
    nsub = ATTN_TILE // blk
    lane = lax.broadcasted_iota(jnp.int32, (1, LANES), 1)
    k_lo_mask = (lane % HEAD_DIM) < (HEAD_DIM // 2)
    v_lo_mask = lane < HEAD_DIM
    zero = jnp.zeros((), BF16)

    k_all = jnp.concatenate([kvp_ref[:, 0:LANES], kv_ref[:, 0:LANES]], axis=0)
    v_all = jnp.concatenate([kvp_ref[:, LANES:2 * LANES], kv_ref[:, LANES:2 * LANES]], axis=0)
    k_sel = (jnp.where(k_lo_mask, k_all, zero), jnp.where(k_lo_mask, zero, k_all))
    v_sel = (jnp.where(v_lo_mask, v_all, zero), jnp.where(v_lo_mask, zero, v_all))

    qi = lax.broadcasted_iota(jnp.int32, (blk, blk), 0)
    col = lax.broadcasted_iota(jnp.int32, (blk, blk), 1)
    cur = col <= qi
    no_prev = (col > qi) & (col > jnp.where(first, -1, blk))

    for n in range(nsub):
        q_st = jnp.concatenate([q_ref[n * blk:(n + 1) * blk, i * LANES:(i + 1) * LANES]
                                for i in range(Q_TILES)], axis=0)
        out = None
        for hk in range(KV_HEADS):
            k_n = k_sel[hk][n * blk:(n + 2) * blk]
            v_n = v_sel[hk][n * blk:(n + 2) * blk]
            s = lax.dot_general(q_st, k_n, (((1,), (1,)), ((), ())), preferred_element_type=F32)
            probs, scales = [], []
            for i in range(GROUP):
                s_i = s[i * blk:(i + 1) * blk]
                sc = jnp.where(cur, s_i[:, blk:], s_i[:, :blk])
                if n == 0:
                    sc = jnp.where(no_prev, -jnp.inf, sc)
                sink = sink_ref[hk * GROUP + i]
                m = jnp.maximum(jnp.max(sc, axis=-1, keepdims=True), sink)
                p = jnp.exp(sc - m)
                scales.append(1.0 / (jnp.sum(p, axis=-1, keepdims=True) + jnp.exp(sink - m)))
                probs.append(jnp.concatenate([jnp.where(cur, 0.0, p), jnp.where(cur, p, 0.0)],
                                             axis=1).astype(BF16))
            o = jnp.dot(jnp.concatenate(probs, axis=0), v_n, preferred_element_type=F32)
            o = o * jnp.concatenate(scales, axis=0)
            out = o if out is None else out + o
        for i in range(Q_TILES):
            o_ref[n * blk:(n + 1) * blk, i * LANES:(i + 1) * LANES] = out[i * blk:(i + 1) * blk].astype(BF16)


def _attention(q, kv, sinks, seq):
    t = q.shape[0]
    tq = ATTN_TILE
    r = tq // WINDOW
    kern = functools.partial(_attn_kernel, tiles_per_seq=seq // tq)
    return pl.pallas_call(
        kern,
        grid=(t // tq,),
        in_specs=[pl.BlockSpec(memory_space=pltpu.SMEM),
                  pl.BlockSpec((tq, ATTN_WIDTH), lambda i: (i, 0)),
                  pl.BlockSpec((tq, 2 * KV_WIDTH), lambda i: (i, 0)),
                  pl.BlockSpec((WINDOW, 2 * KV_WIDTH), lambda i: (jnp.maximum(i * r - 1, 0), 0))],
        out_specs=pl.BlockSpec((tq, ATTN_WIDTH), lambda i: (i, 0)),
        out_shape=jax.ShapeDtypeStruct((t, ATTN_WIDTH), BF16),
        compiler_params=_cparams("parallel"),
    )(sinks, q, kv, kv)


def _split3(v):
    p0 = v.astype(BF16)
    r1 = v - p0.astype(F32)
    p1 = r1.astype(BF16)
    return p0, p1, (r1 - p1.astype(F32)).astype(BF16)


def _dot3(lhs_exact_bf16, v):
    return sum(jnp.dot(lhs_exact_bf16, p, preferred_element_type=F32) for p in _split3(v))


def _ssd_kernel(xbc_ref, z_ref, dt_ref, cw_ref, cb_ref, alog_ref, aloge_ref, dskip_ref, ng_ref,
                o_ref, ext_ref, st_ref):
    L = CHUNK
    rows = xbc_ref.shape[0]
    halo = SUBLANES

    @pl.when(pl.program_id(1) == 0)
    def _():
        ext_ref[0:halo, :] = jnp.zeros((halo, CONV_CH), F32)
        st_ref[...] = jnp.zeros(st_ref.shape, F32)

    ext_ref[halo:halo + rows, :] = xbc_ref[...]
    acc = cb_ref[...] + cw_ref[CONV_WIDTH - 1:CONV_WIDTH, :] * ext_ref[halo:halo + rows, :]
    for k in range(CONV_WIDTH - 1):
        off = halo - (CONV_WIDTH - 1) + k
        acc = acc + cw_ref[k:k + 1, :] * ext_ref[off:off + rows, :]
    ext_ref[0:halo, :] = ext_ref[rows:rows + halo, :]
    u_all = _silu(acc)

    lane = lax.broadcasted_iota(jnp.int32, (1, LANES), 1)
    a = jnp.where(lane < SSM_HEADS, -jnp.exp(alog_ref[...]), 0.0)
    a_e = -jnp.exp(aloge_ref[...])
    row = lax.broadcasted_iota(jnp.int32, (L, L), 0)
    col = lax.broadcasted_iota(jnp.int32, (L, L), 1)
    causal = row >= col
    tri = jnp.where(causal, 1.0, 0.0).astype(BF16)
    spread = jnp.where(lax.broadcasted_iota(jnp.int32, (LANES, SSM_WIDTH), 1) // SSM_HEAD_DIM
                       == lax.broadcasted_iota(jnp.int32, (LANES, SSM_WIDTH), 0), 1.0, 0.0).astype(BF16)
    r_heads = SSM_HEADS // SSM_GROUPS
    glane = lax.broadcasted_iota(jnp.int32, (1, GROUP_WIDTH), 1) // SSM_HEAD_DIM
    zero = jnp.zeros((), BF16)
    states = [st_ref[g] for g in range(SSM_GROUPS)]

    for c in range(rows // L):
        rs = slice(c * L, (c + 1) * L)
        u = u_all[rs]
        xs = u[:, 0:SSM_WIDTH]
        bm = u[:, SSM_WIDTH:SSM_WIDTH + SSM_GROUPS * SSM_STATE]
        cm = u[:, SSM_WIDTH + SSM_GROUPS * SSM_STATE:]
        dt = dt_ref[rs, :]
        acs = _dot3(tri, dt * a)
        acs_t = acs.T
        dt_e = sum(jnp.dot(p, spread, preferred_element_type=F32) for p in _split3(dt))
        acs_e = _dot3(tri, dt_e * a_e)
        last = acs_e[L - 1:L, :]
        xd = xs * dt_e
        xd_b = xd.astype(BF16)
        xdw_b = (xd * jnp.exp(last - acs_e)).astype(BF16)
        e_acs = jnp.exp(acs_e)
        c_dec = jnp.exp(last)

        ys = []
        for g in range(SSM_GROUPS):
            gs = slice(g * GROUP_WIDTH, (g + 1) * GROUP_WIDTH)
            b_g = bm[:, g * SSM_STATE:(g + 1) * SSM_STATE]
            c_b = cm[:, g * SSM_STATE:(g + 1) * SSM_STATE].astype(BF16)
            cb = lax.dot_general(c_b, b_g.astype(BF16), (((1,), (1,)), ((), ())),
                                 preferred_element_type=F32)
            st = states[g]
            y_g = jnp.dot(c_b, st.astype(BF16), preferred_element_type=F32) * e_acs[:, gs]
            xd_g = xd_b[:, gs]
            for r in range(r_heads):
                h = g * r_heads + r
                seg = acs[:, h:h + 1] - acs_t[h:h + 1, :]
                m_h = (cb * jnp.exp(jnp.where(causal, seg, -jnp.inf))).astype(BF16)
                y_g = y_g + jnp.dot(m_h, jnp.where(glane == r, xd_g, zero), preferred_element_type=F32)
            new = jnp.dot(b_g.T.astype(BF16), xdw_b[:, gs], preferred_element_type=F32)
            states[g] = c_dec[:, gs] * st + new
            ys.append(y_g)

        y = jnp.concatenate(ys, axis=1) + dskip_ref[...] * xs
        y = y * _silu(z_ref[rs, :])
        outs = []
        for g in range(SSM_GROUPS):
            yg = y[:, g * GROUP_WIDTH:(g + 1) * GROUP_WIDTH]
            outs.append(yg * lax.rsqrt(jnp.mean(yg * yg, axis=-1, keepdims=True) + EPS))
        o_ref[rs, :] = (jnp.concatenate(outs, axis=1) * ng_ref[...]).astype(BF16)

    for g in range(SSM_GROUPS):
        st_ref[g] = states[g]


def _ssd(xbc, z, dt, cw, cb, alog, aloge, dskip, ng, batch, seq):
    t = xbc.shape[0]
    rows = SSD_CHUNKS_PER_STEP * CHUNK
    ns = seq // rows
    row = lambda b, c: (b * ns + c, 0)
    const = lambda b, c: (0, 0)
    return pl.pallas_call(
        _ssd_kernel,
        grid=(batch, ns),
        in_specs=[pl.BlockSpec((rows, CONV_CH), row),
                  pl.BlockSpec((rows, SSM_WIDTH), row),
                  pl.BlockSpec((rows, LANES), row),
                  pl.BlockSpec((CONV_WIDTH, CONV_CH), const),
                  pl.BlockSpec((1, CONV_CH), const),
                  pl.BlockSpec((1, LANES), const),
                  pl.BlockSpec((1, SSM_WIDTH), const),
                  pl.BlockSpec((1, SSM_WIDTH), const),
                  pl.BlockSpec((1, SSM_WIDTH), const)],
        out_specs=pl.BlockSpec((rows, SSM_WIDTH), row),
        out_shape=jax.ShapeDtypeStruct((t, SSM_WIDTH), BF16),
        scratch_shapes=[pltpu.VMEM((SUBLANES + rows, CONV_CH), F32),
                        pltpu.VMEM((SSM_GROUPS, SSM_STATE, GROUP_WIDTH), F32)],
        compiler_params=_cparams("parallel", "arbitrary"),
    )(xbc, z, dt, cw, cb, alog, aloge, dskip, ng)


def _outproj_core(a_ref, s_ref, x_ref, w_ref, gm_ref, g_ref, sc_ref, sh_ref):
    half = a_ref.shape[1]
    mixed = (jnp.dot(a_ref[...], w_ref[0:half, :], preferred_element_type=F32)
             + jnp.dot(s_ref[...], w_ref[half:, :], preferred_element_type=F32))
    x_new = x_ref[...] + gm_ref[0] * mixed
    return x_new, _rms_mod(x_new, g_ref[...], sc_ref[0], sh_ref[0])


def _outproj_kernel(a_ref, s_ref, x_ref, w_ref, gm_ref, g_ref, sc_ref, sh_ref, xo_ref, h_ref):
    x_new, h = _outproj_core(a_ref, s_ref, x_ref, w_ref, gm_ref, g_ref, sc_ref, sh_ref)
    xo_ref[...] = x_new
    h_ref[...] = h.astype(BF16)


def _outproj_router_kernel(a_ref, s_ref, x_ref, w_ref, gm_ref, g_ref, sc_ref, sh_ref, rw_ref,
                           xo_ref, h_ref, route_ref, gate_ref, cnt_ref, carry_ref):
    x_new, h = _outproj_core(a_ref, s_ref, x_ref, w_ref, gm_ref, g_ref, sc_ref, sh_ref)
    xo_ref[...] = x_new
    tm = h.shape[0]
    for s in range(h.shape[1] // LANES):
        h_ref[pl.ds(s, tm, stride=SUBLANES), :] = h[:, s * LANES:(s + 1) * LANES]

    h_hi = h.astype(BF16)
    h_lo = (h - h_hi.astype(F32)).astype(BF16)
    hi_part = jnp.dot(h_hi, rw_ref[...], preferred_element_type=F32)
    logits = (hi_part[:, 0:LANES] + hi_part[:, LANES:]
              + jnp.dot(h_lo, rw_ref[:, 0:LANES], preferred_element_type=F32))
    lane = lax.broadcasted_iota(jnp.int32, (tm, LANES), 1).astype(F32)
    logits = jnp.where(lane < N_EXPERTS, logits, -jnp.inf)
    m0 = jnp.max(logits, axis=-1, keepdims=True)
    i0 = jnp.min(jnp.where(logits == m0, lane, float(LANES)), axis=-1, keepdims=True)
    rest = jnp.where(lane == i0, -jnp.inf, logits)
    m1 = jnp.max(rest, axis=-1, keepdims=True)
    i1 = jnp.min(jnp.where(rest == m1, lane, float(LANES)), axis=-1, keepdims=True)
    e = jnp.exp(m1 - m0)
    g0 = 1.0 / (1.0 + e)
    g1 = e / (1.0 + e)

    @pl.when(pl.program_id(0) == 0)
    def _():
        carry_ref[...] = jnp.zeros(carry_ref.shape, F32)

    sel0 = lane == i0
    sel1 = lane == i1
    member = jnp.where(sel0 | sel1, 1.0, 0.0)
    r_i = lax.broadcasted_iota(jnp.int32, (tm, tm), 0)
    c_i = lax.broadcasted_iota(jnp.int32, (tm, tm), 1)
    strict = jnp.where(r_i > c_i, 1.0, 0.0).astype(BF16)
    rank = carry_ref[...] + jnp.dot(strict, member.astype(BF16), preferred_element_type=F32)
    r0 = jnp.sum(jnp.where(sel0, rank, 0.0), axis=-1, keepdims=True)
    r1 = jnp.sum(jnp.where(sel1, rank, 0.0), axis=-1, keepdims=True)
    carry_ref[...] = carry_ref[...] + jnp.sum(member, axis=0, keepdims=True)
    cnt_ref[...] = carry_ref[...]

    route = jnp.where(lane == 0.0, i0, jnp.where(lane == 1.0, i1, jnp.where(lane == 2.0, r0, r1)))
    route_ref[...] = route[:, 0:SUBLANES].astype(jnp.int32)
    gate_ref[...] = jnp.where(lane == 0.0, g0, g1)[:, 0:SUBLANES]


def _outproj(attn, ssm, x2, w, gm, g, sc, sh, tiles_per_batch, router=None):
    t, d = x2.shape
    tm = ROW_TILE
    row = lambda i: (i, 0)
    const = lambda i: (0, 0)
    per_b = lambda i: (i // tiles_per_batch, 0, 0)
    half = attn.shape[1]
    in_specs = [pl.BlockSpec((tm, half), row),
                pl.BlockSpec((tm, half), row),
                pl.BlockSpec((tm, d), row),
                pl.BlockSpec((2 * half, d), const),
                pl.BlockSpec((1, 1, d), per_b),
                pl.BlockSpec((1, d), const),
                pl.BlockSpec((1, 1, d), per_b),
                pl.BlockSpec((1, 1, d), per_b)]
    if router is None:
        return pl.pallas_call(
            _outproj_kernel,
            grid=(t // tm,),
            in_specs=in_specs,
            out_specs=[pl.BlockSpec((tm, d), row), pl.BlockSpec((tm, d), row)],
            out_shape=[jax.ShapeDtypeStruct((t, d), F32), jax.ShapeDtypeStruct((t, d), BF16)],
            compiler_params=_cparams("parallel"),
        )(attn, ssm, x2, w, gm, g, sc, sh)
    return pl.pallas_call(
        _outproj_router_kernel,
        grid=(t // tm,),
        in_specs=in_specs + [pl.BlockSpec((d, 2 * LANES), const)],
        out_specs=[pl.BlockSpec((tm, d), row),
                   pl.BlockSpec((tm * SUBLANES, LANES), row),
                   pl.BlockSpec((tm, SUBLANES), row),
                   pl.BlockSpec((tm, SUBLANES), row),
                   pl.BlockSpec((1, LANES), const)],
        out_shape=[jax.ShapeDtypeStruct((t, d), F32),
                   jax.ShapeDtypeStruct((t * SUBLANES, LANES), F32),
                   jax.ShapeDtypeStruct((t, SUBLANES), jnp.int32),
                   jax.ShapeDtypeStruct((t, SUBLANES), F32),
                   jax.ShapeDtypeStruct((1, LANES), F32)],
        scratch_shapes=[pltpu.VMEM((1, LANES), F32)],
        compiler_params=_cparams("arbitrary"),
    )(attn, ssm, x2, w, gm, g, sc, sh, router)


def _swiglu_rows(x, wg, wu, wd):
    gate = jnp.dot(x, wg, preferred_element_type=F32)
    up = jnp.dot(x, wu, preferred_element_type=F32)
    return jnp.dot((_silu(gate) * up).astype(BF16), wd, preferred_element_type=F32)


def _ffn_kernel(h_ref, x_ref, wg_ref, wu_ref, wd_ref, gf_ref, o_ref):
    piece = h_ref.shape[0] // ROW_SPLIT
    for r in range(ROW_SPLIT):
        rs = slice(r * piece, (r + 1) * piece)
        f = _swiglu_rows(h_ref[rs, :], wg_ref[...], wu_ref[...], wd_ref[...])
        o_ref[rs, :] = x_ref[rs, :] + gf_ref[0] * f


def _ffn(h, x2, wg, wu, wd, gf, tiles_per_batch):
    t, d = x2.shape
    f = wg.shape[1]
    tm = ROW_TILE
    row = lambda i: (i, 0)
    const = lambda i: (0, 0)
    resident = pl.Buffered(1)
    return pl.pallas_call(
        _ffn_kernel,
        grid=(t // tm,),
        in_specs=[pl.BlockSpec((tm, d), row),
                  pl.BlockSpec((tm, d), row),
                  pl.BlockSpec((d, f), const, pipeline_mode=resident),
                  pl.BlockSpec((d, f), const, pipeline_mode=resident),
                  pl.BlockSpec((f, d), const, pipeline_mode=resident),
                  pl.BlockSpec((1, 1, d), lambda i: (i // tiles_per_batch, 0, 0))],
        out_specs=pl.BlockSpec((tm, d), row),
        out_shape=jax.ShapeDtypeStruct((t, d), F32),
        compiler_params=_cparams("parallel"),
    )(h, x2, wg, wu, wd, gf)


def _row_copy(src, src_row, dst, dst_row, sem):
    return pltpu.make_async_copy(
        src.at[pl.ds(pl.multiple_of(src_row * SUBLANES, SUBLANES), SUBLANES)],
        dst.at[pl.ds(pl.multiple_of(dst_row * SUBLANES, SUBLANES), SUBLANES)], sem)


def _dispatch_kernel(zs_ref, ze_ref, nvalid_ref, dest_ref, h_ref, xb_hbm, zero_ref, sem, zsem):
    i = pl.program_id(0)
    n_tok = dest_ref.shape[0] // TOP_K
    blk_rows = zero_ref.shape[0]
    n_blocks = xb_hbm.shape[0] // blk_rows

    def zero_row(r):
        return pltpu.make_async_copy(
            zero_ref.at[pl.ds(0, SUBLANES)],
            xb_hbm.at[pl.ds(pl.multiple_of(r * SUBLANES, SUBLANES), SUBLANES)], zsem)

    def zero_block(b):
        return pltpu.make_async_copy(
            zero_ref, xb_hbm.at[pl.ds(pl.multiple_of(b * blk_rows, blk_rows), blk_rows)], zsem)

    @pl.when(i == 0)
    def _():
        zero_ref[...] = jnp.zeros(zero_ref.shape, F32)
        for start in (True, False):
            for e in range(N_EXPERTS):
                def rows_body(r, c):
                    zero_row(r).start() if start else zero_row(r).wait()
                    return c
                lax.fori_loop(zs_ref[e], ze_ref[e], rows_body, 0)

            def blocks_body(b, c):
                zero_block(b).start() if start else zero_block(b).wait()
                return c
            lax.fori_loop(nvalid_ref[0], n_blocks, blocks_body, 0)

    def issue(t, c):
        for k in range(TOP_K):
            _row_copy(h_ref, t, xb_hbm, dest_ref[t * TOP_K + k], sem).start()
        return c

    lax.fori_loop(0, n_tok, issue, 0)
    rows = n_tok * TOP_K * SUBLANES
    pltpu.make_async_copy(xb_hbm.at[pl.ds(0, rows)], xb_hbm.at[pl.ds(0, rows)], sem).wait()


def _dispatch(h_rows, dest, zs, ze, nvalid, n_blocks, rows):
    t = h_rows.shape[0] // SUBLANES
    td = min(DISPATCH_TOKENS, t)
    grid_spec = pltpu.PrefetchScalarGridSpec(
        num_scalar_prefetch=3,
        grid=(t // td,),
        in_specs=[pl.BlockSpec((td * TOP_K,), lambda i, zs, ze, nv: (i,), memory_space=pltpu.SMEM),
                  pl.BlockSpec((td * SUBLANES, LANES), lambda i, zs, ze, nv: (i, 0))],
        out_specs=pl.BlockSpec(memory_space=pl.ANY),
        scratch_shapes=[pltpu.VMEM((rows * SUBLANES, LANES), F32),
                        pltpu.SemaphoreType.DMA(()), pltpu.SemaphoreType.DMA(())],
    )
    return pl.pallas_call(
        _dispatch_kernel,
        grid_spec=grid_spec,
        out_shape=jax.ShapeDtypeStruct((n_blocks * rows * SUBLANES, LANES), F32),
        compiler_params=pltpu.CompilerParams(dimension_semantics=("arbitrary",), has_side_effects=True,
                                             vmem_limit_bytes=VMEM_LIMIT),
    )(zs, ze, nvalid, dest, h_rows)


def _moe_kernel(bexp_ref, nvalid_ref, xb_ref, wg_ref, wu_ref, wd_ref, y_ref, x_scr, acc_ref):
    b = pl.program_id(0)
    j = pl.program_id(1)
    nf = pl.num_programs(1)
    rows = x_scr.shape[0]
    nsl = x_scr.shape[1] // LANES
    valid = b < nvalid_ref[0]

    @pl.when(valid & (j == 0))
    def _():
        for s in range(nsl):
            x_scr[:, s * LANES:(s + 1) * LANES] = xb_ref[pl.ds(s, rows, stride=SUBLANES), :].astype(BF16)
        acc_ref[...] = jnp.zeros(acc_ref.shape, F32)

    @pl.when(valid)
    def _():
        wg = wg_ref[0].astype(BF16)
        wu = wu_ref[0].astype(BF16)
        wd = wd_ref[0].astype(BF16)
        piece = rows // ROW_SPLIT
        for r in range(ROW_SPLIT):
            rs = slice(r * piece, (r + 1) * piece)
            acc_ref[rs, :] += _swiglu_rows(x_scr[rs, :], wg, wu, wd)

    @pl.when(valid & (j == nf - 1))
    def _():
        for s in range(nsl):
            y_ref[pl.ds(s, rows, stride=SUBLANES), :] = acc_ref[:, s * LANES:(s + 1) * LANES]

    @pl.when(jnp.logical_not(valid) & (j == nf - 1))
    def _():
        y_ref[...] = jnp.zeros(y_ref.shape, F32)


def _moe_experts(xb, bexp, nvalid, wg, wu, wd, n_blocks, rows):
    d = wg.shape[1]
    f = wg.shape[2]
    nf = f // MOE_FT

    def blk(b, j, bexp, nvalid):
        return (jnp.minimum(b, nvalid[0] - 1), 0)

    def fidx(b, j, nvalid):
        return jnp.where(b < nvalid[0], j, nf - 1)

    grid_spec = pltpu.PrefetchScalarGridSpec(
        num_scalar_prefetch=2,
        grid=(n_blocks, nf),
        in_specs=[pl.BlockSpec((rows * SUBLANES, LANES), blk),
                  pl.BlockSpec((1, d, MOE_FT), lambda b, j, bexp, nvalid: (bexp[b], 0, fidx(b, j, nvalid))),
                  pl.BlockSpec((1, d, MOE_FT), lambda b, j, bexp, nvalid: (bexp[b], 0, fidx(b, j, nvalid))),
                  pl.BlockSpec((1, MOE_FT, d), lambda b, j, bexp, nvalid: (bexp[b], fidx(b, j, nvalid), 0))],
        out_specs=pl.BlockSpec((rows * SUBLANES, LANES), lambda b, j, bexp, nvalid: (b, 0)),
        scratch_shapes=[pltpu.VMEM((rows, d), BF16), pltpu.VMEM((rows, d), F32)],
    )
    return pl.pallas_call(
        _moe_kernel,
        grid_spec=grid_spec,
        out_shape=jax.ShapeDtypeStruct(xb.shape, F32),
        compiler_params=_cparams("arbitrary", "arbitrary"),
    )(bexp, nvalid, xb, wg, wu, wd)


def _combine_kernel(dest_ref, y_hbm, x_ref, gate_ref, gf_ref, fg_ref, o_ref, buf_ref, sem):
    tm = x_ref.shape[0]
    nsl = x_ref.shape[1] // LANES

    def issue(t, c):
        for k in range(TOP_K):
            _row_copy(y_hbm, dest_ref[t * TOP_K + k], buf_ref, k * tm + t, sem).start()
        return c

    lax.fori_loop(0, tm, issue, 0)
    rows = tm * TOP_K * SUBLANES
    pltpu.make_async_copy(y_hbm.at[pl.ds(0, rows)], buf_ref, sem).wait()

    g0 = gate_ref[:, 0:1]
    g1 = gate_ref[:, 1:2]
    parts = []
    for s in range(nsl):
        y0 = buf_ref[pl.ds(s, tm, stride=SUBLANES), :]
        y1 = buf_ref[pl.ds(tm * SUBLANES + s, tm, stride=SUBLANES), :]
        parts.append(g0 * y0 + g1 * y1)
    x = x_ref[...] + gf_ref[0] * jnp.concatenate(parts, axis=1)
    ms = jnp.mean(x * x, axis=-1, keepdims=True)
    o_ref[...] = x * lax.rsqrt(ms + EPS) * fg_ref[...]


def _combine(y, dest, x2, gates, gf, fg, tiles_per_batch):
    t, d = x2.shape
    tm = COMBINE_TOKENS
    return pl.pallas_call(
        _combine_kernel,
        grid=(t // tm,),
        in_specs=[pl.BlockSpec((tm * TOP_K,), lambda i: (i,), memory_space=pltpu.SMEM),
                  pl.BlockSpec(memory_space=pl.ANY),
                  pl.BlockSpec((tm, d), lambda i: (i, 0)),
                  pl.BlockSpec((tm, SUBLANES), lambda i: (i, 0)),
                  pl.BlockSpec((1, 1, d), lambda i: (i // tiles_per_batch, 0, 0)),
                  pl.BlockSpec((1, d), lambda i: (0, 0))],
        out_specs=pl.BlockSpec((tm, d), lambda i: (i, 0)),
        out_shape=jax.ShapeDtypeStruct((t, d), F32),
        scratch_shapes=[pltpu.VMEM((tm * TOP_K * SUBLANES, LANES), F32), pltpu.SemaphoreType.DMA(())],
        compiler_params=_cparams("arbitrary"),
    )(dest, y, x2, gates, gf, fg)


def _final_norm_kernel(x_ref, g_ref, o_ref):
    x = x_ref[...]
    o_ref[...] = x * lax.rsqrt(jnp.mean(x * x, axis=-1, keepdims=True) + EPS) * g_ref[...]


def _final_norm(x2, g):
    t, d = x2.shape
    tm = ROW_TILE
    return pl.pallas_call(
        _final_norm_kernel,
        grid=(t // tm,),
        in_specs=[pl.BlockSpec((tm, d), lambda i: (i, 0)), pl.BlockSpec((1, d), lambda i: (0, 0))],
        out_specs=pl.BlockSpec((tm, d), lambda i: (i, 0)),
        out_shape=jax.ShapeDtypeStruct((t, d), F32),
        compiler_params=_cparams("parallel"),
    )(x2, g)


def _qk_column_perm():
    half = HEAD_DIM // 2
    lane = np.arange(LANES)
    pair = (lane % HEAD_DIM) // half
    dim = lane % half + half * (lane // HEAD_DIM)
    q = np.concatenate([(i + Q_TILES * pair) * HEAD_DIM + dim for i in range(Q_TILES)])
    k = ATTN_WIDTH + pair * HEAD_DIM + dim
    return q, k


def _attn_out_row_perm():
    lane = np.arange(LANES)
    return np.concatenate([(i + Q_TILES * (lane // HEAD_DIM)) * HEAD_DIM + lane % HEAD_DIM
                           for i in range(Q_TILES)])


def _pad_lanes(v):
    return jnp.pad(v.astype(F32), (0, LANES - v.shape[0])).reshape(1, LANES)


def _moe_route_tables(route, counts, rows, n_blocks):
    e0, e1, r0, r1 = route[:, 0], route[:, 1], route[:, 2], route[:, 3]
    cnt = counts[0, :N_EXPERTS].astype(jnp.int32)
    padded = (cnt + rows - 1) // rows * rows
    pad_end = jnp.cumsum(padded)
    pad_start = pad_end - padded
    dest = jnp.stack([pad_start[e0] + r0, pad_start[e1] + r1], axis=1).reshape(-1)
    bexp = jnp.minimum(jnp.searchsorted(pad_end, jnp.arange(n_blocks, dtype=jnp.int32) * rows, side='right'),
                       N_EXPERTS - 1).astype(jnp.int32)
    nvalid = (pad_end[-1:] // rows).astype(jnp.int32)
    return dest.astype(jnp.int32), bexp, nvalid, (pad_start + cnt).astype(jnp.int32), pad_end.astype(jnp.int32)


def kernel(x, c, positions, ada_w, ada_b, norm_mix_g, norm_ffn_g, w_in, w_out, attn_sinks, conv_w, conv_b,
           dt_bias, a_log, d_skip, ssm_norm_g, ffn_w_gate, ffn_w_up, ffn_w_down, router_w, moe_w_gate,
           moe_w_up, moe_w_down, final_norm_g):
    batch, seq, d = x.shape
    depth = w_in.shape[0]
    t = batch * seq
    tiles_per_batch = seq // ROW_TILE
    x2 = x.reshape(t, d)

    c8 = jnp.pad(c, ((0, SUBLANES - batch), (0, 0)))
    mod = _ada_mod(c8, ada_w, ada_b)[:, :batch].reshape(depth, batch, 6, 1, d)
    cos, sin = _rope_tables(positions)

    q_perm, k_perm = _qk_column_perm()
    o_perm = _attn_out_row_perm()
    for l in range(depth):
        sh_m, sc_m, g_m, sh_f, sc_f, g_f = (mod[l, :, k] for k in range(6))
        wl = w_in[l]
        w_cat = jnp.concatenate(
            [wl[:, q_perm], wl[:, k_perm], wl[:, C_V:C_DT],
             jnp.pad(wl[:, C_DT:], ((0, 0), (0, LANES - SSM_HEADS)))], axis=1).astype(BF16)
        q, kv, z, xbc, dt = _inproj(x2, norm_mix_g[l].reshape(1, d), sc_m, sh_m, cos, sin, w_cat,
                                    _pad_lanes(dt_bias[l]), tiles_per_batch)
        attn = _attention(q, kv, attn_sinks[l].astype(F32), seq)
        ssm = _ssd(xbc, z, dt, conv_w[l], conv_b[l].reshape(1, CONV_CH), _pad_lanes(a_log[l]),
                   jnp.repeat(a_log[l].astype(F32), SSM_HEAD_DIM).reshape(1, SSM_WIDTH),
                   jnp.repeat(d_skip[l].astype(F32), SSM_HEAD_DIM).reshape(1, SSM_WIDTH),
                   ssm_norm_g[l].reshape(1, SSM_WIDTH), batch, seq)
        wo = jnp.concatenate([w_out[l][o_perm], w_out[l][ATTN_WIDTH:]], axis=0).astype(BF16)
        ffn_g = norm_ffn_g[l].reshape(1, d)
        if l % 2 == 0:
            x_new, h = _outproj(attn, ssm, x2, wo, g_m, ffn_g, sc_f, sh_f, tiles_per_batch)
            x2 = _ffn(h, x_new, ffn_w_gate[l // 2].astype(BF16), ffn_w_up[l // 2].astype(BF16),
                      ffn_w_down[l // 2].astype(BF16), g_f, tiles_per_batch)
            if l == depth - 1:
                x2 = _final_norm(x2, final_norm_g.reshape(1, d))
        else:
            if l != depth - 1:
                raise NotImplementedError("the expert layer fuses the final norm and must be last")
            rw = jnp.pad(router_w[l // 2].astype(F32), ((0, 0), (0, LANES - N_EXPERTS)))
            rw_hi = rw.astype(BF16)
            rw_lo = (rw - rw_hi.astype(F32)).astype(BF16)
            x_new, h_rows, route, gates, counts = _outproj(
                attn, ssm, x2, wo, g_m, ffn_g, sc_f, sh_f, tiles_per_batch,
                router=jnp.concatenate([rw_hi, rw_lo], axis=1))
            rows = min(MOE_ROWS, t)
            n_blocks = (t * TOP_K) // rows + N_EXPERTS
            dest, bexp, nvalid, zs, ze = _moe_route_tables(route, counts, rows, n_blocks)
            xb = _dispatch(h_rows, dest, zs, ze, nvalid, n_blocks, rows)
            y = _moe_experts(xb, bexp, nvalid, moe_w_gate[l // 2], moe_w_up[l // 2], moe_w_down[l // 2],
                             n_blocks, rows)
            x2 = _combine(y, dest, x_new, gates, g_f, final_norm_g.reshape(1, d),
                          seq // COMBINE_TOKENS)
    return x2.reshape(batch, seq, d)
```

```python
import functools

import numpy as np
import jax
import jax.numpy as jnp
from jax import lax
from jax.experimental import pallas as pl
from jax.experimental.pallas import tpu as pltpu

F32 = jnp.float32
BF16 = jnp.bfloat16

LANES = 128
SUBLANES = 8
VMEM_LIMIT = 48 * 1024 * 1024

EPS = 1e-6
HEAD_DIM = 64
Q_HEADS = 8
KV_HEADS = 2
GROUP = Q_HEADS // KV_HEADS
ATTN_WIDTH = Q_HEADS * HEAD_DIM
KV_WIDTH = KV_HEADS * HEAD_DIM
WINDOW = 128
ROPE_THETA = 10000.0
SSM_HEADS = 8
SSM_HEAD_DIM = 64
SSM_WIDTH = SSM_HEADS * SSM_HEAD_DIM
SSM_GROUPS = 2
SSM_STATE = 128
GROUP_WIDTH = SSM_WIDTH // SSM_GROUPS
CONV_WIDTH = 4
CONV_CH = SSM_WIDTH + 2 * SSM_GROUPS * SSM_STATE
CHUNK = 128
N_EXPERTS = 8
TOP_K = 2

Q_TILES = ATTN_WIDTH // LANES
C_Q = 0
C_K = C_Q + ATTN_WIDTH
C_V = C_K + KV_WIDTH
C_Z = C_V + KV_WIDTH
C_X = C_Z + SSM_WIDTH
C_DT = C_X + CONV_CH
C_END = C_DT + LANES

ROW_TILE = 512
ATTN_TILE = 512
MOE_ROWS = 1024
MOE_FT = 512
DISPATCH_TOKENS = 2048
COMBINE_TOKENS = 256
ISSUE_UNROLL = 8
SSD_CHUNKS_PER_STEP = 2
ROW_SPLIT = 2


def _cparams(*sem):
    return pltpu.CompilerParams(dimension_semantics=sem, vmem_limit_bytes=VMEM_LIMIT)


def _silu(v):
    return v * (1.0 / (1.0 + jnp.exp(-v)))


def _softplus(v):
    return jnp.maximum(v, 0.0) + jnp.log1p(jnp.exp(-jnp.abs(v)))


def _rms_mod(x, g, scale, shift):
    ms = jnp.mean(x * x, axis=-1, keepdims=True)
    return (x * lax.rsqrt(ms + EPS) * g) * (1.0 + scale) + shift


def _ada_kernel(c_ref, w_ref, b_ref, o_ref):
    c = c_ref[...]
    o_ref[0] = jnp.dot(_silu(c), w_ref[0], preferred_element_type=F32,
                       precision=lax.Precision.HIGHEST) + b_ref[0]


def _ada_mod(c8, ada_w, ada_b):
    depth, d, n = ada_w.shape
    tn = 1536
    return pl.pallas_call(
        _ada_kernel,
        grid=(depth, n // tn),
        in_specs=[pl.BlockSpec((SUBLANES, d), lambda l, j: (0, 0)),
                  pl.BlockSpec((1, d, tn), lambda l, j: (l, 0, j)),
                  pl.BlockSpec((1, 1, tn), lambda l, j: (l, 0, j))],
        out_specs=pl.BlockSpec((1, SUBLANES, tn), lambda l, j: (l, 0, j)),
        out_shape=jax.ShapeDtypeStruct((depth, SUBLANES, n), F32),
        compiler_params=_cparams("parallel", "parallel"),
    )(c8, ada_w, ada_b.reshape(depth, 1, n))


ROPE_PER_ROW = LANES // (HEAD_DIM // 2)


def _rope_kernel(pos_ref, inv_ref, sign_ref, cos_ref, sin_ref):
    ang = pos_ref[...].astype(F32) * inv_ref[...]
    tr = ang.shape[0]
    group = lax.broadcasted_iota(jnp.int32, (1, LANES), 1) // (HEAD_DIM // 2)
    for table, out_ref, sign in ((jnp.cos(ang), cos_ref, None), (jnp.sin(ang), sin_ref, sign_ref[...])):
        rolled = [table] + [pltpu.roll(table, (HEAD_DIM // 2) * s, axis=1) for s in range(1, ROPE_PER_ROW)]
        for j in range(ROPE_PER_ROW):
            rep = rolled[(0 - j) % ROPE_PER_ROW]
            for g in range(1, ROPE_PER_ROW):
                rep = jnp.where(group == g, rolled[(g - j) % ROPE_PER_ROW], rep)
            out_ref[pl.ds(j, tr, stride=ROPE_PER_ROW), :] = rep if sign is None else rep * sign


def _rope_tables(positions):
    t = positions.size
    half = HEAD_DIM // 2
    pos_dense = jnp.repeat(positions.reshape(t // ROPE_PER_ROW, ROPE_PER_ROW), half, axis=1)
    inv_freq = ROPE_THETA ** (-jnp.arange(0, HEAD_DIM, 2, dtype=F32) / HEAD_DIM)
    inv_dense = jnp.tile(inv_freq, ROPE_PER_ROW).reshape(1, LANES)
    sign = jnp.repeat(jnp.array([-1.0, -1.0, 1.0, 1.0], F32), half).reshape(1, LANES)
    rows = t // ROPE_PER_ROW
    tr = min(1024, rows)
    return pl.pallas_call(
        _rope_kernel,
        grid=(rows // tr,),
        in_specs=[pl.BlockSpec((tr, LANES), lambda i: (i, 0)),
                  pl.BlockSpec((1, LANES), lambda i: (0, 0)),
                  pl.BlockSpec((1, LANES), lambda i: (0, 0))],
        out_specs=[pl.BlockSpec((tr * ROPE_PER_ROW, LANES), lambda i: (i, 0))] * 2,
        out_shape=[jax.ShapeDtypeStruct((t, LANES), F32)] * 2,
        compiler_params=_cparams("parallel"),
    )(pos_dense, inv_dense, sign)


def _inproj_kernel(x_ref, g_ref, sc_ref, sh_ref, cos_ref, sin_ref, w_ref, dtb_ref,
                   q_ref, kv_ref, z_ref, xbc_ref, dt_ref):
    piece = x_ref.shape[0] // ROW_SPLIT
    for r in range(ROW_SPLIT):
        rs = slice(r * piece, (r + 1) * piece)
        h = _rms_mod(x_ref[rs, :], g_ref[...], sc_ref[0], sh_ref[0]).astype(BF16)
        cos = cos_ref[rs, :]
        sin = sin_ref[rs, :]

        def rope(t):
            return t * cos + pltpu.roll(t, LANES // 2, axis=1) * sin

        qkv = jnp.dot(h, w_ref[:, C_Q:C_Z], preferred_element_type=F32)
        for i in range(Q_TILES):
            q_ref[rs, i * LANES:(i + 1) * LANES] = (
                rope(qkv[:, i * LANES:(i + 1) * LANES]) * (HEAD_DIM ** -0.5)).astype(BF16)
        kv_ref[rs, 0:LANES] = rope(qkv[:, C_K:C_V]).astype(BF16)
        kv_ref[rs, LANES:2 * LANES] = qkv[:, C_V:C_Z].astype(BF16)
        z_ref[rs, :] = jnp.dot(h, w_ref[:, C_Z:C_X], preferred_element_type=F32)
        xbc_ref[rs, :] = jnp.dot(h, w_ref[:, C_X:C_DT], preferred_element_type=F32)
        dt_raw = jnp.dot(h, w_ref[:, C_DT:C_END], preferred_element_type=F32)
        dt_ref[rs, :] = _softplus(dt_raw + dtb_ref[...])


def _inproj(x2, g, sc, sh, cos, sin, w, dtb, tiles_per_batch):
    t, d = x2.shape
    tm = ROW_TILE
    row = lambda i: (i, 0)
    const = lambda i: (0, 0)
    per_b = lambda i: (i // tiles_per_batch, 0, 0)
    return pl.pallas_call(
        _inproj_kernel,
        grid=(t // tm,),
        in_specs=[pl.BlockSpec((tm, d), row),
                  pl.BlockSpec((1, d), const),
                  pl.BlockSpec((1, 1, d), per_b),
                  pl.BlockSpec((1, 1, d), per_b),
                  pl.BlockSpec((tm, LANES), row),
                  pl.BlockSpec((tm, LANES), row),
                  pl.BlockSpec((d, C_END), const),
                  pl.BlockSpec((1, LANES), const)],
        out_specs=[pl.BlockSpec((tm, ATTN_WIDTH), row),
                   pl.BlockSpec((tm, 2 * KV_WIDTH), row),
                   pl.BlockSpec((tm, SSM_WIDTH), row),
                   pl.BlockSpec((tm, CONV_CH), row),
                   pl.BlockSpec((tm, LANES), row)],
        out_shape=[jax.ShapeDtypeStruct((t, ATTN_WIDTH), BF16),
                   jax.ShapeDtypeStruct((t, 2 * KV_WIDTH), BF16),
                   jax.ShapeDtypeStruct((t, SSM_WIDTH), F32),
                   jax.ShapeDtypeStruct((t, CONV_CH), F32),
                   jax.ShapeDtypeStruct((t, LANES), F32)],
        compiler_params=_cparams("parallel"),
    )(x2, g, sc, sh, cos, sin, w, dtb)


def _attn_kernel(sink_ref, q_ref, kv_ref, kvp_ref, o_ref, *, tiles_per_seq):
    first = (pl.program_id(0) % tiles_per_seq) == 0
    blk = WINDOW
    nsub = ATTN_TILE // blk
    lane = lax.broadcasted_iota(jnp.int32, (1, LANES), 1)
    k_lo_mask = (lane % HEAD_DIM) < (HEAD_DIM // 2)
    v_lo_mask = lane < HEAD_DIM
    zero = jnp.zeros((), BF16)

    k_all = jnp.concatenate([kvp_ref[:, 0:LANES], kv_ref[:, 0:LANES]], axis=0)
    v_all = jnp.concatenate([kvp_ref[:, LANES:2 * LANES], kv_ref[:, LANES:2 * LANES]], axis=0)
    k_sel = (jnp.where(k_lo_mask, k_all, zero), jnp.where(k_lo_mask, zero, k_all))
    v_sel = (jnp.where(v_lo_mask, v_all, zero), jnp.where(v_lo_mask, zero, v_all))

    qi = lax.broadcasted_iota(jnp.int32, (blk, blk), 0)
    col = lax.broadcasted_iota(jnp.int32, (blk, blk), 1)
    cur = col <= qi
    no_prev = (col > qi) & (col > jnp.where(first, -1, blk))

    for n in range(nsub):
        q_st = jnp.concatenate([q_ref[n * blk:(n + 1) * blk, i * LANES:(i + 1) * LANES]
                                for i in range(Q_TILES)], axis=0)
        out = None
        for hk in range(KV_HEADS):
            k_n = k_sel[hk][n * blk:(n + 2) * blk]
            v_n = v_sel[hk][n * blk:(n + 2) * blk]
            s = lax.dot_general(q_st, k_n, (((1,), (1,)), ((), ())), preferred_element_type=F32)
            probs, scales = [], []
            for i in range(GROUP):
                s_i = s[i * blk:(i + 1) * blk]
                sc = jnp.where(cur, s_i[:, blk:], s_i[:, :blk])
                if n == 0:
                    sc = jnp.where(no_prev, -jnp.inf, sc)
                sink = sink_ref[hk * GROUP + i]
                m = jnp.maximum(jnp.max(sc, axis=-1, keepdims=True), sink)
                p = jnp.exp(sc - m)
                scales.append(1.0 / (jnp.sum(p, axis=-1, keepdims=True) + jnp.exp(sink - m)))
                probs.append(jnp.concatenate([jnp.where(cur, 0.0, p), jnp.where(cur, p, 0.0)],
                                             axis=1).astype(BF16))
            o = jnp.dot(jnp.concatenate(probs, axis=0), v_n, preferred_element_type=F32)
            o = o * jnp.concatenate(scales, axis=0)
            out = o if out is None else out + o
        for i in range(Q_TILES):
            o_ref[n * blk:(n + 1) * blk, i * LANES:(i + 1) * LANES] = out[i * blk:(i + 1) * blk].astype(BF16)


def _attention(q, kv, sinks, seq):
    t = q.shape[0]
    tq = ATTN_TILE
    r = tq // WINDOW
    kern = functools.partial(_attn_kernel, tiles_per_seq=seq // tq)
    return pl.pallas_call(
        kern,
        grid=(t // tq,),
        in_specs=[pl.BlockSpec(memory_space=pltpu.SMEM),
                  pl.BlockSpec((tq, ATTN_WIDTH), lambda i: (i, 0)),
                  pl.BlockSpec((tq, 2 * KV_WIDTH), lambda i: (i, 0)),
                  pl.BlockSpec((WINDOW, 2 * KV_WIDTH), lambda i: (jnp.maximum(i * r - 1, 0), 0))],
        out_specs=pl.BlockSpec((tq, ATTN_WIDTH), lambda i: (i, 0)),
        out_shape=jax.ShapeDtypeStruct((t, ATTN_WIDTH), BF16),
        compiler_params=_cparams("parallel"),
    )(sinks, q, kv, kv)


def _split3(v):
    p0 = v.astype(BF16)
    r1 = v - p0.astype(F32)
    p1 = r1.astype(BF16)
    return p0, p1, (r1 - p1.astype(F32)).astype(BF16)


def _dot3(lhs_exact_bf16, v):
    return sum(jnp.dot(lhs_exact_bf16, p, preferred_element_type=F32) for p in _split3(v))


def _ssd_kernel(xbc_ref, z_ref, dt_ref, cw_ref, cb_ref, alog_ref, aloge_ref, dskip_ref, ng_ref,
                o_ref, ext_ref, st_ref):
    L = CHUNK
    rows = xbc_ref.shape[0]
    halo = SUBLANES

    @pl.when(pl.program_id(1) == 0)
    def _():
        ext_ref[0:halo, :] = jnp.zeros((halo, CONV_CH), F32)
        st_ref[...] = jnp.zeros(st_ref.shape, F32)

    ext_ref[halo:halo + rows, :] = xbc_ref[...]
    acc = cb_ref[...] + cw_ref[CONV_WIDTH - 1:CONV_WIDTH, :] * ext_ref[halo:halo + rows, :]
    for k in range(CONV_WIDTH - 1):
        off = halo - (CONV_WIDTH - 1) + k
        acc = acc + cw_ref[k:k + 1, :] * ext_ref[off:off + rows, :]
    ext_ref[0:halo, :] = ext_ref[rows:rows + halo, :]
    u_all = _silu(acc)

    lane = lax.broadcasted_iota(jnp.int32, (1, LANES), 1)
    a = jnp.where(lane < SSM_HEADS, -jnp.exp(alog_ref[...]), 0.0)
    a_e = -jnp.exp(aloge_ref[...])
    row = lax.broadcasted_iota(jnp.int32, (L, L), 0)
    col = lax.broadcasted_iota(jnp.int32, (L, L), 1)
    causal = row >= col
    tri = jnp.where(causal, 1.0, 0.0).astype(BF16)
    spread = jnp.where(lax.broadcasted_iota(jnp.int32, (LANES, SSM_WIDTH), 1) // SSM_HEAD_DIM
                       == lax.broadcasted_iota(jnp.int32, (LANES, SSM_WIDTH), 0), 1.0, 0.0).astype(BF16)
    r_heads = SSM_HEADS // SSM_GROUPS
    glane = lax.broadcasted_iota(jnp.int32, (1, GROUP_WIDTH), 1) // SSM_HEAD_DIM
    zero = jnp.zeros((), BF16)
    states = [st_ref[g] for g in range(SSM_GROUPS)]

    for c in range(rows // L):
        rs = slice(c * L, (c + 1) * L)
        u = u_all[rs]
        xs = u[:, 0:SSM_WIDTH]
        bm = u[:, SSM_WIDTH:SSM_WIDTH + SSM_GROUPS * SSM_STATE]
        cm = u[:, SSM_WIDTH + SSM_GROUPS * SSM_STATE:]
        dt = dt_ref[rs, :]
        acs = _dot3(tri, dt * a)
        acs_t = acs.T
        dt_e = sum(jnp.dot(p, spread, preferred_element_type=F32) for p in _split3(dt))
        acs_e = _dot3(tri, dt_e * a_e)
        last = acs_e[L - 1:L, :]
        xd = xs * dt_e
        xd_b = xd.astype(BF16)
        xdw_b = (xd * jnp.exp(last - acs_e)).astype(BF16)
        e_acs = jnp.exp(acs_e)
        c_dec = jnp.exp(last)

        ys = []
        for g in range(SSM_GROUPS):
            gs = slice(g * GROUP_WIDTH, (g + 1) * GROUP_WIDTH)
            b_g = bm[:, g * SSM_STATE:(g + 1) * SSM_STATE]
            c_b = cm[:, g * SSM_STATE:(g + 1) * SSM_STATE].astype(BF16)
            cb = lax.dot_general(c_b, b_g.astype(BF16), (((1,), (1,)), ((), ())),
                                 preferred_element_type=F32)
            st = states[g]
            y_g = jnp.dot(c_b, st.astype(BF16), preferred_element_type=F32) * e_acs[:, gs]
            xd_g = xd_b[:, gs]
            for r in range(r_heads):
                h = g * r_heads + r
                seg = acs[:, h:h + 1] - acs_t[h:h + 1, :]
                m_h = (cb * jnp.exp(jnp.where(causal, seg, -jnp.inf))).astype(BF16)
                y_g = y_g + jnp.dot(m_h, jnp.where(glane == r, xd_g, zero), preferred_element_type=F32)
            new = jnp.dot(b_g.T.astype(BF16), xdw_b[:, gs], preferred_element_type=F32)
            states[g] = c_dec[:, gs] * st + new
            ys.append(y_g)

        y = jnp.concatenate(ys, axis=1) + dskip_ref[...] * xs
        y = y * _silu(z_ref[rs, :])
        outs = []
        for g in range(SSM_GROUPS):
            yg = y[:, g * GROUP_WIDTH:(g + 1) * GROUP_WIDTH]
            outs.append(yg * lax.rsqrt(jnp.mean(yg * yg, axis=-1, keepdims=True) + EPS))
        o_ref[rs, :] = (jnp.concatenate(outs, axis=1) * ng_ref[...]).astype(BF16)

    for g in range(SSM_GROUPS):
        st_ref[g] = states[g]


def _ssd(xbc, z, dt, cw, cb, alog, aloge, dskip, ng, batch, seq):
    t = xbc.shape[0]
    rows = SSD_CHUNKS_PER_STEP * CHUNK
    ns = seq // rows
    row = lambda b, c: (b * ns + c, 0)
    const = lambda b, c: (0, 0)
    return pl.pallas_call(
        _ssd_kernel,
        grid=(batch, ns),
        in_specs=[pl.BlockSpec((rows, CONV_CH), row),
                  pl.BlockSpec((rows, SSM_WIDTH), row),
                  pl.BlockSpec((rows, LANES), row),
                  pl.BlockSpec((CONV_WIDTH, CONV_CH), const),
                  pl.BlockSpec((1, CONV_CH), const),
                  pl.BlockSpec((1, LANES), const),
                  pl.BlockSpec((1, SSM_WIDTH), const),
                  pl.BlockSpec((1, SSM_WIDTH), const),
                  pl.BlockSpec((1, SSM_WIDTH), const)],
        out_specs=pl.BlockSpec((rows, SSM_WIDTH), row),
        out_shape=jax.ShapeDtypeStruct((t, SSM_WIDTH), BF16),
        scratch_shapes=[pltpu.VMEM((SUBLANES + rows, CONV_CH), F32),
                        pltpu.VMEM((SSM_GROUPS, SSM_STATE, GROUP_WIDTH), F32)],
        compiler_params=_cparams("parallel", "arbitrary"),
    )(xbc, z, dt, cw, cb, alog, aloge, dskip, ng)


def _outproj_core(a_ref, s_ref, x_ref, w_ref, gm_ref, g_ref, sc_ref, sh_ref):
    half = a_ref.shape[1]
    mixed = (jnp.dot(a_ref[...], w_ref[0:half, :], preferred_element_type=F32)
             + jnp.dot(s_ref[...], w_ref[half:, :], preferred_element_type=F32))
    x_new = x_ref[...] + gm_ref[0] * mixed
    return x_new, _rms_mod(x_new, g_ref[...], sc_ref[0], sh_ref[0])


def _outproj_kernel(a_ref, s_ref, x_ref, w_ref, gm_ref, g_ref, sc_ref, sh_ref, xo_ref, h_ref):
    x_new, h = _outproj_core(a_ref, s_ref, x_ref, w_ref, gm_ref, g_ref, sc_ref, sh_ref)
    xo_ref[...] = x_new
    h_ref[...] = h.astype(BF16)


def _outproj_router_kernel(a_ref, s_ref, x_ref, w_ref, gm_ref, g_ref, sc_ref, sh_ref, rw_ref,
                           xo_ref, h_ref, route_ref, gate_ref, cnt_ref, carry_ref):
    x_new, h = _outproj_core(a_ref, s_ref, x_ref, w_ref, gm_ref, g_ref, sc_ref, sh_ref)
    xo_ref[...] = x_new
    tm = h.shape[0]
    for s in range(h.shape[1] // LANES):
        h_ref[pl.ds(s, tm, stride=SUBLANES), :] = h[:, s * LANES:(s + 1) * LANES]

    h_hi = h.astype(BF16)
    h_lo = (h - h_hi.astype(F32)).astype(BF16)
    hi_part = jnp.dot(h_hi, rw_ref[...], preferred_element_type=F32)
    logits = (hi_part[:, 0:LANES] + hi_part[:, LANES:]
              + jnp.dot(h_lo, rw_ref[:, 0:LANES], preferred_element_type=F32))
    lane = lax.broadcasted_iota(jnp.int32, (tm, LANES), 1).astype(F32)
    logits = jnp.where(lane < N_EXPERTS, logits, -jnp.inf)
    m0 = jnp.max(logits, axis=-1, keepdims=True)
    i0 = jnp.min(jnp.where(logits == m0, lane, float(LANES)), axis=-1, keepdims=True)
    rest = jnp.where(lane == i0, -jnp.inf, logits)
    m1 = jnp.max(rest, axis=-1, keepdims=True)
    i1 = jnp.min(jnp.where(rest == m1, lane, float(LANES)), axis=-1, keepdims=True)
    e = jnp.exp(m1 - m0)
    g0 = 1.0 / (1.0 + e)
    g1 = e / (1.0 + e)

    @pl.when(pl.program_id(0) == 0)
    def _():
        carry_ref[...] = jnp.zeros(carry_ref.shape, F32)

    sel0 = lane == i0
    sel1 = lane == i1
    member = jnp.where(sel0 | sel1, 1.0, 0.0)
    r_i = lax.broadcasted_iota(jnp.int32, (tm, tm), 0)
    c_i = lax.broadcasted_iota(jnp.int32, (tm, tm), 1)
    strict = jnp.where(r_i > c_i, 1.0, 0.0).astype(BF16)
    rank = carry_ref[...] + jnp.dot(strict, member.astype(BF16), preferred_element_type=F32)
    r0 = jnp.sum(jnp.where(sel0, rank, 0.0), axis=-1, keepdims=True)
    r1 = jnp.sum(jnp.where(sel1, rank, 0.0), axis=-1, keepdims=True)
    carry_ref[...] = carry_ref[...] + jnp.sum(member, axis=0, keepdims=True)
    cnt_ref[...] = carry_ref[...]

    gate_ref[...] = jnp.where(lane == 0.0, g0, g1)[:, 0:SUBLANES]
    diag = (lax.broadcasted_iota(jnp.int32, (tm, LANES), 0) % LANES
            == lax.broadcasted_iota(jnp.int32, (tm, LANES), 1))
    nrow = tm // LANES
    for q, colv in enumerate((i0, i1, r0, r1)):
        spread = jnp.where(diag, colv, 0.0)
        rows = [jnp.sum(spread[b * LANES:(b + 1) * LANES], axis=0, keepdims=True) for b in range(nrow)]
        route_ref[0, q * nrow:(q + 1) * nrow, :] = jnp.concatenate(rows, axis=0).astype(jnp.int32)


def _outproj(attn, ssm, x2, w, gm, g, sc, sh, tiles_per_batch, router=None):
    t, d = x2.shape
    tm = ROW_TILE
    row = lambda i: (i, 0)
    const = lambda i: (0, 0)
    per_b = lambda i: (i // tiles_per_batch, 0, 0)
    half = attn.shape[1]
    in_specs = [pl.BlockSpec((tm, half), row),
                pl.BlockSpec((tm, half), row),
                pl.BlockSpec((tm, d), row),
                pl.BlockSpec((2 * half, d), const),
                pl.BlockSpec((1, 1, d), per_b),
                pl.BlockSpec((1, d), const),
                pl.BlockSpec((1, 1, d), per_b),
                pl.BlockSpec((1, 1, d), per_b)]
    if router is None:
        return pl.pallas_call(
            _outproj_kernel,
            grid=(t // tm,),
            in_specs=in_specs,
            out_specs=[pl.BlockSpec((tm, d), row), pl.BlockSpec((tm, d), row)],
            out_shape=[jax.ShapeDtypeStruct((t, d), F32), jax.ShapeDtypeStruct((t, d), BF16)],
            compiler_params=_cparams("parallel"),
        )(attn, ssm, x2, w, gm, g, sc, sh)
    return pl.pallas_call(
        _outproj_router_kernel,
        grid=(t // tm,),
        in_specs=in_specs + [pl.BlockSpec((d, 2 * LANES), const)],
        out_specs=[pl.BlockSpec((tm, d), row),
                   pl.BlockSpec((tm * SUBLANES, LANES), row),
                   pl.BlockSpec((1, 4 * tm // LANES, LANES), lambda i: (i, 0, 0)),
                   pl.BlockSpec((tm, SUBLANES), row),
                   pl.BlockSpec((1, LANES), const)],
        out_shape=[jax.ShapeDtypeStruct((t, d), F32),
                   jax.ShapeDtypeStruct((t * SUBLANES, LANES), F32),
                   jax.ShapeDtypeStruct((t // tm, 4 * tm // LANES, LANES), jnp.int32),
                   jax.ShapeDtypeStruct((t, SUBLANES), F32),
                   jax.ShapeDtypeStruct((1, LANES), F32)],
        scratch_shapes=[pltpu.VMEM((1, LANES), F32)],
        compiler_params=_cparams("arbitrary"),
    )(attn, ssm, x2, w, gm, g, sc, sh, router)


def _swiglu_rows(x, wg, wu, wd):
    gate = jnp.dot(x, wg, preferred_element_type=F32)
    up = jnp.dot(x, wu, preferred_element_type=F32)
    return jnp.dot((_silu(gate) * up).astype(BF16), wd, preferred_element_type=F32)


def _ffn_kernel(h_ref, x_ref, wg_ref, wu_ref, wd_ref, gf_ref, o_ref):
    piece = h_ref.shape[0] // ROW_SPLIT
    for r in range(ROW_SPLIT):
        rs = slice(r * piece, (r + 1) * piece)
        f = _swiglu_rows(h_ref[rs, :], wg_ref[...], wu_ref[...], wd_ref[...])
        o_ref[rs, :] = x_ref[rs, :] + gf_ref[0] * f


def _ffn(h, x2, wg, wu, wd, gf, tiles_per_batch):
    t, d = x2.shape
    f = wg.shape[1]
    tm = ROW_TILE
    row = lambda i: (i, 0)
    const = lambda i: (0, 0)
    resident = pl.Buffered(1)
    return pl.pallas_call(
        _ffn_kernel,
        grid=(t // tm,),
        in_specs=[pl.BlockSpec((tm, d), row),
                  pl.BlockSpec((tm, d), row),
                  pl.BlockSpec((d, f), const, pipeline_mode=resident),
                  pl.BlockSpec((d, f), const, pipeline_mode=resident),
                  pl.BlockSpec((f, d), const, pipeline_mode=resident),
                  pl.BlockSpec((1, 1, d), lambda i: (i // tiles_per_batch, 0, 0))],
        out_specs=pl.BlockSpec((tm, d), row),
        out_shape=jax.ShapeDtypeStruct((t, d), F32),
        compiler_params=_cparams("parallel"),
    )(h, x2, wg, wu, wd, gf)


def _row_copy(src, src_row, dst, dst_row, sem):
    return pltpu.make_async_copy(
        src.at[pl.ds(pl.multiple_of(src_row * SUBLANES, SUBLANES), SUBLANES)],
        dst.at[pl.ds(pl.multiple_of(dst_row * SUBLANES, SUBLANES), SUBLANES)], sem)


def _dispatch_kernel(zs_ref, ze_ref, nvalid_ref, d0_ref, d1_ref, h_ref, xb_hbm, zero_ref, sem, zsem):
    i = pl.program_id(0)
    n_tok = d0_ref.shape[0]
    blk_rows = zero_ref.shape[0]
    n_blocks = xb_hbm.shape[0] // blk_rows

    def zero_row(r):
        return pltpu.make_async_copy(
            zero_ref.at[pl.ds(0, SUBLANES)],
            xb_hbm.at[pl.ds(pl.multiple_of(r * SUBLANES, SUBLANES), SUBLANES)], zsem)

    def zero_block(b):
        return pltpu.make_async_copy(
            zero_ref, xb_hbm.at[pl.ds(pl.multiple_of(b * blk_rows, blk_rows), blk_rows)], zsem)

    @pl.when(i == 0)
    def _():
        zero_ref[...] = jnp.zeros(zero_ref.shape, F32)
        for start in (True, False):
            for e in range(N_EXPERTS):
                def rows_body(r, c):
                    zero_row(r).start() if start else zero_row(r).wait()
                    return c
                lax.fori_loop(zs_ref[e], ze_ref[e], rows_body, 0)

            def blocks_body(b, c):
                zero_block(b).start() if start else zero_block(b).wait()
                return c
            lax.fori_loop(nvalid_ref[0], n_blocks, blocks_body, 0)

    def issue(t, c):
        _row_copy(h_ref, t, xb_hbm, d0_ref[t], sem).start()
        _row_copy(h_ref, t, xb_hbm, d1_ref[t], sem).start()
        return c

    lax.fori_loop(0, n_tok, issue, 0, unroll=ISSUE_UNROLL)
    rows = n_tok * TOP_K * SUBLANES
    pltpu.make_async_copy(xb_hbm.at[pl.ds(0, rows)], xb_hbm.at[pl.ds(0, rows)], sem).wait()


def _dispatch(h_rows, dest, zs, ze, nvalid, n_blocks, rows):
    t = h_rows.shape[0] // SUBLANES
    td = min(DISPATCH_TOKENS, t)
    grid_spec = pltpu.PrefetchScalarGridSpec(
        num_scalar_prefetch=3,
        grid=(t // td,),
        in_specs=[pl.BlockSpec((td,), lambda i, zs, ze, nv: (i,), memory_space=pltpu.SMEM),
                  pl.BlockSpec((td,), lambda i, zs, ze, nv: (i,), memory_space=pltpu.SMEM),
                  pl.BlockSpec((td * SUBLANES, LANES), lambda i, zs, ze, nv: (i, 0))],
        out_specs=pl.BlockSpec(memory_space=pl.ANY),
        scratch_shapes=[pltpu.VMEM((rows * SUBLANES, LANES), F32),
                        pltpu.SemaphoreType.DMA(()), pltpu.SemaphoreType.DMA(())],
    )
    return pl.pallas_call(
        _dispatch_kernel,
        grid_spec=grid_spec,
        out_shape=jax.ShapeDtypeStruct((n_blocks * rows * SUBLANES, LANES), F32),
        compiler_params=pltpu.CompilerParams(dimension_semantics=("arbitrary",), has_side_effects=True,
                                             vmem_limit_bytes=VMEM_LIMIT),
    )(zs, ze, nvalid, dest[0], dest[1], h_rows)


def _moe_kernel(bexp_ref, nvalid_ref, xb_ref, wg_ref, wu_ref, wd_ref, y_ref, x_scr, acc_ref):
    b = pl.program_id(0)
    j = pl.program_id(1)
    nf = pl.num_programs(1)
    rows = x_scr.shape[0]
    nsl = x_scr.shape[1] // LANES
    valid = b < nvalid_ref[0]

    @pl.when(valid & (j == 0))
    def _():
        for s in range(nsl):
            x_scr[:, s * LANES:(s + 1) * LANES] = xb_ref[pl.ds(s, rows, stride=SUBLANES), :].astype(BF16)
        acc_ref[...] = jnp.zeros(acc_ref.shape, F32)

    @pl.when(valid)
    def _():
        wg = wg_ref[0].astype(BF16)
        wu = wu_ref[0].astype(BF16)
        wd = wd_ref[0].astype(BF16)
        piece = rows // ROW_SPLIT
        for r in range(ROW_SPLIT):
            rs = slice(r * piece, (r + 1) * piece)
            acc_ref[rs, :] += _swiglu_rows(x_scr[rs, :], wg, wu, wd)

    @pl.when(valid & (j == nf - 1))
    def _():
        for s in range(nsl):
            y_ref[pl.ds(s, rows, stride=SUBLANES), :] = acc_ref[:, s * LANES:(s + 1) * LANES]

    @pl.when(jnp.logical_not(valid) & (j == nf - 1))
    def _():
        y_ref[...] = jnp.zeros(y_ref.shape, F32)


def _moe_experts(xb, bexp, nvalid, wg, wu, wd, n_blocks, rows):
    d = wg.shape[1]
    f = wg.shape[2]
    nf = f // MOE_FT

    def blk(b, j, bexp, nvalid):
        return (jnp.minimum(b, nvalid[0] - 1), 0)

    def fidx(b, j, nvalid):
        return jnp.where(b < nvalid[0], j, nf - 1)

    grid_spec = pltpu.PrefetchScalarGridSpec(
        num_scalar_prefetch=2,
        grid=(n_blocks, nf),
        in_specs=[pl.BlockSpec((rows * SUBLANES, LANES), blk),
                  pl.BlockSpec((1, d, MOE_FT), lambda b, j, bexp, nvalid: (bexp[b], 0, fidx(b, j, nvalid))),
                  pl.BlockSpec((1, d, MOE_FT), lambda b, j, bexp, nvalid: (bexp[b], 0, fidx(b, j, nvalid))),
                  pl.BlockSpec((1, MOE_FT, d), lambda b, j, bexp, nvalid: (bexp[b], fidx(b, j, nvalid), 0))],
        out_specs=pl.BlockSpec((rows * SUBLANES, LANES), lambda b, j, bexp, nvalid: (b, 0)),
        scratch_shapes=[pltpu.VMEM((rows, d), BF16), pltpu.VMEM((rows, d), F32)],
    )
    return pl.pallas_call(
        _moe_kernel,
        grid_spec=grid_spec,
        out_shape=jax.ShapeDtypeStruct(xb.shape, F32),
        compiler_params=_cparams("arbitrary", "arbitrary"),
    )(bexp, nvalid, xb, wg, wu, wd)


def _combine_kernel(d0_ref, d1_ref, d0n_ref, d1n_ref, y_hbm, x_ref, gate_ref, gf_ref, fg_ref, o_ref,
                    buf_ref, sem):
    i = pl.program_id(0)
    tm = x_ref.shape[0]
    nsl = x_ref.shape[1] // LANES
    slot = i % 2

    def gather(a_ref, b_ref, to_slot):
        def issue(t, c):
            _row_copy(y_hbm, a_ref[t], buf_ref.at[to_slot], t, sem.at[to_slot]).start()
            _row_copy(y_hbm, b_ref[t], buf_ref.at[to_slot], tm + t, sem.at[to_slot]).start()
            return c
        lax.fori_loop(0, tm, issue, 0, unroll=ISSUE_UNROLL)

    @pl.when(i == 0)
    def _():
        gather(d0_ref, d1_ref, 0)

    @pl.when(i + 1 < pl.num_programs(0))
    def _():
        gather(d0n_ref, d1n_ref, 1 - slot)

    rows = tm * TOP_K * SUBLANES
    pltpu.make_async_copy(y_hbm.at[pl.ds(0, rows)], buf_ref.at[slot], sem.at[slot]).wait()

    g0 = gate_ref[:, 0:1]
    g1 = gate_ref[:, 1:2]
    parts = []
    for s in range(nsl):
        y0 = buf_ref[slot, pl.ds(s, tm, stride=SUBLANES), :]
        y1 = buf_ref[slot, pl.ds(tm * SUBLANES + s, tm, stride=SUBLANES), :]
        parts.append(g0 * y0 + g1 * y1)
    x = x_ref[...] + gf_ref[0] * jnp.concatenate(parts, axis=1)
    ms = jnp.mean(x * x, axis=-1, keepdims=True)
    o_ref[...] = x * lax.rsqrt(ms + EPS) * fg_ref[...]


def _combine(y, dest, x2, gates, gf, fg, tiles_per_batch):
    t, d = x2.shape
    tm = COMBINE_TOKENS
    n = t // tm
    cur = pl.BlockSpec((tm,), lambda i: (i,), memory_space=pltpu.SMEM)
    nxt = pl.BlockSpec((tm,), lambda i: (jnp.minimum(i + 1, n - 1),), memory_space=pltpu.SMEM)
    return pl.pallas_call(
        _combine_kernel,
        grid=(n,),
        in_specs=[cur, cur, nxt, nxt,
                  pl.BlockSpec(memory_space=pl.ANY),
                  pl.BlockSpec((tm, d), lambda i: (i, 0)),
                  pl.BlockSpec((tm, SUBLANES), lambda i: (i, 0)),
                  pl.BlockSpec((1, 1, d), lambda i: (i // tiles_per_batch, 0, 0)),
                  pl.BlockSpec((1, d), lambda i: (0, 0))],
        out_specs=pl.BlockSpec((tm, d), lambda i: (i, 0)),
        out_shape=jax.ShapeDtypeStruct((t, d), F32),
        scratch_shapes=[pltpu.VMEM((2, tm * TOP_K * SUBLANES, LANES), F32), pltpu.SemaphoreType.DMA((2,))],
        compiler_params=_cparams("arbitrary"),
    )(dest[0], dest[1], dest[0], dest[1], y, x2, gates, gf, fg)


def _final_norm_kernel(x_ref, g_ref, o_ref):
    x = x_ref[...]
    o_ref[...] = x * lax.rsqrt(jnp.mean(x * x, axis=-1, keepdims=True) + EPS) * g_ref[...]


def _final_norm(x2, g):
    t, d = x2.shape
    tm = ROW_TILE
    return pl.pallas_call(
        _final_norm_kernel,
        grid=(t // tm,),
        in_specs=[pl.BlockSpec((tm, d), lambda i: (i, 0)), pl.BlockSpec((1, d), lambda i: (0, 0))],
        out_specs=pl.BlockSpec((tm, d), lambda i: (i, 0)),
        out_shape=jax.ShapeDtypeStruct((t, d), F32),
        compiler_params=_cparams("parallel"),
    )(x2, g)


def _qk_column_perm():
    half = HEAD_DIM // 2
    lane = np.arange(LANES)
    pair = (lane % HEAD_DIM) // half
    dim = lane % half + half * (lane // HEAD_DIM)
    q = np.concatenate([(i + Q_TILES * pair) * HEAD_DIM + dim for i in range(Q_TILES)])
    k = ATTN_WIDTH + pair * HEAD_DIM + dim
    return q, k


def _attn_out_row_perm():
    lane = np.arange(LANES)
    return np.concatenate([(i + Q_TILES * (lane // HEAD_DIM)) * HEAD_DIM + lane % HEAD_DIM
                           for i in range(Q_TILES)])


def _pad_lanes(v):
    return jnp.pad(v.astype(F32), (0, LANES - v.shape[0])).reshape(1, LANES)


def _moe_route_tables(route, counts, rows, n_blocks):
    nt = route.shape[0]
    e0, e1, r0, r1 = (route.reshape(nt, 4, -1)[:, q].reshape(-1) for q in range(4))
    cnt = counts[0, :N_EXPERTS].astype(jnp.int32)
    padded = (cnt + rows - 1) // rows * rows
    pad_end = jnp.cumsum(padded)
    pad_start = pad_end - padded

    def first_row(e):
        return sum(jnp.where(e == k, pad_start[k], 0) for k in range(N_EXPERTS))

    dest = (first_row(e0) + r0).astype(jnp.int32), (first_row(e1) + r1).astype(jnp.int32)
    blk_start = jnp.arange(n_blocks, dtype=jnp.int32) * rows
    bexp = jnp.minimum(jnp.sum(blk_start[:, None] >= pad_end[None, :], axis=1), N_EXPERTS - 1).astype(jnp.int32)
    nvalid = (pad_end[-1:] // rows).astype(jnp.int32)
    return dest, bexp, nvalid, (pad_start + cnt).astype(jnp.int32), pad_end.astype(jnp.int32)


def kernel(x, c, positions, ada_w, ada_b, norm_mix_g, norm_ffn_g, w_in, w_out, attn_sinks, conv_w, conv_b,
           dt_bias, a_log, d_skip, ssm_norm_g, ffn_w_gate, ffn_w_up, ffn_w_down, router_w, moe_w_gate,
           moe_w_up, moe_w_down, final_norm_g):
    batch, seq, d = x.shape
    depth = w_in.shape[0]
    t = batch * seq
    tiles_per_batch = seq // ROW_TILE
    x2 = x.reshape(t, d)

    c8 = jnp.pad(c, ((0, SUBLANES - batch), (0, 0)))
    mod = _ada_mod(c8, ada_w, ada_b)[:, :batch].reshape(depth, batch, 6, 1, d)
    cos, sin = _rope_tables(positions)

    q_perm, k_perm = _qk_column_perm()
    o_perm = _attn_out_row_perm()
    for l in range(depth):
        sh_m, sc_m, g_m, sh_f, sc_f, g_f = (mod[l, :, k] for k in range(6))
        wl = w_in[l]
        w_cat = jnp.concatenate(
            [wl[:, q_perm], wl[:, k_perm], wl[:, C_V:C_DT],
             jnp.pad(wl[:, C_DT:], ((0, 0), (0, LANES - SSM_HEADS)))], axis=1).astype(BF16)
        q, kv, z, xbc, dt = _inproj(x2, norm_mix_g[l].reshape(1, d), sc_m, sh_m, cos, sin, w_cat,
                                    _pad_lanes(dt_bias[l]), tiles_per_batch)
        attn = _attention(q, kv, attn_sinks[l].astype(F32), seq)
        ssm = _ssd(xbc, z, dt, conv_w[l], conv_b[l].reshape(1, CONV_CH), _pad_lanes(a_log[l]),
                   jnp.repeat(a_log[l].astype(F32), SSM_HEAD_DIM).reshape(1, SSM_WIDTH),
                   jnp.repeat(d_skip[l].astype(F32), SSM_HEAD_DIM).reshape(1, SSM_WIDTH),
                   ssm_norm_g[l].reshape(1, SSM_WIDTH), batch, seq)
        wo = jnp.concatenate([w_out[l][o_perm], w_out[l][ATTN_WIDTH:]], axis=0).astype(BF16)
        ffn_g = norm_ffn_g[l].reshape(1, d)
        if l % 2 == 0:
            x_new, h = _outproj(attn, ssm, x2, wo, g_m, ffn_g, sc_f, sh_f, tiles_per_batch)
            x2 = _ffn(h, x_new, ffn_w_gate[l // 2].astype(BF16), ffn_w_up[l // 2].astype(BF16),
                      ffn_w_down[l // 2].astype(BF16), g_f, tiles_per_batch)
            if l == depth - 1:
                x2 = _final_norm(x2, final_norm_g.reshape(1, d))
        else:
            if l != depth - 1:
                raise NotImplementedError("the expert layer fuses the final norm and must be last")
            rw = jnp.pad(router_w[l // 2].astype(F32), ((0, 0), (0, LANES - N_EXPERTS)))
            rw_hi = rw.astype(BF16)
            rw_lo = (rw - rw_hi.astype(F32)).astype(BF16)
            x_new, h_rows, route, gates, counts = _outproj(
                attn, ssm, x2, wo, g_m, ffn_g, sc_f, sh_f, tiles_per_batch,
                router=jnp.concatenate([rw_hi, rw_lo], axis=1))
            rows = min(MOE_ROWS, t)
            n_blocks = (t * TOP_K) // rows + N_EXPERTS
            dest, bexp, nvalid, zs, ze = _moe_route_tables(route, counts, rows, n_blocks)
            xb = _dispatch(h_rows, dest, zs, ze, nvalid, n_blocks, rows)
            y = _moe_experts(xb, bexp, nvalid, moe_w_gate[l // 2], moe_w_up[l // 2], moe_w_down[l // 2],
                             n_blocks, rows)
            x2 = _combine(y, dest, x_new, gates, g_f, final_norm_g.reshape(1, d),
                          seq // COMBINE_TOKENS)
    return x2.reshape(batch, seq, d)
```

```python
import functools

import numpy as np
import jax
import jax.numpy as jnp
from jax import lax
from jax.experimental import pallas as pl
from jax.experimental.pallas import tpu as pltpu

F32 = jnp.float32
BF16 = jnp.bfloat16

LANES = 128
SUBLANES = 8
VMEM_LIMIT = 48 * 1024 * 1024

EPS = 1e-6
HEAD_DIM = 64
Q_HEADS = 8
KV_HEADS = 2
GROUP = Q_HEADS // KV_HEADS
ATTN_WIDTH = Q_HEADS * HEAD_DIM
KV_WIDTH = KV_HEADS * HEAD_DIM
WINDOW = 128
ROPE_THETA = 10000.0
SSM_HEADS = 8
SSM_HEAD_DIM = 64
SSM_WIDTH = SSM_HEADS * SSM_HEAD_DIM
SSM_GROUPS = 2
SSM_STATE = 128
GROUP_WIDTH = SSM_WIDTH // SSM_GROUPS
CONV_WIDTH = 4
CONV_CH = SSM_WIDTH + 2 * SSM_GROUPS * SSM_STATE
CHUNK = 128
N_EXPERTS = 8
TOP_K = 2

Q_TILES = ATTN_WIDTH // LANES
C_Q = 0
C_K = C_Q + ATTN_WIDTH
C_V = C_K + KV_WIDTH
C_Z = C_V + KV_WIDTH
C_X = C_Z + SSM_WIDTH
C_DT = C_X + CONV_CH
C_END = C_DT + LANES

ROW_TILE = 512
ATTN_TILE = 512
MOE_ROWS = 1024
MOE_FT = 512
COMBINE_TOKENS = 256
ISSUE_UNROLL = 8
SSD_CHUNKS_PER_STEP = 2
ROW_SPLIT = 2
INPROJ_SPLIT = 1
GATHER_STEPS = 4


def _cparams(*sem):
    return pltpu.CompilerParams(dimension_semantics=sem, vmem_limit_bytes=VMEM_LIMIT)


def _silu(v):
    return v * (1.0 / (1.0 + jnp.exp(-v)))


def _softplus(v):
    return jnp.maximum(v, 0.0) + jnp.log1p(jnp.exp(-jnp.abs(v)))


def _rms_mod(x, g, scale, shift):
    ms = jnp.mean(x * x, axis=-1, keepdims=True)
    return (x * lax.rsqrt(ms + EPS) * g) * (1.0 + scale) + shift


def _ada_kernel(c_ref, w_ref, b_ref, o_ref):
    c = c_ref[...]
    o_ref[0] = jnp.dot(_silu(c), w_ref[0], preferred_element_type=F32,
                       precision=lax.Precision.HIGHEST) + b_ref[0]


def _ada_mod(c8, ada_w, ada_b):
    depth, d, n = ada_w.shape
    tn = 1536
    return pl.pallas_call(
        _ada_kernel,
        grid=(depth, n // tn),
        in_specs=[pl.BlockSpec((SUBLANES, d), lambda l, j: (0, 0)),
                  pl.BlockSpec((1, d, tn), lambda l, j: (l, 0, j)),
                  pl.BlockSpec((1, 1, tn), lambda l, j: (l, 0, j))],
        out_specs=pl.BlockSpec((1, SUBLANES, tn), lambda l, j: (l, 0, j)),
        out_shape=jax.ShapeDtypeStruct((depth, SUBLANES, n), F32),
        compiler_params=_cparams("parallel", "parallel"),
    )(c8, ada_w, ada_b.reshape(depth, 1, n))


ROPE_PER_ROW = LANES // (HEAD_DIM // 2)


def _rope_kernel(pos_ref, inv_ref, sign_ref, cos_ref, sin_ref):
    ang = pos_ref[...].astype(F32) * inv_ref[...]
    tr = ang.shape[0]
    group = lax.broadcasted_iota(jnp.int32, (1, LANES), 1) // (HEAD_DIM // 2)
    for table, out_ref, sign in ((jnp.cos(ang), cos_ref, None), (jnp.sin(ang), sin_ref, sign_ref[...])):
        rolled = [table] + [pltpu.roll(table, (HEAD_DIM // 2) * s, axis=1) for s in range(1, ROPE_PER_ROW)]
        for j in range(ROPE_PER_ROW):
            rep = rolled[(0 - j) % ROPE_PER_ROW]
            for g in range(1, ROPE_PER_ROW):
                rep = jnp.where(group == g, rolled[(g - j) % ROPE_PER_ROW], rep)
            out_ref[pl.ds(j, tr, stride=ROPE_PER_ROW), :] = rep if sign is None else rep * sign


def _rope_tables(positions):
    t = positions.size
    half = HEAD_DIM // 2
    pos_dense = jnp.repeat(positions.reshape(t // ROPE_PER_ROW, ROPE_PER_ROW), half, axis=1)
    inv_freq = ROPE_THETA ** (-jnp.arange(0, HEAD_DIM, 2, dtype=F32) / HEAD_DIM)
    inv_dense = jnp.tile(inv_freq, ROPE_PER_ROW).reshape(1, LANES)
    sign = jnp.repeat(jnp.array([-1.0, -1.0, 1.0, 1.0], F32), half).reshape(1, LANES)
    rows = t // ROPE_PER_ROW
    tr = min(1024, rows)
    return pl.pallas_call(
        _rope_kernel,
        grid=(rows // tr,),
        in_specs=[pl.BlockSpec((tr, LANES), lambda i: (i, 0)),
                  pl.BlockSpec((1, LANES), lambda i: (0, 0)),
                  pl.BlockSpec((1, LANES), lambda i: (0, 0))],
        out_specs=[pl.BlockSpec((tr * ROPE_PER_ROW, LANES), lambda i: (i, 0))] * 2,
        out_shape=[jax.ShapeDtypeStruct((t, LANES), F32)] * 2,
        compiler_params=_cparams("parallel"),
    )(pos_dense, inv_dense, sign)


def _inproj_kernel(x_ref, g_ref, sc_ref, sh_ref, cos_ref, sin_ref, w_ref, dtb_ref,
                   q_ref, kv_ref, z_ref, xbc_ref, dt_ref):
    piece = x_ref.shape[0] // INPROJ_SPLIT
    for r in range(INPROJ_SPLIT):
        rs = slice(r * piece, (r + 1) * piece)
        h = _rms_mod(x_ref[rs, :], g_ref[...], sc_ref[0], sh_ref[0]).astype(BF16)
        cos = cos_ref[rs, :]
        sin = sin_ref[rs, :]

        def rope(t):
            return t * cos + pltpu.roll(t, LANES // 2, axis=1) * sin

        qkv = jnp.dot(h, w_ref[:, C_Q:C_Z], preferred_element_type=F32)
        for i in range(Q_TILES):
            q_ref[rs, i * LANES:(i + 1) * LANES] = (
                rope(qkv[:, i * LANES:(i + 1) * LANES]) * (HEAD_DIM ** -0.5)).astype(BF16)
        kv_ref[rs, 0:LANES] = rope(qkv[:, C_K:C_V]).astype(BF16)
        kv_ref[rs, LANES:2 * LANES] = qkv[:, C_V:C_Z].astype(BF16)
        z_ref[rs, :] = jnp.dot(h, w_ref[:, C_Z:C_X], preferred_element_type=F32)
        xbc_ref[rs, :] = jnp.dot(h, w_ref[:, C_X:C_DT], preferred_element_type=F32)
        dt_raw = jnp.dot(h, w_ref[:, C_DT:C_END], preferred_element_type=F32)
        dt_ref[rs, :] = _softplus(dt_raw + dtb_ref[...])


def _inproj(x2, g, sc, sh, cos, sin, w, dtb, tiles_per_batch):
    t, d = x2.shape
    tm = ROW_TILE
    row = lambda i: (i, 0)
    const = lambda i: (0, 0)
    per_b = lambda i: (i // tiles_per_batch, 0, 0)
    return pl.pallas_call(
        _inproj_kernel,
        grid=(t // tm,),
        in_specs=[pl.BlockSpec((tm, d), row),
                  pl.BlockSpec((1, d), const),
                  pl.BlockSpec((1, 1, d), per_b),
                  pl.BlockSpec((1, 1, d), per_b),
                  pl.BlockSpec((tm, LANES), row),
                  pl.BlockSpec((tm, LANES), row),
                  pl.BlockSpec((d, C_END), const),
                  pl.BlockSpec((1, LANES), const)],
        out_specs=[pl.BlockSpec((tm, ATTN_WIDTH), row),
                   pl.BlockSpec((tm, 2 * KV_WIDTH), row),
                   pl.BlockSpec((tm, SSM_WIDTH), row),
                   pl.BlockSpec((tm, CONV_CH), row),
                   pl.BlockSpec((tm, LANES), row)],
        out_shape=[jax.ShapeDtypeStruct((t, ATTN_WIDTH), BF16),
                   jax.ShapeDtypeStruct((t, 2 * KV_WIDTH), BF16),
                   jax.ShapeDtypeStruct((t, SSM_WIDTH), F32),
                   jax.ShapeDtypeStruct((t, CONV_CH), F32),
                   jax.ShapeDtypeStruct((t, LANES), F32)],
        compiler_params=_cparams("parallel"),
    )(x2, g, sc, sh, cos, sin, w, dtb)


def _attn_kernel(sink_ref, q_ref, kv_ref, kvp_ref, o_ref, *, tiles_per_seq):
    first = (pl.program_id(0) % tiles_per_seq) == 0
    blk = WINDOW
    nsub = ATTN_TILE // blk
    lane = lax.broadcasted_iota(jnp.int32, (1, LANES), 1)
    k_lo_mask = (lane % HEAD_DIM) < (HEAD_DIM // 2)
    v_lo_mask = lane < HEAD_DIM
    zero = jnp.zeros((), BF16)

    k_all = jnp.concatenate([kvp_ref[:, 0:LANES], kv_ref[:, 0:LANES]], axis=0)
    v_all = jnp.concatenate([kvp_ref[:, LANES:2 * LANES], kv_ref[:, LANES:2 * LANES]], axis=0)
    k_sel = (jnp.where(k_lo_mask, k_all, zero), jnp.where(k_lo_mask, zero, k_all))
    v_sel = (jnp.where(v_lo_mask, v_all, zero), jnp.where(v_lo_mask, zero, v_all))

    qi = lax.broadcasted_iota(jnp.int32, (blk, blk), 0)
    col = lax.broadcasted_iota(jnp.int32, (blk, blk), 1)
    cur = col <= qi
    no_prev = (col > qi) & (col > jnp.where(first, -1, blk))

    for n in range(nsub):
        q_st = jnp.concatenate([q_ref[n * blk:(n + 1) * blk, i * LANES:(i + 1) * LANES]
                                for i in range(Q_TILES)], axis=0)
        out = None
        for hk in range(KV_HEADS):
            k_n = k_sel[hk][n * blk:(n + 2) * blk]
            v_n = v_sel[hk][n * blk:(n + 2) * blk]
            s = lax.dot_general(q_st, k_n, (((1,), (1,)), ((), ())), preferred_element_type=F32)
            probs, scales = [], []
            for i in range(GROUP):
                s_i = s[i * blk:(i + 1) * blk]
                sc = jnp.where(cur, s_i[:, blk:], s_i[:, :blk])
                if n == 0:
                    sc = jnp.where(no_prev, -jnp.inf, sc)
                sink = sink_ref[hk * GROUP + i]
                m = jnp.maximum(jnp.max(sc, axis=-1, keepdims=True), sink)
                p = jnp.exp(sc - m)
                scales.append(1.0 / (jnp.sum(p, axis=-1, keepdims=True) + jnp.exp(sink - m)))
                probs.append(jnp.concatenate([jnp.where(cur, 0.0, p), jnp.where(cur, p, 0.0)],
                                             axis=1).astype(BF16))
            o = jnp.dot(jnp.concatenate(probs, axis=0), v_n, preferred_element_type=F32)
            o = o * jnp.concatenate(scales, axis=0)
            out = o if out is None else out + o
        for i in range(Q_TILES):
            o_ref[n * blk:(n + 1) * blk, i * LANES:(i + 1) * LANES] = out[i * blk:(i + 1) * blk].astype(BF16)


def _attention(q, kv, sinks, seq):
    t = q.shape[0]
    tq = ATTN_TILE
    r = tq // WINDOW
    kern = functools.partial(_attn_kernel, tiles_per_seq=seq // tq)
    return pl.pallas_call(
        kern,
        grid=(t // tq,),
        in_specs=[pl.BlockSpec(memory_space=pltpu.SMEM),
                  pl.BlockSpec((tq, ATTN_WIDTH), lambda i: (i, 0)),
                  pl.BlockSpec((tq, 2 * KV_WIDTH), lambda i: (i, 0)),
                  pl.BlockSpec((WINDOW, 2 * KV_WIDTH), lambda i: (jnp.maximum(i * r - 1, 0), 0))],
        out_specs=pl.BlockSpec((tq, ATTN_WIDTH), lambda i: (i, 0)),
        out_shape=jax.ShapeDtypeStruct((t, ATTN_WIDTH), BF16),
        compiler_params=_cparams("parallel"),
    )(sinks, q, kv, kv)


def _split3(v):
    p0 = v.astype(BF16)
    r1 = v - p0.astype(F32)
    p1 = r1.astype(BF16)
    return p0, p1, (r1 - p1.astype(F32)).astype(BF16)


def _dot3(lhs_exact_bf16, v):
    return sum(jnp.dot(lhs_exact_bf16, p, preferred_element_type=F32) for p in _split3(v))


def _ssd_kernel(xbc_ref, z_ref, dt_ref, cw_ref, cb_ref, alog_ref, dskip_ref, ng_ref,
                o_ref, ext_ref, st_ref):
    L = CHUNK
    rows = xbc_ref.shape[0]
    halo = SUBLANES

    @pl.when(pl.program_id(1) == 0)
    def _():
        ext_ref[0:halo, :] = jnp.zeros((halo, CONV_CH), F32)
        st_ref[...] = jnp.zeros(st_ref.shape, F32)

    ext_ref[halo:halo + rows, :] = xbc_ref[...]
    acc = cb_ref[...] + cw_ref[CONV_WIDTH - 1:CONV_WIDTH, :] * ext_ref[halo:halo + rows, :]
    for k in range(CONV_WIDTH - 1):
        off = halo - (CONV_WIDTH - 1) + k
        acc = acc + cw_ref[k:k + 1, :] * ext_ref[off:off + rows, :]
    ext_ref[0:halo, :] = ext_ref[rows:rows + halo, :]
    u_all = _silu(acc)

    lane = lax.broadcasted_iota(jnp.int32, (1, LANES), 1)
    a = jnp.where(lane < SSM_HEADS, -jnp.exp(alog_ref[...]), 0.0)
    row = lax.broadcasted_iota(jnp.int32, (L, L), 0)
    col = lax.broadcasted_iota(jnp.int32, (L, L), 1)
    causal = row >= col
    tri = jnp.where(causal, 1.0, 0.0).astype(BF16)
    spread = jnp.where(lax.broadcasted_iota(jnp.int32, (LANES, SSM_WIDTH), 1) // SSM_HEAD_DIM
                       == lax.broadcasted_iota(jnp.int32, (LANES, SSM_WIDTH), 0), 1.0, 0.0).astype(BF16)
    r_heads = SSM_HEADS // SSM_GROUPS
    glane = lax.broadcasted_iota(jnp.int32, (1, GROUP_WIDTH), 1) // SSM_HEAD_DIM
    zero = jnp.zeros((), BF16)
    states = [st_ref[g] for g in range(SSM_GROUPS)]

    for c in range(rows // L):
        rs = slice(c * L, (c + 1) * L)
        u = u_all[rs]
        xs = u[:, 0:SSM_WIDTH]
        bm = u[:, SSM_WIDTH:SSM_WIDTH + SSM_GROUPS * SSM_STATE]
        cm = u[:, SSM_WIDTH + SSM_GROUPS * SSM_STATE:]
        dt = dt_ref[rs, :]
        acs = _dot3(tri, dt * a)
        acs_t = acs.T
        dt_e = sum(jnp.dot(p, spread, preferred_element_type=F32) for p in _split3(dt))
        acs_e = sum(jnp.dot(p, spread, preferred_element_type=F32) for p in _split3(acs))
        last = acs_e[L - 1:L, :]
        xd = xs * dt_e
        xd_b = xd.astype(BF16)
        xdw_b = (xd * jnp.exp(last - acs_e)).astype(BF16)
        e_acs = jnp.exp(acs_e)
        c_dec = jnp.exp(last)

        ys = []
        for g in range(SSM_GROUPS):
            gs = slice(g * GROUP_WIDTH, (g + 1) * GROUP_WIDTH)
            b_g = bm[:, g * SSM_STATE:(g + 1) * SSM_STATE]
            c_b = cm[:, g * SSM_STATE:(g + 1) * SSM_STATE].astype(BF16)
            cb = lax.dot_general(c_b, b_g.astype(BF16), (((1,), (1,)), ((), ())),
                                 preferred_element_type=F32)
            st = states[g]
            y_g = jnp.dot(c_b, st.astype(BF16), preferred_element_type=F32) * e_acs[:, gs]
            xd_g = xd_b[:, gs]
            for r in range(r_heads):
                h = g * r_heads + r
                seg = acs[:, h:h + 1] - acs_t[h:h + 1, :]
                m_h = (cb * jnp.exp(jnp.where(causal, seg, -jnp.inf))).astype(BF16)
                y_g = y_g + jnp.dot(m_h, jnp.where(glane == r, xd_g, zero), preferred_element_type=F32)
            new = jnp.dot(b_g.T.astype(BF16), xdw_b[:, gs], preferred_element_type=F32)
            states[g] = c_dec[:, gs] * st + new
            ys.append(y_g)

        y = jnp.concatenate(ys, axis=1) + dskip_ref[...] * xs
        y = y * _silu(z_ref[rs, :])
        outs = []
        for g in range(SSM_GROUPS):
            yg = y[:, g * GROUP_WIDTH:(g + 1) * GROUP_WIDTH]
            outs.append(yg * lax.rsqrt(jnp.mean(yg * yg, axis=-1, keepdims=True) + EPS))
        o_ref[rs, :] = (jnp.concatenate(outs, axis=1) * ng_ref[...]).astype(BF16)

    for g in range(SSM_GROUPS):
        st_ref[g] = states[g]


def _ssd(xbc, z, dt, cw, cb, alog, dskip, ng, batch, seq):
    t = xbc.shape[0]
    rows = SSD_CHUNKS_PER_STEP * CHUNK
    ns = seq // rows
    row = lambda b, c: (b * ns + c, 0)
    const = lambda b, c: (0, 0)
    return pl.pallas_call(
        _ssd_kernel,
        grid=(batch, ns),
        in_specs=[pl.BlockSpec((rows, CONV_CH), row),
                  pl.BlockSpec((rows, SSM_WIDTH), row),
                  pl.BlockSpec((rows, LANES), row),
                  pl.BlockSpec((CONV_WIDTH, CONV_CH), const),
                  pl.BlockSpec((1, CONV_CH), const),
                  pl.BlockSpec((1, LANES), const),
                  pl.BlockSpec((1, SSM_WIDTH), const),
                  pl.BlockSpec((1, SSM_WIDTH), const)],
        out_specs=pl.BlockSpec((rows, SSM_WIDTH), row),
        out_shape=jax.ShapeDtypeStruct((t, SSM_WIDTH), BF16),
        scratch_shapes=[pltpu.VMEM((SUBLANES + rows, CONV_CH), F32),
                        pltpu.VMEM((SSM_GROUPS, SSM_STATE, GROUP_WIDTH), F32)],
        compiler_params=_cparams("parallel", "arbitrary"),
    )(xbc, z, dt, cw, cb, alog, dskip, ng)


def _outproj_core(a_ref, s_ref, x_ref, w_ref, gm_ref, g_ref, sc_ref, sh_ref):
    half = a_ref.shape[1]
    mixed = (jnp.dot(a_ref[...], w_ref[0:half, :], preferred_element_type=F32)
             + jnp.dot(s_ref[...], w_ref[half:, :], preferred_element_type=F32))
    x_new = x_ref[...] + gm_ref[0] * mixed
    return x_new, _rms_mod(x_new, g_ref[...], sc_ref[0], sh_ref[0])


def _outproj_kernel(a_ref, s_ref, x_ref, w_ref, gm_ref, g_ref, sc_ref, sh_ref, xo_ref, h_ref):
    x_new, h = _outproj_core(a_ref, s_ref, x_ref, w_ref, gm_ref, g_ref, sc_ref, sh_ref)
    xo_ref[...] = x_new
    h_ref[...] = h.astype(BF16)


def _outproj_router_kernel(a_ref, s_ref, x_ref, w_ref, gm_ref, g_ref, sc_ref, sh_ref, rw_ref,
                           xo_ref, h_ref, route_ref, gate_ref, cnt_ref, carry_ref):
    x_new, h = _outproj_core(a_ref, s_ref, x_ref, w_ref, gm_ref, g_ref, sc_ref, sh_ref)
    xo_ref[...] = x_new
    tm = h.shape[0]
    for s in range(h.shape[1] // LANES):
        h_ref[pl.ds(s, tm, stride=SUBLANES), :] = h[:, s * LANES:(s + 1) * LANES]

    h_hi = h.astype(BF16)
    h_lo = (h - h_hi.astype(F32)).astype(BF16)
    hi_part = jnp.dot(h_hi, rw_ref[...], preferred_element_type=F32)
    logits = (hi_part[:, 0:LANES] + hi_part[:, LANES:]
              + jnp.dot(h_lo, rw_ref[:, 0:LANES], preferred_element_type=F32))
    lane = lax.broadcasted_iota(jnp.int32, (tm, LANES), 1).astype(F32)
    logits = jnp.where(lane < N_EXPERTS, logits, -jnp.inf)
    m0 = jnp.max(logits, axis=-1, keepdims=True)
    i0 = jnp.min(jnp.where(logits == m0, lane, float(LANES)), axis=-1, keepdims=True)
    rest = jnp.where(lane == i0, -jnp.inf, logits)
    m1 = jnp.max(rest, axis=-1, keepdims=True)
    i1 = jnp.min(jnp.where(rest == m1, lane, float(LANES)), axis=-1, keepdims=True)
    e = jnp.exp(m1 - m0)
    g0 = 1.0 / (1.0 + e)
    g1 = e / (1.0 + e)

    @pl.when(pl.program_id(0) == 0)
    def _():
        carry_ref[...] = jnp.zeros(carry_ref.shape, F32)

    sel0 = lane == i0
    sel1 = lane == i1
    member = jnp.where(sel0 | sel1, 1.0, 0.0)
    r_i = lax.broadcasted_iota(jnp.int32, (tm, tm), 0)
    c_i = lax.broadcasted_iota(jnp.int32, (tm, tm), 1)
    strict = jnp.where(r_i > c_i, 1.0, 0.0).astype(BF16)
    rank = carry_ref[...] + jnp.dot(strict, member.astype(BF16), preferred_element_type=F32)
    r0 = jnp.sum(jnp.where(sel0, rank, 0.0), axis=-1, keepdims=True)
    r1 = jnp.sum(jnp.where(sel1, rank, 0.0), axis=-1, keepdims=True)
    carry_ref[...] = carry_ref[...] + jnp.sum(member, axis=0, keepdims=True)
    cnt_ref[...] = carry_ref[...]

    gate_ref[...] = jnp.where(lane == 0.0, g0, g1)[:, 0:SUBLANES]
    diag = (lax.broadcasted_iota(jnp.int32, (tm, LANES), 0) % LANES
            == lax.broadcasted_iota(jnp.int32, (tm, LANES), 1))
    nrow = tm // LANES
    for q, colv in enumerate((i0, i1, r0, r1)):
        spread = jnp.where(diag, colv, 0.0)
        rows = [jnp.sum(spread[b * LANES:(b + 1) * LANES], axis=0, keepdims=True) for b in range(nrow)]
        route_ref[0, q * nrow:(q + 1) * nrow, :] = jnp.concatenate(rows, axis=0).astype(jnp.int32)


def _outproj(attn, ssm, x2, w, gm, g, sc, sh, tiles_per_batch, router=None):
    t, d = x2.shape
    tm = ROW_TILE
    row = lambda i: (i, 0)
    const = lambda i: (0, 0)
    per_b = lambda i: (i // tiles_per_batch, 0, 0)
    half = attn.shape[1]
    in_specs = [pl.BlockSpec((tm, half), row),
                pl.BlockSpec((tm, half), row),
                pl.BlockSpec((tm, d), row),
                pl.BlockSpec((2 * half, d), const),
                pl.BlockSpec((1, 1, d), per_b),
                pl.BlockSpec((1, d), const),
                pl.BlockSpec((1, 1, d), per_b),
                pl.BlockSpec((1, 1, d), per_b)]
    if router is None:
        return pl.pallas_call(
            _outproj_kernel,
            grid=(t // tm,),
            in_specs=in_specs,
            out_specs=[pl.BlockSpec((tm, d), row), pl.BlockSpec((tm, d), row)],
            out_shape=[jax.ShapeDtypeStruct((t, d), F32), jax.ShapeDtypeStruct((t, d), BF16)],
            compiler_params=_cparams("parallel"),
        )(attn, ssm, x2, w, gm, g, sc, sh)
    return pl.pallas_call(
        _outproj_router_kernel,
        grid=(t // tm,),
        in_specs=in_specs + [pl.BlockSpec((d, 2 * LANES), const)],
        out_specs=[pl.BlockSpec((tm, d), row),
                   pl.BlockSpec((tm * SUBLANES, LANES), row),
                   pl.BlockSpec((1, 4 * tm // LANES, LANES), lambda i: (i, 0, 0)),
                   pl.BlockSpec((tm, SUBLANES), row),
                   pl.BlockSpec((1, LANES), const)],
        out_shape=[jax.ShapeDtypeStruct((t, d), F32),
                   jax.ShapeDtypeStruct((t * SUBLANES, LANES), F32),
                   jax.ShapeDtypeStruct((t // tm, 4 * tm // LANES, LANES), jnp.int32),
                   jax.ShapeDtypeStruct((t, SUBLANES), F32),
                   jax.ShapeDtypeStruct((1, LANES), F32)],
        scratch_shapes=[pltpu.VMEM((1, LANES), F32)],
        compiler_params=_cparams("arbitrary"),
    )(attn, ssm, x2, w, gm, g, sc, sh, router)


def _swiglu_rows(x, wg, wu, wd):
    gate = jnp.dot(x, wg, preferred_element_type=F32)
    up = jnp.dot(x, wu, preferred_element_type=F32)
    return jnp.dot((_silu(gate) * up).astype(BF16), wd, preferred_element_type=F32)


def _ffn_kernel(h_ref, x_ref, wg_ref, wu_ref, wd_ref, gf_ref, o_ref):
    piece = h_ref.shape[0] // ROW_SPLIT
    for r in range(ROW_SPLIT):
        rs = slice(r * piece, (r + 1) * piece)
        f = _swiglu_rows(h_ref[rs, :], wg_ref[...], wu_ref[...], wd_ref[...])
        o_ref[rs, :] = x_ref[rs, :] + gf_ref[0] * f


def _ffn(h, x2, wg, wu, wd, gf, tiles_per_batch):
    t, d = x2.shape
    f = wg.shape[1]
    tm = ROW_TILE
    row = lambda i: (i, 0)
    const = lambda i: (0, 0)
    resident = pl.Buffered(1)
    return pl.pallas_call(
        _ffn_kernel,
        grid=(t // tm,),
        in_specs=[pl.BlockSpec((tm, d), row),
                  pl.BlockSpec((tm, d), row),
                  pl.BlockSpec((d, f), const, pipeline_mode=resident),
                  pl.BlockSpec((d, f), const, pipeline_mode=resident),
                  pl.BlockSpec((f, d), const, pipeline_mode=resident),
                  pl.BlockSpec((1, 1, d), lambda i: (i // tiles_per_batch, 0, 0))],
        out_specs=pl.BlockSpec((tm, d), row),
        out_shape=jax.ShapeDtypeStruct((t, d), F32),
        compiler_params=_cparams("parallel"),
    )(h, x2, wg, wu, wd, gf)


def _row_copy(src, src_row, dst, dst_row, sem):
    return pltpu.make_async_copy(
        src.at[pl.ds(pl.multiple_of(src_row * SUBLANES, SUBLANES), SUBLANES)],
        dst.at[pl.ds(pl.multiple_of(dst_row * SUBLANES, SUBLANES), SUBLANES)], sem)


def _moe_kernel(bexp_ref, nvalid_ref, tok_ref, tokn_ref, h_hbm, wg_ref, wu_ref, wd_ref, y_ref,
                xbuf_ref, x_scr, acc_ref, sem):
    b = pl.program_id(0)
    j = pl.program_id(1)
    nf = pl.num_programs(1)
    rows = x_scr.shape[0]
    nsl = x_scr.shape[1] // LANES
    valid = b < nvalid_ref[0]
    slot = b % 2

    def gather(src_tok_ref, to_slot, lo, hi):
        def issue(r, c):
            _row_copy(h_hbm, src_tok_ref[r], xbuf_ref.at[to_slot], r, sem.at[to_slot]).start()
            return c
        lax.fori_loop(lo, hi, issue, 0, unroll=ISSUE_UNROLL)

    @pl.when((b == 0) & (j == 0))
    def _():
        gather(tok_ref, 0, 0, rows)

    per = rows // GATHER_STEPS
    for s in range(GATHER_STEPS):
        @pl.when((j == s) & (b + 1 < nvalid_ref[0]))
        def _():
            gather(tokn_ref, 1 - slot, s * per, (s + 1) * per)

    @pl.when(valid & (j == 0))
    def _():
        pltpu.make_async_copy(h_hbm.at[pl.ds(0, rows * SUBLANES)], xbuf_ref.at[slot], sem.at[slot]).wait()
        for s in range(nsl):
            x_scr[:, s * LANES:(s + 1) * LANES] = xbuf_ref[slot, pl.ds(s, rows, stride=SUBLANES), :].astype(BF16)
        acc_ref[...] = jnp.zeros(acc_ref.shape, F32)

    @pl.when(valid)
    def _():
        wg = wg_ref[0].astype(BF16)
        wu = wu_ref[0].astype(BF16)
        wd = wd_ref[0].astype(BF16)
        piece = rows // ROW_SPLIT
        for r in range(ROW_SPLIT):
            rs = slice(r * piece, (r + 1) * piece)
            acc_ref[rs, :] += _swiglu_rows(x_scr[rs, :], wg, wu, wd)

    @pl.when(valid & (j == nf - 1))
    def _():
        for s in range(nsl):
            y_ref[pl.ds(s, rows, stride=SUBLANES), :] = acc_ref[:, s * LANES:(s + 1) * LANES]

    @pl.when(jnp.logical_not(valid) & (j == nf - 1))
    def _():
        y_ref[...] = jnp.zeros(y_ref.shape, F32)


def _moe_experts(h_rows, row_tok, bexp, nvalid, wg, wu, wd, n_blocks, rows):
    d = wg.shape[1]
    f = wg.shape[2]
    nf = f // MOE_FT
    assert GATHER_STEPS <= nf

    def fidx(b, j, nvalid):
        return jnp.where(b < nvalid[0], j, nf - 1)

    grid_spec = pltpu.PrefetchScalarGridSpec(
        num_scalar_prefetch=2,
        grid=(n_blocks, nf),
        in_specs=[pl.BlockSpec((rows,), lambda b, j, bexp, nvalid: (jnp.minimum(b, nvalid[0] - 1),),
                               memory_space=pltpu.SMEM),
                  pl.BlockSpec((rows,), lambda b, j, bexp, nvalid: (jnp.minimum(b + 1, nvalid[0] - 1),),
                               memory_space=pltpu.SMEM),
                  pl.BlockSpec(memory_space=pl.ANY),
                  pl.BlockSpec((1, d, MOE_FT), lambda b, j, bexp, nvalid: (bexp[b], 0, fidx(b, j, nvalid))),
                  pl.BlockSpec((1, d, MOE_FT), lambda b, j, bexp, nvalid: (bexp[b], 0, fidx(b, j, nvalid))),
                  pl.BlockSpec((1, MOE_FT, d), lambda b, j, bexp, nvalid: (bexp[b], fidx(b, j, nvalid), 0))],
        out_specs=pl.BlockSpec((rows * SUBLANES, LANES), lambda b, j, bexp, nvalid: (b, 0)),
        scratch_shapes=[pltpu.VMEM((2, rows * SUBLANES, LANES), F32),
                        pltpu.VMEM((rows, d), BF16), pltpu.VMEM((rows, d), F32),
                        pltpu.SemaphoreType.DMA((2,))],
    )
    return pl.pallas_call(
        _moe_kernel,
        grid_spec=grid_spec,
        out_shape=jax.ShapeDtypeStruct((n_blocks * rows * SUBLANES, LANES), F32),
        compiler_params=_cparams("arbitrary", "arbitrary"),
    )(bexp, nvalid, row_tok, row_tok, h_rows, wg, wu, wd)


def _combine_kernel(d0_ref, d1_ref, d0n_ref, d1n_ref, y_hbm, x_ref, gate_ref, gf_ref, fg_ref, o_ref,
                    buf_ref, sem):
    i = pl.program_id(0)
    tm = x_ref.shape[0]
    nsl = x_ref.shape[1] // LANES
    slot = i % 2

    def gather(a_ref, b_ref, to_slot):
        def issue(t, c):
            _row_copy(y_hbm, a_ref[t], buf_ref.at[to_slot], t, sem.at[to_slot]).start()
            _row_copy(y_hbm, b_ref[t], buf_ref.at[to_slot], tm + t, sem.at[to_slot]).start()
            return c
        lax.fori_loop(0, tm, issue, 0, unroll=ISSUE_UNROLL)

    @pl.when(i == 0)
    def _():
        gather(d0_ref, d1_ref, 0)

    @pl.when(i + 1 < pl.num_programs(0))
    def _():
        gather(d0n_ref, d1n_ref, 1 - slot)

    rows = tm * TOP_K * SUBLANES
    pltpu.make_async_copy(y_hbm.at[pl.ds(0, rows)], buf_ref.at[slot], sem.at[slot]).wait()

    g0 = gate_ref[:, 0:1]
    g1 = gate_ref[:, 1:2]
    parts = []
    for s in range(nsl):
        y0 = buf_ref[slot, pl.ds(s, tm, stride=SUBLANES), :]
        y1 = buf_ref[slot, pl.ds(tm * SUBLANES + s, tm, stride=SUBLANES), :]
        parts.append(g0 * y0 + g1 * y1)
    x = x_ref[...] + gf_ref[0] * jnp.concatenate(parts, axis=1)
    ms = jnp.mean(x * x, axis=-1, keepdims=True)
    o_ref[...] = x * lax.rsqrt(ms + EPS) * fg_ref[...]


def _combine(y, dest, x2, gates, gf, fg, tiles_per_batch):
    t, d = x2.shape
    tm = COMBINE_TOKENS
    n = t // tm
    cur = pl.BlockSpec((tm,), lambda i: (i,), memory_space=pltpu.SMEM)
    nxt = pl.BlockSpec((tm,), lambda i: (jnp.minimum(i + 1, n - 1),), memory_space=pltpu.SMEM)
    return pl.pallas_call(
        _combine_kernel,
        grid=(n,),
        in_specs=[cur, cur, nxt, nxt,
                  pl.BlockSpec(memory_space=pl.ANY),
                  pl.BlockSpec((tm, d), lambda i: (i, 0)),
                  pl.BlockSpec((tm, SUBLANES), lambda i: (i, 0)),
                  pl.BlockSpec((1, 1, d), lambda i: (i // tiles_per_batch, 0, 0)),
                  pl.BlockSpec((1, d), lambda i: (0, 0))],
        out_specs=pl.BlockSpec((tm, d), lambda i: (i, 0)),
        out_shape=jax.ShapeDtypeStruct((t, d), F32),
        scratch_shapes=[pltpu.VMEM((2, tm * TOP_K * SUBLANES, LANES), F32), pltpu.SemaphoreType.DMA((2,))],
        compiler_params=_cparams("arbitrary"),
    )(dest[0], dest[1], dest[0], dest[1], y, x2, gates, gf, fg)


def _final_norm_kernel(x_ref, g_ref, o_ref):
    x = x_ref[...]
    o_ref[...] = x * lax.rsqrt(jnp.mean(x * x, axis=-1, keepdims=True) + EPS) * g_ref[...]


def _final_norm(x2, g):
    t, d = x2.shape
    tm = ROW_TILE
    return pl.pallas_call(
        _final_norm_kernel,
        grid=(t // tm,),
        in_specs=[pl.BlockSpec((tm, d), lambda i: (i, 0)), pl.BlockSpec((1, d), lambda i: (0, 0))],
        out_specs=pl.BlockSpec((tm, d), lambda i: (i, 0)),
        out_shape=jax.ShapeDtypeStruct((t, d), F32),
        compiler_params=_cparams("parallel"),
    )(x2, g)


def _qk_column_perm():
    half = HEAD_DIM // 2
    lane = np.arange(LANES)
    pair = (lane % HEAD_DIM) // half
    dim = lane % half + half * (lane // HEAD_DIM)
    q = np.concatenate([(i + Q_TILES * pair) * HEAD_DIM + dim for i in range(Q_TILES)])
    k = ATTN_WIDTH + pair * HEAD_DIM + dim
    return q, k


def _attn_out_row_perm():
    lane = np.arange(LANES)
    return np.concatenate([(i + Q_TILES * (lane // HEAD_DIM)) * HEAD_DIM + lane % HEAD_DIM
                           for i in range(Q_TILES)])


def _pad_lanes(v):
    return jnp.pad(v.astype(F32), (0, LANES - v.shape[0])).reshape(1, LANES)


def _moe_route_tables(route, counts, rows, n_blocks):
    nt = route.shape[0]
    e0, e1, r0, r1 = (route.reshape(nt, 4, -1)[:, q].reshape(-1) for q in range(4))
    cnt = counts[0, :N_EXPERTS].astype(jnp.int32)
    padded = (cnt + rows - 1) // rows * rows
    pad_end = jnp.cumsum(padded)
    pad_start = pad_end - padded

    def first_row(e):
        return sum(jnp.where(e == k, pad_start[k], 0) for k in range(N_EXPERTS))

    dest = (first_row(e0) + r0).astype(jnp.int32), (first_row(e1) + r1).astype(jnp.int32)
    blk_start = jnp.arange(n_blocks, dtype=jnp.int32) * rows
    bexp = jnp.minimum(jnp.sum(blk_start[:, None] >= pad_end[None, :], axis=1), N_EXPERTS - 1).astype(jnp.int32)
    nvalid = (pad_end[-1:] // rows).astype(jnp.int32)
    t = e0.shape[0]
    tok = jnp.arange(t, dtype=jnp.int32)
    row_tok = jnp.zeros((n_blocks * rows,), jnp.int32).at[jnp.concatenate(dest)].set(
        jnp.concatenate([tok, tok]), unique_indices=True)
    return dest, row_tok, bexp, nvalid


def kernel(x, c, positions, ada_w, ada_b, norm_mix_g, norm_ffn_g, w_in, w_out, attn_sinks, conv_w, conv_b,
           dt_bias, a_log, d_skip, ssm_norm_g, ffn_w_gate, ffn_w_up, ffn_w_down, router_w, moe_w_gate,
           moe_w_up, moe_w_down, final_norm_g):
    batch, seq, d = x.shape
    depth = w_in.shape[0]
    t = batch * seq
    tiles_per_batch = seq // ROW_TILE
    x2 = x.reshape(t, d)

    c8 = jnp.pad(c, ((0, SUBLANES - batch), (0, 0)))
    mod = _ada_mod(c8, ada_w, ada_b)[:, :batch].reshape(depth, batch, 6, 1, d)
    cos, sin = _rope_tables(positions)

    q_perm, k_perm = _qk_column_perm()
    o_perm = _attn_out_row_perm()
    for l in range(depth):
        sh_m, sc_m, g_m, sh_f, sc_f, g_f = (mod[l, :, k] for k in range(6))
        wl = w_in[l]
        w_cat = jnp.concatenate(
            [wl[:, q_perm], wl[:, k_perm], wl[:, C_V:C_DT],
             jnp.pad(wl[:, C_DT:], ((0, 0), (0, LANES - SSM_HEADS)))], axis=1).astype(BF16)
        q, kv, z, xbc, dt = _inproj(x2, norm_mix_g[l].reshape(1, d), sc_m, sh_m, cos, sin, w_cat,
                                    _pad_lanes(dt_bias[l]), tiles_per_batch)
        attn = _attention(q, kv, attn_sinks[l].astype(F32), seq)
        ssm = _ssd(xbc, z, dt, conv_w[l], conv_b[l].reshape(1, CONV_CH), _pad_lanes(a_log[l]),
                   jnp.repeat(d_skip[l].astype(F32), SSM_HEAD_DIM).reshape(1, SSM_WIDTH),
                   ssm_norm_g[l].reshape(1, SSM_WIDTH), batch, seq)
        wo = jnp.concatenate([w_out[l][o_perm], w_out[l][ATTN_WIDTH:]], axis=0).astype(BF16)
        ffn_g = norm_ffn_g[l].reshape(1, d)
        if l % 2 == 0:
            x_new, h = _outproj(attn, ssm, x2, wo, g_m, ffn_g, sc_f, sh_f, tiles_per_batch)
            x2 = _ffn(h, x_new, ffn_w_gate[l // 2].astype(BF16), ffn_w_up[l // 2].astype(BF16),
                      ffn_w_down[l // 2].astype(BF16), g_f, tiles_per_batch)
            if l == depth - 1:
                x2 = _final_norm(x2, final_norm_g.reshape(1, d))
        else:
            if l != depth - 1:
                raise NotImplementedError("the expert layer fuses the final norm and must be last")
            rw = jnp.pad(router_w[l // 2].astype(F32), ((0, 0), (0, LANES - N_EXPERTS)))
            rw_hi = rw.astype(BF16)
            rw_lo = (rw - rw_hi.astype(F32)).astype(BF16)
            x_new, h_rows, route, gates, counts = _outproj(
                attn, ssm, x2, wo, g_m, ffn_g, sc_f, sh_f, tiles_per_batch,
                router=jnp.concatenate([rw_hi, rw_lo], axis=1))
            rows = min(MOE_ROWS, t)
            n_blocks = (t * TOP_K) // rows + N_EXPERTS
            dest, row_tok, bexp, nvalid = _moe_route_tables(route, counts, rows, n_blocks)
            y = _moe_experts(h_rows, row_tok, bexp, nvalid, moe_w_gate[l // 2], moe_w_up[l // 2],
                             moe_w_down[l // 2], n_blocks, rows)
            x2 = _combine(y, dest, x_new, gates, g_f, final_norm_g.reshape(1, d),
                          seq // COMBINE_TOKENS)
    return x2.reshape(batch, seq, d)
```

```python
import functools

import numpy as np
import jax
import jax.numpy as jnp
from jax import lax
from jax.experimental import pallas as pl
from jax.experimental.pallas import tpu as pltpu

F32 = jnp.float32
BF16 = jnp.bfloat16

LANES = 128
SUBLANES = 8
VMEM_LIMIT = 48 * 1024 * 1024

EPS = 1e-6
HEAD_DIM = 64
Q_HEADS = 8
KV_HEADS = 2
GROUP = Q_HEADS // KV_HEADS
ATTN_WIDTH = Q_HEADS * HEAD_DIM
KV_WIDTH = KV_HEADS * HEAD_DIM
WINDOW = 128
ROPE_THETA = 10000.0
SSM_HEADS = 8
SSM_HEAD_DIM = 64
SSM_WIDTH = SSM_HEADS * SSM_HEAD_DIM
SSM_GROUPS = 2
SSM_STATE = 128
GROUP_WIDTH = SSM_WIDTH // SSM_GROUPS
CONV_WIDTH = 4
CONV_CH = SSM_WIDTH + 2 * SSM_GROUPS * SSM_STATE
CHUNK = 128
N_EXPERTS = 8
TOP_K = 2

Q_TILES = ATTN_WIDTH // LANES
C_Q = 0
C_K = C_Q + ATTN_WIDTH
C_V = C_K + KV_WIDTH
C_Z = C_V + KV_WIDTH
C_X = C_Z + SSM_WIDTH
C_DT = C_X + CONV_CH
C_END = C_DT + LANES

ROW_TILE = 512
ATTN_TILE = 512
MOE_ROWS = 1024
MOE_FT = 512
DISPATCH_TOKENS = 2048
COMBINE_TOKENS = 256
ISSUE_UNROLL = 8
SSD_CHUNKS_PER_STEP = 4
ROW_SPLIT = 2


def _cparams(*sem):
    return pltpu.CompilerParams(dimension_semantics=sem, vmem_limit_bytes=VMEM_LIMIT)


def _silu(v):
    return v * (1.0 / (1.0 + jnp.exp(-v)))


def _softplus(v):
    return jnp.maximum(v, 0.0) + jnp.log1p(jnp.exp(-jnp.abs(v)))


def _rms_mod(x, g, scale, shift):
    ms = jnp.mean(x * x, axis=-1, keepdims=True)
    return (x * lax.rsqrt(ms + EPS) * g) * (1.0 + scale) + shift


def _ada_kernel(c_ref, w_ref, b_ref, o_ref):
    c = c_ref[...]
    o_ref[0] = jnp.dot(_silu(c), w_ref[0], preferred_element_type=F32,
                       precision=lax.Precision.HIGHEST) + b_ref[0]


def _ada_mod(c8, ada_w, ada_b):
    depth, d, n = ada_w.shape
    tn = 1536
    return pl.pallas_call(
        _ada_kernel,
        grid=(depth, n // tn),
        in_specs=[pl.BlockSpec((SUBLANES, d), lambda l, j: (0, 0)),
                  pl.BlockSpec((1, d, tn), lambda l, j: (l, 0, j)),
                  pl.BlockSpec((1, 1, tn), lambda l, j: (l, 0, j))],
        out_specs=pl.BlockSpec((1, SUBLANES, tn), lambda l, j: (l, 0, j)),
        out_shape=jax.ShapeDtypeStruct((depth, SUBLANES, n), F32),
        compiler_params=_cparams("parallel", "parallel"),
    )(c8, ada_w, ada_b.reshape(depth, 1, n))


ROPE_PER_ROW = LANES // (HEAD_DIM // 2)


def _rope_kernel(pos_ref, inv_ref, sign_ref, cos_ref, sin_ref):
    ang = pos_ref[...].astype(F32) * inv_ref[...]
    tr = ang.shape[0]
    group = lax.broadcasted_iota(jnp.int32, (1, LANES), 1) // (HEAD_DIM // 2)
    for table, out_ref, sign in ((jnp.cos(ang), cos_ref, None), (jnp.sin(ang), sin_ref, sign_ref[...])):
        rolled = [table] + [pltpu.roll(table, (HEAD_DIM // 2) * s, axis=1) for s in range(1, ROPE_PER_ROW)]
        for j in range(ROPE_PER_ROW):
            rep = rolled[(0 - j) % ROPE_PER_ROW]
            for g in range(1, ROPE_PER_ROW):
                rep = jnp.where(group == g, rolled[(g - j) % ROPE_PER_ROW], rep)
            out_ref[pl.ds(j, tr, stride=ROPE_PER_ROW), :] = rep if sign is None else rep * sign


def _rope_tables(positions):
    t = positions.size
    half = HEAD_DIM // 2
    pos_dense = jnp.repeat(positions.reshape(t // ROPE_PER_ROW, ROPE_PER_ROW), half, axis=1)
    inv_freq = ROPE_THETA ** (-jnp.arange(0, HEAD_DIM, 2, dtype=F32) / HEAD_DIM)
    inv_dense = jnp.tile(inv_freq, ROPE_PER_ROW).reshape(1, LANES)
    sign = jnp.repeat(jnp.array([-1.0, -1.0, 1.0, 1.0], F32), half).reshape(1, LANES)
    rows = t // ROPE_PER_ROW
    tr = min(1024, rows)
    return pl.pallas_call(
        _rope_kernel,
        grid=(rows // tr,),
        in_specs=[pl.BlockSpec((tr, LANES), lambda i: (i, 0)),
                  pl.BlockSpec((1, LANES), lambda i: (0, 0)),
                  pl.BlockSpec((1, LANES), lambda i: (0, 0))],
        out_specs=[pl.BlockSpec((tr * ROPE_PER_ROW, LANES), lambda i: (i, 0))] * 2,
        out_shape=[jax.ShapeDtypeStruct((t, LANES), F32)] * 2,
        compiler_params=_cparams("parallel"),
    )(pos_dense, inv_dense, sign)


def _inproj_kernel(x_ref, g_ref, sc_ref, sh_ref, cos_ref, sin_ref, w_ref, dtb_ref,
                   q_ref, kv_ref, z_ref, xbc_ref, dt_ref):
    h = _rms_mod(x_ref[...], g_ref[...], sc_ref[0], sh_ref[0]).astype(BF16)
    cos = cos_ref[...]
    sin = sin_ref[...]

    def rope(t):
        return t * cos + pltpu.roll(t, LANES // 2, axis=1) * sin

    qkv = jnp.dot(h, w_ref[:, C_Q:C_Z], preferred_element_type=F32)
    for i in range(Q_TILES):
        q_ref[:, i * LANES:(i + 1) * LANES] = (
            rope(qkv[:, i * LANES:(i + 1) * LANES]) * (HEAD_DIM ** -0.5)).astype(BF16)
    kv_ref[:, 0:LANES] = rope(qkv[:, C_K:C_V]).astype(BF16)
    kv_ref[:, LANES:2 * LANES] = qkv[:, C_V:C_Z].astype(BF16)
    z_ref[...] = jnp.dot(h, w_ref[:, C_Z:C_X], preferred_element_type=F32)
    xbc_ref[...] = jnp.dot(h, w_ref[:, C_X:C_DT], preferred_element_type=F32)
    dt_raw = jnp.dot(h, w_ref[:, C_DT:C_END], preferred_element_type=F32)
    dt_ref[...] = _softplus(dt_raw + dtb_ref[...])


def _inproj(x2, g, sc, sh, cos, sin, w, dtb, tiles_per_batch):
    t, d = x2.shape
    tm = ROW_TILE
    row = lambda i: (i, 0)
    const = lambda i: (0, 0)
    per_b = lambda i: (i // tiles_per_batch, 0, 0)
    return pl.pallas_call(
        _inproj_kernel,
        grid=(t // tm,),
        in_specs=[pl.BlockSpec((tm, d), row),
                  pl.BlockSpec((1, d), const),
                  pl.BlockSpec((1, 1, d), per_b),
                  pl.BlockSpec((1, 1, d), per_b),
                  pl.BlockSpec((tm, LANES), row),
                  pl.BlockSpec((tm, LANES), row),
                  pl.BlockSpec((d, C_END), const),
                  pl.BlockSpec((1, LANES), const)],
        out_specs=[pl.BlockSpec((tm, ATTN_WIDTH), row),
                   pl.BlockSpec((tm, 2 * KV_WIDTH), row),
                   pl.BlockSpec((tm, SSM_WIDTH), row),
                   pl.BlockSpec((tm, CONV_CH), row),
                   pl.BlockSpec((tm, LANES), row)],
        out_shape=[jax.ShapeDtypeStruct((t, ATTN_WIDTH), BF16),
                   jax.ShapeDtypeStruct((t, 2 * KV_WIDTH), BF16),
                   jax.ShapeDtypeStruct((t, SSM_WIDTH), F32),
                   jax.ShapeDtypeStruct((t, CONV_CH), F32),
                   jax.ShapeDtypeStruct((t, LANES), F32)],
        compiler_params=_cparams("parallel"),
    )(x2, g, sc, sh, cos, sin, w, dtb)


def _attn_kernel(sink_ref, q_ref, kv_ref, kvp_ref, o_ref, *, tiles_per_seq):
    first = (pl.program_id(0) % tiles_per_seq) == 0
    blk = WINDOW
    nsub = ATTN_TILE // blk
    lane = lax.broadcasted_iota(jnp.int32, (1, LANES), 1)
    k_lo_mask = (lane % HEAD_DIM) < (HEAD_DIM // 2)
    v_lo_mask = lane < HEAD_DIM
    zero = jnp.zeros((), BF16)

    k_all = jnp.concatenate([kvp_ref[:, 0:LANES], kv_ref[:, 0:LANES]], axis=0)
    v_all = jnp.concatenate([kvp_ref[:, LANES:2 * LANES], kv_ref[:, LANES:2 * LANES]], axis=0)
    k_sel = (jnp.where(k_lo_mask, k_all, zero), jnp.where(k_lo_mask, zero, k_all))
    v_sel = (jnp.where(v_lo_mask, v_all, zero), jnp.where(v_lo_mask, zero, v_all))

    qi = lax.broadcasted_iota(jnp.int32, (blk, blk), 0)
    col = lax.broadcasted_iota(jnp.int32, (blk, blk), 1)
    cur = col <= qi
    no_prev = (col > qi) & (col > jnp.where(first, -1, blk))

    for n in range(nsub):
        q_st = jnp.concatenate([q_ref[n * blk:(n + 1) * blk, i * LANES:(i + 1) * LANES]
                                for i in range(Q_TILES)], axis=0)
        out = None
        for hk in range(KV_HEADS):
            k_n = k_sel[hk][n * blk:(n + 2) * blk]
            v_n = v_sel[hk][n * blk:(n + 2) * blk]
            s = lax.dot_general(q_st, k_n, (((1,), (1,)), ((), ())), preferred_element_type=F32)
            probs, scales = [], []
            for i in range(GROUP):
                s_i = s[i * blk:(i + 1) * blk]
                sc = jnp.where(cur, s_i[:, blk:], s_i[:, :blk])
                if n == 0:
                    sc = jnp.where(no_prev, -jnp.inf, sc)
                sink = sink_ref[hk * GROUP + i]
                m = jnp.maximum(jnp.max(sc, axis=-1, keepdims=True), sink)
                p = jnp.exp(sc - m)
                scales.append(1.0 / (jnp.sum(p, axis=-1, keepdims=True) + jnp.exp(sink - m)))
                probs.append(jnp.concatenate([jnp.where(cur, 0.0, p), jnp.where(cur, p, 0.0)],
                                             axis=1).astype(BF16))
            o = jnp.dot(jnp.concatenate(probs, axis=0), v_n, preferred_element_type=F32)
            o = o * jnp.concatenate(scales, axis=0)
            out = o if out is None else out + o
        for i in range(Q_TILES):
            o_ref[n * blk:(n + 1) * blk, i * LANES:(i + 1) * LANES] = out[i * blk:(i + 1) * blk].astype(BF16)


def _attention(q, kv, sinks, seq):
    t = q.shape[0]
    tq = ATTN_TILE
    r = tq // WINDOW
    kern = functools.partial(_attn_kernel, tiles_per_seq=seq // tq)
    return pl.pallas_call(
        kern,
        grid=(t // tq,),
        in_specs=[pl.BlockSpec(memory_space=pltpu.SMEM),
                  pl.BlockSpec((tq, ATTN_WIDTH), lambda i: (i, 0)),
                  pl.BlockSpec((tq, 2 * KV_WIDTH), lambda i: (i, 0)),
                  pl.BlockSpec((WINDOW, 2 * KV_WIDTH), lambda i: (jnp.maximum(i * r - 1, 0), 0))],
        out_specs=pl.BlockSpec((tq, ATTN_WIDTH), lambda i: (i, 0)),
        out_shape=jax.ShapeDtypeStruct((t, ATTN_WIDTH), BF16),
        compiler_params=_cparams("parallel"),
    )(sinks, q, kv, kv)


def _split3(v):
    p0 = v.astype(BF16)
    r1 = v - p0.astype(F32)
    p1 = r1.astype(BF16)
    return p0, p1, (r1 - p1.astype(F32)).astype(BF16)


def _dot3(lhs_exact_bf16, v):
    return sum(jnp.dot(lhs_exact_bf16, p, preferred_element_type=F32) for p in _split3(v))


def _ssd_kernel(xbc_ref, z_ref, dt_ref, cw_ref, cb_ref, alog_ref, dskip_ref, ng_ref,
                o_ref, ext_ref, st_ref):
    L = CHUNK
    rows = xbc_ref.shape[0]
    halo = SUBLANES

    @pl.when(pl.program_id(1) == 0)
    def _():
        ext_ref[0:halo, :] = jnp.zeros((halo, CONV_CH), F32)
        st_ref[...] = jnp.zeros(st_ref.shape, F32)

    ext_ref[halo:halo + rows, :] = xbc_ref[...]
    acc = cb_ref[...] + cw_ref[CONV_WIDTH - 1:CONV_WIDTH, :] * ext_ref[halo:halo + rows, :]
    for k in range(CONV_WIDTH - 1):
        off = halo - (CONV_WIDTH - 1) + k
        acc = acc + cw_ref[k:k + 1, :] * ext_ref[off:off + rows, :]
    ext_ref[0:halo, :] = ext_ref[rows:rows + halo, :]
    u_all = _silu(acc)

    lane = lax.broadcasted_iota(jnp.int32, (1, LANES), 1)
    a = jnp.where(lane < SSM_HEADS, -jnp.exp(alog_ref[...]), 0.0)
    row = lax.broadcasted_iota(jnp.int32, (L, L), 0)
    col = lax.broadcasted_iota(jnp.int32, (L, L), 1)
    causal = row >= col
    tri = jnp.where(causal, 1.0, 0.0).astype(BF16)
    spread = jnp.where(lax.broadcasted_iota(jnp.int32, (LANES, SSM_WIDTH), 1) // SSM_HEAD_DIM
                       == lax.broadcasted_iota(jnp.int32, (LANES, SSM_WIDTH), 0), 1.0, 0.0).astype(BF16)
    r_heads = SSM_HEADS // SSM_GROUPS
    glane = lax.broadcasted_iota(jnp.int32, (1, GROUP_WIDTH), 1) // SSM_HEAD_DIM
    zero = jnp.zeros((), BF16)
    states = [st_ref[g] for g in range(SSM_GROUPS)]

    for c in range(rows // L):
        rs = slice(c * L, (c + 1) * L)
        u = u_all[rs]
        xs = u[:, 0:SSM_WIDTH]
        bm = u[:, SSM_WIDTH:SSM_WIDTH + SSM_GROUPS * SSM_STATE]
        cm = u[:, SSM_WIDTH + SSM_GROUPS * SSM_STATE:]
        dt = dt_ref[rs, :]
        acs = _dot3(tri, dt * a)
        acs_t = acs.T
        dt_e = sum(jnp.dot(p, spread, preferred_element_type=F32) for p in _split3(dt))
        acs_e = sum(jnp.dot(p, spread, preferred_element_type=F32) for p in _split3(acs))
        last = acs_e[L - 1:L, :]
        xd = xs * dt_e
        xd_b = xd.astype(BF16)
        xdw_b = (xd * jnp.exp(last - acs_e)).astype(BF16)
        e_acs = jnp.exp(acs_e)
        c_dec = jnp.exp(last)

        ys = []
        for g in range(SSM_GROUPS):
            gs = slice(g * GROUP_WIDTH, (g + 1) * GROUP_WIDTH)
            b_g = bm[:, g * SSM_STATE:(g + 1) * SSM_STATE]
            c_b = cm[:, g * SSM_STATE:(g + 1) * SSM_STATE].astype(BF16)
            cb = lax.dot_general(c_b, b_g.astype(BF16), (((1,), (1,)), ((), ())),
                                 preferred_element_type=F32)
            st = states[g]
            y_g = jnp.dot(c_b, st.astype(BF16), preferred_element_type=F32) * e_acs[:, gs]
            xd_g = xd_b[:, gs]
            for r in range(r_heads):
                h = g * r_heads + r
                seg = acs[:, h:h + 1] - acs_t[h:h + 1, :]
                m_h = (cb * jnp.exp(jnp.where(causal, seg, -jnp.inf))).astype(BF16)
                y_g = y_g + jnp.dot(m_h, jnp.where(glane == r, xd_g, zero), preferred_element_type=F32)
            new = jnp.dot(b_g.T.astype(BF16), xdw_b[:, gs], preferred_element_type=F32)
            states[g] = c_dec[:, gs] * st + new
            ys.append(y_g)

        y = jnp.concatenate(ys, axis=1) + dskip_ref[...] * xs
        y = y * _silu(z_ref[rs, :])
        outs = []
        for g in range(SSM_GROUPS):
            yg = y[:, g * GROUP_WIDTH:(g + 1) * GROUP_WIDTH]
            outs.append(yg * lax.rsqrt(jnp.mean(yg * yg, axis=-1, keepdims=True) + EPS))
        o_ref[rs, :] = (jnp.concatenate(outs, axis=1) * ng_ref[...]).astype(BF16)

    for g in range(SSM_GROUPS):
        st_ref[g] = states[g]


def _ssd(xbc, z, dt, cw, cb, alog, dskip, ng, batch, seq):
    t = xbc.shape[0]
    rows = SSD_CHUNKS_PER_STEP * CHUNK
    ns = seq // rows
    row = lambda b, c: (b * ns + c, 0)
    const = lambda b, c: (0, 0)
    return pl.pallas_call(
        _ssd_kernel,
        grid=(batch, ns),
        in_specs=[pl.BlockSpec((rows, CONV_CH), row),
                  pl.BlockSpec((rows, SSM_WIDTH), row),
                  pl.BlockSpec((rows, LANES), row),
                  pl.BlockSpec((CONV_WIDTH, CONV_CH), const),
                  pl.BlockSpec((1, CONV_CH), const),
                  pl.BlockSpec((1, LANES), const),
                  pl.BlockSpec((1, SSM_WIDTH), const),
                  pl.BlockSpec((1, SSM_WIDTH), const)],
        out_specs=pl.BlockSpec((rows, SSM_WIDTH), row),
        out_shape=jax.ShapeDtypeStruct((t, SSM_WIDTH), BF16),
        scratch_shapes=[pltpu.VMEM((SUBLANES + rows, CONV_CH), F32),
                        pltpu.VMEM((SSM_GROUPS, SSM_STATE, GROUP_WIDTH), F32)],
        compiler_params=_cparams("parallel", "arbitrary"),
    )(xbc, z, dt, cw, cb, alog, dskip, ng)


def _outproj_core(a_ref, s_ref, x_ref, w_ref, gm_ref, g_ref, sc_ref, sh_ref):
    half = a_ref.shape[1]
    mixed = (jnp.dot(a_ref[...], w_ref[0:half, :], preferred_element_type=F32)
             + jnp.dot(s_ref[...], w_ref[half:, :], preferred_element_type=F32))
    x_new = x_ref[...] + gm_ref[0] * mixed
    return x_new, _rms_mod(x_new, g_ref[...], sc_ref[0], sh_ref[0])


def _outproj_router_kernel(a_ref, s_ref, x_ref, w_ref, gm_ref, g_ref, sc_ref, sh_ref, rw_ref,
                           xo_ref, h_ref, route_ref, gate_ref, cnt_ref, carry_ref):
    x_new, h = _outproj_core(a_ref, s_ref, x_ref, w_ref, gm_ref, g_ref, sc_ref, sh_ref)
    xo_ref[...] = x_new
    tm = h.shape[0]
    for s in range(h.shape[1] // LANES):
        h_ref[pl.ds(s, tm, stride=SUBLANES), :] = h[:, s * LANES:(s + 1) * LANES]

    h_hi = h.astype(BF16)
    h_lo = (h - h_hi.astype(F32)).astype(BF16)
    hi_part = jnp.dot(h_hi, rw_ref[...], preferred_element_type=F32)
    logits = (hi_part[:, 0:LANES] + hi_part[:, LANES:]
              + jnp.dot(h_lo, rw_ref[:, 0:LANES], preferred_element_type=F32))
    lane = lax.broadcasted_iota(jnp.int32, (tm, LANES), 1).astype(F32)
    logits = jnp.where(lane < N_EXPERTS, logits, -jnp.inf)
    m0 = jnp.max(logits, axis=-1, keepdims=True)
    i0 = jnp.min(jnp.where(logits == m0, lane, float(LANES)), axis=-1, keepdims=True)
    rest = jnp.where(lane == i0, -jnp.inf, logits)
    m1 = jnp.max(rest, axis=-1, keepdims=True)
    i1 = jnp.min(jnp.where(rest == m1, lane, float(LANES)), axis=-1, keepdims=True)
    e = jnp.exp(m1 - m0)
    g0 = 1.0 / (1.0 + e)
    g1 = e / (1.0 + e)

    @pl.when(pl.program_id(0) == 0)
    def _():
        carry_ref[...] = jnp.zeros(carry_ref.shape, F32)

    sel0 = lane == i0
    sel1 = lane == i1
    member = jnp.where(sel0 | sel1, 1.0, 0.0)
    r_i = lax.broadcasted_iota(jnp.int32, (tm, tm), 0)
    c_i = lax.broadcasted_iota(jnp.int32, (tm, tm), 1)
    strict = jnp.where(r_i > c_i, 1.0, 0.0).astype(BF16)
    rank = carry_ref[...] + jnp.dot(strict, member.astype(BF16), preferred_element_type=F32)
    r0 = jnp.sum(jnp.where(sel0, rank, 0.0), axis=-1, keepdims=True)
    r1 = jnp.sum(jnp.where(sel1, rank, 0.0), axis=-1, keepdims=True)
    carry_ref[...] = carry_ref[...] + jnp.sum(member, axis=0, keepdims=True)
    cnt_ref[...] = carry_ref[...]

    gate_ref[...] = jnp.where(lane == 0.0, g0, g1)[:, 0:SUBLANES]
    diag = (lax.broadcasted_iota(jnp.int32, (tm, LANES), 0) % LANES
            == lax.broadcasted_iota(jnp.int32, (tm, LANES), 1))
    nrow = tm // LANES
    for q, colv in enumerate((i0, i1, r0, r1)):
        spread = jnp.where(diag, colv, 0.0)
        rows = [jnp.sum(spread[b * LANES:(b + 1) * LANES], axis=0, keepdims=True) for b in range(nrow)]
        route_ref[0, q * nrow:(q + 1) * nrow, :] = jnp.concatenate(rows, axis=0).astype(jnp.int32)


def _outproj_router(attn, ssm, x2, w, gm, g, sc, sh, tiles_per_batch, router):
    t, d = x2.shape
    tm = ROW_TILE
    row = lambda i: (i, 0)
    const = lambda i: (0, 0)
    per_b = lambda i: (i // tiles_per_batch, 0, 0)
    half = attn.shape[1]
    return pl.pallas_call(
        _outproj_router_kernel,
        grid=(t // tm,),
        in_specs=[pl.BlockSpec((tm, half), row),
                  pl.BlockSpec((tm, half), row),
                  pl.BlockSpec((tm, d), row),
                  pl.BlockSpec((2 * half, d), const),
                  pl.BlockSpec((1, 1, d), per_b),
                  pl.BlockSpec((1, d), const),
                  pl.BlockSpec((1, 1, d), per_b),
                  pl.BlockSpec((1, 1, d), per_b),
                  pl.BlockSpec((d, 2 * LANES), const)],
        out_specs=[pl.BlockSpec((tm, d), row),
                   pl.BlockSpec((tm * SUBLANES, LANES), row),
                   pl.BlockSpec((1, 4 * tm // LANES, LANES), lambda i: (i, 0, 0)),
                   pl.BlockSpec((tm, SUBLANES), row),
                   pl.BlockSpec((1, LANES), const)],
        out_shape=[jax.ShapeDtypeStruct((t, d), F32),
                   jax.ShapeDtypeStruct((t * SUBLANES, LANES), F32),
                   jax.ShapeDtypeStruct((t // tm, 4 * tm // LANES, LANES), jnp.int32),
                   jax.ShapeDtypeStruct((t, SUBLANES), F32),
                   jax.ShapeDtypeStruct((1, LANES), F32)],
        scratch_shapes=[pltpu.VMEM((1, LANES), F32)],
        compiler_params=_cparams("arbitrary"),
    )(attn, ssm, x2, w, gm, g, sc, sh, router)


def _swiglu_rows(x, wg, wu, wd):
    gate = jnp.dot(x, wg, preferred_element_type=F32)
    up = jnp.dot(x, wu, preferred_element_type=F32)
    return jnp.dot((_silu(gate) * up).astype(BF16), wd, preferred_element_type=F32)


def _outproj_ffn_kernel(a_ref, s_ref, x_ref, w_ref, gm_ref, g_ref, sc_ref, sh_ref, wg_ref, wu_ref, wd_ref, gf_ref,
                        o_ref):
    half = a_ref.shape[1]
    piece = x_ref.shape[0] // ROW_SPLIT
    for r in range(ROW_SPLIT):
        rs = slice(r * piece, (r + 1) * piece)
        mixed = (jnp.dot(a_ref[rs, :], w_ref[0:half, :], preferred_element_type=F32)
                 + jnp.dot(s_ref[rs, :], w_ref[half:, :], preferred_element_type=F32))
        x_new = x_ref[rs, :] + gm_ref[0] * mixed
        h = _rms_mod(x_new, g_ref[...], sc_ref[0], sh_ref[0]).astype(BF16)
        o_ref[rs, :] = x_new + gf_ref[0] * _swiglu_rows(h, wg_ref[...], wu_ref[...], wd_ref[...])


def _outproj_ffn(attn, ssm, x2, w, gm, g, sc, sh, wg, wu, wd, gf, tiles_per_batch):
    t, d = x2.shape
    f = wg.shape[1]
    tm = ROW_TILE
    half = attn.shape[1]
    row = lambda i: (i, 0)
    const = lambda i: (0, 0)
    per_b = lambda i: (i // tiles_per_batch, 0, 0)
    resident = pl.Buffered(1)
    return pl.pallas_call(
        _outproj_ffn_kernel,
        grid=(t // tm,),
        in_specs=[pl.BlockSpec((tm, half), row),
                  pl.BlockSpec((tm, half), row),
                  pl.BlockSpec((tm, d), row),
                  pl.BlockSpec((2 * half, d), const, pipeline_mode=resident),
                  pl.BlockSpec((1, 1, d), per_b),
                  pl.BlockSpec((1, d), const),
                  pl.BlockSpec((1, 1, d), per_b),
                  pl.BlockSpec((1, 1, d), per_b),
                  pl.BlockSpec((d, f), const, pipeline_mode=resident),
                  pl.BlockSpec((d, f), const, pipeline_mode=resident),
                  pl.BlockSpec((f, d), const, pipeline_mode=resident),
                  pl.BlockSpec((1, 1, d), per_b)],
        out_specs=pl.BlockSpec((tm, d), row),
        out_shape=jax.ShapeDtypeStruct((t, d), F32),
        compiler_params=_cparams("parallel"),
    )(attn, ssm, x2, w, gm, g, sc, sh, wg, wu, wd, gf)


def _row_copy(src, src_row, dst, dst_row, sem):
    return pltpu.make_async_copy(
        src.at[pl.ds(pl.multiple_of(src_row * SUBLANES, SUBLANES), SUBLANES)],
        dst.at[pl.ds(pl.multiple_of(dst_row * SUBLANES, SUBLANES), SUBLANES)], sem)


def _dispatch_kernel(zs_ref, ze_ref, nvalid_ref, d0_ref, d1_ref, h_ref, xb_hbm, zero_ref, sem, zsem):
    i = pl.program_id(0)
    n_tok = d0_ref.shape[0]
    blk_rows = zero_ref.shape[0]
    n_blocks = xb_hbm.shape[0] // blk_rows

    def zero_row(r):
        return pltpu.make_async_copy(
            zero_ref.at[pl.ds(0, SUBLANES)],
            xb_hbm.at[pl.ds(pl.multiple_of(r * SUBLANES, SUBLANES), SUBLANES)], zsem)

    def zero_block(b):
        return pltpu.make_async_copy(
            zero_ref, xb_hbm.at[pl.ds(pl.multiple_of(b * blk_rows, blk_rows), blk_rows)], zsem)

    @pl.when(i == 0)
    def _():
        zero_ref[...] = jnp.zeros(zero_ref.shape, F32)
        for start in (True, False):
            for e in range(N_EXPERTS):
                def rows_body(r, c):
                    zero_row(r).start() if start else zero_row(r).wait()
                    return c
                lax.fori_loop(zs_ref[e], ze_ref[e], rows_body, 0)

            def blocks_body(b, c):
                zero_block(b).start() if start else zero_block(b).wait()
                return c
            lax.fori_loop(nvalid_ref[0], n_blocks, blocks_body, 0)

    def issue(t, c):
        _row_copy(h_ref, t, xb_hbm, d0_ref[t], sem).start()
        _row_copy(h_ref, t, xb_hbm, d1_ref[t], sem).start()
        return c

    lax.fori_loop(0, n_tok, issue, 0, unroll=ISSUE_UNROLL)
    rows = n_tok * TOP_K * SUBLANES
    pltpu.make_async_copy(xb_hbm.at[pl.ds(0, rows)], xb_hbm.at[pl.ds(0, rows)], sem).wait()


def _dispatch(h_rows, dest, zs, ze, nvalid, n_blocks, rows):
    t = h_rows.shape[0] // SUBLANES
    td = min(DISPATCH_TOKENS, t)
    grid_spec = pltpu.PrefetchScalarGridSpec(
        num_scalar_prefetch=3,
        grid=(t // td,),
        in_specs=[pl.BlockSpec((td,), lambda i, zs, ze, nv: (i,), memory_space=pltpu.SMEM),
                  pl.BlockSpec((td,), lambda i, zs, ze, nv: (i,), memory_space=pltpu.SMEM),
                  pl.BlockSpec((td * SUBLANES, LANES), lambda i, zs, ze, nv: (i, 0))],
        out_specs=pl.BlockSpec(memory_space=pl.ANY),
        scratch_shapes=[pltpu.VMEM((rows * SUBLANES, LANES), F32),
                        pltpu.SemaphoreType.DMA(()), pltpu.SemaphoreType.DMA(())],
    )
    return pl.pallas_call(
        _dispatch_kernel,
        grid_spec=grid_spec,
        out_shape=jax.ShapeDtypeStruct((n_blocks * rows * SUBLANES, LANES), F32),
        compiler_params=pltpu.CompilerParams(dimension_semantics=("arbitrary",), has_side_effects=True,
                                             vmem_limit_bytes=VMEM_LIMIT),
    )(zs, ze, nvalid, dest[0], dest[1], h_rows)


def _moe_kernel(bexp_ref, nvalid_ref, brows_ref, xb_ref, wg_ref, wu_ref, wd_ref, y_ref, x_scr, acc_ref):
    b = pl.program_id(0)
    j = pl.program_id(1)
    nf = pl.num_programs(1)
    rows = x_scr.shape[0]
    nsl = x_scr.shape[1] // LANES
    valid = b < nvalid_ref[0]

    @pl.when(valid & (j == 0))
    def _():
        for s in range(nsl):
            x_scr[:, s * LANES:(s + 1) * LANES] = xb_ref[pl.ds(s, rows, stride=SUBLANES), :].astype(BF16)
        acc_ref[...] = jnp.zeros(acc_ref.shape, F32)

    piece = rows // ROW_SPLIT
    used = (brows_ref[b] + piece - 1) // piece
    for n_used in range(1, ROW_SPLIT + 1):
        @pl.when(valid & (used == n_used))
        def _():
            wg = wg_ref[0].astype(BF16)
            wu = wu_ref[0].astype(BF16)
            wd = wd_ref[0].astype(BF16)
            for r in range(n_used):
                rs = slice(r * piece, (r + 1) * piece)
                acc_ref[rs, :] += _swiglu_rows(x_scr[rs, :], wg, wu, wd)

    @pl.when(valid & (j == nf - 1))
    def _():
        for s in range(nsl):
            y_ref[pl.ds(s, rows, stride=SUBLANES), :] = acc_ref[:, s * LANES:(s + 1) * LANES]

    @pl.when(jnp.logical_not(valid) & (j == nf - 1))
    def _():
        y_ref[...] = jnp.zeros(y_ref.shape, F32)


def _moe_experts(xb, bexp, nvalid, brows, wg, wu, wd, n_blocks, rows):
    d = wg.shape[1]
    f = wg.shape[2]
    nf = f // MOE_FT

    def blk(b, j, bexp, nvalid, brows):
        return (jnp.minimum(b, nvalid[0] - 1), 0)

    def fidx(b, j, nvalid):
        return jnp.where(b < nvalid[0], j, nf - 1)

    grid_spec = pltpu.PrefetchScalarGridSpec(
        num_scalar_prefetch=3,
        grid=(n_blocks, nf),
        in_specs=[pl.BlockSpec((rows * SUBLANES, LANES), blk),
                  pl.BlockSpec((1, d, MOE_FT), lambda b, j, bexp, nvalid, brows: (bexp[b], 0, fidx(b, j, nvalid))),
                  pl.BlockSpec((1, d, MOE_FT), lambda b, j, bexp, nvalid, brows: (bexp[b], 0, fidx(b, j, nvalid))),
                  pl.BlockSpec((1, MOE_FT, d), lambda b, j, bexp, nvalid, brows: (bexp[b], fidx(b, j, nvalid), 0))],
        out_specs=pl.BlockSpec((rows * SUBLANES, LANES), lambda b, j, bexp, nvalid, brows: (b, 0)),
        scratch_shapes=[pltpu.VMEM((rows, d), BF16), pltpu.VMEM((rows, d), F32)],
    )
    return pl.pallas_call(
        _moe_kernel,
        grid_spec=grid_spec,
        out_shape=jax.ShapeDtypeStruct(xb.shape, F32),
        compiler_params=_cparams("arbitrary", "arbitrary"),
    )(bexp, nvalid, brows, xb, wg, wu, wd)


def _combine_kernel(d0_ref, d1_ref, d0n_ref, d1n_ref, y_hbm, x_ref, gate_ref, gf_ref, fg_ref, o_ref,
                    buf_ref, sem):
    i = pl.program_id(0)
    tm = x_ref.shape[0]
    nsl = x_ref.shape[1] // LANES
    slot = i % 2

    def gather(a_ref, b_ref, to_slot):
        def issue(t, c):
            _row_copy(y_hbm, a_ref[t], buf_ref.at[to_slot], t, sem.at[to_slot]).start()
            _row_copy(y_hbm, b_ref[t], buf_ref.at[to_slot], tm + t, sem.at[to_slot]).start()
            return c
        lax.fori_loop(0, tm, issue, 0, unroll=ISSUE_UNROLL)

    @pl.when(i == 0)
    def _():
        gather(d0_ref, d1_ref, 0)

    @pl.when(i + 1 < pl.num_programs(0))
    def _():
        gather(d0n_ref, d1n_ref, 1 - slot)

    rows = tm * TOP_K * SUBLANES
    pltpu.make_async_copy(y_hbm.at[pl.ds(0, rows)], buf_ref.at[slot], sem.at[slot]).wait()

    g0 = gate_ref[:, 0:1]
    g1 = gate_ref[:, 1:2]
    parts = []
    for s in range(nsl):
        y0 = buf_ref[slot, pl.ds(s, tm, stride=SUBLANES), :]
        y1 = buf_ref[slot, pl.ds(tm * SUBLANES + s, tm, stride=SUBLANES), :]
        parts.append(g0 * y0 + g1 * y1)
    x = x_ref[...] + gf_ref[0] * jnp.concatenate(parts, axis=1)
    ms = jnp.mean(x * x, axis=-1, keepdims=True)
    o_ref[...] = x * lax.rsqrt(ms + EPS) * fg_ref[...]


def _combine(y, dest, x2, gates, gf, fg, tiles_per_batch):
    t, d = x2.shape
    tm = COMBINE_TOKENS
    n = t // tm
    cur = pl.BlockSpec((tm,), lambda i: (i,), memory_space=pltpu.SMEM)
    nxt = pl.BlockSpec((tm,), lambda i: (jnp.minimum(i + 1, n - 1),), memory_space=pltpu.SMEM)
    return pl.pallas_call(
        _combine_kernel,
        grid=(n,),
        in_specs=[cur, cur, nxt, nxt,
                  pl.BlockSpec(memory_space=pl.ANY),
                  pl.BlockSpec((tm, d), lambda i: (i, 0)),
                  pl.BlockSpec((tm, SUBLANES), lambda i: (i, 0)),
                  pl.BlockSpec((1, 1, d), lambda i: (i // tiles_per_batch, 0, 0)),
                  pl.BlockSpec((1, d), lambda i: (0, 0))],
        out_specs=pl.BlockSpec((tm, d), lambda i: (i, 0)),
        out_shape=jax.ShapeDtypeStruct((t, d), F32),
        scratch_shapes=[pltpu.VMEM((2, tm * TOP_K * SUBLANES, LANES), F32), pltpu.SemaphoreType.DMA((2,))],
        compiler_params=_cparams("arbitrary"),
    )(dest[0], dest[1], dest[0], dest[1], y, x2, gates, gf, fg)


def _final_norm_kernel(x_ref, g_ref, o_ref):
    x = x_ref[...]
    o_ref[...] = x * lax.rsqrt(jnp.mean(x * x, axis=-1, keepdims=True) + EPS) * g_ref[...]


def _final_norm(x2, g):
    t, d = x2.shape
    tm = ROW_TILE
    return pl.pallas_call(
        _final_norm_kernel,
        grid=(t // tm,),
        in_specs=[pl.BlockSpec((tm, d), lambda i: (i, 0)), pl.BlockSpec((1, d), lambda i: (0, 0))],
        out_specs=pl.BlockSpec((tm, d), lambda i: (i, 0)),
        out_shape=jax.ShapeDtypeStruct((t, d), F32),
        compiler_params=_cparams("parallel"),
    )(x2, g)


def _qk_column_perm():
    half = HEAD_DIM // 2
    lane = np.arange(LANES)
    pair = (lane % HEAD_DIM) // half
    dim = lane % half + half * (lane // HEAD_DIM)
    q = np.concatenate([(i + Q_TILES * pair) * HEAD_DIM + dim for i in range(Q_TILES)])
    k = ATTN_WIDTH + pair * HEAD_DIM + dim
    return q, k


def _attn_out_row_perm():
    lane = np.arange(LANES)
    return np.concatenate([(i + Q_TILES * (lane // HEAD_DIM)) * HEAD_DIM + lane % HEAD_DIM
                           for i in range(Q_TILES)])


def _pad_lanes(v):
    return jnp.pad(v.astype(F32), (0, LANES - v.shape[0])).reshape(1, LANES)


def _moe_route_tables(route, counts, rows, n_blocks):
    nt = route.shape[0]
    e0, e1, r0, r1 = (route.reshape(nt, 4, -1)[:, q].reshape(-1) for q in range(4))
    cnt = counts[0, :N_EXPERTS].astype(jnp.int32)
    padded = (cnt + rows - 1) // rows * rows
    pad_end = jnp.cumsum(padded)
    pad_start = pad_end - padded

    def first_row(e):
        return sum(jnp.where(e == k, pad_start[k], 0) for k in range(N_EXPERTS))

    dest = (first_row(e0) + r0).astype(jnp.int32), (first_row(e1) + r1).astype(jnp.int32)
    blk_start = jnp.arange(n_blocks, dtype=jnp.int32) * rows
    bexp = jnp.minimum(jnp.sum(blk_start[:, None] >= pad_end[None, :], axis=1), N_EXPERTS - 1).astype(jnp.int32)
    nvalid = (pad_end[-1:] // rows).astype(jnp.int32)
    row_end = pad_start + cnt
    brows = jnp.clip(row_end[bexp] - blk_start, 0, rows).astype(jnp.int32)
    return dest, bexp, nvalid, brows, row_end.astype(jnp.int32), pad_end.astype(jnp.int32)


def kernel(x, c, positions, ada_w, ada_b, norm_mix_g, norm_ffn_g, w_in, w_out, attn_sinks, conv_w, conv_b,
           dt_bias, a_log, d_skip, ssm_norm_g, ffn_w_gate, ffn_w_up, ffn_w_down, router_w, moe_w_gate,
           moe_w_up, moe_w_down, final_norm_g):
    batch, seq, d = x.shape
    depth = w_in.shape[0]
    t = batch * seq
    tiles_per_batch = seq // ROW_TILE
    x2 = x.reshape(t, d)

    c8 = jnp.pad(c, ((0, SUBLANES - batch), (0, 0)))
    mod = _ada_mod(c8, ada_w, ada_b)[:, :batch].reshape(depth, batch, 6, 1, d)
    cos, sin = _rope_tables(positions)

    q_perm, k_perm = _qk_column_perm()
    o_perm = _attn_out_row_perm()
    for l in range(depth):
        sh_m, sc_m, g_m, sh_f, sc_f, g_f = (mod[l, :, k] for k in range(6))
        wl = w_in[l]
        w_cat = jnp.concatenate(
            [wl[:, q_perm], wl[:, k_perm], wl[:, C_V:C_DT],
             jnp.pad(wl[:, C_DT:], ((0, 0), (0, LANES - SSM_HEADS)))], axis=1).astype(BF16)
        q, kv, z, xbc, dt = _inproj(x2, norm_mix_g[l].reshape(1, d), sc_m, sh_m, cos, sin, w_cat,
                                    _pad_lanes(dt_bias[l]), tiles_per_batch)
        attn = _attention(q, kv, attn_sinks[l].astype(F32), seq)
        ssm = _ssd(xbc, z, dt, conv_w[l], conv_b[l].reshape(1, CONV_CH), _pad_lanes(a_log[l]),
                   jnp.repeat(d_skip[l].astype(F32), SSM_HEAD_DIM).reshape(1, SSM_WIDTH),
                   ssm_norm_g[l].reshape(1, SSM_WIDTH), batch, seq)
        wo = jnp.concatenate([w_out[l][o_perm], w_out[l][ATTN_WIDTH:]], axis=0).astype(BF16)
        ffn_g = norm_ffn_g[l].reshape(1, d)
        if l % 2 == 0:
            x2 = _outproj_ffn(attn, ssm, x2, wo, g_m, ffn_g, sc_f, sh_f, ffn_w_gate[l // 2].astype(BF16),
                              ffn_w_up[l // 2].astype(BF16), ffn_w_down[l // 2].astype(BF16), g_f, tiles_per_batch)
            if l == depth - 1:
                x2 = _final_norm(x2, final_norm_g.reshape(1, d))
        else:
            if l != depth - 1:
                raise NotImplementedError("the expert layer fuses the final norm and must be last")
            rw = jnp.pad(router_w[l // 2].astype(F32), ((0, 0), (0, LANES - N_EXPERTS)))
            rw_hi = rw.astype(BF16)
            rw_lo = (rw - rw_hi.astype(F32)).astype(BF16)
            x_new, h_rows, route, gates, counts = _outproj_router(
                attn, ssm, x2, wo, g_m, ffn_g, sc_f, sh_f, tiles_per_batch,
                jnp.concatenate([rw_hi, rw_lo], axis=1))
            rows = min(MOE_ROWS, t)
            n_blocks = (t * TOP_K) // rows + N_EXPERTS
            dest, bexp, nvalid, brows, zs, ze = _moe_route_tables(route, counts, rows, n_blocks)
            xb = _dispatch(h_rows, dest, zs, ze, nvalid, n_blocks, rows)
            y = _moe_experts(xb, bexp, nvalid, brows, moe_w_gate[l // 2], moe_w_up[l // 2], moe_w_down[l // 2],
                             n_blocks, rows)
            x2 = _combine(y, dest, x_new, gates, g_f, final_norm_g.reshape(1, d),
                          seq // COMBINE_TOKENS)
    return x2.reshape(batch, seq, d)
```

```python
import functools

import numpy as np
import jax
import jax.numpy as jnp
from jax import lax
from jax.experimental import pallas as pl
from jax.experimental.pallas import tpu as pltpu

F32 = jnp.float32
BF16 = jnp.bfloat16

LANES = 128
SUBLANES = 8
VMEM_LIMIT = 48 * 1024 * 1024

EPS = 1e-6
HEAD_DIM = 64
Q_HEADS = 8
KV_HEADS = 2
GROUP = Q_HEADS // KV_HEADS
ATTN_WIDTH = Q_HEADS * HEAD_DIM
KV_WIDTH = KV_HEADS * HEAD_DIM
WINDOW = 128
ROPE_THETA = 10000.0
SSM_HEADS = 8
SSM_HEAD_DIM = 64
SSM_WIDTH = SSM_HEADS * SSM_HEAD_DIM
SSM_GROUPS = 2
SSM_STATE = 128
GROUP_WIDTH = SSM_WIDTH // SSM_GROUPS
CONV_WIDTH = 4
CONV_CH = SSM_WIDTH + 2 * SSM_GROUPS * SSM_STATE
CHUNK = 128
N_EXPERTS = 8
TOP_K = 2

Q_TILES = ATTN_WIDTH // LANES
C_Q = 0
C_K = C_Q + ATTN_WIDTH
C_V = C_K + KV_WIDTH
C_Z = C_V + KV_WIDTH
C_X = C_Z + SSM_WIDTH
C_DT = C_X + CONV_CH
C_END = C_DT + LANES

ROW_TILE = 512
ATTN_TILE = 512
MOE_ROWS = 1024
MOE_FT = 512
SSD_CHUNKS_PER_STEP = 4
ROW_SPLIT = 2


def _cparams(*sem):
    return pltpu.CompilerParams(dimension_semantics=sem, vmem_limit_bytes=VMEM_LIMIT)


def _silu(v):
    return v * (1.0 / (1.0 + jnp.exp(-v)))


def _softplus(v):
    return jnp.maximum(v, 0.0) + jnp.log1p(jnp.exp(-jnp.abs(v)))


def _rms_mod(x, g, scale, shift):
    ms = jnp.mean(x * x, axis=-1, keepdims=True)
    return (x * lax.rsqrt(ms + EPS) * g) * (1.0 + scale) + shift


def _ada_kernel(c_ref, w_ref, b_ref, o_ref):
    c = c_ref[...]
    o_ref[0] = jnp.dot(_silu(c), w_ref[0], preferred_element_type=F32,
                       precision=lax.Precision.HIGHEST) + b_ref[0]


def _ada_mod(c8, ada_w, ada_b):
    depth, d, n = ada_w.shape
    tn = 1536
    return pl.pallas_call(
        _ada_kernel,
        grid=(depth, n // tn),
        in_specs=[pl.BlockSpec((SUBLANES, d), lambda l, j: (0, 0)),
                  pl.BlockSpec((1, d, tn), lambda l, j: (l, 0, j)),
                  pl.BlockSpec((1, 1, tn), lambda l, j: (l, 0, j))],
        out_specs=pl.BlockSpec((1, SUBLANES, tn), lambda l, j: (l, 0, j)),
        out_shape=jax.ShapeDtypeStruct((depth, SUBLANES, n), F32),
        compiler_params=_cparams("parallel", "parallel"),
    )(c8, ada_w, ada_b.reshape(depth, 1, n))


ROPE_PER_ROW = LANES // (HEAD_DIM // 2)


def _rope_kernel(pos_ref, inv_ref, sign_ref, cos_ref, sin_ref):
    ang = pos_ref[...].astype(F32) * inv_ref[...]
    tr = ang.shape[0]
    group = lax.broadcasted_iota(jnp.int32, (1, LANES), 1) // (HEAD_DIM // 2)
    for table, out_ref, sign in ((jnp.cos(ang), cos_ref, None), (jnp.sin(ang), sin_ref, sign_ref[...])):
        rolled = [table] + [pltpu.roll(table, (HEAD_DIM // 2) * s, axis=1) for s in range(1, ROPE_PER_ROW)]
        for j in range(ROPE_PER_ROW):
            rep = rolled[(0 - j) % ROPE_PER_ROW]
            for g in range(1, ROPE_PER_ROW):
                rep = jnp.where(group == g, rolled[(g - j) % ROPE_PER_ROW], rep)
            out_ref[pl.ds(j, tr, stride=ROPE_PER_ROW), :] = rep if sign is None else rep * sign


def _rope_tables(positions):
    t = positions.size
    half = HEAD_DIM // 2
    pos_dense = jnp.repeat(positions.reshape(t // ROPE_PER_ROW, ROPE_PER_ROW), half, axis=1)
    inv_freq = ROPE_THETA ** (-jnp.arange(0, HEAD_DIM, 2, dtype=F32) / HEAD_DIM)
    inv_dense = jnp.tile(inv_freq, ROPE_PER_ROW).reshape(1, LANES)
    sign = jnp.repeat(jnp.array([-1.0, -1.0, 1.0, 1.0], F32), half).reshape(1, LANES)
    rows = t // ROPE_PER_ROW
    tr = min(1024, rows)
    return pl.pallas_call(
        _rope_kernel,
        grid=(rows // tr,),
        in_specs=[pl.BlockSpec((tr, LANES), lambda i: (i, 0)),
                  pl.BlockSpec((1, LANES), lambda i: (0, 0)),
                  pl.BlockSpec((1, LANES), lambda i: (0, 0))],
        out_specs=[pl.BlockSpec((tr * ROPE_PER_ROW, LANES), lambda i: (i, 0))] * 2,
        out_shape=[jax.ShapeDtypeStruct((t, LANES), F32)] * 2,
        compiler_params=_cparams("parallel"),
    )(pos_dense, inv_dense, sign)


def _inproj_kernel(x_ref, g_ref, sc_ref, sh_ref, cos_ref, sin_ref, w_ref, dtb_ref,
                   q_ref, kv_ref, z_ref, xbc_ref, dt_ref):
    h = _rms_mod(x_ref[...], g_ref[...], sc_ref[0], sh_ref[0]).astype(BF16)
    cos = cos_ref[...]
    sin = sin_ref[...]

    def rope(t):
        return t * cos + pltpu.roll(t, LANES // 2, axis=1) * sin

    qkv = jnp.dot(h, w_ref[:, C_Q:C_Z], preferred_element_type=F32)
    for i in range(Q_TILES):
        q_ref[:, i * LANES:(i + 1) * LANES] = (
            rope(qkv[:, i * LANES:(i + 1) * LANES]) * (HEAD_DIM ** -0.5)).astype(BF16)
    kv_ref[:, 0:LANES] = rope(qkv[:, C_K:C_V]).astype(BF16)
    kv_ref[:, LANES:2 * LANES] = qkv[:, C_V:C_Z].astype(BF16)
    z_ref[...] = jnp.dot(h, w_ref[:, C_Z:C_X], preferred_element_type=F32)
    xbc_ref[...] = jnp.dot(h, w_ref[:, C_X:C_DT], preferred_element_type=F32)
    dt_raw = jnp.dot(h, w_ref[:, C_DT:C_END], preferred_element_type=F32)
    dt_ref[...] = _softplus(dt_raw + dtb_ref[...])


def _inproj(x2, g, sc, sh, cos, sin, w, dtb, tiles_per_batch):
    t, d = x2.shape
    tm = ROW_TILE
    row = lambda i: (i, 0)
    const = lambda i: (0, 0)
    per_b = lambda i: (i // tiles_per_batch, 0, 0)
    return pl.pallas_call(
        _inproj_kernel,
        grid=(t // tm,),
        in_specs=[pl.BlockSpec((tm, d), row),
                  pl.BlockSpec((1, d), const),
                  pl.BlockSpec((1, 1, d), per_b),
                  pl.BlockSpec((1, 1, d), per_b),
                  pl.BlockSpec((tm, LANES), row),
                  pl.BlockSpec((tm, LANES), row),
                  pl.BlockSpec((d, C_END), const),
                  pl.BlockSpec((1, LANES), const)],
        out_specs=[pl.BlockSpec((tm, ATTN_WIDTH), row),
                   pl.BlockSpec((tm, 2 * KV_WIDTH), row),
                   pl.BlockSpec((tm, SSM_WIDTH), row),
                   pl.BlockSpec((tm, CONV_CH), row),
                   pl.BlockSpec((tm, LANES), row)],
        out_shape=[jax.ShapeDtypeStruct((t, ATTN_WIDTH), BF16),
                   jax.ShapeDtypeStruct((t, 2 * KV_WIDTH), BF16),
                   jax.ShapeDtypeStruct((t, SSM_WIDTH), F32),
                   jax.ShapeDtypeStruct((t, CONV_CH), F32),
                   jax.ShapeDtypeStruct((t, LANES), F32)],
        compiler_params=_cparams("parallel"),
    )(x2, g, sc, sh, cos, sin, w, dtb)


def _attn_kernel(sink_ref, q_ref, kv_ref, kvp_ref, o_ref, *, tiles_per_seq):
    first = (pl.program_id(0) % tiles_per_seq) == 0
    blk = WINDOW
    nsub = ATTN_TILE // blk
    lane = lax.broadcasted_iota(jnp.int32, (1, LANES), 1)
    k_lo_mask = (lane % HEAD_DIM) < (HEAD_DIM // 2)
    v_lo_mask = lane < HEAD_DIM
    zero = jnp.zeros((), BF16)

    k_all = jnp.concatenate([kvp_ref[:, 0:LANES], kv_ref[:, 0:LANES]], axis=0)
    v_all = jnp.concatenate([kvp_ref[:, LANES:2 * LANES], kv_ref[:, LANES:2 * LANES]], axis=0)
    k_sel = (jnp.where(k_lo_mask, k_all, zero), jnp.where(k_lo_mask, zero, k_all))
    v_sel = (jnp.where(v_lo_mask, v_all, zero), jnp.where(v_lo_mask, zero, v_all))

    qi = lax.broadcasted_iota(jnp.int32, (blk, blk), 0)
    col = lax.broadcasted_iota(jnp.int32, (blk, blk), 1)
    cur = col <= qi
    no_prev = (col > qi) & (col > jnp.where(first, -1, blk))

    for n in range(nsub):
        q_st = jnp.concatenate([q_ref[n * blk:(n + 1) * blk, i * LANES:(i + 1) * LANES]
                                for i in range(Q_TILES)], axis=0)
        out = None
        for hk in range(KV_HEADS):
            k_n = k_sel[hk][n * blk:(n + 2) * blk]
            v_n = v_sel[hk][n * blk:(n + 2) * blk]
            s = lax.dot_general(q_st, k_n, (((1,), (1,)), ((), ())), preferred_element_type=F32)
            probs, scales = [], []
            for i in range(GROUP):
                s_i = s[i * blk:(i + 1) * blk]
                sc = jnp.where(cur, s_i[:, blk:], s_i[:, :blk])
                if n == 0:
                    sc = jnp.where(no_prev, -jnp.inf, sc)
                sink = sink_ref[hk * GROUP + i]
                m = jnp.maximum(jnp.max(sc, axis=-1, keepdims=True), sink)
                p = jnp.exp(sc - m)
                scales.append(1.0 / (jnp.sum(p, axis=-1, keepdims=True) + jnp.exp(sink - m)))
                probs.append(jnp.concatenate([jnp.where(cur, 0.0, p), jnp.where(cur, p, 0.0)],
                                             axis=1).astype(BF16))
            o = jnp.dot(jnp.concatenate(probs, axis=0), v_n, preferred_element_type=F32)
            o = o * jnp.concatenate(scales, axis=0)
            out = o if out is None else out + o
        for i in range(Q_TILES):
            o_ref[n * blk:(n + 1) * blk, i * LANES:(i + 1) * LANES] = out[i * blk:(i + 1) * blk].astype(BF16)


def _attention(q, kv, sinks, seq):
    t = q.shape[0]
    tq = ATTN_TILE
    r = tq // WINDOW
    kern = functools.partial(_attn_kernel, tiles_per_seq=seq // tq)
    return pl.pallas_call(
        kern,
        grid=(t // tq,),
        in_specs=[pl.BlockSpec(memory_space=pltpu.SMEM),
                  pl.BlockSpec((tq, ATTN_WIDTH), lambda i: (i, 0)),
                  pl.BlockSpec((tq, 2 * KV_WIDTH), lambda i: (i, 0)),
                  pl.BlockSpec((WINDOW, 2 * KV_WIDTH), lambda i: (jnp.maximum(i * r - 1, 0), 0))],
        out_specs=pl.BlockSpec((tq, ATTN_WIDTH), lambda i: (i, 0)),
        out_shape=jax.ShapeDtypeStruct((t, ATTN_WIDTH), BF16),
        compiler_params=_cparams("parallel"),
    )(sinks, q, kv, kv)


def _split3(v):
    p0 = v.astype(BF16)
    r1 = v - p0.astype(F32)
    p1 = r1.astype(BF16)
    return p0, p1, (r1 - p1.astype(F32)).astype(BF16)


def _dot3(lhs_exact_bf16, v):
    return sum(jnp.dot(lhs_exact_bf16, p, preferred_element_type=F32) for p in _split3(v))


def _ssd_kernel(xbc_ref, z_ref, dt_ref, cw_ref, cb_ref, alog_ref, dskip_ref, ng_ref,
                o_ref, ext_ref, st_ref):
    L = CHUNK
    rows = xbc_ref.shape[0]
    halo = SUBLANES

    @pl.when(pl.program_id(1) == 0)
    def _():
        ext_ref[0:halo, :] = jnp.zeros((halo, CONV_CH), F32)
        st_ref[...] = jnp.zeros(st_ref.shape, F32)

    ext_ref[halo:halo + rows, :] = xbc_ref[...]
    acc = cb_ref[...] + cw_ref[CONV_WIDTH - 1:CONV_WIDTH, :] * ext_ref[halo:halo + rows, :]
    for k in range(CONV_WIDTH - 1):
        off = halo - (CONV_WIDTH - 1) + k
        acc = acc + cw_ref[k:k + 1, :] * ext_ref[off:off + rows, :]
    ext_ref[0:halo, :] = ext_ref[rows:rows + halo, :]
    u_all = _silu(acc)

    lane = lax.broadcasted_iota(jnp.int32, (1, LANES), 1)
    a = jnp.where(lane < SSM_HEADS, -jnp.exp(alog_ref[...]), 0.0)
    row = lax.broadcasted_iota(jnp.int32, (L, L), 0)
    col = lax.broadcasted_iota(jnp.int32, (L, L), 1)
    causal = row >= col
    tri = jnp.where(causal, 1.0, 0.0).astype(BF16)
    spread = jnp.where(lax.broadcasted_iota(jnp.int32, (LANES, SSM_WIDTH), 1) // SSM_HEAD_DIM
                       == lax.broadcasted_iota(jnp.int32, (LANES, SSM_WIDTH), 0), 1.0, 0.0).astype(BF16)
    r_heads = SSM_HEADS // SSM_GROUPS
    glane = lax.broadcasted_iota(jnp.int32, (1, GROUP_WIDTH), 1) // SSM_HEAD_DIM
    zero = jnp.zeros((), BF16)
    states = [st_ref[g] for g in range(SSM_GROUPS)]

    for c in range(rows // L):
        rs = slice(c * L, (c + 1) * L)
        u = u_all[rs]
        xs = u[:, 0:SSM_WIDTH]
        bm = u[:, SSM_WIDTH:SSM_WIDTH + SSM_GROUPS * SSM_STATE]
        cm = u[:, SSM_WIDTH + SSM_GROUPS * SSM_STATE:]
        dt = dt_ref[rs, :]
        acs = _dot3(tri, dt * a)
        acs_t = acs.T
        dt_e = sum(jnp.dot(p, spread, preferred_element_type=F32) for p in _split3(dt))
        acs_e = sum(jnp.dot(p, spread, preferred_element_type=F32) for p in _split3(acs))
        last = acs_e[L - 1:L, :]
        xd = xs * dt_e
        xd_b = xd.astype(BF16)
        xdw_b = (xd * jnp.exp(last - acs_e)).astype(BF16)
        e_acs = jnp.exp(acs_e)
        c_dec = jnp.exp(last)

        ys = []
        for g in range(SSM_GROUPS):
            gs = slice(g * GROUP_WIDTH, (g + 1) * GROUP_WIDTH)
            b_g = bm[:, g * SSM_STATE:(g + 1) * SSM_STATE]
            c_b = cm[:, g * SSM_STATE:(g + 1) * SSM_STATE].astype(BF16)
            cb = lax.dot_general(c_b, b_g.astype(BF16), (((1,), (1,)), ((), ())),
                                 preferred_element_type=F32)
            st = states[g]
            y_g = jnp.dot(c_b, st.astype(BF16), preferred_element_type=F32) * e_acs[:, gs]
            xd_g = xd_b[:, gs]
            for r in range(r_heads):
                h = g * r_heads + r
                seg = acs[:, h:h + 1] - acs_t[h:h + 1, :]
                m_h = (cb * jnp.exp(jnp.where(causal, seg, -jnp.inf))).astype(BF16)
                y_g = y_g + jnp.dot(m_h, jnp.where(glane == r, xd_g, zero), preferred_element_type=F32)
            new = jnp.dot(b_g.T.astype(BF16), xdw_b[:, gs], preferred_element_type=F32)
            states[g] = c_dec[:, gs] * st + new
            ys.append(y_g)

        y = jnp.concatenate(ys, axis=1) + dskip_ref[...] * xs
        y = y * _silu(z_ref[rs, :])
        outs = []
        for g in range(SSM_GROUPS):
            yg = y[:, g * GROUP_WIDTH:(g + 1) * GROUP_WIDTH]
            outs.append(yg * lax.rsqrt(jnp.mean(yg * yg, axis=-1, keepdims=True) + EPS))
        o_ref[rs, :] = (jnp.concatenate(outs, axis=1) * ng_ref[...]).astype(BF16)

    for g in range(SSM_GROUPS):
        st_ref[g] = states[g]


def _ssd(xbc, z, dt, cw, cb, alog, dskip, ng, batch, seq):
    t = xbc.shape[0]
    rows = SSD_CHUNKS_PER_STEP * CHUNK
    ns = seq // rows
    row = lambda b, c: (b * ns + c, 0)
    const = lambda b, c: (0, 0)
    return pl.pallas_call(
        _ssd_kernel,
        grid=(batch, ns),
        in_specs=[pl.BlockSpec((rows, CONV_CH), row),
                  pl.BlockSpec((rows, SSM_WIDTH), row),
                  pl.BlockSpec((rows, LANES), row),
                  pl.BlockSpec((CONV_WIDTH, CONV_CH), const),
                  pl.BlockSpec((1, CONV_CH), const),
                  pl.BlockSpec((1, LANES), const),
                  pl.BlockSpec((1, SSM_WIDTH), const),
                  pl.BlockSpec((1, SSM_WIDTH), const)],
        out_specs=pl.BlockSpec((rows, SSM_WIDTH), row),
        out_shape=jax.ShapeDtypeStruct((t, SSM_WIDTH), BF16),
        scratch_shapes=[pltpu.VMEM((SUBLANES + rows, CONV_CH), F32),
                        pltpu.VMEM((SSM_GROUPS, SSM_STATE, GROUP_WIDTH), F32)],
        compiler_params=_cparams("parallel", "arbitrary"),
    )(xbc, z, dt, cw, cb, alog, dskip, ng)


def _outproj_core(a_ref, s_ref, x_ref, w_ref, gm_ref, g_ref, sc_ref, sh_ref):
    half = a_ref.shape[1]
    mixed = (jnp.dot(a_ref[...], w_ref[0:half, :], preferred_element_type=F32)
             + jnp.dot(s_ref[...], w_ref[half:, :], preferred_element_type=F32))
    x_new = x_ref[...] + gm_ref[0] * mixed
    return x_new, _rms_mod(x_new, g_ref[...], sc_ref[0], sh_ref[0])


def _outproj_router_kernel(a_ref, s_ref, x_ref, w_ref, gm_ref, g_ref, sc_ref, sh_ref, rw_ref,
                           xo_ref, h_ref, route_ref, gate_ref, cnt_ref, base_ref, carry_ref):
    x_new, h = _outproj_core(a_ref, s_ref, x_ref, w_ref, gm_ref, g_ref, sc_ref, sh_ref)
    xo_ref[...] = x_new
    tm = h.shape[0]
    h_ref[...] = h.astype(BF16)

    h_hi = h.astype(BF16)
    h_lo = (h - h_hi.astype(F32)).astype(BF16)
    hi_part = jnp.dot(h_hi, rw_ref[...], preferred_element_type=F32)
    logits = (hi_part[:, 0:LANES] + hi_part[:, LANES:]
              + jnp.dot(h_lo, rw_ref[:, 0:LANES], preferred_element_type=F32))
    lane = lax.broadcasted_iota(jnp.int32, (tm, LANES), 1).astype(F32)
    logits = jnp.where(lane < N_EXPERTS, logits, -jnp.inf)
    m0 = jnp.max(logits, axis=-1, keepdims=True)
    i0 = jnp.min(jnp.where(logits == m0, lane, float(LANES)), axis=-1, keepdims=True)
    rest = jnp.where(lane == i0, -jnp.inf, logits)
    m1 = jnp.max(rest, axis=-1, keepdims=True)
    i1 = jnp.min(jnp.where(rest == m1, lane, float(LANES)), axis=-1, keepdims=True)
    e = jnp.exp(m1 - m0)
    g0 = 1.0 / (1.0 + e)
    g1 = e / (1.0 + e)

    @pl.when(pl.program_id(0) == 0)
    def _():
        carry_ref[...] = jnp.zeros(carry_ref.shape, F32)

    sel0 = lane == i0
    sel1 = lane == i1
    member = jnp.where(sel0 | sel1, 1.0, 0.0)
    r_i = lax.broadcasted_iota(jnp.int32, (tm, tm), 0)
    c_i = lax.broadcasted_iota(jnp.int32, (tm, tm), 1)
    strict = jnp.where(r_i > c_i, 1.0, 0.0).astype(BF16)
    base_ref[0] = carry_ref[...]
    rank = carry_ref[...] + jnp.dot(strict, member.astype(BF16), preferred_element_type=F32)
    r0 = jnp.sum(jnp.where(sel0, rank, 0.0), axis=-1, keepdims=True)
    r1 = jnp.sum(jnp.where(sel1, rank, 0.0), axis=-1, keepdims=True)
    carry_ref[...] = carry_ref[...] + jnp.sum(member, axis=0, keepdims=True)
    cnt_ref[...] = carry_ref[...]

    cols = r1
    for k, colv in enumerate((g0, g1, i0, i1, r0)):
        cols = jnp.where(lane == float(k), colv, cols)
    gate_ref[...] = cols[:, 0:SUBLANES]
    diag = (lax.broadcasted_iota(jnp.int32, (tm, LANES), 0) % LANES
            == lax.broadcasted_iota(jnp.int32, (tm, LANES), 1))
    nrow = tm // LANES
    for q, colv in enumerate((i0, i1, r0, r1)):
        spread = jnp.where(diag, colv, 0.0)
        rows = [jnp.sum(spread[b * LANES:(b + 1) * LANES], axis=0, keepdims=True) for b in range(nrow)]
        route_ref[0, q * nrow:(q + 1) * nrow, :] = jnp.concatenate(rows, axis=0).astype(jnp.int32)


def _outproj_router(attn, ssm, x2, w, gm, g, sc, sh, tiles_per_batch, router):
    t, d = x2.shape
    tm = ROW_TILE
    row = lambda i: (i, 0)
    const = lambda i: (0, 0)
    per_b = lambda i: (i // tiles_per_batch, 0, 0)
    half = attn.shape[1]
    return pl.pallas_call(
        _outproj_router_kernel,
        grid=(t // tm,),
        in_specs=[pl.BlockSpec((tm, half), row),
                  pl.BlockSpec((tm, half), row),
                  pl.BlockSpec((tm, d), row),
                  pl.BlockSpec((2 * half, d), const),
                  pl.BlockSpec((1, 1, d), per_b),
                  pl.BlockSpec((1, d), const),
                  pl.BlockSpec((1, 1, d), per_b),
                  pl.BlockSpec((1, 1, d), per_b),
                  pl.BlockSpec((d, 2 * LANES), const)],
        out_specs=[pl.BlockSpec((tm, d), row),
                   pl.BlockSpec((tm, d), row),
                   pl.BlockSpec((1, 4 * tm // LANES, LANES), lambda i: (i, 0, 0)),
                   pl.BlockSpec((tm, SUBLANES), row),
                   pl.BlockSpec((1, LANES), const),
                   pl.BlockSpec((1, 1, LANES), lambda i: (i, 0, 0))],
        out_shape=[jax.ShapeDtypeStruct((t, d), F32),
                   jax.ShapeDtypeStruct((t, d), BF16),
                   jax.ShapeDtypeStruct((t // tm, 4 * tm // LANES, LANES), jnp.int32),
                   jax.ShapeDtypeStruct((t, SUBLANES), F32),
                   jax.ShapeDtypeStruct((1, LANES), F32),
                   jax.ShapeDtypeStruct((t // tm, 1, LANES), F32)],
        scratch_shapes=[pltpu.VMEM((1, LANES), F32)],
        compiler_params=_cparams("arbitrary"),
    )(attn, ssm, x2, w, gm, g, sc, sh, router)


def _swiglu_rows(x, wg, wu, wd):
    gate = jnp.dot(x, wg, preferred_element_type=F32)
    up = jnp.dot(x, wu, preferred_element_type=F32)
    return jnp.dot((_silu(gate) * up).astype(BF16), wd, preferred_element_type=F32)


def _outproj_ffn_kernel(a_ref, s_ref, x_ref, w_ref, gm_ref, g_ref, sc_ref, sh_ref, wg_ref, wu_ref, wd_ref, gf_ref,
                        o_ref):
    half = a_ref.shape[1]
    piece = x_ref.shape[0] // ROW_SPLIT
    for r in range(ROW_SPLIT):
        rs = slice(r * piece, (r + 1) * piece)
        mixed = (jnp.dot(a_ref[rs, :], w_ref[0:half, :], preferred_element_type=F32)
                 + jnp.dot(s_ref[rs, :], w_ref[half:, :], preferred_element_type=F32))
        x_new = x_ref[rs, :] + gm_ref[0] * mixed
        h = _rms_mod(x_new, g_ref[...], sc_ref[0], sh_ref[0]).astype(BF16)
        o_ref[rs, :] = x_new + gf_ref[0] * _swiglu_rows(h, wg_ref[...], wu_ref[...], wd_ref[...])


def _outproj_ffn(attn, ssm, x2, w, gm, g, sc, sh, wg, wu, wd, gf, tiles_per_batch):
    t, d = x2.shape
    f = wg.shape[1]
    tm = ROW_TILE
    half = attn.shape[1]
    row = lambda i: (i, 0)
    const = lambda i: (0, 0)
    per_b = lambda i: (i // tiles_per_batch, 0, 0)
    resident = pl.Buffered(1)
    return pl.pallas_call(
        _outproj_ffn_kernel,
        grid=(t // tm,),
        in_specs=[pl.BlockSpec((tm, half), row),
                  pl.BlockSpec((tm, half), row),
                  pl.BlockSpec((tm, d), row),
                  pl.BlockSpec((2 * half, d), const, pipeline_mode=resident),
                  pl.BlockSpec((1, 1, d), per_b),
                  pl.BlockSpec((1, d), const),
                  pl.BlockSpec((1, 1, d), per_b),
                  pl.BlockSpec((1, 1, d), per_b),
                  pl.BlockSpec((d, f), const, pipeline_mode=resident),
                  pl.BlockSpec((d, f), const, pipeline_mode=resident),
                  pl.BlockSpec((f, d), const, pipeline_mode=resident),
                  pl.BlockSpec((1, 1, d), per_b)],
        out_specs=pl.BlockSpec((tm, d), row),
        out_shape=jax.ShapeDtypeStruct((t, d), F32),
        compiler_params=_cparams("parallel"),
    )(attn, ssm, x2, w, gm, g, sc, sh, wg, wu, wd, gf)


def _segment_copies(count, src, src_row, dst, dst_row, sem, max_rows):
    for k in range(max_rows.bit_length() - 1, -1, -1):
        size = (1 << k) * SUBLANES

        @pl.when(((count >> k) & 1) == 1)
        def _(k=k, size=size):
            done = (count >> (k + 1)) << (k + 1)
            pltpu.make_async_copy(
                src.at[pl.ds(pl.multiple_of((src_row + done) * SUBLANES, SUBLANES), size)],
                dst.at[pl.ds(pl.multiple_of((dst_row + done) * SUBLANES, SUBLANES), size)], sem).start()


def _lane_rows(route_ref, q, nrow):
    return jnp.concatenate([route_ref[0, q * nrow + b:q * nrow + b + 1, :] for b in range(nrow)], axis=1)


def _dispatch_kernel(delta_ref, cnt_ref, off_ref, dst_ref, zs_ref, ze_ref, nvalid_ref,
                     route_ref, h_ref, xb_hbm, g_ref, zero_ref, sem, zsem):
    i = pl.program_id(0)
    n = pl.num_programs(0)
    tm = h_ref.shape[0]
    nslot = TOP_K * tm
    slot = i % 2
    blk_rows = zero_ref.shape[0]
    n_blocks = xb_hbm.shape[0] // blk_rows

    def zero_row(r):
        return pltpu.make_async_copy(
            zero_ref.at[pl.ds(0, SUBLANES)],
            xb_hbm.at[pl.ds(pl.multiple_of(r * SUBLANES, SUBLANES), SUBLANES)], zsem)

    def zero_block(b):
        return pltpu.make_async_copy(
            zero_ref, xb_hbm.at[pl.ds(pl.multiple_of(b * blk_rows, blk_rows), blk_rows)], zsem)

    @pl.when(i == 0)
    def _():
        zero_ref[...] = jnp.zeros(zero_ref.shape, F32)
        for start in (True, False):
            for e in range(N_EXPERTS):
                def rows_body(r, c):
                    zero_row(r).start() if start else zero_row(r).wait()
                    return c
                lax.fori_loop(zs_ref[e], ze_ref[e], rows_body, 0)

            def blocks_body(b, c):
                zero_block(b).start() if start else zero_block(b).wait()
                return c
            lax.fori_loop(nvalid_ref[0], n_blocks, blocks_body, 0)

    def wait_slot(sl):
        pltpu.make_async_copy(g_ref.at[sl], xb_hbm.at[pl.ds(0, nslot * SUBLANES)], sem.at[sl]).wait()

    @pl.when(i >= 2)
    def _():
        wait_slot(slot)

    nrow = tm // LANES
    experts = (_lane_rows(route_ref, 0, nrow), _lane_rows(route_ref, 1, nrow))
    ranks = (_lane_rows(route_ref, 2, nrow), _lane_rows(route_ref, 3, nrow))
    row_id = lax.broadcasted_iota(jnp.int32, (nslot, tm), 0)
    hit = None
    for k in range(TOP_K):
        delta = jnp.zeros((1, tm), jnp.int32)
        for e in range(N_EXPERTS):
            delta = jnp.where(experts[k] == e, delta_ref[i * N_EXPERTS + e], delta)
        mine = row_id == delta + ranks[k]
        hit = mine if hit is None else hit | mine
    grouped = jnp.dot(jnp.where(hit, 1.0, 0.0).astype(BF16), h_ref[...], preferred_element_type=F32)
    for s in range(grouped.shape[1] // LANES):
        g_ref[slot, pl.ds(s, nslot, stride=SUBLANES), :] = grouped[:, s * LANES:(s + 1) * LANES]
    for e in range(N_EXPERTS):
        _segment_copies(cnt_ref[i * N_EXPERTS + e], g_ref.at[slot], off_ref[i * N_EXPERTS + e],
                        xb_hbm, dst_ref[i * N_EXPERTS + e], sem.at[slot], nslot)

    @pl.when(i == n - 1)
    def _():
        wait_slot(slot)

    @pl.when((i == n - 1) & (n >= 2))
    def _():
        wait_slot(1 - slot)


def _dispatch(h, route, seg, zs, ze, nvalid, n_blocks, rows):
    t, d = h.shape
    tm = ROW_TILE
    nslot = TOP_K * tm
    nrow = tm // LANES
    grid_spec = pltpu.PrefetchScalarGridSpec(
        num_scalar_prefetch=7,
        grid=(t // tm,),
        in_specs=[pl.BlockSpec((1, 4 * nrow, LANES), lambda i, *_: (i, 0, 0)),
                  pl.BlockSpec((tm, d), lambda i, *_: (i, 0))],
        out_specs=pl.BlockSpec(memory_space=pl.ANY),
        scratch_shapes=[pltpu.VMEM((2, nslot * SUBLANES, LANES), F32),
                        pltpu.VMEM((rows * SUBLANES, LANES), F32),
                        pltpu.SemaphoreType.DMA((2,)), pltpu.SemaphoreType.DMA(())],
    )
    return pl.pallas_call(
        _dispatch_kernel,
        grid_spec=grid_spec,
        out_shape=jax.ShapeDtypeStruct((n_blocks * rows * SUBLANES, LANES), F32),
        compiler_params=pltpu.CompilerParams(dimension_semantics=("arbitrary",), has_side_effects=True,
                                             vmem_limit_bytes=VMEM_LIMIT),
    )(*seg, zs, ze, nvalid, route, h)


def _moe_kernel(bexp_ref, nvalid_ref, brows_ref, xb_ref, wg_ref, wu_ref, wd_ref, y_ref, x_scr, acc_ref):
    b = pl.program_id(0)
    j = pl.program_id(1)
    nf = pl.num_programs(1)
    rows = x_scr.shape[0]
    nsl = x_scr.shape[1] // LANES
    valid = b < nvalid_ref[0]

    @pl.when(valid & (j == 0))
    def _():
        for s in range(nsl):
            x_scr[:, s * LANES:(s + 1) * LANES] = xb_ref[pl.ds(s, rows, stride=SUBLANES), :].astype(BF16)
        acc_ref[...] = jnp.zeros(acc_ref.shape, F32)

    piece = rows // ROW_SPLIT
    used = (brows_ref[b] + piece - 1) // piece
    for n_used in range(1, ROW_SPLIT + 1):
        @pl.when(valid & (used == n_used))
        def _():
            wg = wg_ref[0].astype(BF16)
            wu = wu_ref[0].astype(BF16)
            wd = wd_ref[0].astype(BF16)
            for r in range(n_used):
                rs = slice(r * piece, (r + 1) * piece)
                acc_ref[rs, :] += _swiglu_rows(x_scr[rs, :], wg, wu, wd)

    @pl.when(valid & (j == nf - 1))
    def _():
        for s in range(nsl):
            y_ref[pl.ds(s, rows, stride=SUBLANES), :] = acc_ref[:, s * LANES:(s + 1) * LANES]

    @pl.when(jnp.logical_not(valid) & (j == nf - 1))
    def _():
        y_ref[...] = jnp.zeros(y_ref.shape, F32)


def _moe_experts(xb, bexp, nvalid, brows, wg, wu, wd, n_blocks, rows):
    d = wg.shape[1]
    f = wg.shape[2]
    nf = f // MOE_FT

    def blk(b, j, bexp, nvalid, brows):
        return (jnp.minimum(b, nvalid[0] - 1), 0)

    def fidx(b, j, nvalid):
        return jnp.where(b < nvalid[0], j, nf - 1)

    grid_spec = pltpu.PrefetchScalarGridSpec(
        num_scalar_prefetch=3,
        grid=(n_blocks, nf),
        in_specs=[pl.BlockSpec((rows * SUBLANES, LANES), blk),
                  pl.BlockSpec((1, d, MOE_FT), lambda b, j, bexp, nvalid, brows: (bexp[b], 0, fidx(b, j, nvalid))),
                  pl.BlockSpec((1, d, MOE_FT), lambda b, j, bexp, nvalid, brows: (bexp[b], 0, fidx(b, j, nvalid))),
                  pl.BlockSpec((1, MOE_FT, d), lambda b, j, bexp, nvalid, brows: (bexp[b], fidx(b, j, nvalid), 0))],
        out_specs=pl.BlockSpec((rows * SUBLANES, LANES), lambda b, j, bexp, nvalid, brows: (b, 0)),
        scratch_shapes=[pltpu.VMEM((rows, d), BF16), pltpu.VMEM((rows, d), F32)],
    )
    return pl.pallas_call(
        _moe_kernel,
        grid_spec=grid_spec,
        out_shape=jax.ShapeDtypeStruct(xb.shape, F32),
        compiler_params=_cparams("arbitrary", "arbitrary"),
    )(bexp, nvalid, brows, xb, wg, wu, wd)


def _combine_kernel(delta_ref, cnt_ref, off_ref, dst_ref, y_hbm, x_ref, gate_ref, gf_ref, fg_ref, o_ref,
                    buf_ref, sem):
    i = pl.program_id(0)
    tm = x_ref.shape[0]
    nslot = TOP_K * tm
    nsl = x_ref.shape[1] // LANES
    slot = i % 2

    def fetch(tile, to_slot):
        for e in range(N_EXPERTS):
            _segment_copies(cnt_ref[tile * N_EXPERTS + e], y_hbm, dst_ref[tile * N_EXPERTS + e],
                            buf_ref.at[to_slot], off_ref[tile * N_EXPERTS + e], sem.at[to_slot], nslot)

    @pl.when(i == 0)
    def _():
        fetch(0, 0)

    @pl.when(i + 1 < pl.num_programs(0))
    def _():
        fetch(i + 1, 1 - slot)

    pltpu.make_async_copy(y_hbm.at[pl.ds(0, nslot * SUBLANES)], buf_ref.at[slot], sem.at[slot]).wait()

    y_rows = jnp.concatenate([buf_ref[slot, pl.ds(s, nslot, stride=SUBLANES), :] for s in range(nsl)],
                             axis=1).astype(BF16)
    cols = gate_ref[...]
    lane = lax.broadcasted_iota(jnp.int32, (tm, nslot), 1).astype(F32)
    weights = None
    for k in range(TOP_K):
        expert = cols[:, 2 + k:3 + k]
        delta = jnp.zeros((tm, 1), F32)
        for e in range(N_EXPERTS):
            delta = jnp.where(expert == float(e), delta_ref[i * N_EXPERTS + e].astype(F32), delta)
        w_k = jnp.where(lane == delta + cols[:, 4 + k:5 + k], cols[:, k:k + 1], 0.0)
        weights = w_k if weights is None else weights + w_k
    f = jnp.dot(weights.astype(BF16), y_rows, preferred_element_type=F32)
    x = x_ref[...] + gf_ref[0] * f
    ms = jnp.mean(x * x, axis=-1, keepdims=True)
    o_ref[...] = x * lax.rsqrt(ms + EPS) * fg_ref[...]


def _combine(y, seg, x2, gates, gf, fg, tiles_per_batch):
    t, d = x2.shape
    tm = ROW_TILE
    nslot = TOP_K * tm
    grid_spec = pltpu.PrefetchScalarGridSpec(
        num_scalar_prefetch=4,
        grid=(t // tm,),
        in_specs=[pl.BlockSpec(memory_space=pl.ANY),
                  pl.BlockSpec((tm, d), lambda i, *_: (i, 0)),
                  pl.BlockSpec((tm, SUBLANES), lambda i, *_: (i, 0)),
                  pl.BlockSpec((1, 1, d), lambda i, *_: (i // tiles_per_batch, 0, 0)),
                  pl.BlockSpec((1, d), lambda i, *_: (0, 0))],
        out_specs=pl.BlockSpec((tm, d), lambda i, *_: (i, 0)),
        scratch_shapes=[pltpu.VMEM((2, nslot * SUBLANES, LANES), F32), pltpu.SemaphoreType.DMA((2,))],
    )
    return pl.pallas_call(
        _combine_kernel,
        grid_spec=grid_spec,
        out_shape=jax.ShapeDtypeStruct((t, d), F32),
        compiler_params=_cparams("arbitrary"),
    )(*seg, y, x2, gates, gf, fg)


def _final_norm_kernel(x_ref, g_ref, o_ref):
    x = x_ref[...]
    o_ref[...] = x * lax.rsqrt(jnp.mean(x * x, axis=-1, keepdims=True) + EPS) * g_ref[...]


def _final_norm(x2, g):
    t, d = x2.shape
    tm = ROW_TILE
    return pl.pallas_call(
        _final_norm_kernel,
        grid=(t // tm,),
        in_specs=[pl.BlockSpec((tm, d), lambda i: (i, 0)), pl.BlockSpec((1, d), lambda i: (0, 0))],
        out_specs=pl.BlockSpec((tm, d), lambda i: (i, 0)),
        out_shape=jax.ShapeDtypeStruct((t, d), F32),
        compiler_params=_cparams("parallel"),
    )(x2, g)


def _qk_column_perm():
    half = HEAD_DIM // 2
    lane = np.arange(LANES)
    pair = (lane % HEAD_DIM) // half
    dim = lane % half + half * (lane // HEAD_DIM)
    q = np.concatenate([(i + Q_TILES * pair) * HEAD_DIM + dim for i in range(Q_TILES)])
    k = ATTN_WIDTH + pair * HEAD_DIM + dim
    return q, k


def _attn_out_row_perm():
    lane = np.arange(LANES)
    return np.concatenate([(i + Q_TILES * (lane // HEAD_DIM)) * HEAD_DIM + lane % HEAD_DIM
                           for i in range(Q_TILES)])


def _pad_lanes(v):
    return jnp.pad(v.astype(F32), (0, LANES - v.shape[0])).reshape(1, LANES)


def _moe_route_tables(counts, base, rows, n_blocks):
    cnt = counts[0, :N_EXPERTS].astype(jnp.int32)
    padded = (cnt + rows - 1) // rows * rows
    pad_end = jnp.cumsum(padded)
    pad_start = pad_end - padded
    first = base[:, 0, :N_EXPERTS].astype(jnp.int32)
    run = jnp.concatenate([first[1:], cnt[None]], axis=0) - first
    off = jnp.cumsum(run, axis=1) - run
    flat = lambda a: a.reshape(-1).astype(jnp.int32)
    seg = (flat(off - first), flat(run), flat(off), flat(pad_start[None, :] + first))
    blk_start = jnp.arange(n_blocks, dtype=jnp.int32) * rows
    bexp = jnp.minimum(jnp.sum(blk_start[:, None] >= pad_end[None, :], axis=1), N_EXPERTS - 1).astype(jnp.int32)
    nvalid = (pad_end[-1:] // rows).astype(jnp.int32)
    row_end = pad_start + cnt
    brows = jnp.clip(row_end[bexp] - blk_start, 0, rows).astype(jnp.int32)
    return seg, bexp, nvalid, brows, row_end.astype(jnp.int32), pad_end.astype(jnp.int32)


def kernel(x, c, positions, ada_w, ada_b, norm_mix_g, norm_ffn_g, w_in, w_out, attn_sinks, conv_w, conv_b,
           dt_bias, a_log, d_skip, ssm_norm_g, ffn_w_gate, ffn_w_up, ffn_w_down, router_w, moe_w_gate,
           moe_w_up, moe_w_down, final_norm_g):
    batch, seq, d = x.shape
    depth = w_in.shape[0]
    t = batch * seq
    tiles_per_batch = seq // ROW_TILE
    x2 = x.reshape(t, d)

    c8 = jnp.pad(c, ((0, SUBLANES - batch), (0, 0)))
    mod = _ada_mod(c8, ada_w, ada_b)[:, :batch].reshape(depth, batch, 6, 1, d)
    cos, sin = _rope_tables(positions)

    q_perm, k_perm = _qk_column_perm()
    o_perm = _attn_out_row_perm()
    for l in range(depth):
        sh_m, sc_m, g_m, sh_f, sc_f, g_f = (mod[l, :, k] for k in range(6))
        wl = w_in[l]
        w_cat = jnp.concatenate(
            [wl[:, q_perm], wl[:, k_perm], wl[:, C_V:C_DT],
             jnp.pad(wl[:, C_DT:], ((0, 0), (0, LANES - SSM_HEADS)))], axis=1).astype(BF16)
        q, kv, z, xbc, dt = _inproj(x2, norm_mix_g[l].reshape(1, d), sc_m, sh_m, cos, sin, w_cat,
                                    _pad_lanes(dt_bias[l]), tiles_per_batch)
        attn = _attention(q, kv, attn_sinks[l].astype(F32), seq)
        ssm = _ssd(xbc, z, dt, conv_w[l], conv_b[l].reshape(1, CONV_CH), _pad_lanes(a_log[l]),
                   jnp.repeat(d_skip[l].astype(F32), SSM_HEAD_DIM).reshape(1, SSM_WIDTH),
                   ssm_norm_g[l].reshape(1, SSM_WIDTH), batch, seq)
        wo = jnp.concatenate([w_out[l][o_perm], w_out[l][ATTN_WIDTH:]], axis=0).astype(BF16)
        ffn_g = norm_ffn_g[l].reshape(1, d)
        if l % 2 == 0:
            x2 = _outproj_ffn(attn, ssm, x2, wo, g_m, ffn_g, sc_f, sh_f, ffn_w_gate[l // 2].astype(BF16),
                              ffn_w_up[l // 2].astype(BF16), ffn_w_down[l // 2].astype(BF16), g_f, tiles_per_batch)
            if l == depth - 1:
                x2 = _final_norm(x2, final_norm_g.reshape(1, d))
        else:
            if l != depth - 1:
                raise NotImplementedError("the expert layer fuses the final norm and must be last")
            rw = jnp.pad(router_w[l // 2].astype(F32), ((0, 0), (0, LANES - N_EXPERTS)))
            rw_hi = rw.astype(BF16)
            rw_lo = (rw - rw_hi.astype(F32)).astype(BF16)
            x_new, h, route, gates, counts, base = _outproj_router(
                attn, ssm, x2, wo, g_m, ffn_g, sc_f, sh_f, tiles_per_batch,
                jnp.concatenate([rw_hi, rw_lo], axis=1))
            rows = min(MOE_ROWS, t)
            n_blocks = (t * TOP_K) // rows + N_EXPERTS
            seg, bexp, nvalid, brows, zs, ze = _moe_route_tables(counts, base, rows, n_blocks)
            xb = _dispatch(h, route, seg, zs, ze, nvalid, n_blocks, rows)
            y = _moe_experts(xb, bexp, nvalid, brows, moe_w_gate[l // 2], moe_w_up[l // 2], moe_w_down[l // 2],
                             n_blocks, rows)
            x2 = _combine(y, seg, x_new, gates, g_f, final_norm_g.reshape(1, d), tiles_per_batch)
    return x2.reshape(batch, seq, d)
```

```python
import functools

import numpy as np
import jax
import jax.numpy as jnp
from jax import lax
from jax.experimental import pallas as pl
from jax.experimental.pallas import tpu as pltpu

F32 = jnp.float32
BF16 = jnp.bfloat16

LANES = 128
SUBLANES = 8
VMEM_LIMIT = 48 * 1024 * 1024
MOE_VMEM_LIMIT = 58 * 1024 * 1024

EPS = 1e-6
HEAD_DIM = 64
Q_HEADS = 8
KV_HEADS = 2
GROUP = Q_HEADS // KV_HEADS
ATTN_WIDTH = Q_HEADS * HEAD_DIM
KV_WIDTH = KV_HEADS * HEAD_DIM
WINDOW = 128
ROPE_THETA = 10000.0
SSM_HEADS = 8
SSM_HEAD_DIM = 64
SSM_WIDTH = SSM_HEADS * SSM_HEAD_DIM
SSM_GROUPS = 2
SSM_STATE = 128
GROUP_WIDTH = SSM_WIDTH // SSM_GROUPS
CONV_WIDTH = 4
CONV_CH = SSM_WIDTH + 2 * SSM_GROUPS * SSM_STATE
CHUNK = 128
N_EXPERTS = 8
TOP_K = 2

Q_TILES = ATTN_WIDTH // LANES
C_Q = 0
C_K = C_Q + ATTN_WIDTH
C_V = C_K + KV_WIDTH
C_Z = C_V + KV_WIDTH
C_X = C_Z + SSM_WIDTH
C_DT = C_X + CONV_CH
C_END = C_DT + LANES

ROW_TILE = 512
ATTN_TILE = 1024
MOE_ROWS = 1536
MOE_FT = 512
SSD_CHUNKS_PER_STEP = 8
ROW_SPLIT = 2
MOE_SPLIT = 3


def _cparams(*sem):
    return pltpu.CompilerParams(dimension_semantics=sem, vmem_limit_bytes=VMEM_LIMIT)


def _silu(v):
    return v * (1.0 / (1.0 + jnp.exp(-v)))


def _softplus(v):
    return jnp.maximum(v, 0.0) + jnp.log1p(jnp.exp(-jnp.abs(v)))


def _rms_mod(x, g, scale, shift):
    ms = jnp.mean(x * x, axis=-1, keepdims=True)
    return (x * lax.rsqrt(ms + EPS) * g) * (1.0 + scale) + shift


def _ada_kernel(c_ref, w_ref, b_ref, o_ref):
    c = c_ref[...]
    o_ref[0] = jnp.dot(_silu(c), w_ref[0], preferred_element_type=F32,
                       precision=lax.Precision.HIGHEST) + b_ref[0]


def _ada_mod(c8, ada_w, ada_b):
    depth, d, n = ada_w.shape
    tn = 1536
    return pl.pallas_call(
        _ada_kernel,
        grid=(depth, n // tn),
        in_specs=[pl.BlockSpec((SUBLANES, d), lambda l, j: (0, 0)),
                  pl.BlockSpec((1, d, tn), lambda l, j: (l, 0, j)),
                  pl.BlockSpec((1, 1, tn), lambda l, j: (l, 0, j))],
        out_specs=pl.BlockSpec((1, SUBLANES, tn), lambda l, j: (l, 0, j)),
        out_shape=jax.ShapeDtypeStruct((depth, SUBLANES, n), F32),
        compiler_params=_cparams("parallel", "parallel"),
    )(c8, ada_w, ada_b.reshape(depth, 1, n))


ROPE_PER_ROW = LANES // (HEAD_DIM // 2)


def _rope_kernel(pos_ref, inv_ref, sign_ref, cos_ref, sin_ref):
    ang = pos_ref[...].astype(F32) * inv_ref[...]
    tr = ang.shape[0]
    group = lax.broadcasted_iota(jnp.int32, (1, LANES), 1) // (HEAD_DIM // 2)
    for table, out_ref, sign in ((jnp.cos(ang), cos_ref, None), (jnp.sin(ang), sin_ref, sign_ref[...])):
        rolled = [table] + [pltpu.roll(table, (HEAD_DIM // 2) * s, axis=1) for s in range(1, ROPE_PER_ROW)]
        for j in range(ROPE_PER_ROW):
            rep = rolled[(0 - j) % ROPE_PER_ROW]
            for g in range(1, ROPE_PER_ROW):
                rep = jnp.where(group == g, rolled[(g - j) % ROPE_PER_ROW], rep)
            out_ref[pl.ds(j, tr, stride=ROPE_PER_ROW), :] = rep if sign is None else rep * sign


def _rope_tables(positions):
    t = positions.size
    half = HEAD_DIM // 2
    pos_dense = jnp.repeat(positions.reshape(t // ROPE_PER_ROW, ROPE_PER_ROW), half, axis=1)
    inv_freq = ROPE_THETA ** (-jnp.arange(0, HEAD_DIM, 2, dtype=F32) / HEAD_DIM)
    inv_dense = jnp.tile(inv_freq, ROPE_PER_ROW).reshape(1, LANES)
    sign = jnp.repeat(jnp.array([-1.0, -1.0, 1.0, 1.0], F32), half).reshape(1, LANES)
    rows = t // ROPE_PER_ROW
    tr = min(1024, rows)
    return pl.pallas_call(
        _rope_kernel,
        grid=(rows // tr,),
        in_specs=[pl.BlockSpec((tr, LANES), lambda i: (i, 0)),
                  pl.BlockSpec((1, LANES), lambda i: (0, 0)),
                  pl.BlockSpec((1, LANES), lambda i: (0, 0))],
        out_specs=[pl.BlockSpec((tr * ROPE_PER_ROW, LANES), lambda i: (i, 0))] * 2,
        out_shape=[jax.ShapeDtypeStruct((t, LANES), F32)] * 2,
        compiler_params=_cparams("parallel"),
    )(pos_dense, inv_dense, sign)


def _inproj_kernel(x_ref, g_ref, sc_ref, sh_ref, cos_ref, sin_ref, w_ref, dtb_ref,
                   q_ref, kv_ref, z_ref, xbc_ref, dt_ref):
    h = _rms_mod(x_ref[...], g_ref[...], sc_ref[0], sh_ref[0]).astype(BF16)
    cos = cos_ref[...]
    sin = sin_ref[...]

    def rope(t):
        return t * cos + pltpu.roll(t, LANES // 2, axis=1) * sin

    qkv = jnp.dot(h, w_ref[:, C_Q:C_Z], preferred_element_type=F32)
    for i in range(Q_TILES):
        q_ref[:, i * LANES:(i + 1) * LANES] = (
            rope(qkv[:, i * LANES:(i + 1) * LANES]) * (HEAD_DIM ** -0.5)).astype(BF16)
    kv_ref[:, 0:LANES] = rope(qkv[:, C_K:C_V]).astype(BF16)
    kv_ref[:, LANES:2 * LANES] = qkv[:, C_V:C_Z].astype(BF16)
    z_ref[...] = jnp.dot(h, w_ref[:, C_Z:C_X], preferred_element_type=F32)
    xbc_ref[...] = jnp.dot(h, w_ref[:, C_X:C_DT], preferred_element_type=F32)
    dt_raw = jnp.dot(h, w_ref[:, C_DT:C_END], preferred_element_type=F32)
    dt_ref[...] = _softplus(dt_raw + dtb_ref[...])


def _inproj(x2, g, sc, sh, cos, sin, w, dtb, tiles_per_batch):
    t, d = x2.shape
    tm = ROW_TILE
    row = lambda i: (i, 0)
    const = lambda i: (0, 0)
    per_b = lambda i: (i // tiles_per_batch, 0, 0)
    return pl.pallas_call(
        _inproj_kernel,
        grid=(t // tm,),
        in_specs=[pl.BlockSpec((tm, d), row),
                  pl.BlockSpec((1, d), const),
                  pl.BlockSpec((1, 1, d), per_b),
                  pl.BlockSpec((1, 1, d), per_b),
                  pl.BlockSpec((tm, LANES), row),
                  pl.BlockSpec((tm, LANES), row),
                  pl.BlockSpec((d, C_END), const),
                  pl.BlockSpec((1, LANES), const)],
        out_specs=[pl.BlockSpec((tm, ATTN_WIDTH), row),
                   pl.BlockSpec((tm, 2 * KV_WIDTH), row),
                   pl.BlockSpec((tm, SSM_WIDTH), row),
                   pl.BlockSpec((tm, CONV_CH), row),
                   pl.BlockSpec((tm, LANES), row)],
        out_shape=[jax.ShapeDtypeStruct((t, ATTN_WIDTH), BF16),
                   jax.ShapeDtypeStruct((t, 2 * KV_WIDTH), BF16),
                   jax.ShapeDtypeStruct((t, SSM_WIDTH), F32),
                   jax.ShapeDtypeStruct((t, CONV_CH), F32),
                   jax.ShapeDtypeStruct((t, LANES), F32)],
        compiler_params=_cparams("parallel"),
    )(x2, g, sc, sh, cos, sin, w, dtb)


def _attn_kernel(sink_ref, q_ref, kv_ref, kvp_ref, o_ref, *, tiles_per_seq):
    first = (pl.program_id(0) % tiles_per_seq) == 0
    blk = WINDOW
    nsub = ATTN_TILE // blk
    lane = lax.broadcasted_iota(jnp.int32, (1, LANES), 1)
    k_lo_mask = (lane % HEAD_DIM) < (HEAD_DIM // 2)
    v_lo_mask = lane < HEAD_DIM
    zero = jnp.zeros((), BF16)

    k_all = jnp.concatenate([kvp_ref[:, 0:LANES], kv_ref[:, 0:LANES]], axis=0)
    v_all = jnp.concatenate([kvp_ref[:, LANES:2 * LANES], kv_ref[:, LANES:2 * LANES]], axis=0)
    k_sel = (jnp.where(k_lo_mask, k_all, zero), jnp.where(k_lo_mask, zero, k_all))
    v_sel = (jnp.where(v_lo_mask, v_all, zero), jnp.where(v_lo_mask, zero, v_all))

    qi = lax.broadcasted_iota(jnp.int32, (blk, blk), 0)
    col = lax.broadcasted_iota(jnp.int32, (blk, blk), 1)
    cur = col <= qi
    no_prev = (col > qi) & (col > jnp.where(first, -1, blk))

    for n in range(nsub):
        q_st = jnp.concatenate([q_ref[n * blk:(n + 1) * blk, i * LANES:(i + 1) * LANES]
                                for i in range(Q_TILES)], axis=0)
        out = None
        for hk in range(KV_HEADS):
            k_n = k_sel[hk][n * blk:(n + 2) * blk]
            v_n = v_sel[hk][n * blk:(n + 2) * blk]
            s = lax.dot_general(q_st, k_n, (((1,), (1,)), ((), ())), preferred_element_type=F32)
            probs, scales = [], []
            for i in range(GROUP):
                s_i = s[i * blk:(i + 1) * blk]
                sc = jnp.where(cur, s_i[:, blk:], s_i[:, :blk])
                if n == 0:
                    sc = jnp.where(no_prev, -jnp.inf, sc)
                sink = sink_ref[hk * GROUP + i]
                m = jnp.maximum(jnp.max(sc, axis=-1, keepdims=True), sink)
                p = jnp.exp(sc - m)
                scales.append(1.0 / (jnp.sum(p, axis=-1, keepdims=True) + jnp.exp(sink - m)))
                probs.append(jnp.concatenate([jnp.where(cur, 0.0, p), jnp.where(cur, p, 0.0)],
                                             axis=1).astype(BF16))
            o = jnp.dot(jnp.concatenate(probs, axis=0), v_n, preferred_element_type=F32)
            o = o * jnp.concatenate(scales, axis=0)
            out = o if out is None else out + o
        for i in range(Q_TILES):
            o_ref[n * blk:(n + 1) * blk, i * LANES:(i + 1) * LANES] = out[i * blk:(i + 1) * blk].astype(BF16)


def _attention(q, kv, sinks, seq):
    t = q.shape[0]
    tq = ATTN_TILE
    r = tq // WINDOW
    kern = functools.partial(_attn_kernel, tiles_per_seq=seq // tq)
    return pl.pallas_call(
        kern,
        grid=(t // tq,),
        in_specs=[pl.BlockSpec(memory_space=pltpu.SMEM),
                  pl.BlockSpec((tq, ATTN_WIDTH), lambda i: (i, 0)),
                  pl.BlockSpec((tq, 2 * KV_WIDTH), lambda i: (i, 0)),
                  pl.BlockSpec((WINDOW, 2 * KV_WIDTH), lambda i: (jnp.maximum(i * r - 1, 0), 0))],
        out_specs=pl.BlockSpec((tq, ATTN_WIDTH), lambda i: (i, 0)),
        out_shape=jax.ShapeDtypeStruct((t, ATTN_WIDTH), BF16),
        compiler_params=_cparams("parallel"),
    )(sinks, q, kv, kv)


def _split3(v):
    p0 = v.astype(BF16)
    r1 = v - p0.astype(F32)
    p1 = r1.astype(BF16)
    return p0, p1, (r1 - p1.astype(F32)).astype(BF16)


def _dot3(lhs_exact_bf16, v):
    return sum(jnp.dot(lhs_exact_bf16, p, preferred_element_type=F32) for p in _split3(v))


def _ssd_kernel(xbc_ref, z_ref, dt_ref, cw_ref, cb_ref, alog_ref, dskip_ref, ng_ref,
                o_ref, ext_ref, st_ref):
    L = CHUNK
    rows = xbc_ref.shape[0]
    halo = SUBLANES

    @pl.when(pl.program_id(1) == 0)
    def _():
        ext_ref[0:halo, :] = jnp.zeros((halo, CONV_CH), F32)
        st_ref[...] = jnp.zeros(st_ref.shape, F32)

    ext_ref[halo:halo + rows, :] = xbc_ref[...]
    acc = cb_ref[...] + cw_ref[CONV_WIDTH - 1:CONV_WIDTH, :] * ext_ref[halo:halo + rows, :]
    for k in range(CONV_WIDTH - 1):
        off = halo - (CONV_WIDTH - 1) + k
        acc = acc + cw_ref[k:k + 1, :] * ext_ref[off:off + rows, :]
    ext_ref[0:halo, :] = ext_ref[rows:rows + halo, :]
    u_all = _silu(acc)

    lane = lax.broadcasted_iota(jnp.int32, (1, LANES), 1)
    a = jnp.where(lane < SSM_HEADS, -jnp.exp(alog_ref[...]), 0.0)
    row = lax.broadcasted_iota(jnp.int32, (L, L), 0)
    col = lax.broadcasted_iota(jnp.int32, (L, L), 1)
    causal = row >= col
    tri = jnp.where(causal, 1.0, 0.0).astype(BF16)
    spread = jnp.where(lax.broadcasted_iota(jnp.int32, (LANES, SSM_WIDTH), 1) // SSM_HEAD_DIM
                       == lax.broadcasted_iota(jnp.int32, (LANES, SSM_WIDTH), 0), 1.0, 0.0).astype(BF16)
    r_heads = SSM_HEADS // SSM_GROUPS
    glane = lax.broadcasted_iota(jnp.int32, (1, GROUP_WIDTH), 1) // SSM_HEAD_DIM
    zero = jnp.zeros((), BF16)
    states = [st_ref[g] for g in range(SSM_GROUPS)]

    for c in range(rows // L):
        rs = slice(c * L, (c + 1) * L)
        u = u_all[rs]
        xs = u[:, 0:SSM_WIDTH]
        bm = u[:, SSM_WIDTH:SSM_WIDTH + SSM_GROUPS * SSM_STATE]
        cm = u[:, SSM_WIDTH + SSM_GROUPS * SSM_STATE:]
        dt = dt_ref[rs, :]
        acs = _dot3(tri, dt * a)
        acs_t = acs.T
        dt_e = sum(jnp.dot(p, spread, preferred_element_type=F32) for p in _split3(dt))
        acs_e = sum(jnp.dot(p, spread, preferred_element_type=F32) for p in _split3(acs))
        last = acs_e[L - 1:L, :]
        xd = xs * dt_e
        xd_b = xd.astype(BF16)
        xdw_b = (xd * jnp.exp(last - acs_e)).astype(BF16)
        e_acs = jnp.exp(acs_e)
        c_dec = jnp.exp(last)

        ys = []
        for g in range(SSM_GROUPS):
            gs = slice(g * GROUP_WIDTH, (g + 1) * GROUP_WIDTH)
            b_g = bm[:, g * SSM_STATE:(g + 1) * SSM_STATE]
            c_b = cm[:, g * SSM_STATE:(g + 1) * SSM_STATE].astype(BF16)
            cb = lax.dot_general(c_b, b_g.astype(BF16), (((1,), (1,)), ((), ())),
                                 preferred_element_type=F32)
            st = states[g]
            y_g = jnp.dot(c_b, st.astype(BF16), preferred_element_type=F32) * e_acs[:, gs]
            xd_g = xd_b[:, gs]
            for r in range(r_heads):
                h = g * r_heads + r
                seg = acs[:, h:h + 1] - acs_t[h:h + 1, :]
                m_h = (cb * jnp.exp(jnp.where(causal, seg, -jnp.inf))).astype(BF16)
                y_g = y_g + jnp.dot(m_h, jnp.where(glane == r, xd_g, zero), preferred_element_type=F32)
            new = jnp.dot(b_g.T.astype(BF16), xdw_b[:, gs], preferred_element_type=F32)
            states[g] = c_dec[:, gs] * st + new
            ys.append(y_g)

        y = jnp.concatenate(ys, axis=1) + dskip_ref[...] * xs
        y = y * _silu(z_ref[rs, :])
        outs = []
        for g in range(SSM_GROUPS):
            yg = y[:, g * GROUP_WIDTH:(g + 1) * GROUP_WIDTH]
            outs.append(yg * lax.rsqrt(jnp.mean(yg * yg, axis=-1, keepdims=True) + EPS))
        o_ref[rs, :] = (jnp.concatenate(outs, axis=1) * ng_ref[...]).astype(BF16)

    for g in range(SSM_GROUPS):
        st_ref[g] = states[g]


def _ssd(xbc, z, dt, cw, cb, alog, dskip, ng, batch, seq):
    t = xbc.shape[0]
    rows = SSD_CHUNKS_PER_STEP * CHUNK
    ns = seq // rows
    row = lambda b, c: (b * ns + c, 0)
    const = lambda b, c: (0, 0)
    return pl.pallas_call(
        _ssd_kernel,
        grid=(batch, ns),
        in_specs=[pl.BlockSpec((rows, CONV_CH), row),
                  pl.BlockSpec((rows, SSM_WIDTH), row),
                  pl.BlockSpec((rows, LANES), row),
                  pl.BlockSpec((CONV_WIDTH, CONV_CH), const),
                  pl.BlockSpec((1, CONV_CH), const),
                  pl.BlockSpec((1, LANES), const),
                  pl.BlockSpec((1, SSM_WIDTH), const),
                  pl.BlockSpec((1, SSM_WIDTH), const)],
        out_specs=pl.BlockSpec((rows, SSM_WIDTH), row),
        out_shape=jax.ShapeDtypeStruct((t, SSM_WIDTH), BF16),
        scratch_shapes=[pltpu.VMEM((SUBLANES + rows, CONV_CH), F32),
                        pltpu.VMEM((SSM_GROUPS, SSM_STATE, GROUP_WIDTH), F32)],
        compiler_params=_cparams("parallel", "arbitrary"),
    )(xbc, z, dt, cw, cb, alog, dskip, ng)


def _outproj_core(a_ref, s_ref, x_ref, w_ref, gm_ref, g_ref, sc_ref, sh_ref):
    half = a_ref.shape[1]
    mixed = (jnp.dot(a_ref[...], w_ref[0:half, :], preferred_element_type=F32)
             + jnp.dot(s_ref[...], w_ref[half:, :], preferred_element_type=F32))
    x_new = x_ref[...] + gm_ref[0] * mixed
    return x_new, _rms_mod(x_new, g_ref[...], sc_ref[0], sh_ref[0])


def _outproj_router_kernel(a_ref, s_ref, x_ref, w_ref, gm_ref, g_ref, sc_ref, sh_ref, rw_ref,
                           xo_ref, h_ref, route_ref, gate_ref, cnt_ref, base_ref, carry_ref):
    x_new, h = _outproj_core(a_ref, s_ref, x_ref, w_ref, gm_ref, g_ref, sc_ref, sh_ref)
    xo_ref[...] = x_new
    tm = h.shape[0]
    h_ref[...] = h.astype(BF16)

    h_hi = h.astype(BF16)
    h_lo = (h - h_hi.astype(F32)).astype(BF16)
    hi_part = jnp.dot(h_hi, rw_ref[...], preferred_element_type=F32)
    logits = (hi_part[:, 0:LANES] + hi_part[:, LANES:]
              + jnp.dot(h_lo, rw_ref[:, 0:LANES], preferred_element_type=F32))
    lane = lax.broadcasted_iota(jnp.int32, (tm, LANES), 1).astype(F32)
    logits = jnp.where(lane < N_EXPERTS, logits, -jnp.inf)
    m0 = jnp.max(logits, axis=-1, keepdims=True)
    i0 = jnp.min(jnp.where(logits == m0, lane, float(LANES)), axis=-1, keepdims=True)
    rest = jnp.where(lane == i0, -jnp.inf, logits)
    m1 = jnp.max(rest, axis=-1, keepdims=True)
    i1 = jnp.min(jnp.where(rest == m1, lane, float(LANES)), axis=-1, keepdims=True)
    e = jnp.exp(m1 - m0)
    g0 = 1.0 / (1.0 + e)
    g1 = e / (1.0 + e)

    @pl.when(pl.program_id(0) == 0)
    def _():
        carry_ref[...] = jnp.zeros(carry_ref.shape, F32)

    sel0 = lane == i0
    sel1 = lane == i1
    member = jnp.where(sel0 | sel1, 1.0, 0.0)
    r_i = lax.broadcasted_iota(jnp.int32, (tm, tm), 0)
    c_i = lax.broadcasted_iota(jnp.int32, (tm, tm), 1)
    strict = jnp.where(r_i > c_i, 1.0, 0.0).astype(BF16)
    base_ref[0] = carry_ref[...]
    rank = carry_ref[...] + jnp.dot(strict, member.astype(BF16), preferred_element_type=F32)
    r0 = jnp.sum(jnp.where(sel0, rank, 0.0), axis=-1, keepdims=True)
    r1 = jnp.sum(jnp.where(sel1, rank, 0.0), axis=-1, keepdims=True)
    carry_ref[...] = carry_ref[...] + jnp.sum(member, axis=0, keepdims=True)
    cnt_ref[...] = carry_ref[...]

    cols = r1
    for k, colv in enumerate((g0, g1, i0, i1, r0)):
        cols = jnp.where(lane == float(k), colv, cols)
    gate_ref[...] = cols[:, 0:SUBLANES]
    diag = (lax.broadcasted_iota(jnp.int32, (tm, LANES), 0) % LANES
            == lax.broadcasted_iota(jnp.int32, (tm, LANES), 1))
    nrow = tm // LANES
    for q, colv in enumerate((i0, i1, r0, r1)):
        spread = jnp.where(diag, colv, 0.0)
        rows = [jnp.sum(spread[b * LANES:(b + 1) * LANES], axis=0, keepdims=True) for b in range(nrow)]
        route_ref[0, q * nrow:(q + 1) * nrow, :] = jnp.concatenate(rows, axis=0).astype(jnp.int32)


def _outproj_router(attn, ssm, x2, w, gm, g, sc, sh, tiles_per_batch, router):
    t, d = x2.shape
    tm = ROW_TILE
    row = lambda i: (i, 0)
    const = lambda i: (0, 0)
    per_b = lambda i: (i // tiles_per_batch, 0, 0)
    half = attn.shape[1]
    return pl.pallas_call(
        _outproj_router_kernel,
        grid=(t // tm,),
        in_specs=[pl.BlockSpec((tm, half), row),
                  pl.BlockSpec((tm, half), row),
                  pl.BlockSpec((tm, d), row),
                  pl.BlockSpec((2 * half, d), const),
                  pl.BlockSpec((1, 1, d), per_b),
                  pl.BlockSpec((1, d), const),
                  pl.BlockSpec((1, 1, d), per_b),
                  pl.BlockSpec((1, 1, d), per_b),
                  pl.BlockSpec((d, 2 * LANES), const)],
        out_specs=[pl.BlockSpec((tm, d), row),
                   pl.BlockSpec((tm, d), row),
                   pl.BlockSpec((1, 4 * tm // LANES, LANES), lambda i: (i, 0, 0)),
                   pl.BlockSpec((tm, SUBLANES), row),
                   pl.BlockSpec((1, LANES), const),
                   pl.BlockSpec((1, 1, LANES), lambda i: (i, 0, 0))],
        out_shape=[jax.ShapeDtypeStruct((t, d), F32),
                   jax.ShapeDtypeStruct((t, d), BF16),
                   jax.ShapeDtypeStruct((t // tm, 4 * tm // LANES, LANES), jnp.int32),
                   jax.ShapeDtypeStruct((t, SUBLANES), F32),
                   jax.ShapeDtypeStruct((1, LANES), F32),
                   jax.ShapeDtypeStruct((t // tm, 1, LANES), F32)],
        scratch_shapes=[pltpu.VMEM((1, LANES), F32)],
        compiler_params=_cparams("arbitrary"),
    )(attn, ssm, x2, w, gm, g, sc, sh, router)


def _swiglu_rows(x, wg, wu, wd):
    gate = jnp.dot(x, wg, preferred_element_type=F32)
    up = jnp.dot(x, wu, preferred_element_type=F32)
    return jnp.dot((_silu(gate) * up).astype(BF16), wd, preferred_element_type=F32)


def _outproj_ffn_kernel(a_ref, s_ref, x_ref, w_ref, gm_ref, g_ref, sc_ref, sh_ref, wg_ref, wu_ref, wd_ref, gf_ref,
                        o_ref):
    half = a_ref.shape[1]
    piece = x_ref.shape[0] // ROW_SPLIT
    for r in range(ROW_SPLIT):
        rs = slice(r * piece, (r + 1) * piece)
        mixed = (jnp.dot(a_ref[rs, :], w_ref[0:half, :], preferred_element_type=F32)
                 + jnp.dot(s_ref[rs, :], w_ref[half:, :], preferred_element_type=F32))
        x_new = x_ref[rs, :] + gm_ref[0] * mixed
        h = _rms_mod(x_new, g_ref[...], sc_ref[0], sh_ref[0]).astype(BF16)
        o_ref[rs, :] = x_new + gf_ref[0] * _swiglu_rows(h, wg_ref[...], wu_ref[...], wd_ref[...])


def _outproj_ffn(attn, ssm, x2, w, gm, g, sc, sh, wg, wu, wd, gf, tiles_per_batch):
    t, d = x2.shape
    f = wg.shape[1]
    tm = ROW_TILE
    half = attn.shape[1]
    row = lambda i: (i, 0)
    const = lambda i: (0, 0)
    per_b = lambda i: (i // tiles_per_batch, 0, 0)
    resident = pl.Buffered(1)
    return pl.pallas_call(
        _outproj_ffn_kernel,
        grid=(t // tm,),
        in_specs=[pl.BlockSpec((tm, half), row),
                  pl.BlockSpec((tm, half), row),
                  pl.BlockSpec((tm, d), row),
                  pl.BlockSpec((2 * half, d), const, pipeline_mode=resident),
                  pl.BlockSpec((1, 1, d), per_b),
                  pl.BlockSpec((1, d), const),
                  pl.BlockSpec((1, 1, d), per_b),
                  pl.BlockSpec((1, 1, d), per_b),
                  pl.BlockSpec((d, f), const, pipeline_mode=resident),
                  pl.BlockSpec((d, f), const, pipeline_mode=resident),
                  pl.BlockSpec((f, d), const, pipeline_mode=resident),
                  pl.BlockSpec((1, 1, d), per_b)],
        out_specs=pl.BlockSpec((tm, d), row),
        out_shape=jax.ShapeDtypeStruct((t, d), F32),
        compiler_params=_cparams("parallel"),
    )(attn, ssm, x2, w, gm, g, sc, sh, wg, wu, wd, gf)


def _segment_copies(count, src, src_row, dst, dst_row, sem, max_rows):
    for k in range(max_rows.bit_length() - 1, -1, -1):
        size = (1 << k) * SUBLANES

        @pl.when(((count >> k) & 1) == 1)
        def _(k=k, size=size):
            done = (count >> (k + 1)) << (k + 1)
            pltpu.make_async_copy(
                src.at[pl.ds(pl.multiple_of((src_row + done) * SUBLANES, SUBLANES), size)],
                dst.at[pl.ds(pl.multiple_of((dst_row + done) * SUBLANES, SUBLANES), size)], sem).start()


def _lane_rows(route_ref, q, nrow):
    return jnp.concatenate([route_ref[0, q * nrow + b:q * nrow + b + 1, :] for b in range(nrow)], axis=1)


def _dispatch_kernel(delta_ref, cnt_ref, off_ref, dst_ref, zs_ref, ze_ref, nvalid_ref,
                     route_ref, h_ref, xb_hbm, g_ref, zero_ref, sem, zsem):
    i = pl.program_id(0)
    n = pl.num_programs(0)
    tm = h_ref.shape[0]
    nslot = TOP_K * tm
    slot = i % 2
    blk_rows = zero_ref.shape[0]
    n_blocks = xb_hbm.shape[0] // blk_rows

    def zero_row(r):
        return pltpu.make_async_copy(
            zero_ref.at[pl.ds(0, SUBLANES)],
            xb_hbm.at[pl.ds(pl.multiple_of(r * SUBLANES, SUBLANES), SUBLANES)], zsem)

    def zero_block(b):
        return pltpu.make_async_copy(
            zero_ref, xb_hbm.at[pl.ds(pl.multiple_of(b * blk_rows, blk_rows), blk_rows)], zsem)

    @pl.when(i == 0)
    def _():
        zero_ref[...] = jnp.zeros(zero_ref.shape, F32)
        for start in (True, False):
            for e in range(N_EXPERTS):
                def rows_body(r, c):
                    zero_row(r).start() if start else zero_row(r).wait()
                    return c
                lax.fori_loop(zs_ref[e], ze_ref[e], rows_body, 0)

            def blocks_body(b, c):
                zero_block(b).start() if start else zero_block(b).wait()
                return c
            lax.fori_loop(nvalid_ref[0], n_blocks, blocks_body, 0)

    def wait_slot(sl):
        pltpu.make_async_copy(g_ref.at[sl], xb_hbm.at[pl.ds(0, nslot * SUBLANES)], sem.at[sl]).wait()

    @pl.when(i >= 2)
    def _():
        wait_slot(slot)

    nrow = tm // LANES
    experts = (_lane_rows(route_ref, 0, nrow), _lane_rows(route_ref, 1, nrow))
    ranks = (_lane_rows(route_ref, 2, nrow), _lane_rows(route_ref, 3, nrow))
    row_id = lax.broadcasted_iota(jnp.int32, (nslot, tm), 0)
    hit = None
    for k in range(TOP_K):
        delta = jnp.zeros((1, tm), jnp.int32)
        for e in range(N_EXPERTS):
            delta = jnp.where(experts[k] == e, delta_ref[i * N_EXPERTS + e], delta)
        mine = row_id == delta + ranks[k]
        hit = mine if hit is None else hit | mine
    grouped = jnp.dot(jnp.where(hit, 1.0, 0.0).astype(BF16), h_ref[...], preferred_element_type=F32)
    for s in range(grouped.shape[1] // LANES):
        g_ref[slot, pl.ds(s, nslot, stride=SUBLANES), :] = grouped[:, s * LANES:(s + 1) * LANES]
    for e in range(N_EXPERTS):
        _segment_copies(cnt_ref[i * N_EXPERTS + e], g_ref.at[slot], off_ref[i * N_EXPERTS + e],
                        xb_hbm, dst_ref[i * N_EXPERTS + e], sem.at[slot], nslot)

    @pl.when(i == n - 1)
    def _():
        wait_slot(slot)

    @pl.when((i == n - 1) & (n >= 2))
    def _():
        wait_slot(1 - slot)


def _dispatch(h, route, seg, zs, ze, nvalid, n_blocks, rows):
    t, d = h.shape
    tm = ROW_TILE
    nslot = TOP_K * tm
    nrow = tm // LANES
    grid_spec = pltpu.PrefetchScalarGridSpec(
        num_scalar_prefetch=7,
        grid=(t // tm,),
        in_specs=[pl.BlockSpec((1, 4 * nrow, LANES), lambda i, *_: (i, 0, 0)),
                  pl.BlockSpec((tm, d), lambda i, *_: (i, 0))],
        out_specs=pl.BlockSpec(memory_space=pl.ANY),
        scratch_shapes=[pltpu.VMEM((2, nslot * SUBLANES, LANES), F32),
                        pltpu.VMEM((rows * SUBLANES, LANES), F32),
                        pltpu.SemaphoreType.DMA((2,)), pltpu.SemaphoreType.DMA(())],
    )
    return pl.pallas_call(
        _dispatch_kernel,
        grid_spec=grid_spec,
        out_shape=jax.ShapeDtypeStruct((n_blocks * rows * SUBLANES, LANES), F32),
        compiler_params=pltpu.CompilerParams(dimension_semantics=("arbitrary",), has_side_effects=True,
                                             vmem_limit_bytes=VMEM_LIMIT),
    )(*seg, zs, ze, nvalid, route, h)


def _moe_kernel(bexp_ref, nvalid_ref, brows_ref, xb_ref, wg_ref, wu_ref, wd_ref, y_ref, x_scr, acc_ref):
    b = pl.program_id(0)
    j = pl.program_id(1)
    nf = pl.num_programs(1)
    rows = x_scr.shape[0]
    nsl = x_scr.shape[1] // LANES
    valid = b < nvalid_ref[0]

    @pl.when(valid & (j == 0))
    def _():
        for s in range(nsl):
            x_scr[:, s * LANES:(s + 1) * LANES] = xb_ref[pl.ds(s, rows, stride=SUBLANES), :].astype(BF16)
        acc_ref[...] = jnp.zeros(acc_ref.shape, F32)

    piece = rows // MOE_SPLIT
    used = (brows_ref[b] + piece - 1) // piece
    for n_used in range(1, MOE_SPLIT + 1):
        @pl.when(valid & (used == n_used))
        def _():
            wg = wg_ref[0].astype(BF16)
            wu = wu_ref[0].astype(BF16)
            wd = wd_ref[0].astype(BF16)
            for r in range(n_used):
                rs = slice(r * piece, (r + 1) * piece)
                acc_ref[rs, :] += _swiglu_rows(x_scr[rs, :], wg, wu, wd)

    @pl.when(valid & (j == nf - 1))
    def _():
        for s in range(nsl):
            y_ref[pl.ds(s, rows, stride=SUBLANES), :] = acc_ref[:, s * LANES:(s + 1) * LANES]

    @pl.when(jnp.logical_not(valid) & (j == nf - 1))
    def _():
        y_ref[...] = jnp.zeros(y_ref.shape, F32)


def _moe_experts(xb, bexp, nvalid, brows, wg, wu, wd, n_blocks, rows):
    d = wg.shape[1]
    f = wg.shape[2]
    nf = f // MOE_FT

    def blk(b, j, bexp, nvalid, brows):
        return (jnp.minimum(b, nvalid[0] - 1), 0)

    def fidx(b, j, nvalid):
        return jnp.where(b < nvalid[0], j, nf - 1)

    grid_spec = pltpu.PrefetchScalarGridSpec(
        num_scalar_prefetch=3,
        grid=(n_blocks, nf),
        in_specs=[pl.BlockSpec((rows * SUBLANES, LANES), blk),
                  pl.BlockSpec((1, d, MOE_FT), lambda b, j, bexp, nvalid, brows: (bexp[b], 0, fidx(b, j, nvalid))),
                  pl.BlockSpec((1, d, MOE_FT), lambda b, j, bexp, nvalid, brows: (bexp[b], 0, fidx(b, j, nvalid))),
                  pl.BlockSpec((1, MOE_FT, d), lambda b, j, bexp, nvalid, brows: (bexp[b], fidx(b, j, nvalid), 0))],
        out_specs=pl.BlockSpec((rows * SUBLANES, LANES), lambda b, j, bexp, nvalid, brows: (b, 0)),
        scratch_shapes=[pltpu.VMEM((rows, d), BF16), pltpu.VMEM((rows, d), F32)],
    )
    return pl.pallas_call(
        _moe_kernel,
        grid_spec=grid_spec,
        out_shape=jax.ShapeDtypeStruct(xb.shape, F32),
        compiler_params=pltpu.CompilerParams(dimension_semantics=("arbitrary", "arbitrary"),
                                             vmem_limit_bytes=MOE_VMEM_LIMIT),
    )(bexp, nvalid, brows, xb, wg, wu, wd)


def _combine_kernel(delta_ref, cnt_ref, off_ref, dst_ref, y_hbm, x_ref, gate_ref, gf_ref, fg_ref, o_ref,
                    buf_ref, sem):
    i = pl.program_id(0)
    tm = x_ref.shape[0]
    nslot = TOP_K * tm
    nsl = x_ref.shape[1] // LANES
    slot = i % 2

    def fetch(tile, to_slot):
        for e in range(N_EXPERTS):
            _segment_copies(cnt_ref[tile * N_EXPERTS + e], y_hbm, dst_ref[tile * N_EXPERTS + e],
                            buf_ref.at[to_slot], off_ref[tile * N_EXPERTS + e], sem.at[to_slot], nslot)

    @pl.when(i == 0)
    def _():
        fetch(0, 0)

    @pl.when(i + 1 < pl.num_programs(0))
    def _():
        fetch(i + 1, 1 - slot)

    pltpu.make_async_copy(y_hbm.at[pl.ds(0, nslot * SUBLANES)], buf_ref.at[slot], sem.at[slot]).wait()

    y_rows = jnp.concatenate([buf_ref[slot, pl.ds(s, nslot, stride=SUBLANES), :] for s in range(nsl)],
                             axis=1).astype(BF16)
    cols = gate_ref[...]
    lane = lax.broadcasted_iota(jnp.int32, (tm, nslot), 1).astype(F32)
    weights = None
    for k in range(TOP_K):
        expert = cols[:, 2 + k:3 + k]
        delta = jnp.zeros((tm, 1), F32)
        for e in range(N_EXPERTS):
            delta = jnp.where(expert == float(e), delta_ref[i * N_EXPERTS + e].astype(F32), delta)
        w_k = jnp.where(lane == delta + cols[:, 4 + k:5 + k], cols[:, k:k + 1], 0.0)
        weights = w_k if weights is None else weights + w_k
    f = jnp.dot(weights.astype(BF16), y_rows, preferred_element_type=F32)
    x = x_ref[...] + gf_ref[0] * f
    ms = jnp.mean(x * x, axis=-1, keepdims=True)
    o_ref[...] = x * lax.rsqrt(ms + EPS) * fg_ref[...]


def _combine(y, seg, x2, gates, gf, fg, tiles_per_batch):
    t, d = x2.shape
    tm = ROW_TILE
    nslot = TOP_K * tm
    grid_spec = pltpu.PrefetchScalarGridSpec(
        num_scalar_prefetch=4,
        grid=(t // tm,),
        in_specs=[pl.BlockSpec(memory_space=pl.ANY),
                  pl.BlockSpec((tm, d), lambda i, *_: (i, 0)),
                  pl.BlockSpec((tm, SUBLANES), lambda i, *_: (i, 0)),
                  pl.BlockSpec((1, 1, d), lambda i, *_: (i // tiles_per_batch, 0, 0)),
                  pl.BlockSpec((1, d), lambda i, *_: (0, 0))],
        out_specs=pl.BlockSpec((tm, d), lambda i, *_: (i, 0)),
        scratch_shapes=[pltpu.VMEM((2, nslot * SUBLANES, LANES), F32), pltpu.SemaphoreType.DMA((2,))],
    )
    return pl.pallas_call(
        _combine_kernel,
        grid_spec=grid_spec,
        out_shape=jax.ShapeDtypeStruct((t, d), F32),
        compiler_params=_cparams("arbitrary"),
    )(*seg, y, x2, gates, gf, fg)


def _final_norm_kernel(x_ref, g_ref, o_ref):
    x = x_ref[...]
    o_ref[...] = x * lax.rsqrt(jnp.mean(x * x, axis=-1, keepdims=True) + EPS) * g_ref[...]


def _final_norm(x2, g):
    t, d = x2.shape
    tm = ROW_TILE
    return pl.pallas_call(
        _final_norm_kernel,
        grid=(t // tm,),
        in_specs=[pl.BlockSpec((tm, d), lambda i: (i, 0)), pl.BlockSpec((1, d), lambda i: (0, 0))],
        out_specs=pl.BlockSpec((tm, d), lambda i: (i, 0)),
        out_shape=jax.ShapeDtypeStruct((t, d), F32),
        compiler_params=_cparams("parallel"),
    )(x2, g)


def _qk_column_perm():
    half = HEAD_DIM // 2
    lane = np.arange(LANES)
    pair = (lane % HEAD_DIM) // half
    dim = lane % half + half * (lane // HEAD_DIM)
    q = np.concatenate([(i + Q_TILES * pair) * HEAD_DIM + dim for i in range(Q_TILES)])
    k = ATTN_WIDTH + pair * HEAD_DIM + dim
    return q, k


def _attn_out_row_perm():
    lane = np.arange(LANES)
    return np.concatenate([(i + Q_TILES * (lane // HEAD_DIM)) * HEAD_DIM + lane % HEAD_DIM
                           for i in range(Q_TILES)])


def _pad_lanes(v):
    return jnp.pad(v.astype(F32), (0, LANES - v.shape[0])).reshape(1, LANES)


def _moe_route_tables(counts, base, rows, n_blocks):
    cnt = counts[0, :N_EXPERTS].astype(jnp.int32)
    padded = (cnt + rows - 1) // rows * rows
    pad_end = jnp.cumsum(padded)
    pad_start = pad_end - padded
    first = base[:, 0, :N_EXPERTS].astype(jnp.int32)
    run = jnp.concatenate([first[1:], cnt[None]], axis=0) - first
    off = jnp.cumsum(run, axis=1) - run
    flat = lambda a: a.reshape(-1).astype(jnp.int32)
    seg = (flat(off - first), flat(run), flat(off), flat(pad_start[None, :] + first))
    blk_start = jnp.arange(n_blocks, dtype=jnp.int32) * rows
    bexp = jnp.minimum(jnp.sum(blk_start[:, None] >= pad_end[None, :], axis=1), N_EXPERTS - 1).astype(jnp.int32)
    nvalid = (pad_end[-1:] // rows).astype(jnp.int32)
    row_end = pad_start + cnt
    brows = jnp.clip(row_end[bexp] - blk_start, 0, rows).astype(jnp.int32)
    return seg, bexp, nvalid, brows, row_end.astype(jnp.int32), pad_end.astype(jnp.int32)


def kernel(x, c, positions, ada_w, ada_b, norm_mix_g, norm_ffn_g, w_in, w_out, attn_sinks, conv_w, conv_b,
           dt_bias, a_log, d_skip, ssm_norm_g, ffn_w_gate, ffn_w_up, ffn_w_down, router_w, moe_w_gate,
           moe_w_up, moe_w_down, final_norm_g):
    batch, seq, d = x.shape
    depth = w_in.shape[0]
    t = batch * seq
    tiles_per_batch = seq // ROW_TILE
    x2 = x.reshape(t, d)

    c8 = jnp.pad(c, ((0, SUBLANES - batch), (0, 0)))
    mod = _ada_mod(c8, ada_w, ada_b)[:, :batch].reshape(depth, batch, 6, 1, d)
    cos, sin = _rope_tables(positions)

    q_perm, k_perm = _qk_column_perm()
    o_perm = _attn_out_row_perm()
    for l in range(depth):
        sh_m, sc_m, g_m, sh_f, sc_f, g_f = (mod[l, :, k] for k in range(6))
        wl = w_in[l]
        w_cat = jnp.concatenate(
            [wl[:, q_perm], wl[:, k_perm], wl[:, C_V:C_DT],
             jnp.pad(wl[:, C_DT:], ((0, 0), (0, LANES - SSM_HEADS)))], axis=1).astype(BF16)
        q, kv, z, xbc, dt = _inproj(x2, norm_mix_g[l].reshape(1, d), sc_m, sh_m, cos, sin, w_cat,
                                    _pad_lanes(dt_bias[l]), tiles_per_batch)
        attn = _attention(q, kv, attn_sinks[l].astype(F32), seq)
        ssm = _ssd(xbc, z, dt, conv_w[l], conv_b[l].reshape(1, CONV_CH), _pad_lanes(a_log[l]),
                   jnp.repeat(d_skip[l].astype(F32), SSM_HEAD_DIM).reshape(1, SSM_WIDTH),
                   ssm_norm_g[l].reshape(1, SSM_WIDTH), batch, seq)
        wo = jnp.concatenate([w_out[l][o_perm], w_out[l][ATTN_WIDTH:]], axis=0).astype(BF16)
        ffn_g = norm_ffn_g[l].reshape(1, d)
        if l % 2 == 0:
            x2 = _outproj_ffn(attn, ssm, x2, wo, g_m, ffn_g, sc_f, sh_f, ffn_w_gate[l // 2].astype(BF16),
                              ffn_w_up[l // 2].astype(BF16), ffn_w_down[l // 2].astype(BF16), g_f, tiles_per_batch)
            if l == depth - 1:
                x2 = _final_norm(x2, final_norm_g.reshape(1, d))
        else:
            if l != depth - 1:
                raise NotImplementedError("the expert layer fuses the final norm and must be last")
            rw = jnp.pad(router_w[l // 2].astype(F32), ((0, 0), (0, LANES - N_EXPERTS)))
            rw_hi = rw.astype(BF16)
            rw_lo = (rw - rw_hi.astype(F32)).astype(BF16)
            x_new, h, route, gates, counts, base = _outproj_router(
                attn, ssm, x2, wo, g_m, ffn_g, sc_f, sh_f, tiles_per_batch,
                jnp.concatenate([rw_hi, rw_lo], axis=1))
            rows = min(MOE_ROWS, t)
            n_blocks = -(-(t * TOP_K) // rows) + N_EXPERTS
            seg, bexp, nvalid, brows, zs, ze = _moe_route_tables(counts, base, rows, n_blocks)
            xb = _dispatch(h, route, seg, zs, ze, nvalid, n_blocks, rows)
            y = _moe_experts(xb, bexp, nvalid, brows, moe_w_gate[l // 2], moe_w_up[l // 2], moe_w_down[l // 2],
                             n_blocks, rows)
            x2 = _combine(y, seg, x_new, gates, g_f, final_norm_g.reshape(1, d), tiles_per_batch)
    return x2.reshape(batch, seq, d)
```

```python
import functools

import numpy as np
import jax
import jax.numpy as jnp
from jax import lax
from jax.experimental import pallas as pl
from jax.experimental.pallas import tpu as pltpu

F32 = jnp.float32
BF16 = jnp.bfloat16

LANES = 128
SUBLANES = 8
VMEM_LIMIT = 48 * 1024 * 1024
MOE_VMEM_LIMIT = 58 * 1024 * 1024

EPS = 1e-6
HEAD_DIM = 64
Q_HEADS = 8
KV_HEADS = 2
GROUP = Q_HEADS // KV_HEADS
ATTN_WIDTH = Q_HEADS * HEAD_DIM
KV_WIDTH = KV_HEADS * HEAD_DIM
WINDOW = 128
ROPE_THETA = 10000.0
SSM_HEADS = 8
SSM_HEAD_DIM = 64
SSM_WIDTH = SSM_HEADS * SSM_HEAD_DIM
SSM_GROUPS = 2
SSM_STATE = 128
GROUP_WIDTH = SSM_WIDTH // SSM_GROUPS
CONV_WIDTH = 4
CONV_CH = SSM_WIDTH + 2 * SSM_GROUPS * SSM_STATE
CHUNK = 128
N_EXPERTS = 8
TOP_K = 2

Q_TILES = ATTN_WIDTH // LANES
C_Q = 0
C_K = C_Q + ATTN_WIDTH
C_V = C_K + KV_WIDTH
C_Z = C_V + KV_WIDTH
C_X = C_Z + SSM_WIDTH
C_DT = C_X + CONV_CH
C_END = C_DT + LANES

ROW_TILE = 512
ATTN_TILE = 1024
MOE_ROWS = 1536
MOE_FT = 512
SSD_CHUNKS_PER_STEP = 8
ROW_SPLIT = 2
MOE_SPLIT = 3


def _cparams(*sem):
    return pltpu.CompilerParams(dimension_semantics=sem, vmem_limit_bytes=VMEM_LIMIT)


def _silu(v):
    return v * (1.0 / (1.0 + jnp.exp(-v)))


def _softplus(v):
    return jnp.maximum(v, 0.0) + jnp.log1p(jnp.exp(-jnp.abs(v)))


def _rms_mod(x, g, scale, shift):
    ms = jnp.mean(x * x, axis=-1, keepdims=True)
    return (x * lax.rsqrt(ms + EPS) * g) * (1.0 + scale) + shift


def _ada_kernel(c_ref, w_ref, b_ref, o_ref):
    c = c_ref[...]
    o_ref[0] = jnp.dot(_silu(c), w_ref[0], preferred_element_type=F32,
                       precision=lax.Precision.HIGHEST) + b_ref[0]


def _ada_mod(c8, ada_w, ada_b):
    depth, d, n = ada_w.shape
    tn = 1536
    return pl.pallas_call(
        _ada_kernel,
        grid=(depth, n // tn),
        in_specs=[pl.BlockSpec((SUBLANES, d), lambda l, j: (0, 0)),
                  pl.BlockSpec((1, d, tn), lambda l, j: (l, 0, j)),
                  pl.BlockSpec((1, 1, tn), lambda l, j: (l, 0, j))],
        out_specs=pl.BlockSpec((1, SUBLANES, tn), lambda l, j: (l, 0, j)),
        out_shape=jax.ShapeDtypeStruct((depth, SUBLANES, n), F32),
        compiler_params=_cparams("parallel", "parallel"),
    )(c8, ada_w, ada_b.reshape(depth, 1, n))


ROPE_PER_ROW = LANES // (HEAD_DIM // 2)


def _rope_kernel(pos_ref, inv_ref, sign_ref, cos_ref, sin_ref):
    ang = pos_ref[...].astype(F32) * inv_ref[...]
    tr = ang.shape[0]
    group = lax.broadcasted_iota(jnp.int32, (1, LANES), 1) // (HEAD_DIM // 2)
    for table, out_ref, sign in ((jnp.cos(ang), cos_ref, None), (jnp.sin(ang), sin_ref, sign_ref[...])):
        rolled = [table] + [pltpu.roll(table, (HEAD_DIM // 2) * s, axis=1) for s in range(1, ROPE_PER_ROW)]
        for j in range(ROPE_PER_ROW):
            rep = rolled[(0 - j) % ROPE_PER_ROW]
            for g in range(1, ROPE_PER_ROW):
                rep = jnp.where(group == g, rolled[(g - j) % ROPE_PER_ROW], rep)
            out_ref[pl.ds(j, tr, stride=ROPE_PER_ROW), :] = rep if sign is None else rep * sign


def _rope_tables(positions):
    t = positions.size
    half = HEAD_DIM // 2
    pos_dense = jnp.repeat(positions.reshape(t // ROPE_PER_ROW, ROPE_PER_ROW), half, axis=1)
    inv_freq = ROPE_THETA ** (-jnp.arange(0, HEAD_DIM, 2, dtype=F32) / HEAD_DIM)
    inv_dense = jnp.tile(inv_freq, ROPE_PER_ROW).reshape(1, LANES)
    sign = jnp.repeat(jnp.array([-1.0, -1.0, 1.0, 1.0], F32), half).reshape(1, LANES)
    rows = t // ROPE_PER_ROW
    tr = min(1024, rows)
    return pl.pallas_call(
        _rope_kernel,
        grid=(rows // tr,),
        in_specs=[pl.BlockSpec((tr, LANES), lambda i: (i, 0)),
                  pl.BlockSpec((1, LANES), lambda i: (0, 0)),
                  pl.BlockSpec((1, LANES), lambda i: (0, 0))],
        out_specs=[pl.BlockSpec((tr * ROPE_PER_ROW, LANES), lambda i: (i, 0))] * 2,
        out_shape=[jax.ShapeDtypeStruct((t, LANES), F32)] * 2,
        compiler_params=_cparams("parallel"),
    )(pos_dense, inv_dense, sign)


def _inproj_kernel(x_ref, g_ref, sc_ref, sh_ref, cos_ref, sin_ref, w_ref, dtb_ref,
                   q_ref, kv_ref, z_ref, xbc_ref, dt_ref):
    h = _rms_mod(x_ref[...], g_ref[...], sc_ref[0], sh_ref[0]).astype(BF16)
    cos = cos_ref[...]
    sin = sin_ref[...]

    def rope(t):
        return t * cos + pltpu.roll(t, LANES // 2, axis=1) * sin

    qkv = jnp.dot(h, w_ref[:, C_Q:C_Z], preferred_element_type=F32)
    for i in range(Q_TILES):
        q_ref[:, i * LANES:(i + 1) * LANES] = (
            rope(qkv[:, i * LANES:(i + 1) * LANES]) * (HEAD_DIM ** -0.5)).astype(BF16)
    kv_ref[:, 0:LANES] = rope(qkv[:, C_K:C_V]).astype(BF16)
    kv_ref[:, LANES:2 * LANES] = qkv[:, C_V:C_Z].astype(BF16)
    z_ref[...] = jnp.dot(h, w_ref[:, C_Z:C_X], preferred_element_type=F32)
    xbc_ref[...] = jnp.dot(h, w_ref[:, C_X:C_DT], preferred_element_type=F32)
    dt_raw = jnp.dot(h, w_ref[:, C_DT:C_END], preferred_element_type=F32)
    dt_ref[...] = _softplus(dt_raw + dtb_ref[...])


def _inproj(x2, g, sc, sh, cos, sin, w, dtb, tiles_per_batch):
    t, d = x2.shape
    tm = ROW_TILE
    row = lambda i: (i, 0)
    const = lambda i: (0, 0)
    per_b = lambda i: (i // tiles_per_batch, 0, 0)
    return pl.pallas_call(
        _inproj_kernel,
        grid=(t // tm,),
        in_specs=[pl.BlockSpec((tm, d), row),
                  pl.BlockSpec((1, d), const),
                  pl.BlockSpec((1, 1, d), per_b),
                  pl.BlockSpec((1, 1, d), per_b),
                  pl.BlockSpec((tm, LANES), row),
                  pl.BlockSpec((tm, LANES), row),
                  pl.BlockSpec((d, C_END), const),
                  pl.BlockSpec((1, LANES), const)],
        out_specs=[pl.BlockSpec((tm, ATTN_WIDTH), row),
                   pl.BlockSpec((tm, 2 * KV_WIDTH), row),
                   pl.BlockSpec((tm, SSM_WIDTH), row),
                   pl.BlockSpec((tm, CONV_CH), row),
                   pl.BlockSpec((tm, LANES), row)],
        out_shape=[jax.ShapeDtypeStruct((t, ATTN_WIDTH), BF16),
                   jax.ShapeDtypeStruct((t, 2 * KV_WIDTH), BF16),
                   jax.ShapeDtypeStruct((t, SSM_WIDTH), F32),
                   jax.ShapeDtypeStruct((t, CONV_CH), F32),
                   jax.ShapeDtypeStruct((t, LANES), F32)],
        compiler_params=_cparams("parallel"),
    )(x2, g, sc, sh, cos, sin, w, dtb)


def _attn_kernel(sink_ref, q_ref, kv_ref, kvp_ref, o_ref, *, tiles_per_seq):
    first = (pl.program_id(0) % tiles_per_seq) == 0
    blk = WINDOW
    nsub = ATTN_TILE // blk
    lane = lax.broadcasted_iota(jnp.int32, (1, LANES), 1)
    k_lo_mask = (lane % HEAD_DIM) < (HEAD_DIM // 2)
    v_lo_mask = lane < HEAD_DIM
    zero = jnp.zeros((), BF16)

    k_all = jnp.concatenate([kvp_ref[:, 0:LANES], kv_ref[:, 0:LANES]], axis=0)
    v_all = jnp.concatenate([kvp_ref[:, LANES:2 * LANES], kv_ref[:, LANES:2 * LANES]], axis=0)
    k_sel = (jnp.where(k_lo_mask, k_all, zero), jnp.where(k_lo_mask, zero, k_all))
    v_sel = (jnp.where(v_lo_mask, v_all, zero), jnp.where(v_lo_mask, zero, v_all))

    qi = lax.broadcasted_iota(jnp.int32, (blk, blk), 0)
    col = lax.broadcasted_iota(jnp.int32, (blk, blk), 1)
    cur = col <= qi
    no_prev = (col > qi) & (col > jnp.where(first, -1, blk))

    for n in range(nsub):
        q_st = jnp.concatenate([q_ref[n * blk:(n + 1) * blk, i * LANES:(i + 1) * LANES]
                                for i in range(Q_TILES)], axis=0)
        out = None
        for hk in range(KV_HEADS):
            k_n = k_sel[hk][n * blk:(n + 2) * blk]
            v_n = v_sel[hk][n * blk:(n + 2) * blk]
            s = lax.dot_general(q_st, k_n, (((1,), (1,)), ((), ())), preferred_element_type=F32)
            probs, scales = [], []
            for i in range(GROUP):
                s_i = s[i * blk:(i + 1) * blk]
                sc = jnp.where(cur, s_i[:, blk:], s_i[:, :blk])
                if n == 0:
                    sc = jnp.where(no_prev, -jnp.inf, sc)
                sink = sink_ref[hk * GROUP + i]
                m = jnp.maximum(jnp.max(sc, axis=-1, keepdims=True), sink)
                p = jnp.exp(sc - m)
                scales.append(1.0 / (jnp.sum(p, axis=-1, keepdims=True) + jnp.exp(sink - m)))
                probs.append(jnp.concatenate([jnp.where(cur, 0.0, p), jnp.where(cur, p, 0.0)],
                                             axis=1).astype(BF16))
            o = jnp.dot(jnp.concatenate(probs, axis=0), v_n, preferred_element_type=F32)
            o = o * jnp.concatenate(scales, axis=0)
            out = o if out is None else out + o
        for i in range(Q_TILES):
            o_ref[n * blk:(n + 1) * blk, i * LANES:(i + 1) * LANES] = out[i * blk:(i + 1) * blk].astype(BF16)


def _attention(q, kv, sinks, seq):
    t = q.shape[0]
    tq = ATTN_TILE
    r = tq // WINDOW
    kern = functools.partial(_attn_kernel, tiles_per_seq=seq // tq)
    return pl.pallas_call(
        kern,
        grid=(t // tq,),
        in_specs=[pl.BlockSpec(memory_space=pltpu.SMEM),
                  pl.BlockSpec((tq, ATTN_WIDTH), lambda i: (i, 0)),
                  pl.BlockSpec((tq, 2 * KV_WIDTH), lambda i: (i, 0)),
                  pl.BlockSpec((WINDOW, 2 * KV_WIDTH), lambda i: (jnp.maximum(i * r - 1, 0), 0))],
        out_specs=pl.BlockSpec((tq, ATTN_WIDTH), lambda i: (i, 0)),
        out_shape=jax.ShapeDtypeStruct((t, ATTN_WIDTH), BF16),
        compiler_params=_cparams("parallel"),
    )(sinks, q, kv, kv)


def _split3(v):
    p0 = v.astype(BF16)
    r1 = v - p0.astype(F32)
    p1 = r1.astype(BF16)
    return p0, p1, (r1 - p1.astype(F32)).astype(BF16)


def _dot3(lhs_exact_bf16, v):
    return sum(jnp.dot(lhs_exact_bf16, p, preferred_element_type=F32) for p in _split3(v))


def _ssd_kernel(xbc_ref, z_ref, dt_ref, cw_ref, cb_ref, alog_ref, dskip_ref, ng_ref,
                o_ref, ext_ref, st_ref):
    L = CHUNK
    rows = xbc_ref.shape[0]
    halo = SUBLANES

    @pl.when(pl.program_id(1) == 0)
    def _():
        ext_ref[0:halo, :] = jnp.zeros((halo, CONV_CH), F32)
        st_ref[...] = jnp.zeros(st_ref.shape, F32)

    ext_ref[halo:halo + rows, :] = xbc_ref[...]
    acc = cb_ref[...] + cw_ref[CONV_WIDTH - 1:CONV_WIDTH, :] * ext_ref[halo:halo + rows, :]
    for k in range(CONV_WIDTH - 1):
        off = halo - (CONV_WIDTH - 1) + k
        acc = acc + cw_ref[k:k + 1, :] * ext_ref[off:off + rows, :]
    ext_ref[0:halo, :] = ext_ref[rows:rows + halo, :]
    u_all = _silu(acc)

    lane = lax.broadcasted_iota(jnp.int32, (1, LANES), 1)
    a = jnp.where(lane < SSM_HEADS, -jnp.exp(alog_ref[...]), 0.0)
    row = lax.broadcasted_iota(jnp.int32, (L, L), 0)
    col = lax.broadcasted_iota(jnp.int32, (L, L), 1)
    causal = row >= col
    tri = jnp.where(causal, 1.0, 0.0).astype(BF16)
    spread = jnp.where(lax.broadcasted_iota(jnp.int32, (LANES, SSM_WIDTH), 1) // SSM_HEAD_DIM
                       == lax.broadcasted_iota(jnp.int32, (LANES, SSM_WIDTH), 0), 1.0, 0.0).astype(BF16)
    r_heads = SSM_HEADS // SSM_GROUPS
    glane = lax.broadcasted_iota(jnp.int32, (1, GROUP_WIDTH), 1) // SSM_HEAD_DIM
    zero = jnp.zeros((), BF16)
    states = [st_ref[g] for g in range(SSM_GROUPS)]

    for c in range(rows // L):
        rs = slice(c * L, (c + 1) * L)
        u = u_all[rs]
        xs = u[:, 0:SSM_WIDTH]
        bm = u[:, SSM_WIDTH:SSM_WIDTH + SSM_GROUPS * SSM_STATE]
        cm = u[:, SSM_WIDTH + SSM_GROUPS * SSM_STATE:]
        dt = dt_ref[rs, :]
        acs = _dot3(tri, dt * a)
        acs_t = acs.T
        dt_e = sum(jnp.dot(p, spread, preferred_element_type=F32) for p in _split3(dt))
        acs_e = sum(jnp.dot(p, spread, preferred_element_type=F32) for p in _split3(acs))
        last = acs_e[L - 1:L, :]
        xd = xs * dt_e
        xd_b = xd.astype(BF16)
        xdw_b = (xd * jnp.exp(last - acs_e)).astype(BF16)
        e_acs = jnp.exp(acs_e)
        c_dec = jnp.exp(last)

        ys = []
        for g in range(SSM_GROUPS):
            gs = slice(g * GROUP_WIDTH, (g + 1) * GROUP_WIDTH)
            b_g = bm[:, g * SSM_STATE:(g + 1) * SSM_STATE]
            c_b = cm[:, g * SSM_STATE:(g + 1) * SSM_STATE].astype(BF16)
            cb = lax.dot_general(c_b, b_g.astype(BF16), (((1,), (1,)), ((), ())),
                                 preferred_element_type=F32)
            st = states[g]
            y_g = jnp.dot(c_b, st.astype(BF16), preferred_element_type=F32) * e_acs[:, gs]
            xd_g = xd_b[:, gs]
            for r in range(r_heads):
                h = g * r_heads + r
                seg = acs[:, h:h + 1] - acs_t[h:h + 1, :]
                m_h = (cb * jnp.exp(jnp.where(causal, seg, -jnp.inf))).astype(BF16)
                y_g = y_g + jnp.dot(m_h, jnp.where(glane == r, xd_g, zero), preferred_element_type=F32)
            new = jnp.dot(b_g.T.astype(BF16), xdw_b[:, gs], preferred_element_type=F32)
            states[g] = c_dec[:, gs] * st + new
            ys.append(y_g)

        y = jnp.concatenate(ys, axis=1) + dskip_ref[...] * xs
        y = y * _silu(z_ref[rs, :])
        outs = []
        for g in range(SSM_GROUPS):
            yg = y[:, g * GROUP_WIDTH:(g + 1) * GROUP_WIDTH]
            outs.append(yg * lax.rsqrt(jnp.mean(yg * yg, axis=-1, keepdims=True) + EPS))
        o_ref[rs, :] = (jnp.concatenate(outs, axis=1) * ng_ref[...]).astype(BF16)

    for g in range(SSM_GROUPS):
        st_ref[g] = states[g]


def _ssd(xbc, z, dt, cw, cb, alog, dskip, ng, batch, seq):
    t = xbc.shape[0]
    rows = SSD_CHUNKS_PER_STEP * CHUNK
    ns = seq // rows
    row = lambda b, c: (b * ns + c, 0)
    const = lambda b, c: (0, 0)
    return pl.pallas_call(
        _ssd_kernel,
        grid=(batch, ns),
        in_specs=[pl.BlockSpec((rows, CONV_CH), row),
                  pl.BlockSpec((rows, SSM_WIDTH), row),
                  pl.BlockSpec((rows, LANES), row),
                  pl.BlockSpec((CONV_WIDTH, CONV_CH), const),
                  pl.BlockSpec((1, CONV_CH), const),
                  pl.BlockSpec((1, LANES), const),
                  pl.BlockSpec((1, SSM_WIDTH), const),
                  pl.BlockSpec((1, SSM_WIDTH), const)],
        out_specs=pl.BlockSpec((rows, SSM_WIDTH), row),
        out_shape=jax.ShapeDtypeStruct((t, SSM_WIDTH), BF16),
        scratch_shapes=[pltpu.VMEM((SUBLANES + rows, CONV_CH), F32),
                        pltpu.VMEM((SSM_GROUPS, SSM_STATE, GROUP_WIDTH), F32)],
        compiler_params=_cparams("parallel", "arbitrary"),
    )(xbc, z, dt, cw, cb, alog, dskip, ng)


def _outproj_core(a_ref, s_ref, x_ref, w_ref, gm_ref, g_ref, sc_ref, sh_ref):
    half = a_ref.shape[1]
    mixed = (jnp.dot(a_ref[...], w_ref[0:half, :], preferred_element_type=F32)
             + jnp.dot(s_ref[...], w_ref[half:, :], preferred_element_type=F32))
    x_new = x_ref[...] + gm_ref[0] * mixed
    return x_new, _rms_mod(x_new, g_ref[...], sc_ref[0], sh_ref[0])


def _outproj_router_kernel(a_ref, s_ref, x_ref, w_ref, gm_ref, g_ref, sc_ref, sh_ref, rw_ref,
                           xo_ref, h_ref, route_ref, gate_ref, cnt_ref, base_ref, carry_ref):
    x_new, h = _outproj_core(a_ref, s_ref, x_ref, w_ref, gm_ref, g_ref, sc_ref, sh_ref)
    xo_ref[...] = x_new
    tm = h.shape[0]
    h_ref[...] = h.astype(BF16)

    h_hi = h.astype(BF16)
    h_lo = (h - h_hi.astype(F32)).astype(BF16)
    hi_part = jnp.dot(h_hi, rw_ref[...], preferred_element_type=F32)
    logits = (hi_part[:, 0:LANES] + hi_part[:, LANES:]
              + jnp.dot(h_lo, rw_ref[:, 0:LANES], preferred_element_type=F32))
    lane = lax.broadcasted_iota(jnp.int32, (tm, LANES), 1).astype(F32)
    logits = jnp.where(lane < N_EXPERTS, logits, -jnp.inf)
    m0 = jnp.max(logits, axis=-1, keepdims=True)
    i0 = jnp.min(jnp.where(logits == m0, lane, float(LANES)), axis=-1, keepdims=True)
    rest = jnp.where(lane == i0, -jnp.inf, logits)
    m1 = jnp.max(rest, axis=-1, keepdims=True)
    i1 = jnp.min(jnp.where(rest == m1, lane, float(LANES)), axis=-1, keepdims=True)
    e = jnp.exp(m1 - m0)
    g0 = 1.0 / (1.0 + e)
    g1 = e / (1.0 + e)

    @pl.when(pl.program_id(0) == 0)
    def _():
        carry_ref[...] = jnp.zeros(carry_ref.shape, F32)

    sel0 = lane == i0
    sel1 = lane == i1
    member = jnp.where(sel0 | sel1, 1.0, 0.0)
    r_i = lax.broadcasted_iota(jnp.int32, (tm, tm), 0)
    c_i = lax.broadcasted_iota(jnp.int32, (tm, tm), 1)
    strict = jnp.where(r_i > c_i, 1.0, 0.0).astype(BF16)
    base_ref[0] = carry_ref[...]
    rank = carry_ref[...] + jnp.dot(strict, member.astype(BF16), preferred_element_type=F32)
    r0 = jnp.sum(jnp.where(sel0, rank, 0.0), axis=-1, keepdims=True)
    r1 = jnp.sum(jnp.where(sel1, rank, 0.0), axis=-1, keepdims=True)
    carry_ref[...] = carry_ref[...] + jnp.sum(member, axis=0, keepdims=True)
    cnt_ref[...] = carry_ref[...]

    cols = r1
    for k, colv in enumerate((g0, g1, i0, i1, r0)):
        cols = jnp.where(lane == float(k), colv, cols)
    gate_ref[...] = cols[:, 0:SUBLANES]
    diag = (lax.broadcasted_iota(jnp.int32, (tm, LANES), 0) % LANES
            == lax.broadcasted_iota(jnp.int32, (tm, LANES), 1))
    nrow = tm // LANES
    for q, colv in enumerate((i0, i1, r0, r1)):
        spread = jnp.where(diag, colv, 0.0)
        rows = [jnp.sum(spread[b * LANES:(b + 1) * LANES], axis=0, keepdims=True) for b in range(nrow)]
        route_ref[0, q * nrow:(q + 1) * nrow, :] = jnp.concatenate(rows, axis=0).astype(jnp.int32)


def _outproj_router(attn, ssm, x2, w, gm, g, sc, sh, tiles_per_batch, router):
    t, d = x2.shape
    tm = ROW_TILE
    row = lambda i: (i, 0)
    const = lambda i: (0, 0)
    per_b = lambda i: (i // tiles_per_batch, 0, 0)
    half = attn.shape[1]
    return pl.pallas_call(
        _outproj_router_kernel,
        grid=(t // tm,),
        in_specs=[pl.BlockSpec((tm, half), row),
                  pl.BlockSpec((tm, half), row),
                  pl.BlockSpec((tm, d), row),
                  pl.BlockSpec((2 * half, d), const),
                  pl.BlockSpec((1, 1, d), per_b),
                  pl.BlockSpec((1, d), const),
                  pl.BlockSpec((1, 1, d), per_b),
                  pl.BlockSpec((1, 1, d), per_b),
                  pl.BlockSpec((d, 2 * LANES), const)],
        out_specs=[pl.BlockSpec((tm, d), row),
                   pl.BlockSpec((tm, d), row),
                   pl.BlockSpec((1, 4 * tm // LANES, LANES), lambda i: (i, 0, 0)),
                   pl.BlockSpec((tm, SUBLANES), row),
                   pl.BlockSpec((1, LANES), const),
                   pl.BlockSpec((1, 1, LANES), lambda i: (i, 0, 0))],
        out_shape=[jax.ShapeDtypeStruct((t, d), F32),
                   jax.ShapeDtypeStruct((t, d), BF16),
                   jax.ShapeDtypeStruct((t // tm, 4 * tm // LANES, LANES), jnp.int32),
                   jax.ShapeDtypeStruct((t, SUBLANES), F32),
                   jax.ShapeDtypeStruct((1, LANES), F32),
                   jax.ShapeDtypeStruct((t // tm, 1, LANES), F32)],
        scratch_shapes=[pltpu.VMEM((1, LANES), F32)],
        compiler_params=_cparams("arbitrary"),
    )(attn, ssm, x2, w, gm, g, sc, sh, router)


def _swiglu_rows(x, wg, wu, wd):
    gate = jnp.dot(x, wg, preferred_element_type=F32)
    up = jnp.dot(x, wu, preferred_element_type=F32)
    return jnp.dot((_silu(gate) * up).astype(BF16), wd, preferred_element_type=F32)


def _outproj_ffn_kernel(a_ref, s_ref, x_ref, w_ref, gm_ref, g_ref, sc_ref, sh_ref, wg_ref, wu_ref, wd_ref, gf_ref,
                        o_ref):
    half = a_ref.shape[1]
    piece = x_ref.shape[0] // ROW_SPLIT
    for r in range(ROW_SPLIT):
        rs = slice(r * piece, (r + 1) * piece)
        mixed = (jnp.dot(a_ref[rs, :], w_ref[0:half, :], preferred_element_type=F32)
                 + jnp.dot(s_ref[rs, :], w_ref[half:, :], preferred_element_type=F32))
        x_new = x_ref[rs, :] + gm_ref[0] * mixed
        h = _rms_mod(x_new, g_ref[...], sc_ref[0], sh_ref[0]).astype(BF16)
        o_ref[rs, :] = x_new + gf_ref[0] * _swiglu_rows(h, wg_ref[...], wu_ref[...], wd_ref[...])


def _outproj_ffn(attn, ssm, x2, w, gm, g, sc, sh, wg, wu, wd, gf, tiles_per_batch):
    t, d = x2.shape
    f = wg.shape[1]
    tm = ROW_TILE
    half = attn.shape[1]
    row = lambda i: (i, 0)
    const = lambda i: (0, 0)
    per_b = lambda i: (i // tiles_per_batch, 0, 0)
    resident = pl.Buffered(1)
    return pl.pallas_call(
        _outproj_ffn_kernel,
        grid=(t // tm,),
        in_specs=[pl.BlockSpec((tm, half), row),
                  pl.BlockSpec((tm, half), row),
                  pl.BlockSpec((tm, d), row),
                  pl.BlockSpec((2 * half, d), const, pipeline_mode=resident),
                  pl.BlockSpec((1, 1, d), per_b),
                  pl.BlockSpec((1, d), const),
                  pl.BlockSpec((1, 1, d), per_b),
                  pl.BlockSpec((1, 1, d), per_b),
                  pl.BlockSpec((d, f), const, pipeline_mode=resident),
                  pl.BlockSpec((d, f), const, pipeline_mode=resident),
                  pl.BlockSpec((f, d), const, pipeline_mode=resident),
                  pl.BlockSpec((1, 1, d), per_b)],
        out_specs=pl.BlockSpec((tm, d), row),
        out_shape=jax.ShapeDtypeStruct((t, d), F32),
        compiler_params=_cparams("parallel"),
    )(attn, ssm, x2, w, gm, g, sc, sh, wg, wu, wd, gf)


def _segment_copies(count, src, src_row, dst, dst_row, sem, max_rows):
    for k in range(max_rows.bit_length() - 1, -1, -1):
        size = (1 << k) * SUBLANES

        @pl.when(((count >> k) & 1) == 1)
        def _(k=k, size=size):
            done = (count >> (k + 1)) << (k + 1)
            pltpu.make_async_copy(
                src.at[pl.ds(pl.multiple_of((src_row + done) * SUBLANES, SUBLANES), size)],
                dst.at[pl.ds(pl.multiple_of((dst_row + done) * SUBLANES, SUBLANES), size)], sem).start()


def _lane_rows(route_ref, q, nrow):
    return jnp.concatenate([route_ref[0, q * nrow + b:q * nrow + b + 1, :] for b in range(nrow)], axis=1)


def _dispatch_kernel(delta_ref, cnt_ref, off_ref, dst_ref, zs_ref, zm_ref, ze_ref, nvalid_ref,
                     route_ref, h_ref, xb_hbm, g_ref, zero_ref, sem, zsem):
    i = pl.program_id(0)
    n = pl.num_programs(0)
    tm = h_ref.shape[0]
    nslot = TOP_K * tm
    slot = i % 2
    blk_rows = zero_ref.shape[0]
    n_blocks = xb_hbm.shape[0] // blk_rows

    def zero_row(r):
        return pltpu.make_async_copy(
            zero_ref.at[pl.ds(0, SUBLANES)],
            xb_hbm.at[pl.ds(pl.multiple_of(r * SUBLANES, SUBLANES), SUBLANES)], zsem)

    piece_rows = blk_rows // MOE_SPLIT

    def zero_piece(p):
        return pltpu.make_async_copy(
            zero_ref.at[pl.ds(0, piece_rows)],
            xb_hbm.at[pl.ds(pl.multiple_of(p * piece_rows, piece_rows), piece_rows)], zsem)

    def zero_block(b):
        return pltpu.make_async_copy(
            zero_ref, xb_hbm.at[pl.ds(pl.multiple_of(b * blk_rows, blk_rows), blk_rows)], zsem)

    @pl.when(i == 0)
    def _():
        zero_ref[...] = jnp.zeros(zero_ref.shape, F32)
        for start in (True, False):
            for e in range(N_EXPERTS):
                def rows_body(r, c):
                    zero_row(r).start() if start else zero_row(r).wait()
                    return c
                lax.fori_loop(zs_ref[e], zm_ref[e], rows_body, 0)

                def pieces_body(p, c):
                    zero_piece(p).start() if start else zero_piece(p).wait()
                    return c
                lax.fori_loop(zm_ref[e] // (piece_rows // SUBLANES), ze_ref[e] // (piece_rows // SUBLANES),
                              pieces_body, 0)

            def blocks_body(b, c):
                zero_block(b).start() if start else zero_block(b).wait()
                return c
            lax.fori_loop(nvalid_ref[0], n_blocks, blocks_body, 0)

    def wait_slot(sl):
        pltpu.make_async_copy(g_ref.at[sl], xb_hbm.at[pl.ds(0, nslot * SUBLANES)], sem.at[sl]).wait()

    @pl.when(i >= 2)
    def _():
        wait_slot(slot)

    nrow = tm // LANES
    experts = (_lane_rows(route_ref, 0, nrow), _lane_rows(route_ref, 1, nrow))
    ranks = (_lane_rows(route_ref, 2, nrow), _lane_rows(route_ref, 3, nrow))
    row_id = lax.broadcasted_iota(jnp.int32, (nslot, tm), 0)
    hit = None
    for k in range(TOP_K):
        delta = jnp.zeros((1, tm), jnp.int32)
        for e in range(N_EXPERTS):
            delta = jnp.where(experts[k] == e, delta_ref[i * N_EXPERTS + e], delta)
        mine = row_id == delta + ranks[k]
        hit = mine if hit is None else hit | mine
    grouped = jnp.dot(jnp.where(hit, 1.0, 0.0).astype(BF16), h_ref[...], preferred_element_type=F32)
    for s in range(grouped.shape[1] // LANES):
        g_ref[slot, pl.ds(s, nslot, stride=SUBLANES), :] = grouped[:, s * LANES:(s + 1) * LANES]
    for e in range(N_EXPERTS):
        _segment_copies(cnt_ref[i * N_EXPERTS + e], g_ref.at[slot], off_ref[i * N_EXPERTS + e],
                        xb_hbm, dst_ref[i * N_EXPERTS + e], sem.at[slot], nslot)

    @pl.when(i == n - 1)
    def _():
        wait_slot(slot)

    @pl.when((i == n - 1) & (n >= 2))
    def _():
        wait_slot(1 - slot)


def _dispatch(h, route, seg, zero_spans, nvalid, n_blocks, rows):
    t, d = h.shape
    tm = ROW_TILE
    nslot = TOP_K * tm
    nrow = tm // LANES
    grid_spec = pltpu.PrefetchScalarGridSpec(
        num_scalar_prefetch=8,
        grid=(t // tm,),
        in_specs=[pl.BlockSpec((1, 4 * nrow, LANES), lambda i, *_: (i, 0, 0)),
                  pl.BlockSpec((tm, d), lambda i, *_: (i, 0))],
        out_specs=pl.BlockSpec(memory_space=pl.ANY),
        scratch_shapes=[pltpu.VMEM((2, nslot * SUBLANES, LANES), F32),
                        pltpu.VMEM((rows * SUBLANES, LANES), F32),
                        pltpu.SemaphoreType.DMA((2,)), pltpu.SemaphoreType.DMA(())],
    )
    return pl.pallas_call(
        _dispatch_kernel,
        grid_spec=grid_spec,
        out_shape=jax.ShapeDtypeStruct((n_blocks * rows * SUBLANES, LANES), F32),
        compiler_params=pltpu.CompilerParams(dimension_semantics=("arbitrary",), has_side_effects=True,
                                             vmem_limit_bytes=VMEM_LIMIT),
    )(*seg, *zero_spans, nvalid, route, h)


def _moe_kernel(bexp_ref, nvalid_ref, brows_ref, xb_ref, wg_ref, wu_ref, wd_ref, y_ref, x_scr, acc_ref):
    b = pl.program_id(0)
    j = pl.program_id(1)
    nf = pl.num_programs(1)
    rows = x_scr.shape[0]
    nsl = x_scr.shape[1] // LANES
    valid = b < nvalid_ref[0]
    piece = rows // MOE_SPLIT
    used = (brows_ref[b] + piece - 1) // piece

    def tile_rows(r, s):
        return pl.ds(r * piece * SUBLANES + s, piece, stride=SUBLANES)

    def body(first, last, n_used):
        wg = wg_ref[0].astype(BF16)
        wu = wu_ref[0].astype(BF16)
        wd = wd_ref[0].astype(BF16)
        for r in range(n_used):
            rs = slice(r * piece, (r + 1) * piece)
            if first:
                for s in range(nsl):
                    x_scr[rs, s * LANES:(s + 1) * LANES] = xb_ref[tile_rows(r, s), :].astype(BF16)
            part = _swiglu_rows(x_scr[rs, :], wg, wu, wd)
            total = part if first else acc_ref[rs, :] + part
            if last:
                for s in range(nsl):
                    y_ref[tile_rows(r, s), :] = total[:, s * LANES:(s + 1) * LANES]
            else:
                acc_ref[rs, :] = total
        if last:
            for r in range(n_used, MOE_SPLIT):
                for s in range(nsl):
                    y_ref[tile_rows(r, s), :] = jnp.zeros((piece, LANES), F32)

    for first, last, when_j in ((True, False, j == 0), (False, False, (j > 0) & (j < nf - 1)),
                                (False, True, j == nf - 1)):
        for n_used in range(1, MOE_SPLIT + 1):
            pl.when(valid & when_j & (used == n_used))(functools.partial(body, first, last, n_used))

    @pl.when(jnp.logical_not(valid) & (j == nf - 1))
    def _():
        y_ref[...] = jnp.zeros(y_ref.shape, F32)


def _moe_experts(xb, bexp, nvalid, brows, wg, wu, wd, n_blocks, rows):
    d = wg.shape[1]
    f = wg.shape[2]
    nf = f // MOE_FT
    assert nf >= 2

    def blk(b, j, bexp, nvalid, brows):
        return (jnp.minimum(b, nvalid[0] - 1), 0)

    def fidx(b, j, nvalid):
        return jnp.where(b < nvalid[0], j, nf - 1)

    grid_spec = pltpu.PrefetchScalarGridSpec(
        num_scalar_prefetch=3,
        grid=(n_blocks, nf),
        in_specs=[pl.BlockSpec((rows * SUBLANES, LANES), blk),
                  pl.BlockSpec((1, d, MOE_FT), lambda b, j, bexp, nvalid, brows: (bexp[b], 0, fidx(b, j, nvalid))),
                  pl.BlockSpec((1, d, MOE_FT), lambda b, j, bexp, nvalid, brows: (bexp[b], 0, fidx(b, j, nvalid))),
                  pl.BlockSpec((1, MOE_FT, d), lambda b, j, bexp, nvalid, brows: (bexp[b], fidx(b, j, nvalid), 0))],
        out_specs=pl.BlockSpec((rows * SUBLANES, LANES), lambda b, j, bexp, nvalid, brows: (b, 0)),
        scratch_shapes=[pltpu.VMEM((rows, d), BF16), pltpu.VMEM((rows, d), F32)],
    )
    return pl.pallas_call(
        _moe_kernel,
        grid_spec=grid_spec,
        out_shape=jax.ShapeDtypeStruct(xb.shape, F32),
        compiler_params=pltpu.CompilerParams(dimension_semantics=("arbitrary", "arbitrary"),
                                             vmem_limit_bytes=MOE_VMEM_LIMIT),
    )(bexp, nvalid, brows, xb, wg, wu, wd)


def _combine_kernel(delta_ref, cnt_ref, off_ref, dst_ref, y_hbm, x_ref, gate_ref, gf_ref, fg_ref, o_ref,
                    buf_ref, sem):
    i = pl.program_id(0)
    tm = x_ref.shape[0]
    nslot = TOP_K * tm
    nsl = x_ref.shape[1] // LANES
    slot = i % 2

    def fetch(tile, to_slot):
        for e in range(N_EXPERTS):
            _segment_copies(cnt_ref[tile * N_EXPERTS + e], y_hbm, dst_ref[tile * N_EXPERTS + e],
                            buf_ref.at[to_slot], off_ref[tile * N_EXPERTS + e], sem.at[to_slot], nslot)

    @pl.when(i == 0)
    def _():
        fetch(0, 0)

    @pl.when(i + 1 < pl.num_programs(0))
    def _():
        fetch(i + 1, 1 - slot)

    pltpu.make_async_copy(y_hbm.at[pl.ds(0, nslot * SUBLANES)], buf_ref.at[slot], sem.at[slot]).wait()

    y_rows = jnp.concatenate([buf_ref[slot, pl.ds(s, nslot, stride=SUBLANES), :] for s in range(nsl)],
                             axis=1).astype(BF16)
    cols = gate_ref[...]
    lane = lax.broadcasted_iota(jnp.int32, (tm, nslot), 1).astype(F32)
    weights = None
    for k in range(TOP_K):
        expert = cols[:, 2 + k:3 + k]
        delta = jnp.zeros((tm, 1), F32)
        for e in range(N_EXPERTS):
            delta = jnp.where(expert == float(e), delta_ref[i * N_EXPERTS + e].astype(F32), delta)
        w_k = jnp.where(lane == delta + cols[:, 4 + k:5 + k], cols[:, k:k + 1], 0.0)
        weights = w_k if weights is None else weights + w_k
    f = jnp.dot(weights.astype(BF16), y_rows, preferred_element_type=F32)
    x = x_ref[...] + gf_ref[0] * f
    ms = jnp.mean(x * x, axis=-1, keepdims=True)
    o_ref[...] = x * lax.rsqrt(ms + EPS) * fg_ref[...]


def _combine(y, seg, x2, gates, gf, fg, tiles_per_batch):
    t, d = x2.shape
    tm = ROW_TILE
    nslot = TOP_K * tm
    grid_spec = pltpu.PrefetchScalarGridSpec(
        num_scalar_prefetch=4,
        grid=(t // tm,),
        in_specs=[pl.BlockSpec(memory_space=pl.ANY),
                  pl.BlockSpec((tm, d), lambda i, *_: (i, 0)),
                  pl.BlockSpec((tm, SUBLANES), lambda i, *_: (i, 0)),
                  pl.BlockSpec((1, 1, d), lambda i, *_: (i // tiles_per_batch, 0, 0)),
                  pl.BlockSpec((1, d), lambda i, *_: (0, 0))],
        out_specs=pl.BlockSpec((tm, d), lambda i, *_: (i, 0)),
        scratch_shapes=[pltpu.VMEM((2, nslot * SUBLANES, LANES), F32), pltpu.SemaphoreType.DMA((2,))],
    )
    return pl.pallas_call(
        _combine_kernel,
        grid_spec=grid_spec,
        out_shape=jax.ShapeDtypeStruct((t, d), F32),
        compiler_params=_cparams("arbitrary"),
    )(*seg, y, x2, gates, gf, fg)


def _final_norm_kernel(x_ref, g_ref, o_ref):
    x = x_ref[...]
    o_ref[...] = x * lax.rsqrt(jnp.mean(x * x, axis=-1, keepdims=True) + EPS) * g_ref[...]


def _final_norm(x2, g):
    t, d = x2.shape
    tm = ROW_TILE
    return pl.pallas_call(
        _final_norm_kernel,
        grid=(t // tm,),
        in_specs=[pl.BlockSpec((tm, d), lambda i: (i, 0)), pl.BlockSpec((1, d), lambda i: (0, 0))],
        out_specs=pl.BlockSpec((tm, d), lambda i: (i, 0)),
        out_shape=jax.ShapeDtypeStruct((t, d), F32),
        compiler_params=_cparams("parallel"),
    )(x2, g)


def _qk_column_perm():
    half = HEAD_DIM // 2
    lane = np.arange(LANES)
    pair = (lane % HEAD_DIM) // half
    dim = lane % half + half * (lane // HEAD_DIM)
    q = np.concatenate([(i + Q_TILES * pair) * HEAD_DIM + dim for i in range(Q_TILES)])
    k = ATTN_WIDTH + pair * HEAD_DIM + dim
    return q, k


def _attn_out_row_perm():
    lane = np.arange(LANES)
    return np.concatenate([(i + Q_TILES * (lane // HEAD_DIM)) * HEAD_DIM + lane % HEAD_DIM
                           for i in range(Q_TILES)])


def _pad_lanes(v):
    return jnp.pad(v.astype(F32), (0, LANES - v.shape[0])).reshape(1, LANES)


def _moe_route_tables(counts, base, rows, n_blocks):
    cnt = counts[0, :N_EXPERTS].astype(jnp.int32)
    padded = (cnt + rows - 1) // rows * rows
    pad_end = jnp.cumsum(padded)
    pad_start = pad_end - padded
    first = base[:, 0, :N_EXPERTS].astype(jnp.int32)
    run = jnp.concatenate([first[1:], cnt[None]], axis=0) - first
    off = jnp.cumsum(run, axis=1) - run
    flat = lambda a: a.reshape(-1).astype(jnp.int32)
    seg = (flat(off - first), flat(run), flat(off), flat(pad_start[None, :] + first))
    blk_start = jnp.arange(n_blocks, dtype=jnp.int32) * rows
    bexp = jnp.minimum(jnp.sum(blk_start[:, None] >= pad_end[None, :], axis=1), N_EXPERTS - 1).astype(jnp.int32)
    nvalid = (pad_end[-1:] // rows).astype(jnp.int32)
    row_end = pad_start + cnt
    brows = jnp.clip(row_end[bexp] - blk_start, 0, rows).astype(jnp.int32)
    piece = rows // MOE_SPLIT
    piece_end = (row_end + piece - 1) // piece * piece
    zero_spans = (row_end.astype(jnp.int32), piece_end.astype(jnp.int32), pad_end.astype(jnp.int32))
    return seg, bexp, nvalid, brows, zero_spans


def kernel(x, c, positions, ada_w, ada_b, norm_mix_g, norm_ffn_g, w_in, w_out, attn_sinks, conv_w, conv_b,
           dt_bias, a_log, d_skip, ssm_norm_g, ffn_w_gate, ffn_w_up, ffn_w_down, router_w, moe_w_gate,
           moe_w_up, moe_w_down, final_norm_g):
    batch, seq, d = x.shape
    depth = w_in.shape[0]
    t = batch * seq
    tiles_per_batch = seq // ROW_TILE
    x2 = x.reshape(t, d)

    c8 = jnp.pad(c, ((0, SUBLANES - batch), (0, 0)))
    mod = _ada_mod(c8, ada_w, ada_b)[:, :batch].reshape(depth, batch, 6, 1, d)
    cos, sin = _rope_tables(positions)

    q_perm, k_perm = _qk_column_perm()
    o_perm = _attn_out_row_perm()
    for l in range(depth):
        sh_m, sc_m, g_m, sh_f, sc_f, g_f = (mod[l, :, k] for k in range(6))
        wl = w_in[l]
        w_cat = jnp.concatenate(
            [wl[:, q_perm], wl[:, k_perm], wl[:, C_V:C_DT],
             jnp.pad(wl[:, C_DT:], ((0, 0), (0, LANES - SSM_HEADS)))], axis=1).astype(BF16)
        q, kv, z, xbc, dt = _inproj(x2, norm_mix_g[l].reshape(1, d), sc_m, sh_m, cos, sin, w_cat,
                                    _pad_lanes(dt_bias[l]), tiles_per_batch)
        attn = _attention(q, kv, attn_sinks[l].astype(F32), seq)
        ssm = _ssd(xbc, z, dt, conv_w[l], conv_b[l].reshape(1, CONV_CH), _pad_lanes(a_log[l]),
                   jnp.repeat(d_skip[l].astype(F32), SSM_HEAD_DIM).reshape(1, SSM_WIDTH),
                   ssm_norm_g[l].reshape(1, SSM_WIDTH), batch, seq)
        wo = jnp.concatenate([w_out[l][o_perm], w_out[l][ATTN_WIDTH:]], axis=0).astype(BF16)
        ffn_g = norm_ffn_g[l].reshape(1, d)
        if l % 2 == 0:
            x2 = _outproj_ffn(attn, ssm, x2, wo, g_m, ffn_g, sc_f, sh_f, ffn_w_gate[l // 2].astype(BF16),
                              ffn_w_up[l // 2].astype(BF16), ffn_w_down[l // 2].astype(BF16), g_f, tiles_per_batch)
            if l == depth - 1:
                x2 = _final_norm(x2, final_norm_g.reshape(1, d))
        else:
            if l != depth - 1:
                raise NotImplementedError("the expert layer fuses the final norm and must be last")
            rw = jnp.pad(router_w[l // 2].astype(F32), ((0, 0), (0, LANES - N_EXPERTS)))
            rw_hi = rw.astype(BF16)
            rw_lo = (rw - rw_hi.astype(F32)).astype(BF16)
            x_new, h, route, gates, counts, base = _outproj_router(
                attn, ssm, x2, wo, g_m, ffn_g, sc_f, sh_f, tiles_per_batch,
                jnp.concatenate([rw_hi, rw_lo], axis=1))
            rows = min(MOE_ROWS, t)
            n_blocks = -(-(t * TOP_K) // rows) + N_EXPERTS
            seg, bexp, nvalid, brows, zero_spans = _moe_route_tables(counts, base, rows, n_blocks)
            xb = _dispatch(h, route, seg, zero_spans, nvalid, n_blocks, rows)
            y = _moe_experts(xb, bexp, nvalid, brows, moe_w_gate[l // 2], moe_w_up[l // 2], moe_w_down[l // 2],
                             n_blocks, rows)
            x2 = _combine(y, seg, x_new, gates, g_f, final_norm_g.reshape(1, d), tiles_per_batch)
    return x2.reshape(batch, seq, d)
```

```python
import functools

import numpy as np
import jax
import jax.numpy as jnp
from jax import lax
from jax.experimental import pallas as pl
from jax.experimental.pallas import tpu as pltpu

F32 = jnp.float32
BF16 = jnp.bfloat16

LANES = 128
SUBLANES = 8
VMEM_LIMIT = 48 * 1024 * 1024
MOE_VMEM_LIMIT = 58 * 1024 * 1024

EPS = 1e-6
HEAD_DIM = 64
Q_HEADS = 8
KV_HEADS = 2
GROUP = Q_HEADS // KV_HEADS
ATTN_WIDTH = Q_HEADS * HEAD_DIM
KV_WIDTH = KV_HEADS * HEAD_DIM
WINDOW = 128
ROPE_THETA = 10000.0
SSM_HEADS = 8
SSM_HEAD_DIM = 64
SSM_WIDTH = SSM_HEADS * SSM_HEAD_DIM
SSM_GROUPS = 2
SSM_STATE = 128
GROUP_WIDTH = SSM_WIDTH // SSM_GROUPS
CONV_WIDTH = 4
CONV_CH = SSM_WIDTH + 2 * SSM_GROUPS * SSM_STATE
CHUNK = 128
N_EXPERTS = 8
TOP_K = 2

Q_TILES = ATTN_WIDTH // LANES
C_Q = 0
C_K = C_Q + ATTN_WIDTH
C_V = C_K + KV_WIDTH
C_Z = C_V + KV_WIDTH
C_X = C_Z + SSM_WIDTH
C_DT = C_X + CONV_CH
C_END = C_DT + LANES

ROW_TILE = 512
INPROJ_TILE = 1024
ATTN_TILE = 1024
MOE_ROWS = 1536
MOE_FT = 512
SSD_CHUNKS_PER_STEP = 8
ROW_SPLIT = 2
MOE_SPLIT = 3


def _cparams(*sem):
    return pltpu.CompilerParams(dimension_semantics=sem, vmem_limit_bytes=VMEM_LIMIT)


def _silu(v):
    return v * (1.0 / (1.0 + jnp.exp(-v)))


def _softplus(v):
    return jnp.maximum(v, 0.0) + jnp.log1p(jnp.exp(-jnp.abs(v)))


def _rms_mod(x, g, scale, shift):
    ms = jnp.mean(x * x, axis=-1, keepdims=True)
    return (x * lax.rsqrt(ms + EPS) * g) * (1.0 + scale) + shift


def _ada_kernel(c_ref, w_ref, b_ref, o_ref):
    c = c_ref[...]
    o_ref[0] = jnp.dot(_silu(c), w_ref[0], preferred_element_type=F32,
                       precision=lax.Precision.HIGHEST) + b_ref[0]


def _ada_mod(c8, ada_w, ada_b):
    depth, d, n = ada_w.shape
    tn = 1536
    return pl.pallas_call(
        _ada_kernel,
        grid=(depth, n // tn),
        in_specs=[pl.BlockSpec((SUBLANES, d), lambda l, j: (0, 0)),
                  pl.BlockSpec((1, d, tn), lambda l, j: (l, 0, j)),
                  pl.BlockSpec((1, 1, tn), lambda l, j: (l, 0, j))],
        out_specs=pl.BlockSpec((1, SUBLANES, tn), lambda l, j: (l, 0, j)),
        out_shape=jax.ShapeDtypeStruct((depth, SUBLANES, n), F32),
        compiler_params=_cparams("parallel", "parallel"),
    )(c8, ada_w, ada_b.reshape(depth, 1, n))


ROPE_PER_ROW = LANES // (HEAD_DIM // 2)


def _rope_kernel(pos_ref, inv_ref, sign_ref, cos_ref, sin_ref):
    ang = pos_ref[...].astype(F32) * inv_ref[...]
    tr = ang.shape[0]
    group = lax.broadcasted_iota(jnp.int32, (1, LANES), 1) // (HEAD_DIM // 2)
    for table, out_ref, sign in ((jnp.cos(ang), cos_ref, None), (jnp.sin(ang), sin_ref, sign_ref[...])):
        rolled = [table] + [pltpu.roll(table, (HEAD_DIM // 2) * s, axis=1) for s in range(1, ROPE_PER_ROW)]
        for j in range(ROPE_PER_ROW):
            rep = rolled[(0 - j) % ROPE_PER_ROW]
            for g in range(1, ROPE_PER_ROW):
                rep = jnp.where(group == g, rolled[(g - j) % ROPE_PER_ROW], rep)
            out_ref[pl.ds(j, tr, stride=ROPE_PER_ROW), :] = rep if sign is None else rep * sign


def _rope_tables(positions):
    t = positions.size
    half = HEAD_DIM // 2
    pos_dense = jnp.repeat(positions.reshape(t // ROPE_PER_ROW, ROPE_PER_ROW), half, axis=1)
    inv_freq = ROPE_THETA ** (-jnp.arange(0, HEAD_DIM, 2, dtype=F32) / HEAD_DIM)
    inv_dense = jnp.tile(inv_freq, ROPE_PER_ROW).reshape(1, LANES)
    sign = jnp.repeat(jnp.array([-1.0, -1.0, 1.0, 1.0], F32), half).reshape(1, LANES)
    rows = t // ROPE_PER_ROW
    tr = min(1024, rows)
    return pl.pallas_call(
        _rope_kernel,
        grid=(rows // tr,),
        in_specs=[pl.BlockSpec((tr, LANES), lambda i: (i, 0)),
                  pl.BlockSpec((1, LANES), lambda i: (0, 0)),
                  pl.BlockSpec((1, LANES), lambda i: (0, 0))],
        out_specs=[pl.BlockSpec((tr * ROPE_PER_ROW, LANES), lambda i: (i, 0))] * 2,
        out_shape=[jax.ShapeDtypeStruct((t, LANES), F32)] * 2,
        compiler_params=_cparams("parallel"),
    )(pos_dense, inv_dense, sign)


def _inproj_kernel(x_ref, g_ref, sc_ref, sh_ref, cos_ref, sin_ref, w_ref, dtb_ref,
                   q_ref, kv_ref, z_ref, xbc_ref, dt_ref):
    h = _rms_mod(x_ref[...], g_ref[...], sc_ref[0], sh_ref[0]).astype(BF16)
    cos = cos_ref[...]
    sin = sin_ref[...]

    def rope(t):
        return t * cos + pltpu.roll(t, LANES // 2, axis=1) * sin

    qkv = jnp.dot(h, w_ref[:, C_Q:C_Z], preferred_element_type=F32)
    for i in range(Q_TILES):
        q_ref[:, i * LANES:(i + 1) * LANES] = (
            rope(qkv[:, i * LANES:(i + 1) * LANES]) * (HEAD_DIM ** -0.5)).astype(BF16)
    kv_ref[:, 0:LANES] = rope(qkv[:, C_K:C_V]).astype(BF16)
    kv_ref[:, LANES:2 * LANES] = qkv[:, C_V:C_Z].astype(BF16)
    z_ref[...] = jnp.dot(h, w_ref[:, C_Z:C_X], preferred_element_type=F32)
    xbc_ref[...] = jnp.dot(h, w_ref[:, C_X:C_DT], preferred_element_type=F32)
    dt_raw = jnp.dot(h, w_ref[:, C_DT:C_END], preferred_element_type=F32)
    dt_ref[...] = _softplus(dt_raw + dtb_ref[...])


def _inproj(x2, g, sc, sh, cos, sin, w, dtb, seq):
    t, d = x2.shape
    tm = min(INPROJ_TILE, seq)
    row = lambda i: (i, 0)
    const = lambda i: (0, 0)
    per_b = lambda i: (i // (seq // tm), 0, 0)
    return pl.pallas_call(
        _inproj_kernel,
        grid=(t // tm,),
        in_specs=[pl.BlockSpec((tm, d), row),
                  pl.BlockSpec((1, d), const),
                  pl.BlockSpec((1, 1, d), per_b),
                  pl.BlockSpec((1, 1, d), per_b),
                  pl.BlockSpec((tm, LANES), row),
                  pl.BlockSpec((tm, LANES), row),
                  pl.BlockSpec((d, C_END), const),
                  pl.BlockSpec((1, LANES), const)],
        out_specs=[pl.BlockSpec((tm, ATTN_WIDTH), row),
                   pl.BlockSpec((tm, 2 * KV_WIDTH), row),
                   pl.BlockSpec((tm, SSM_WIDTH), row),
                   pl.BlockSpec((tm, CONV_CH), row),
                   pl.BlockSpec((tm, LANES), row)],
        out_shape=[jax.ShapeDtypeStruct((t, ATTN_WIDTH), BF16),
                   jax.ShapeDtypeStruct((t, 2 * KV_WIDTH), BF16),
                   jax.ShapeDtypeStruct((t, SSM_WIDTH), F32),
                   jax.ShapeDtypeStruct((t, CONV_CH), F32),
                   jax.ShapeDtypeStruct((t, LANES), F32)],
        compiler_params=_cparams("parallel"),
    )(x2, g, sc, sh, cos, sin, w, dtb)


def _attn_kernel(sink_ref, q_ref, kv_ref, kvp_ref, o_ref, *, tiles_per_seq):
    first = (pl.program_id(0) % tiles_per_seq) == 0
    blk = WINDOW
    nsub = ATTN_TILE // blk
    lane = lax.broadcasted_iota(jnp.int32, (1, LANES), 1)
    k_lo_mask = (lane % HEAD_DIM) < (HEAD_DIM // 2)
    v_lo_mask = lane < HEAD_DIM
    zero = jnp.zeros((), BF16)

    k_all = jnp.concatenate([kvp_ref[:, 0:LANES], kv_ref[:, 0:LANES]], axis=0)
    v_all = jnp.concatenate([kvp_ref[:, LANES:2 * LANES], kv_ref[:, LANES:2 * LANES]], axis=0)
    k_sel = (jnp.where(k_lo_mask, k_all, zero), jnp.where(k_lo_mask, zero, k_all))
    v_sel = (jnp.where(v_lo_mask, v_all, zero), jnp.where(v_lo_mask, zero, v_all))

    qi = lax.broadcasted_iota(jnp.int32, (blk, blk), 0)
    col = lax.broadcasted_iota(jnp.int32, (blk, blk), 1)
    cur = col <= qi
    no_prev = (col > qi) & (col > jnp.where(first, -1, blk))

    for n in range(nsub):
        q_st = jnp.concatenate([q_ref[n * blk:(n + 1) * blk, i * LANES:(i + 1) * LANES]
                                for i in range(Q_TILES)], axis=0)
        out = None
        for hk in range(KV_HEADS):
            k_n = k_sel[hk][n * blk:(n + 2) * blk]
            v_n = v_sel[hk][n * blk:(n + 2) * blk]
            s = lax.dot_general(q_st, k_n, (((1,), (1,)), ((), ())), preferred_element_type=F32)
            probs, scales = [], []
            for i in range(GROUP):
                s_i = s[i * blk:(i + 1) * blk]
                sc = jnp.where(cur, s_i[:, blk:], s_i[:, :blk])
                if n == 0:
                    sc = jnp.where(no_prev, -jnp.inf, sc)
                sink = sink_ref[hk * GROUP + i]
                m = jnp.maximum(jnp.max(sc, axis=-1, keepdims=True), sink)
                p = jnp.exp(sc - m)
                scales.append(1.0 / (jnp.sum(p, axis=-1, keepdims=True) + jnp.exp(sink - m)))
                probs.append(jnp.concatenate([jnp.where(cur, 0.0, p), jnp.where(cur, p, 0.0)],
                                             axis=1).astype(BF16))
            o = jnp.dot(jnp.concatenate(probs, axis=0), v_n, preferred_element_type=F32)
            o = o * jnp.concatenate(scales, axis=0)
            out = o if out is None else out + o
        for i in range(Q_TILES):
            o_ref[n * blk:(n + 1) * blk, i * LANES:(i + 1) * LANES] = out[i * blk:(i + 1) * blk].astype(BF16)


def _attention(q, kv, sinks, seq):
    t = q.shape[0]
    tq = ATTN_TILE
    r = tq // WINDOW
    kern = functools.partial(_attn_kernel, tiles_per_seq=seq // tq)
    return pl.pallas_call(
        kern,
        grid=(t // tq,),
        in_specs=[pl.BlockSpec(memory_space=pltpu.SMEM),
                  pl.BlockSpec((tq, ATTN_WIDTH), lambda i: (i, 0)),
                  pl.BlockSpec((tq, 2 * KV_WIDTH), lambda i: (i, 0)),
                  pl.BlockSpec((WINDOW, 2 * KV_WIDTH), lambda i: (jnp.maximum(i * r - 1, 0), 0))],
        out_specs=pl.BlockSpec((tq, ATTN_WIDTH), lambda i: (i, 0)),
        out_shape=jax.ShapeDtypeStruct((t, ATTN_WIDTH), BF16),
        compiler_params=_cparams("parallel"),
    )(sinks, q, kv, kv)


def _split3(v):
    p0 = v.astype(BF16)
    r1 = v - p0.astype(F32)
    p1 = r1.astype(BF16)
    return p0, p1, (r1 - p1.astype(F32)).astype(BF16)


def _dot3(lhs_exact_bf16, v):
    return sum(jnp.dot(lhs_exact_bf16, p, preferred_element_type=F32) for p in _split3(v))


def _ssd_kernel(xbc_ref, z_ref, dt_ref, cw_ref, cb_ref, alog_ref, dskip_ref, ng_ref,
                o_ref, ext_ref, st_ref):
    L = CHUNK
    rows = xbc_ref.shape[0]
    halo = SUBLANES

    @pl.when(pl.program_id(1) == 0)
    def _():
        ext_ref[0:halo, :] = jnp.zeros((halo, CONV_CH), F32)
        st_ref[...] = jnp.zeros(st_ref.shape, F32)

    ext_ref[halo:halo + rows, :] = xbc_ref[...]
    acc = cb_ref[...] + cw_ref[CONV_WIDTH - 1:CONV_WIDTH, :] * ext_ref[halo:halo + rows, :]
    for k in range(CONV_WIDTH - 1):
        off = halo - (CONV_WIDTH - 1) + k
        acc = acc + cw_ref[k:k + 1, :] * ext_ref[off:off + rows, :]
    ext_ref[0:halo, :] = ext_ref[rows:rows + halo, :]
    u_all = _silu(acc)

    lane = lax.broadcasted_iota(jnp.int32, (1, LANES), 1)
    a = jnp.where(lane < SSM_HEADS, -jnp.exp(alog_ref[...]), 0.0)
    row = lax.broadcasted_iota(jnp.int32, (L, L), 0)
    col = lax.broadcasted_iota(jnp.int32, (L, L), 1)
    causal = row >= col
    tri = jnp.where(causal, 1.0, 0.0).astype(BF16)
    spread = jnp.where(lax.broadcasted_iota(jnp.int32, (LANES, SSM_WIDTH), 1) // SSM_HEAD_DIM
                       == lax.broadcasted_iota(jnp.int32, (LANES, SSM_WIDTH), 0), 1.0, 0.0).astype(BF16)
    r_heads = SSM_HEADS // SSM_GROUPS
    glane = lax.broadcasted_iota(jnp.int32, (1, GROUP_WIDTH), 1) // SSM_HEAD_DIM
    zero = jnp.zeros((), BF16)
    states = [st_ref[g] for g in range(SSM_GROUPS)]

    for c in range(rows // L):
        rs = slice(c * L, (c + 1) * L)
        u = u_all[rs]
        xs = u[:, 0:SSM_WIDTH]
        bm = u[:, SSM_WIDTH:SSM_WIDTH + SSM_GROUPS * SSM_STATE]
        cm = u[:, SSM_WIDTH + SSM_GROUPS * SSM_STATE:]
        dt = dt_ref[rs, :]
        acs = _dot3(tri, dt * a)
        acs_t = acs.T
        dt_e = sum(jnp.dot(p, spread, preferred_element_type=F32) for p in _split3(dt))
        acs_e = sum(jnp.dot(p, spread, preferred_element_type=F32) for p in _split3(acs))
        last = acs_e[L - 1:L, :]
        xd = xs * dt_e
        xd_b = xd.astype(BF16)
        xdw_b = (xd * jnp.exp(last - acs_e)).astype(BF16)
        e_acs = jnp.exp(acs_e)
        c_dec = jnp.exp(last)

        ys = []
        for g in range(SSM_GROUPS):
            gs = slice(g * GROUP_WIDTH, (g + 1) * GROUP_WIDTH)
            b_g = bm[:, g * SSM_STATE:(g + 1) * SSM_STATE]
            c_b = cm[:, g * SSM_STATE:(g + 1) * SSM_STATE].astype(BF16)
            cb = lax.dot_general(c_b, b_g.astype(BF16), (((1,), (1,)), ((), ())),
                                 preferred_element_type=F32)
            st = states[g]
            y_g = jnp.dot(c_b, st.astype(BF16), preferred_element_type=F32) * e_acs[:, gs]
            xd_g = xd_b[:, gs]
            for r in range(r_heads):
                h = g * r_heads + r
                seg = acs[:, h:h + 1] - acs_t[h:h + 1, :]
                m_h = (cb * jnp.exp(jnp.where(causal, seg, -jnp.inf))).astype(BF16)
                y_g = y_g + jnp.dot(m_h, jnp.where(glane == r, xd_g, zero), preferred_element_type=F32)
            new = jnp.dot(b_g.T.astype(BF16), xdw_b[:, gs], preferred_element_type=F32)
            states[g] = c_dec[:, gs] * st + new
            ys.append(y_g)

        y = jnp.concatenate(ys, axis=1) + dskip_ref[...] * xs
        y = y * _silu(z_ref[rs, :])
        outs = []
        for g in range(SSM_GROUPS):
            yg = y[:, g * GROUP_WIDTH:(g + 1) * GROUP_WIDTH]
            outs.append(yg * lax.rsqrt(jnp.mean(yg * yg, axis=-1, keepdims=True) + EPS))
        o_ref[rs, :] = (jnp.concatenate(outs, axis=1) * ng_ref[...]).astype(BF16)

    for g in range(SSM_GROUPS):
        st_ref[g] = states[g]


def _ssd(xbc, z, dt, cw, cb, alog, dskip, ng, batch, seq):
    t = xbc.shape[0]
    rows = SSD_CHUNKS_PER_STEP * CHUNK
    ns = seq // rows
    row = lambda b, c: (b * ns + c, 0)
    const = lambda b, c: (0, 0)
    return pl.pallas_call(
        _ssd_kernel,
        grid=(batch, ns),
        in_specs=[pl.BlockSpec((rows, CONV_CH), row),
                  pl.BlockSpec((rows, SSM_WIDTH), row),
                  pl.BlockSpec((rows, LANES), row),
                  pl.BlockSpec((CONV_WIDTH, CONV_CH), const),
                  pl.BlockSpec((1, CONV_CH), const),
                  pl.BlockSpec((1, LANES), const),
                  pl.BlockSpec((1, SSM_WIDTH), const),
                  pl.BlockSpec((1, SSM_WIDTH), const)],
        out_specs=pl.BlockSpec((rows, SSM_WIDTH), row),
        out_shape=jax.ShapeDtypeStruct((t, SSM_WIDTH), BF16),
        scratch_shapes=[pltpu.VMEM((SUBLANES + rows, CONV_CH), F32),
                        pltpu.VMEM((SSM_GROUPS, SSM_STATE, GROUP_WIDTH), F32)],
        compiler_params=_cparams("parallel", "arbitrary"),
    )(xbc, z, dt, cw, cb, alog, dskip, ng)


def _outproj_core(a_ref, s_ref, x_ref, w_ref, gm_ref, g_ref, sc_ref, sh_ref):
    half = a_ref.shape[1]
    mixed = (jnp.dot(a_ref[...], w_ref[0:half, :], preferred_element_type=F32)
             + jnp.dot(s_ref[...], w_ref[half:, :], preferred_element_type=F32))
    x_new = x_ref[...] + gm_ref[0] * mixed
    return x_new, _rms_mod(x_new, g_ref[...], sc_ref[0], sh_ref[0])


def _outproj_router_kernel(a_ref, s_ref, x_ref, w_ref, gm_ref, g_ref, sc_ref, sh_ref, rw_ref,
                           xo_ref, h_ref, route_ref, gate_ref, cnt_ref, base_ref, carry_ref):
    x_new, h = _outproj_core(a_ref, s_ref, x_ref, w_ref, gm_ref, g_ref, sc_ref, sh_ref)
    xo_ref[...] = x_new
    tm = h.shape[0]
    h_ref[...] = h.astype(BF16)

    h_hi = h.astype(BF16)
    h_lo = (h - h_hi.astype(F32)).astype(BF16)
    hi_part = jnp.dot(h_hi, rw_ref[...], preferred_element_type=F32)
    logits = (hi_part[:, 0:LANES] + hi_part[:, LANES:]
              + jnp.dot(h_lo, rw_ref[:, 0:LANES], preferred_element_type=F32))
    lane = lax.broadcasted_iota(jnp.int32, (tm, LANES), 1).astype(F32)
    logits = jnp.where(lane < N_EXPERTS, logits, -jnp.inf)
    m0 = jnp.max(logits, axis=-1, keepdims=True)
    i0 = jnp.min(jnp.where(logits == m0, lane, float(LANES)), axis=-1, keepdims=True)
    rest = jnp.where(lane == i0, -jnp.inf, logits)
    m1 = jnp.max(rest, axis=-1, keepdims=True)
    i1 = jnp.min(jnp.where(rest == m1, lane, float(LANES)), axis=-1, keepdims=True)
    e = jnp.exp(m1 - m0)
    g0 = 1.0 / (1.0 + e)
    g1 = e / (1.0 + e)

    @pl.when(pl.program_id(0) == 0)
    def _():
        carry_ref[...] = jnp.zeros(carry_ref.shape, F32)

    sel0 = lane == i0
    sel1 = lane == i1
    member = jnp.where(sel0 | sel1, 1.0, 0.0)
    r_i = lax.broadcasted_iota(jnp.int32, (tm, tm), 0)
    c_i = lax.broadcasted_iota(jnp.int32, (tm, tm), 1)
    strict = jnp.where(r_i > c_i, 1.0, 0.0).astype(BF16)
    base_ref[0] = carry_ref[...]
    rank = carry_ref[...] + jnp.dot(strict, member.astype(BF16), preferred_element_type=F32)
    r0 = jnp.sum(jnp.where(sel0, rank, 0.0), axis=-1, keepdims=True)
    r1 = jnp.sum(jnp.where(sel1, rank, 0.0), axis=-1, keepdims=True)
    carry_ref[...] = carry_ref[...] + jnp.sum(member, axis=0, keepdims=True)
    cnt_ref[...] = carry_ref[...]

    cols = r1
    for k, colv in enumerate((g0, g1, i0, i1, r0)):
        cols = jnp.where(lane == float(k), colv, cols)
    gate_ref[...] = cols[:, 0:SUBLANES]
    diag = (lax.broadcasted_iota(jnp.int32, (tm, LANES), 0) % LANES
            == lax.broadcasted_iota(jnp.int32, (tm, LANES), 1))
    nrow = tm // LANES
    for q, colv in enumerate((i0, i1, r0, r1)):
        spread = jnp.where(diag, colv, 0.0)
        rows = [jnp.sum(spread[b * LANES:(b + 1) * LANES], axis=0, keepdims=True) for b in range(nrow)]
        route_ref[0, q * nrow:(q + 1) * nrow, :] = jnp.concatenate(rows, axis=0).astype(jnp.int32)


def _outproj_router(attn, ssm, x2, w, gm, g, sc, sh, tiles_per_batch, router):
    t, d = x2.shape
    tm = ROW_TILE
    row = lambda i: (i, 0)
    const = lambda i: (0, 0)
    per_b = lambda i: (i // tiles_per_batch, 0, 0)
    half = attn.shape[1]
    return pl.pallas_call(
        _outproj_router_kernel,
        grid=(t // tm,),
        in_specs=[pl.BlockSpec((tm, half), row),
                  pl.BlockSpec((tm, half), row),
                  pl.BlockSpec((tm, d), row),
                  pl.BlockSpec((2 * half, d), const),
                  pl.BlockSpec((1, 1, d), per_b),
                  pl.BlockSpec((1, d), const),
                  pl.BlockSpec((1, 1, d), per_b),
                  pl.BlockSpec((1, 1, d), per_b),
                  pl.BlockSpec((d, 2 * LANES), const)],
        out_specs=[pl.BlockSpec((tm, d), row),
                   pl.BlockSpec((tm, d), row),
                   pl.BlockSpec((1, 4 * tm // LANES, LANES), lambda i: (i, 0, 0)),
                   pl.BlockSpec((tm, SUBLANES), row),
                   pl.BlockSpec((1, LANES), const),
                   pl.BlockSpec((1, 1, LANES), lambda i: (i, 0, 0))],
        out_shape=[jax.ShapeDtypeStruct((t, d), F32),
                   jax.ShapeDtypeStruct((t, d), BF16),
                   jax.ShapeDtypeStruct((t // tm, 4 * tm // LANES, LANES), jnp.int32),
                   jax.ShapeDtypeStruct((t, SUBLANES), F32),
                   jax.ShapeDtypeStruct((1, LANES), F32),
                   jax.ShapeDtypeStruct((t // tm, 1, LANES), F32)],
        scratch_shapes=[pltpu.VMEM((1, LANES), F32)],
        compiler_params=_cparams("arbitrary"),
    )(attn, ssm, x2, w, gm, g, sc, sh, router)


def _swiglu_rows(x, wg, wu, wd):
    gate = jnp.dot(x, wg, preferred_element_type=F32)
    up = jnp.dot(x, wu, preferred_element_type=F32)
    return jnp.dot((_silu(gate) * up).astype(BF16), wd, preferred_element_type=F32)


def _outproj_ffn_kernel(a_ref, s_ref, x_ref, w_ref, gm_ref, g_ref, sc_ref, sh_ref, wg_ref, wu_ref, wd_ref, gf_ref,
                        o_ref):
    half = a_ref.shape[1]
    piece = x_ref.shape[0] // ROW_SPLIT
    for r in range(ROW_SPLIT):
        rs = slice(r * piece, (r + 1) * piece)
        mixed = (jnp.dot(a_ref[rs, :], w_ref[0:half, :], preferred_element_type=F32)
                 + jnp.dot(s_ref[rs, :], w_ref[half:, :], preferred_element_type=F32))
        x_new = x_ref[rs, :] + gm_ref[0] * mixed
        h = _rms_mod(x_new, g_ref[...], sc_ref[0], sh_ref[0]).astype(BF16)
        o_ref[rs, :] = x_new + gf_ref[0] * _swiglu_rows(h, wg_ref[...], wu_ref[...], wd_ref[...])


def _outproj_ffn(attn, ssm, x2, w, gm, g, sc, sh, wg, wu, wd, gf, tiles_per_batch):
    t, d = x2.shape
    f = wg.shape[1]
    tm = ROW_TILE
    half = attn.shape[1]
    row = lambda i: (i, 0)
    const = lambda i: (0, 0)
    per_b = lambda i: (i // tiles_per_batch, 0, 0)
    resident = pl.Buffered(1)
    return pl.pallas_call(
        _outproj_ffn_kernel,
        grid=(t // tm,),
        in_specs=[pl.BlockSpec((tm, half), row),
                  pl.BlockSpec((tm, half), row),
                  pl.BlockSpec((tm, d), row),
                  pl.BlockSpec((2 * half, d), const, pipeline_mode=resident),
                  pl.BlockSpec((1, 1, d), per_b),
                  pl.BlockSpec((1, d), const),
                  pl.BlockSpec((1, 1, d), per_b),
                  pl.BlockSpec((1, 1, d), per_b),
                  pl.BlockSpec((d, f), const, pipeline_mode=resident),
                  pl.BlockSpec((d, f), const, pipeline_mode=resident),
                  pl.BlockSpec((f, d), const, pipeline_mode=resident),
                  pl.BlockSpec((1, 1, d), per_b)],
        out_specs=pl.BlockSpec((tm, d), row),
        out_shape=jax.ShapeDtypeStruct((t, d), F32),
        compiler_params=_cparams("parallel"),
    )(attn, ssm, x2, w, gm, g, sc, sh, wg, wu, wd, gf)


def _segment_copies(count, src, src_row, dst, dst_row, sem, max_rows):
    for k in range(max_rows.bit_length() - 1, -1, -1):
        size = (1 << k) * SUBLANES

        @pl.when(((count >> k) & 1) == 1)
        def _(k=k, size=size):
            done = (count >> (k + 1)) << (k + 1)
            pltpu.make_async_copy(
                src.at[pl.ds(pl.multiple_of((src_row + done) * SUBLANES, SUBLANES), size)],
                dst.at[pl.ds(pl.multiple_of((dst_row + done) * SUBLANES, SUBLANES), size)], sem).start()


def _lane_rows(route_ref, q, nrow):
    return jnp.concatenate([route_ref[0, q * nrow + b:q * nrow + b + 1, :] for b in range(nrow)], axis=1)


def _dispatch_kernel(delta_ref, cnt_ref, off_ref, dst_ref, zs_ref, zm_ref, ze_ref, nvalid_ref,
                     route_ref, h_ref, xb_hbm, g_ref, zero_ref, sem, zsem):
    i = pl.program_id(0)
    n = pl.num_programs(0)
    tm = h_ref.shape[0]
    nslot = TOP_K * tm
    slot = i % 2
    blk_rows = zero_ref.shape[0]
    n_blocks = xb_hbm.shape[0] // blk_rows

    def zero_row(r):
        return pltpu.make_async_copy(
            zero_ref.at[pl.ds(0, SUBLANES)],
            xb_hbm.at[pl.ds(pl.multiple_of(r * SUBLANES, SUBLANES), SUBLANES)], zsem)

    piece_rows = blk_rows // MOE_SPLIT

    def zero_piece(p):
        return pltpu.make_async_copy(
            zero_ref.at[pl.ds(0, piece_rows)],
            xb_hbm.at[pl.ds(pl.multiple_of(p * piece_rows, piece_rows), piece_rows)], zsem)

    def zero_block(b):
        return pltpu.make_async_copy(
            zero_ref, xb_hbm.at[pl.ds(pl.multiple_of(b * blk_rows, blk_rows), blk_rows)], zsem)

    @pl.when(i == 0)
    def _():
        zero_ref[...] = jnp.zeros(zero_ref.shape, F32)
        for start in (True, False):
            for e in range(N_EXPERTS):
                def rows_body(r, c):
                    zero_row(r).start() if start else zero_row(r).wait()
                    return c
                lax.fori_loop(zs_ref[e], zm_ref[e], rows_body, 0)

                def pieces_body(p, c):
                    zero_piece(p).start() if start else zero_piece(p).wait()
                    return c
                lax.fori_loop(zm_ref[e] // (piece_rows // SUBLANES), ze_ref[e] // (piece_rows // SUBLANES),
                              pieces_body, 0)

            def blocks_body(b, c):
                zero_block(b).start() if start else zero_block(b).wait()
                return c
            lax.fori_loop(nvalid_ref[0], n_blocks, blocks_body, 0)

    def wait_slot(sl):
        pltpu.make_async_copy(g_ref.at[sl], xb_hbm.at[pl.ds(0, nslot * SUBLANES)], sem.at[sl]).wait()

    @pl.when(i >= 2)
    def _():
        wait_slot(slot)

    nrow = tm // LANES
    experts = (_lane_rows(route_ref, 0, nrow), _lane_rows(route_ref, 1, nrow))
    ranks = (_lane_rows(route_ref, 2, nrow), _lane_rows(route_ref, 3, nrow))
    row_id = lax.broadcasted_iota(jnp.int32, (nslot, tm), 0)
    hit = None
    for k in range(TOP_K):
        delta = jnp.zeros((1, tm), jnp.int32)
        for e in range(N_EXPERTS):
            delta = jnp.where(experts[k] == e, delta_ref[i * N_EXPERTS + e], delta)
        mine = row_id == delta + ranks[k]
        hit = mine if hit is None else hit | mine
    grouped = jnp.dot(jnp.where(hit, 1.0, 0.0).astype(BF16), h_ref[...], preferred_element_type=F32)
    for s in range(grouped.shape[1] // LANES):
        g_ref[slot, pl.ds(s, nslot, stride=SUBLANES), :] = grouped[:, s * LANES:(s + 1) * LANES]
    for e in range(N_EXPERTS):
        _segment_copies(cnt_ref[i * N_EXPERTS + e], g_ref.at[slot], off_ref[i * N_EXPERTS + e],
                        xb_hbm, dst_ref[i * N_EXPERTS + e], sem.at[slot], nslot)

    @pl.when(i == n - 1)
    def _():
        wait_slot(slot)

    @pl.when((i == n - 1) & (n >= 2))
    def _():
        wait_slot(1 - slot)


def _dispatch(h, route, seg, zero_spans, nvalid, n_blocks, rows):
    t, d = h.shape
    tm = ROW_TILE
    nslot = TOP_K * tm
    nrow = tm // LANES
    grid_spec = pltpu.PrefetchScalarGridSpec(
        num_scalar_prefetch=8,
        grid=(t // tm,),
        in_specs=[pl.BlockSpec((1, 4 * nrow, LANES), lambda i, *_: (i, 0, 0)),
                  pl.BlockSpec((tm, d), lambda i, *_: (i, 0))],
        out_specs=pl.BlockSpec(memory_space=pl.ANY),
        scratch_shapes=[pltpu.VMEM((2, nslot * SUBLANES, LANES), F32),
                        pltpu.VMEM((rows * SUBLANES, LANES), F32),
                        pltpu.SemaphoreType.DMA((2,)), pltpu.SemaphoreType.DMA(())],
    )
    return pl.pallas_call(
        _dispatch_kernel,
        grid_spec=grid_spec,
        out_shape=jax.ShapeDtypeStruct((n_blocks * rows * SUBLANES, LANES), F32),
        compiler_params=pltpu.CompilerParams(dimension_semantics=("arbitrary",), has_side_effects=True,
                                             vmem_limit_bytes=VMEM_LIMIT),
    )(*seg, *zero_spans, nvalid, route, h)


def _moe_kernel(bexp_ref, nvalid_ref, brows_ref, xb_ref, wg_ref, wu_ref, wd_ref, y_ref, x_scr, acc_ref):
    b = pl.program_id(0)
    j = pl.program_id(1)
    nf = pl.num_programs(1)
    rows = x_scr.shape[0]
    nsl = x_scr.shape[1] // LANES
    valid = b < nvalid_ref[0]
    piece = rows // MOE_SPLIT
    used = (brows_ref[b] + piece - 1) // piece

    def tile_rows(r, s):
        return pl.ds(r * piece * SUBLANES + s, piece, stride=SUBLANES)

    def body(first, last, n_used):
        wg = wg_ref[0].astype(BF16)
        wu = wu_ref[0].astype(BF16)
        wd = wd_ref[0].astype(BF16)
        for r in range(n_used):
            rs = slice(r * piece, (r + 1) * piece)
            if first:
                for s in range(nsl):
                    x_scr[rs, s * LANES:(s + 1) * LANES] = xb_ref[tile_rows(r, s), :].astype(BF16)
            part = _swiglu_rows(x_scr[rs, :], wg, wu, wd)
            total = part if first else acc_ref[rs, :] + part
            if last:
                for s in range(nsl):
                    y_ref[tile_rows(r, s), :] = total[:, s * LANES:(s + 1) * LANES]
            else:
                acc_ref[rs, :] = total
        if last:
            for r in range(n_used, MOE_SPLIT):
                for s in range(nsl):
                    y_ref[tile_rows(r, s), :] = jnp.zeros((piece, LANES), F32)

    for first, last, when_j in ((True, False, j == 0), (False, False, (j > 0) & (j < nf - 1)),
                                (False, True, j == nf - 1)):
        for n_used in range(1, MOE_SPLIT + 1):
            pl.when(valid & when_j & (used == n_used))(functools.partial(body, first, last, n_used))

    @pl.when(jnp.logical_not(valid) & (j == nf - 1))
    def _():
        y_ref[...] = jnp.zeros(y_ref.shape, F32)


def _moe_experts(xb, bexp, nvalid, brows, wg, wu, wd, n_blocks, rows):
    d = wg.shape[1]
    f = wg.shape[2]
    nf = f // MOE_FT
    assert nf >= 2

    def blk(b, j, bexp, nvalid, brows):
        return (jnp.minimum(b, nvalid[0] - 1), 0)

    def fidx(b, j, nvalid):
        return jnp.where(b < nvalid[0], j, nf - 1)

    grid_spec = pltpu.PrefetchScalarGridSpec(
        num_scalar_prefetch=3,
        grid=(n_blocks, nf),
        in_specs=[pl.BlockSpec((rows * SUBLANES, LANES), blk),
                  pl.BlockSpec((1, d, MOE_FT), lambda b, j, bexp, nvalid, brows: (bexp[b], 0, fidx(b, j, nvalid))),
                  pl.BlockSpec((1, d, MOE_FT), lambda b, j, bexp, nvalid, brows: (bexp[b], 0, fidx(b, j, nvalid))),
                  pl.BlockSpec((1, MOE_FT, d), lambda b, j, bexp, nvalid, brows: (bexp[b], fidx(b, j, nvalid), 0))],
        out_specs=pl.BlockSpec((rows * SUBLANES, LANES), lambda b, j, bexp, nvalid, brows: (b, 0)),
        scratch_shapes=[pltpu.VMEM((rows, d), BF16), pltpu.VMEM((rows, d), F32)],
    )
    return pl.pallas_call(
        _moe_kernel,
        grid_spec=grid_spec,
        out_shape=jax.ShapeDtypeStruct(xb.shape, F32),
        compiler_params=pltpu.CompilerParams(dimension_semantics=("arbitrary", "arbitrary"),
                                             vmem_limit_bytes=MOE_VMEM_LIMIT),
    )(bexp, nvalid, brows, xb, wg, wu, wd)


def _combine_kernel(delta_ref, cnt_ref, off_ref, dst_ref, y_hbm, x_ref, gate_ref, gf_ref, fg_ref, o_ref,
                    buf_ref, sem):
    i = pl.program_id(0)
    tm = x_ref.shape[0]
    nslot = TOP_K * tm
    nsl = x_ref.shape[1] // LANES
    slot = i % 2

    def fetch(tile, to_slot):
        for e in range(N_EXPERTS):
            _segment_copies(cnt_ref[tile * N_EXPERTS + e], y_hbm, dst_ref[tile * N_EXPERTS + e],
                            buf_ref.at[to_slot], off_ref[tile * N_EXPERTS + e], sem.at[to_slot], nslot)

    @pl.when(i == 0)
    def _():
        fetch(0, 0)

    @pl.when(i + 1 < pl.num_programs(0))
    def _():
        fetch(i + 1, 1 - slot)

    pltpu.make_async_copy(y_hbm.at[pl.ds(0, nslot * SUBLANES)], buf_ref.at[slot], sem.at[slot]).wait()

    y_rows = jnp.concatenate([buf_ref[slot, pl.ds(s, nslot, stride=SUBLANES), :] for s in range(nsl)],
                             axis=1).astype(BF16)
    cols = gate_ref[...]
    lane = lax.broadcasted_iota(jnp.int32, (tm, nslot), 1).astype(F32)
    weights = None
    for k in range(TOP_K):
        expert = cols[:, 2 + k:3 + k]
        delta = jnp.zeros((tm, 1), F32)
        for e in range(N_EXPERTS):
            delta = jnp.where(expert == float(e), delta_ref[i * N_EXPERTS + e].astype(F32), delta)
        w_k = jnp.where(lane == delta + cols[:, 4 + k:5 + k], cols[:, k:k + 1], 0.0)
        weights = w_k if weights is None else weights + w_k
    f = jnp.dot(weights.astype(BF16), y_rows, preferred_element_type=F32)
    x = x_ref[...] + gf_ref[0] * f
    ms = jnp.mean(x * x, axis=-1, keepdims=True)
    o_ref[...] = x * lax.rsqrt(ms + EPS) * fg_ref[...]


def _combine(y, seg, x2, gates, gf, fg, tiles_per_batch):
    t, d = x2.shape
    tm = ROW_TILE
    nslot = TOP_K * tm
    grid_spec = pltpu.PrefetchScalarGridSpec(
        num_scalar_prefetch=4,
        grid=(t // tm,),
        in_specs=[pl.BlockSpec(memory_space=pl.ANY),
                  pl.BlockSpec((tm, d), lambda i, *_: (i, 0)),
                  pl.BlockSpec((tm, SUBLANES), lambda i, *_: (i, 0)),
                  pl.BlockSpec((1, 1, d), lambda i, *_: (i // tiles_per_batch, 0, 0)),
                  pl.BlockSpec((1, d), lambda i, *_: (0, 0))],
        out_specs=pl.BlockSpec((tm, d), lambda i, *_: (i, 0)),
        scratch_shapes=[pltpu.VMEM((2, nslot * SUBLANES, LANES), F32), pltpu.SemaphoreType.DMA((2,))],
    )
    return pl.pallas_call(
        _combine_kernel,
        grid_spec=grid_spec,
        out_shape=jax.ShapeDtypeStruct((t, d), F32),
        compiler_params=_cparams("arbitrary"),
    )(*seg, y, x2, gates, gf, fg)


def _final_norm_kernel(x_ref, g_ref, o_ref):
    x = x_ref[...]
    o_ref[...] = x * lax.rsqrt(jnp.mean(x * x, axis=-1, keepdims=True) + EPS) * g_ref[...]


def _final_norm(x2, g):
    t, d = x2.shape
    tm = ROW_TILE
    return pl.pallas_call(
        _final_norm_kernel,
        grid=(t // tm,),
        in_specs=[pl.BlockSpec((tm, d), lambda i: (i, 0)), pl.BlockSpec((1, d), lambda i: (0, 0))],
        out_specs=pl.BlockSpec((tm, d), lambda i: (i, 0)),
        out_shape=jax.ShapeDtypeStruct((t, d), F32),
        compiler_params=_cparams("parallel"),
    )(x2, g)


def _qk_column_perm():
    half = HEAD_DIM // 2
    lane = np.arange(LANES)
    pair = (lane % HEAD_DIM) // half
    dim = lane % half + half * (lane // HEAD_DIM)
    q = np.concatenate([(i + Q_TILES * pair) * HEAD_DIM + dim for i in range(Q_TILES)])
    k = ATTN_WIDTH + pair * HEAD_DIM + dim
    return q, k


def _attn_out_row_perm():
    lane = np.arange(LANES)
    return np.concatenate([(i + Q_TILES * (lane // HEAD_DIM)) * HEAD_DIM + lane % HEAD_DIM
                           for i in range(Q_TILES)])


def _pad_lanes(v):
    return jnp.pad(v.astype(F32), (0, LANES - v.shape[0])).reshape(1, LANES)


def _moe_route_tables(counts, base, rows, n_blocks):
    cnt = counts[0, :N_EXPERTS].astype(jnp.int32)
    padded = (cnt + rows - 1) // rows * rows
    pad_end = jnp.cumsum(padded)
    pad_start = pad_end - padded
    first = base[:, 0, :N_EXPERTS].astype(jnp.int32)
    run = jnp.concatenate([first[1:], cnt[None]], axis=0) - first
    off = jnp.cumsum(run, axis=1) - run
    flat = lambda a: a.reshape(-1).astype(jnp.int32)
    seg = (flat(off - first), flat(run), flat(off), flat(pad_start[None, :] + first))
    blk_start = jnp.arange(n_blocks, dtype=jnp.int32) * rows
    bexp = jnp.minimum(jnp.sum(blk_start[:, None] >= pad_end[None, :], axis=1), N_EXPERTS - 1).astype(jnp.int32)
    nvalid = (pad_end[-1:] // rows).astype(jnp.int32)
    row_end = pad_start + cnt
    brows = jnp.clip(row_end[bexp] - blk_start, 0, rows).astype(jnp.int32)
    piece = rows // MOE_SPLIT
    piece_end = (row_end + piece - 1) // piece * piece
    zero_spans = (row_end.astype(jnp.int32), piece_end.astype(jnp.int32), pad_end.astype(jnp.int32))
    return seg, bexp, nvalid, brows, zero_spans


def kernel(x, c, positions, ada_w, ada_b, norm_mix_g, norm_ffn_g, w_in, w_out, attn_sinks, conv_w, conv_b,
           dt_bias, a_log, d_skip, ssm_norm_g, ffn_w_gate, ffn_w_up, ffn_w_down, router_w, moe_w_gate,
           moe_w_up, moe_w_down, final_norm_g):
    batch, seq, d = x.shape
    depth = w_in.shape[0]
    t = batch * seq
    tiles_per_batch = seq // ROW_TILE
    x2 = x.reshape(t, d)

    c8 = jnp.pad(c, ((0, SUBLANES - batch), (0, 0)))
    mod = _ada_mod(c8, ada_w, ada_b)[:, :batch].reshape(depth, batch, 6, 1, d)
    cos, sin = _rope_tables(positions)

    q_perm, k_perm = _qk_column_perm()
    o_perm = _attn_out_row_perm()
    for l in range(depth):
        sh_m, sc_m, g_m, sh_f, sc_f, g_f = (mod[l, :, k] for k in range(6))
        wl = w_in[l]
        w_cat = jnp.concatenate(
            [wl[:, q_perm], wl[:, k_perm], wl[:, C_V:C_DT],
             jnp.pad(wl[:, C_DT:], ((0, 0), (0, LANES - SSM_HEADS)))], axis=1).astype(BF16)
        q, kv, z, xbc, dt = _inproj(x2, norm_mix_g[l].reshape(1, d), sc_m, sh_m, cos, sin, w_cat,
                                    _pad_lanes(dt_bias[l]), seq)
        attn = _attention(q, kv, attn_sinks[l].astype(F32), seq)
        ssm = _ssd(xbc, z, dt, conv_w[l], conv_b[l].reshape(1, CONV_CH), _pad_lanes(a_log[l]),
                   jnp.repeat(d_skip[l].astype(F32), SSM_HEAD_DIM).reshape(1, SSM_WIDTH),
                   ssm_norm_g[l].reshape(1, SSM_WIDTH), batch, seq)
        wo = jnp.concatenate([w_out[l][o_perm], w_out[l][ATTN_WIDTH:]], axis=0).astype(BF16)
        ffn_g = norm_ffn_g[l].reshape(1, d)
        if l % 2 == 0:
            x2 = _outproj_ffn(attn, ssm, x2, wo, g_m, ffn_g, sc_f, sh_f, ffn_w_gate[l // 2].astype(BF16),
                              ffn_w_up[l // 2].astype(BF16), ffn_w_down[l // 2].astype(BF16), g_f, tiles_per_batch)
            if l == depth - 1:
                x2 = _final_norm(x2, final_norm_g.reshape(1, d))
        else:
            if l != depth - 1:
                raise NotImplementedError("the expert layer fuses the final norm and must be last")
            rw = jnp.pad(router_w[l // 2].astype(F32), ((0, 0), (0, LANES - N_EXPERTS)))
            rw_hi = rw.astype(BF16)
            rw_lo = (rw - rw_hi.astype(F32)).astype(BF16)
            x_new, h, route, gates, counts, base = _outproj_router(
                attn, ssm, x2, wo, g_m, ffn_g, sc_f, sh_f, tiles_per_batch,
                jnp.concatenate([rw_hi, rw_lo], axis=1))
            rows = min(MOE_ROWS, t)
            n_blocks = -(-(t * TOP_K) // rows) + N_EXPERTS
            seg, bexp, nvalid, brows, zero_spans = _moe_route_tables(counts, base, rows, n_blocks)
            xb = _dispatch(h, route, seg, zero_spans, nvalid, n_blocks, rows)
            y = _moe_experts(xb, bexp, nvalid, brows, moe_w_gate[l // 2], moe_w_up[l // 2], moe_w_down[l // 2],
                             n_blocks, rows)
            x2 = _combine(y, seg, x_new, gates, g_f, final_norm_g.reshape(1, d), tiles_per_batch)
    return x2.reshape(batch, seq, d)
```

```python
import functools

import numpy as np
import jax
import jax.numpy as jnp
from jax import lax
from jax.experimental import pallas as pl
from jax.experimental.pallas import tpu as pltpu

F32 = jnp.float32
BF16 = jnp.bfloat16

LANES = 128
SUBLANES = 8
VMEM_LIMIT = 48 * 1024 * 1024
BIG_VMEM_LIMIT = 58 * 1024 * 1024

EPS = 1e-6
HEAD_DIM = 64
Q_HEADS = 8
KV_HEADS = 2
GROUP = Q_HEADS // KV_HEADS
ATTN_WIDTH = Q_HEADS * HEAD_DIM
KV_WIDTH = KV_HEADS * HEAD_DIM
WINDOW = 128
ROPE_THETA = 10000.0
SSM_HEADS = 8
SSM_HEAD_DIM = 64
SSM_WIDTH = SSM_HEADS * SSM_HEAD_DIM
SSM_GROUPS = 2
SSM_STATE = 128
GROUP_WIDTH = SSM_WIDTH // SSM_GROUPS
CONV_WIDTH = 4
CONV_CH = SSM_WIDTH + 2 * SSM_GROUPS * SSM_STATE
CHUNK = 128
N_EXPERTS = 8
TOP_K = 2

Q_TILES = ATTN_WIDTH // LANES
C_Q = 0
C_K = C_Q + ATTN_WIDTH
C_V = C_K + KV_WIDTH
C_Z = C_V + KV_WIDTH
C_X = C_Z + SSM_WIDTH
C_DT = C_X + CONV_CH
C_END = C_DT + LANES

ROW_TILE = 512
INPROJ_TILE = 1024
FFN_TILE = 1024
ATTN_TILE = 1024
MOE_ROWS = 1536
MOE_FT = 512
SSD_CHUNKS_PER_STEP = 8
ROW_SPLIT = 2
MOE_SPLIT = 3


def _cparams(*sem):
    return pltpu.CompilerParams(dimension_semantics=sem, vmem_limit_bytes=VMEM_LIMIT)


def _silu(v):
    return v * (1.0 / (1.0 + jnp.exp(-v)))


def _softplus(v):
    return jnp.maximum(v, 0.0) + jnp.log1p(jnp.exp(-jnp.abs(v)))


def _rms_mod(x, g, scale, shift):
    ms = jnp.mean(x * x, axis=-1, keepdims=True)
    return (x * lax.rsqrt(ms + EPS) * g) * (1.0 + scale) + shift


def _ada_kernel(c_ref, w_ref, b_ref, o_ref):
    c = c_ref[...]
    o_ref[0] = jnp.dot(_silu(c), w_ref[0], preferred_element_type=F32,
                       precision=lax.Precision.HIGHEST) + b_ref[0]


def _ada_mod(c8, ada_w, ada_b):
    depth, d, n = ada_w.shape
    tn = 1536
    return pl.pallas_call(
        _ada_kernel,
        grid=(depth, n // tn),
        in_specs=[pl.BlockSpec((SUBLANES, d), lambda l, j: (0, 0)),
                  pl.BlockSpec((1, d, tn), lambda l, j: (l, 0, j)),
                  pl.BlockSpec((1, 1, tn), lambda l, j: (l, 0, j))],
        out_specs=pl.BlockSpec((1, SUBLANES, tn), lambda l, j: (l, 0, j)),
        out_shape=jax.ShapeDtypeStruct((depth, SUBLANES, n), F32),
        compiler_params=_cparams("parallel", "parallel"),
    )(c8, ada_w, ada_b.reshape(depth, 1, n))


ROPE_PER_ROW = LANES // (HEAD_DIM // 2)


def _rope_kernel(pos_ref, inv_ref, sign_ref, cos_ref, sin_ref):
    ang = pos_ref[...].astype(F32) * inv_ref[...]
    tr = ang.shape[0]
    group = lax.broadcasted_iota(jnp.int32, (1, LANES), 1) // (HEAD_DIM // 2)
    for table, out_ref, sign in ((jnp.cos(ang), cos_ref, None), (jnp.sin(ang), sin_ref, sign_ref[...])):
        rolled = [table] + [pltpu.roll(table, (HEAD_DIM // 2) * s, axis=1) for s in range(1, ROPE_PER_ROW)]
        for j in range(ROPE_PER_ROW):
            rep = rolled[(0 - j) % ROPE_PER_ROW]
            for g in range(1, ROPE_PER_ROW):
                rep = jnp.where(group == g, rolled[(g - j) % ROPE_PER_ROW], rep)
            out_ref[pl.ds(j, tr, stride=ROPE_PER_ROW), :] = rep if sign is None else rep * sign


def _rope_tables(positions):
    t = positions.size
    half = HEAD_DIM // 2
    pos_dense = jnp.repeat(positions.reshape(t // ROPE_PER_ROW, ROPE_PER_ROW), half, axis=1)
    inv_freq = ROPE_THETA ** (-jnp.arange(0, HEAD_DIM, 2, dtype=F32) / HEAD_DIM)
    inv_dense = jnp.tile(inv_freq, ROPE_PER_ROW).reshape(1, LANES)
    sign = jnp.repeat(jnp.array([-1.0, -1.0, 1.0, 1.0], F32), half).reshape(1, LANES)
    rows = t // ROPE_PER_ROW
    tr = min(1024, rows)
    return pl.pallas_call(
        _rope_kernel,
        grid=(rows // tr,),
        in_specs=[pl.BlockSpec((tr, LANES), lambda i: (i, 0)),
                  pl.BlockSpec((1, LANES), lambda i: (0, 0)),
                  pl.BlockSpec((1, LANES), lambda i: (0, 0))],
        out_specs=[pl.BlockSpec((tr * ROPE_PER_ROW, LANES), lambda i: (i, 0))] * 2,
        out_shape=[jax.ShapeDtypeStruct((t, LANES), F32)] * 2,
        compiler_params=_cparams("parallel"),
    )(pos_dense, inv_dense, sign)


def _inproj_kernel(x_ref, g_ref, sc_ref, sh_ref, cos_ref, sin_ref, w_ref, dtb_ref,
                   q_ref, kv_ref, z_ref, xbc_ref, dt_ref):
    h = _rms_mod(x_ref[...], g_ref[...], sc_ref[0], sh_ref[0]).astype(BF16)
    cos = cos_ref[...]
    sin = sin_ref[...]

    def rope(t):
        return t * cos + pltpu.roll(t, LANES // 2, axis=1) * sin

    qkv = jnp.dot(h, w_ref[:, C_Q:C_Z], preferred_element_type=F32)
    for i in range(Q_TILES):
        q_ref[:, i * LANES:(i + 1) * LANES] = (
            rope(qkv[:, i * LANES:(i + 1) * LANES]) * (HEAD_DIM ** -0.5)).astype(BF16)
    kv_ref[:, 0:LANES] = rope(qkv[:, C_K:C_V]).astype(BF16)
    kv_ref[:, LANES:2 * LANES] = qkv[:, C_V:C_Z].astype(BF16)
    z_ref[...] = jnp.dot(h, w_ref[:, C_Z:C_X], preferred_element_type=F32)
    xbc_ref[...] = jnp.dot(h, w_ref[:, C_X:C_DT], preferred_element_type=F32)
    dt_raw = jnp.dot(h, w_ref[:, C_DT:C_END], preferred_element_type=F32)
    dt_ref[...] = _softplus(dt_raw + dtb_ref[...])


def _inproj(x2, g, sc, sh, cos, sin, w, dtb, seq):
    t, d = x2.shape
    tm = min(INPROJ_TILE, seq)
    row = lambda i: (i, 0)
    const = lambda i: (0, 0)
    per_b = lambda i: (i // (seq // tm), 0, 0)
    return pl.pallas_call(
        _inproj_kernel,
        grid=(t // tm,),
        in_specs=[pl.BlockSpec((tm, d), row),
                  pl.BlockSpec((1, d), const),
                  pl.BlockSpec((1, 1, d), per_b),
                  pl.BlockSpec((1, 1, d), per_b),
                  pl.BlockSpec((tm, LANES), row),
                  pl.BlockSpec((tm, LANES), row),
                  pl.BlockSpec((d, C_END), const),
                  pl.BlockSpec((1, LANES), const)],
        out_specs=[pl.BlockSpec((tm, ATTN_WIDTH), row),
                   pl.BlockSpec((tm, 2 * KV_WIDTH), row),
                   pl.BlockSpec((tm, SSM_WIDTH), row),
                   pl.BlockSpec((tm, CONV_CH), row),
                   pl.BlockSpec((tm, LANES), row)],
        out_shape=[jax.ShapeDtypeStruct((t, ATTN_WIDTH), BF16),
                   jax.ShapeDtypeStruct((t, 2 * KV_WIDTH), BF16),
                   jax.ShapeDtypeStruct((t, SSM_WIDTH), F32),
                   jax.ShapeDtypeStruct((t, CONV_CH), F32),
                   jax.ShapeDtypeStruct((t, LANES), F32)],
        compiler_params=_cparams("parallel"),
    )(x2, g, sc, sh, cos, sin, w, dtb)


def _attn_kernel(sink_ref, q_ref, kv_ref, kvp_ref, o_ref, *, tiles_per_seq):
    first = (pl.program_id(0) % tiles_per_seq) == 0
    blk = WINDOW
    nsub = ATTN_TILE // blk
    lane = lax.broadcasted_iota(jnp.int32, (1, LANES), 1)
    k_lo_mask = (lane % HEAD_DIM) < (HEAD_DIM // 2)
    v_lo_mask = lane < HEAD_DIM
    zero = jnp.zeros((), BF16)

    k_all = jnp.concatenate([kvp_ref[:, 0:LANES], kv_ref[:, 0:LANES]], axis=0)
    v_all = jnp.concatenate([kvp_ref[:, LANES:2 * LANES], kv_ref[:, LANES:2 * LANES]], axis=0)
    k_sel = (jnp.where(k_lo_mask, k_all, zero), jnp.where(k_lo_mask, zero, k_all))
    v_sel = (jnp.where(v_lo_mask, v_all, zero), jnp.where(v_lo_mask, zero, v_all))

    qi = lax.broadcasted_iota(jnp.int32, (blk, blk), 0)
    col = lax.broadcasted_iota(jnp.int32, (blk, blk), 1)
    cur = col <= qi
    no_prev = (col > qi) & (col > jnp.where(first, -1, blk))

    for n in range(nsub):
        q_st = jnp.concatenate([q_ref[n * blk:(n + 1) * blk, i * LANES:(i + 1) * LANES]
                                for i in range(Q_TILES)], axis=0)
        out = None
        for hk in range(KV_HEADS):
            k_n = k_sel[hk][n * blk:(n + 2) * blk]
            v_n = v_sel[hk][n * blk:(n + 2) * blk]
            s = lax.dot_general(q_st, k_n, (((1,), (1,)), ((), ())), preferred_element_type=F32)
            probs, scales = [], []
            for i in range(GROUP):
                s_i = s[i * blk:(i + 1) * blk]
                sc = jnp.where(cur, s_i[:, blk:], s_i[:, :blk])
                if n == 0:
                    sc = jnp.where(no_prev, -jnp.inf, sc)
                sink = sink_ref[hk * GROUP + i]
                m = jnp.maximum(jnp.max(sc, axis=-1, keepdims=True), sink)
                p = jnp.exp(sc - m)
                scales.append(1.0 / (jnp.sum(p, axis=-1, keepdims=True) + jnp.exp(sink - m)))
                probs.append(jnp.concatenate([jnp.where(cur, 0.0, p), jnp.where(cur, p, 0.0)],
                                             axis=1).astype(BF16))
            o = jnp.dot(jnp.concatenate(probs, axis=0), v_n, preferred_element_type=F32)
            o = o * jnp.concatenate(scales, axis=0)
            out = o if out is None else out + o
        for i in range(Q_TILES):
            o_ref[n * blk:(n + 1) * blk, i * LANES:(i + 1) * LANES] = out[i * blk:(i + 1) * blk].astype(BF16)


def _attention(q, kv, sinks, seq):
    t = q.shape[0]
    tq = ATTN_TILE
    r = tq // WINDOW
    kern = functools.partial(_attn_kernel, tiles_per_seq=seq // tq)
    return pl.pallas_call(
        kern,
        grid=(t // tq,),
        in_specs=[pl.BlockSpec(memory_space=pltpu.SMEM),
                  pl.BlockSpec((tq, ATTN_WIDTH), lambda i: (i, 0)),
                  pl.BlockSpec((tq, 2 * KV_WIDTH), lambda i: (i, 0)),
                  pl.BlockSpec((WINDOW, 2 * KV_WIDTH), lambda i: (jnp.maximum(i * r - 1, 0), 0))],
        out_specs=pl.BlockSpec((tq, ATTN_WIDTH), lambda i: (i, 0)),
        out_shape=jax.ShapeDtypeStruct((t, ATTN_WIDTH), BF16),
        compiler_params=_cparams("parallel"),
    )(sinks, q, kv, kv)


def _split3(v):
    p0 = v.astype(BF16)
    r1 = v - p0.astype(F32)
    p1 = r1.astype(BF16)
    return p0, p1, (r1 - p1.astype(F32)).astype(BF16)


def _dot3(lhs_exact_bf16, v):
    return sum(jnp.dot(lhs_exact_bf16, p, preferred_element_type=F32) for p in _split3(v))


def _ssd_kernel(xbc_ref, z_ref, dt_ref, cw_ref, cb_ref, alog_ref, dskip_ref, ng_ref,
                o_ref, ext_ref, st_ref):
    L = CHUNK
    rows = xbc_ref.shape[0]
    halo = SUBLANES

    @pl.when(pl.program_id(1) == 0)
    def _():
        ext_ref[0:halo, :] = jnp.zeros((halo, CONV_CH), F32)
        st_ref[...] = jnp.zeros(st_ref.shape, F32)

    ext_ref[halo:halo + rows, :] = xbc_ref[...]
    acc = cb_ref[...] + cw_ref[CONV_WIDTH - 1:CONV_WIDTH, :] * ext_ref[halo:halo + rows, :]
    for k in range(CONV_WIDTH - 1):
        off = halo - (CONV_WIDTH - 1) + k
        acc = acc + cw_ref[k:k + 1, :] * ext_ref[off:off + rows, :]
    ext_ref[0:halo, :] = ext_ref[rows:rows + halo, :]
    u_all = _silu(acc)

    lane = lax.broadcasted_iota(jnp.int32, (1, LANES), 1)
    a = jnp.where(lane < SSM_HEADS, -jnp.exp(alog_ref[...]), 0.0)
    row = lax.broadcasted_iota(jnp.int32, (L, L), 0)
    col = lax.broadcasted_iota(jnp.int32, (L, L), 1)
    causal = row >= col
    tri = jnp.where(causal, 1.0, 0.0).astype(BF16)
    spread = jnp.where(lax.broadcasted_iota(jnp.int32, (LANES, SSM_WIDTH), 1) // SSM_HEAD_DIM
                       == lax.broadcasted_iota(jnp.int32, (LANES, SSM_WIDTH), 0), 1.0, 0.0).astype(BF16)
    r_heads = SSM_HEADS // SSM_GROUPS
    glane = lax.broadcasted_iota(jnp.int32, (1, GROUP_WIDTH), 1) // SSM_HEAD_DIM
    zero = jnp.zeros((), BF16)
    states = [st_ref[g] for g in range(SSM_GROUPS)]

    for c in range(rows // L):
        rs = slice(c * L, (c + 1) * L)
        u = u_all[rs]
        xs = u[:, 0:SSM_WIDTH]
        bm = u[:, SSM_WIDTH:SSM_WIDTH + SSM_GROUPS * SSM_STATE]
        cm = u[:, SSM_WIDTH + SSM_GROUPS * SSM_STATE:]
        dt = dt_ref[rs, :]
        acs = _dot3(tri, dt * a)
        acs_t = acs.T
        dt_e = sum(jnp.dot(p, spread, preferred_element_type=F32) for p in _split3(dt))
        acs_e = sum(jnp.dot(p, spread, preferred_element_type=F32) for p in _split3(acs))
        last = acs_e[L - 1:L, :]
        xd = xs * dt_e
        xd_b = xd.astype(BF16)
        xdw_b = (xd * jnp.exp(last - acs_e)).astype(BF16)
        e_acs = jnp.exp(acs_e)
        c_dec = jnp.exp(last)

        ys = []
        for g in range(SSM_GROUPS):
            gs = slice(g * GROUP_WIDTH, (g + 1) * GROUP_WIDTH)
            b_g = bm[:, g * SSM_STATE:(g + 1) * SSM_STATE]
            c_b = cm[:, g * SSM_STATE:(g + 1) * SSM_STATE].astype(BF16)
            cb = lax.dot_general(c_b, b_g.astype(BF16), (((1,), (1,)), ((), ())),
                                 preferred_element_type=F32)
            st = states[g]
            y_g = jnp.dot(c_b, st.astype(BF16), preferred_element_type=F32) * e_acs[:, gs]
            xd_g = xd_b[:, gs]
            for r in range(r_heads):
                h = g * r_heads + r
                seg = acs[:, h:h + 1] - acs_t[h:h + 1, :]
                m_h = (cb * jnp.exp(jnp.where(causal, seg, -jnp.inf))).astype(BF16)
                y_g = y_g + jnp.dot(m_h, jnp.where(glane == r, xd_g, zero), preferred_element_type=F32)
            new = jnp.dot(b_g.T.astype(BF16), xdw_b[:, gs], preferred_element_type=F32)
            states[g] = c_dec[:, gs] * st + new
            ys.append(y_g)

        y = jnp.concatenate(ys, axis=1) + dskip_ref[...] * xs
        y = y * _silu(z_ref[rs, :])
        outs = []
        for g in range(SSM_GROUPS):
            yg = y[:, g * GROUP_WIDTH:(g + 1) * GROUP_WIDTH]
            outs.append(yg * lax.rsqrt(jnp.mean(yg * yg, axis=-1, keepdims=True) + EPS))
        o_ref[rs, :] = (jnp.concatenate(outs, axis=1) * ng_ref[...]).astype(BF16)

    for g in range(SSM_GROUPS):
        st_ref[g] = states[g]


def _ssd(xbc, z, dt, cw, cb, alog, dskip, ng, batch, seq):
    t = xbc.shape[0]
    rows = SSD_CHUNKS_PER_STEP * CHUNK
    ns = seq // rows
    row = lambda b, c: (b * ns + c, 0)
    const = lambda b, c: (0, 0)
    return pl.pallas_call(
        _ssd_kernel,
        grid=(batch, ns),
        in_specs=[pl.BlockSpec((rows, CONV_CH), row),
                  pl.BlockSpec((rows, SSM_WIDTH), row),
                  pl.BlockSpec((rows, LANES), row),
                  pl.BlockSpec((CONV_WIDTH, CONV_CH), const),
                  pl.BlockSpec((1, CONV_CH), const),
                  pl.BlockSpec((1, LANES), const),
                  pl.BlockSpec((1, SSM_WIDTH), const),
                  pl.BlockSpec((1, SSM_WIDTH), const)],
        out_specs=pl.BlockSpec((rows, SSM_WIDTH), row),
        out_shape=jax.ShapeDtypeStruct((t, SSM_WIDTH), BF16),
        scratch_shapes=[pltpu.VMEM((SUBLANES + rows, CONV_CH), F32),
                        pltpu.VMEM((SSM_GROUPS, SSM_STATE, GROUP_WIDTH), F32)],
        compiler_params=_cparams("parallel", "arbitrary"),
    )(xbc, z, dt, cw, cb, alog, dskip, ng)


def _outproj_core(a_ref, s_ref, x_ref, w_ref, gm_ref, g_ref, sc_ref, sh_ref):
    half = a_ref.shape[1]
    mixed = (jnp.dot(a_ref[...], w_ref[0:half, :], preferred_element_type=F32)
             + jnp.dot(s_ref[...], w_ref[half:, :], preferred_element_type=F32))
    x_new = x_ref[...] + gm_ref[0] * mixed
    return x_new, _rms_mod(x_new, g_ref[...], sc_ref[0], sh_ref[0])


def _outproj_router_kernel(a_ref, s_ref, x_ref, w_ref, gm_ref, g_ref, sc_ref, sh_ref, rw_ref,
                           xo_ref, h_ref, route_ref, gate_ref, cnt_ref, base_ref, carry_ref):
    x_new, h = _outproj_core(a_ref, s_ref, x_ref, w_ref, gm_ref, g_ref, sc_ref, sh_ref)
    xo_ref[...] = x_new
    tm = h.shape[0]
    h_ref[...] = h.astype(BF16)

    h_hi = h.astype(BF16)
    h_lo = (h - h_hi.astype(F32)).astype(BF16)
    hi_part = jnp.dot(h_hi, rw_ref[...], preferred_element_type=F32)
    logits = (hi_part[:, 0:LANES] + hi_part[:, LANES:]
              + jnp.dot(h_lo, rw_ref[:, 0:LANES], preferred_element_type=F32))
    lane = lax.broadcasted_iota(jnp.int32, (tm, LANES), 1).astype(F32)
    logits = jnp.where(lane < N_EXPERTS, logits, -jnp.inf)
    m0 = jnp.max(logits, axis=-1, keepdims=True)
    i0 = jnp.min(jnp.where(logits == m0, lane, float(LANES)), axis=-1, keepdims=True)
    rest = jnp.where(lane == i0, -jnp.inf, logits)
    m1 = jnp.max(rest, axis=-1, keepdims=True)
    i1 = jnp.min(jnp.where(rest == m1, lane, float(LANES)), axis=-1, keepdims=True)
    e = jnp.exp(m1 - m0)
    g0 = 1.0 / (1.0 + e)
    g1 = e / (1.0 + e)

    @pl.when(pl.program_id(0) == 0)
    def _():
        carry_ref[...] = jnp.zeros(carry_ref.shape, F32)

    sel0 = lane == i0
    sel1 = lane == i1
    member = jnp.where(sel0 | sel1, 1.0, 0.0)
    r_i = lax.broadcasted_iota(jnp.int32, (tm, tm), 0)
    c_i = lax.broadcasted_iota(jnp.int32, (tm, tm), 1)
    strict = jnp.where(r_i > c_i, 1.0, 0.0).astype(BF16)
    base_ref[0] = carry_ref[...]
    rank = carry_ref[...] + jnp.dot(strict, member.astype(BF16), preferred_element_type=F32)
    r0 = jnp.sum(jnp.where(sel0, rank, 0.0), axis=-1, keepdims=True)
    r1 = jnp.sum(jnp.where(sel1, rank, 0.0), axis=-1, keepdims=True)
    carry_ref[...] = carry_ref[...] + jnp.sum(member, axis=0, keepdims=True)
    cnt_ref[...] = carry_ref[...]

    cols = r1
    for k, colv in enumerate((g0, g1, i0, i1, r0)):
        cols = jnp.where(lane == float(k), colv, cols)
    gate_ref[...] = cols[:, 0:SUBLANES]
    diag = (lax.broadcasted_iota(jnp.int32, (tm, LANES), 0) % LANES
            == lax.broadcasted_iota(jnp.int32, (tm, LANES), 1))
    nrow = tm // LANES
    for q, colv in enumerate((i0, i1, r0, r1)):
        spread = jnp.where(diag, colv, 0.0)
        rows = [jnp.sum(spread[b * LANES:(b + 1) * LANES], axis=0, keepdims=True) for b in range(nrow)]
        route_ref[0, q * nrow:(q + 1) * nrow, :] = jnp.concatenate(rows, axis=0).astype(jnp.int32)


def _outproj_router(attn, ssm, x2, w, gm, g, sc, sh, tiles_per_batch, router):
    t, d = x2.shape
    tm = ROW_TILE
    row = lambda i: (i, 0)
    const = lambda i: (0, 0)
    per_b = lambda i: (i // tiles_per_batch, 0, 0)
    half = attn.shape[1]
    return pl.pallas_call(
        _outproj_router_kernel,
        grid=(t // tm,),
        in_specs=[pl.BlockSpec((tm, half), row),
                  pl.BlockSpec((tm, half), row),
                  pl.BlockSpec((tm, d), row),
                  pl.BlockSpec((2 * half, d), const),
                  pl.BlockSpec((1, 1, d), per_b),
                  pl.BlockSpec((1, d), const),
                  pl.BlockSpec((1, 1, d), per_b),
                  pl.BlockSpec((1, 1, d), per_b),
                  pl.BlockSpec((d, 2 * LANES), const)],
        out_specs=[pl.BlockSpec((tm, d), row),
                   pl.BlockSpec((tm, d), row),
                   pl.BlockSpec((1, 4 * tm // LANES, LANES), lambda i: (i, 0, 0)),
                   pl.BlockSpec((tm, SUBLANES), row),
                   pl.BlockSpec((1, LANES), const),
                   pl.BlockSpec((1, 1, LANES), lambda i: (i, 0, 0))],
        out_shape=[jax.ShapeDtypeStruct((t, d), F32),
                   jax.ShapeDtypeStruct((t, d), BF16),
                   jax.ShapeDtypeStruct((t // tm, 4 * tm // LANES, LANES), jnp.int32),
                   jax.ShapeDtypeStruct((t, SUBLANES), F32),
                   jax.ShapeDtypeStruct((1, LANES), F32),
                   jax.ShapeDtypeStruct((t // tm, 1, LANES), F32)],
        scratch_shapes=[pltpu.VMEM((1, LANES), F32)],
        compiler_params=_cparams("arbitrary"),
    )(attn, ssm, x2, w, gm, g, sc, sh, router)


def _swiglu_rows(x, wg, wu, wd):
    gate = jnp.dot(x, wg, preferred_element_type=F32)
    up = jnp.dot(x, wu, preferred_element_type=F32)
    return jnp.dot((_silu(gate) * up).astype(BF16), wd, preferred_element_type=F32)


def _outproj_ffn_kernel(a_ref, s_ref, x_ref, w_ref, gm_ref, g_ref, sc_ref, sh_ref, wg_ref, wu_ref, wd_ref, gf_ref,
                        o_ref):
    half = a_ref.shape[1]
    piece = x_ref.shape[0] // ROW_SPLIT
    for r in range(ROW_SPLIT):
        rs = slice(r * piece, (r + 1) * piece)
        mixed = (jnp.dot(a_ref[rs, :], w_ref[0:half, :], preferred_element_type=F32)
                 + jnp.dot(s_ref[rs, :], w_ref[half:, :], preferred_element_type=F32))
        x_new = x_ref[rs, :] + gm_ref[0] * mixed
        h = _rms_mod(x_new, g_ref[...], sc_ref[0], sh_ref[0]).astype(BF16)
        o_ref[rs, :] = x_new + gf_ref[0] * _swiglu_rows(h, wg_ref[...], wu_ref[...], wd_ref[...])


def _outproj_ffn(attn, ssm, x2, w, gm, g, sc, sh, wg, wu, wd, gf, seq):
    t, d = x2.shape
    f = wg.shape[1]
    tm = min(FFN_TILE, seq)
    half = attn.shape[1]
    row = lambda i: (i, 0)
    const = lambda i: (0, 0)
    per_b = lambda i: (i // (seq // tm), 0, 0)
    resident = pl.Buffered(1)
    return pl.pallas_call(
        _outproj_ffn_kernel,
        grid=(t // tm,),
        in_specs=[pl.BlockSpec((tm, half), row),
                  pl.BlockSpec((tm, half), row),
                  pl.BlockSpec((tm, d), row),
                  pl.BlockSpec((2 * half, d), const, pipeline_mode=resident),
                  pl.BlockSpec((1, 1, d), per_b),
                  pl.BlockSpec((1, d), const),
                  pl.BlockSpec((1, 1, d), per_b),
                  pl.BlockSpec((1, 1, d), per_b),
                  pl.BlockSpec((d, f), const, pipeline_mode=resident),
                  pl.BlockSpec((d, f), const, pipeline_mode=resident),
                  pl.BlockSpec((f, d), const, pipeline_mode=resident),
                  pl.BlockSpec((1, 1, d), per_b)],
        out_specs=pl.BlockSpec((tm, d), row),
        out_shape=jax.ShapeDtypeStruct((t, d), F32),
        compiler_params=pltpu.CompilerParams(dimension_semantics=("parallel",), vmem_limit_bytes=BIG_VMEM_LIMIT),
    )(attn, ssm, x2, w, gm, g, sc, sh, wg, wu, wd, gf)


def _segment_copies(count, src, src_row, dst, dst_row, sem, max_rows):
    for k in range(max_rows.bit_length() - 1, -1, -1):
        size = (1 << k) * SUBLANES

        @pl.when(((count >> k) & 1) == 1)
        def _(k=k, size=size):
            done = (count >> (k + 1)) << (k + 1)
            pltpu.make_async_copy(
                src.at[pl.ds(pl.multiple_of((src_row + done) * SUBLANES, SUBLANES), size)],
                dst.at[pl.ds(pl.multiple_of((dst_row + done) * SUBLANES, SUBLANES), size)], sem).start()


def _lane_rows(route_ref, q, nrow):
    return jnp.concatenate([route_ref[0, q * nrow + b:q * nrow + b + 1, :] for b in range(nrow)], axis=1)


def _dispatch_kernel(delta_ref, cnt_ref, off_ref, dst_ref, zs_ref, zm_ref, ze_ref, nvalid_ref,
                     route_ref, h_ref, xb_hbm, g_ref, zero_ref, sem, zsem):
    i = pl.program_id(0)
    n = pl.num_programs(0)
    tm = h_ref.shape[0]
    nslot = TOP_K * tm
    slot = i % 2
    blk_rows = zero_ref.shape[0]
    n_blocks = xb_hbm.shape[0] // blk_rows

    def zero_row(r):
        return pltpu.make_async_copy(
            zero_ref.at[pl.ds(0, SUBLANES)],
            xb_hbm.at[pl.ds(pl.multiple_of(r * SUBLANES, SUBLANES), SUBLANES)], zsem)

    piece_rows = blk_rows // MOE_SPLIT

    def zero_piece(p):
        return pltpu.make_async_copy(
            zero_ref.at[pl.ds(0, piece_rows)],
            xb_hbm.at[pl.ds(pl.multiple_of(p * piece_rows, piece_rows), piece_rows)], zsem)

    def zero_block(b):
        return pltpu.make_async_copy(
            zero_ref, xb_hbm.at[pl.ds(pl.multiple_of(b * blk_rows, blk_rows), blk_rows)], zsem)

    @pl.when(i == 0)
    def _():
        zero_ref[...] = jnp.zeros(zero_ref.shape, F32)
        for start in (True, False):
            for e in range(N_EXPERTS):
                def rows_body(r, c):
                    zero_row(r).start() if start else zero_row(r).wait()
                    return c
                lax.fori_loop(zs_ref[e], zm_ref[e], rows_body, 0)

                def pieces_body(p, c):
                    zero_piece(p).start() if start else zero_piece(p).wait()
                    return c
                lax.fori_loop(zm_ref[e] // (piece_rows // SUBLANES), ze_ref[e] // (piece_rows // SUBLANES),
                              pieces_body, 0)

            def blocks_body(b, c):
                zero_block(b).start() if start else zero_block(b).wait()
                return c
            lax.fori_loop(nvalid_ref[0], n_blocks, blocks_body, 0)

    def wait_slot(sl):
        pltpu.make_async_copy(g_ref.at[sl], xb_hbm.at[pl.ds(0, nslot * SUBLANES)], sem.at[sl]).wait()

    @pl.when(i >= 2)
    def _():
        wait_slot(slot)

    nrow = tm // LANES
    experts = (_lane_rows(route_ref, 0, nrow), _lane_rows(route_ref, 1, nrow))
    ranks = (_lane_rows(route_ref, 2, nrow), _lane_rows(route_ref, 3, nrow))
    row_id = lax.broadcasted_iota(jnp.int32, (nslot, tm), 0)
    hit = None
    for k in range(TOP_K):
        delta = jnp.zeros((1, tm), jnp.int32)
        for e in range(N_EXPERTS):
            delta = jnp.where(experts[k] == e, delta_ref[i * N_EXPERTS + e], delta)
        mine = row_id == delta + ranks[k]
        hit = mine if hit is None else hit | mine
    grouped = jnp.dot(jnp.where(hit, 1.0, 0.0).astype(BF16), h_ref[...], preferred_element_type=F32)
    for s in range(grouped.shape[1] // LANES):
        g_ref[slot, pl.ds(s, nslot, stride=SUBLANES), :] = grouped[:, s * LANES:(s + 1) * LANES]
    for e in range(N_EXPERTS):
        _segment_copies(cnt_ref[i * N_EXPERTS + e], g_ref.at[slot], off_ref[i * N_EXPERTS + e],
                        xb_hbm, dst_ref[i * N_EXPERTS + e], sem.at[slot], tm)

    @pl.when(i == n - 1)
    def _():
        wait_slot(slot)

    @pl.when((i == n - 1) & (n >= 2))
    def _():
        wait_slot(1 - slot)


def _dispatch(h, route, seg, zero_spans, nvalid, n_blocks, rows):
    t, d = h.shape
    tm = ROW_TILE
    nslot = TOP_K * tm
    nrow = tm // LANES
    grid_spec = pltpu.PrefetchScalarGridSpec(
        num_scalar_prefetch=8,
        grid=(t // tm,),
        in_specs=[pl.BlockSpec((1, 4 * nrow, LANES), lambda i, *_: (i, 0, 0)),
                  pl.BlockSpec((tm, d), lambda i, *_: (i, 0))],
        out_specs=pl.BlockSpec(memory_space=pl.ANY),
        scratch_shapes=[pltpu.VMEM((2, nslot * SUBLANES, LANES), F32),
                        pltpu.VMEM((rows * SUBLANES, LANES), F32),
                        pltpu.SemaphoreType.DMA((2,)), pltpu.SemaphoreType.DMA(())],
    )
    return pl.pallas_call(
        _dispatch_kernel,
        grid_spec=grid_spec,
        out_shape=jax.ShapeDtypeStruct((n_blocks * rows * SUBLANES, LANES), F32),
        compiler_params=pltpu.CompilerParams(dimension_semantics=("arbitrary",), has_side_effects=True,
                                             vmem_limit_bytes=VMEM_LIMIT),
    )(*seg, *zero_spans, nvalid, route, h)


def _moe_kernel(bexp_ref, nvalid_ref, brows_ref, xb_ref, wg_ref, wu_ref, wd_ref, y_ref, x_scr, acc_ref):
    b = pl.program_id(0)
    j = pl.program_id(1)
    nf = pl.num_programs(1)
    rows = x_scr.shape[0]
    nsl = x_scr.shape[1] // LANES
    valid = b < nvalid_ref[0]
    piece = rows // MOE_SPLIT
    used = (brows_ref[b] + piece - 1) // piece

    def tile_rows(r, s):
        return pl.ds(r * piece * SUBLANES + s, piece, stride=SUBLANES)

    def body(first, last, n_used):
        wg = wg_ref[0].astype(BF16)
        wu = wu_ref[0].astype(BF16)
        wd = wd_ref[0].astype(BF16)
        for r in range(n_used):
            rs = slice(r * piece, (r + 1) * piece)
            if first:
                for s in range(nsl):
                    x_scr[rs, s * LANES:(s + 1) * LANES] = xb_ref[tile_rows(r, s), :].astype(BF16)
            part = _swiglu_rows(x_scr[rs, :], wg, wu, wd)
            total = part if first else acc_ref[rs, :] + part
            if last:
                for s in range(nsl):
                    y_ref[tile_rows(r, s), :] = total[:, s * LANES:(s + 1) * LANES]
            else:
                acc_ref[rs, :] = total
        if last:
            for r in range(n_used, MOE_SPLIT):
                for s in range(nsl):
                    y_ref[tile_rows(r, s), :] = jnp.zeros((piece, LANES), F32)

    for first, last, when_j in ((True, False, j == 0), (False, False, (j > 0) & (j < nf - 1)),
                                (False, True, j == nf - 1)):
        for n_used in range(1, MOE_SPLIT + 1):
            pl.when(valid & when_j & (used == n_used))(functools.partial(body, first, last, n_used))

    @pl.when(jnp.logical_not(valid) & (j == nf - 1))
    def _():
        y_ref[...] = jnp.zeros(y_ref.shape, F32)


def _moe_experts(xb, bexp, nvalid, brows, wg, wu, wd, n_blocks, rows):
    d = wg.shape[1]
    f = wg.shape[2]
    nf = f // MOE_FT
    assert nf >= 2

    def blk(b, j, bexp, nvalid, brows):
        return (jnp.minimum(b, nvalid[0] - 1), 0)

    def fidx(b, j, nvalid):
        return jnp.where(b < nvalid[0], j, nf - 1)

    grid_spec = pltpu.PrefetchScalarGridSpec(
        num_scalar_prefetch=3,
        grid=(n_blocks, nf),
        in_specs=[pl.BlockSpec((rows * SUBLANES, LANES), blk),
                  pl.BlockSpec((1, d, MOE_FT), lambda b, j, bexp, nvalid, brows: (bexp[b], 0, fidx(b, j, nvalid))),
                  pl.BlockSpec((1, d, MOE_FT), lambda b, j, bexp, nvalid, brows: (bexp[b], 0, fidx(b, j, nvalid))),
                  pl.BlockSpec((1, MOE_FT, d), lambda b, j, bexp, nvalid, brows: (bexp[b], fidx(b, j, nvalid), 0))],
        out_specs=pl.BlockSpec((rows * SUBLANES, LANES), lambda b, j, bexp, nvalid, brows: (b, 0)),
        scratch_shapes=[pltpu.VMEM((rows, d), BF16), pltpu.VMEM((rows, d), F32)],
    )
    return pl.pallas_call(
        _moe_kernel,
        grid_spec=grid_spec,
        out_shape=jax.ShapeDtypeStruct(xb.shape, F32),
        compiler_params=pltpu.CompilerParams(dimension_semantics=("arbitrary", "arbitrary"),
                                             vmem_limit_bytes=BIG_VMEM_LIMIT),
    )(bexp, nvalid, brows, xb, wg, wu, wd)


def _combine_kernel(delta_ref, cnt_ref, off_ref, dst_ref, y_hbm, x_ref, gate_ref, gf_ref, fg_ref, o_ref,
                    buf_ref, sem):
    i = pl.program_id(0)
    tm = x_ref.shape[0]
    nslot = TOP_K * tm
    nsl = x_ref.shape[1] // LANES
    slot = i % 2

    def fetch(tile, to_slot):
        for e in range(N_EXPERTS):
            _segment_copies(cnt_ref[tile * N_EXPERTS + e], y_hbm, dst_ref[tile * N_EXPERTS + e],
                            buf_ref.at[to_slot], off_ref[tile * N_EXPERTS + e], sem.at[to_slot], tm)

    @pl.when(i == 0)
    def _():
        fetch(0, 0)

    @pl.when(i + 1 < pl.num_programs(0))
    def _():
        fetch(i + 1, 1 - slot)

    pltpu.make_async_copy(y_hbm.at[pl.ds(0, nslot * SUBLANES)], buf_ref.at[slot], sem.at[slot]).wait()

    y_rows = jnp.concatenate([buf_ref[slot, pl.ds(s, nslot, stride=SUBLANES), :] for s in range(nsl)],
                             axis=1).astype(BF16)
    cols = gate_ref[...]
    lane = lax.broadcasted_iota(jnp.int32, (tm, nslot), 1).astype(F32)
    weights = None
    for k in range(TOP_K):
        expert = cols[:, 2 + k:3 + k]
        delta = jnp.zeros((tm, 1), F32)
        for e in range(N_EXPERTS):
            delta = jnp.where(expert == float(e), delta_ref[i * N_EXPERTS + e].astype(F32), delta)
        w_k = jnp.where(lane == delta + cols[:, 4 + k:5 + k], cols[:, k:k + 1], 0.0)
        weights = w_k if weights is None else weights + w_k
    f = jnp.dot(weights.astype(BF16), y_rows, preferred_element_type=F32)
    x = x_ref[...] + gf_ref[0] * f
    ms = jnp.mean(x * x, axis=-1, keepdims=True)
    o_ref[...] = x * lax.rsqrt(ms + EPS) * fg_ref[...]


def _combine(y, seg, x2, gates, gf, fg, tiles_per_batch):
    t, d = x2.shape
    tm = ROW_TILE
    nslot = TOP_K * tm
    grid_spec = pltpu.PrefetchScalarGridSpec(
        num_scalar_prefetch=4,
        grid=(t // tm,),
        in_specs=[pl.BlockSpec(memory_space=pl.ANY),
                  pl.BlockSpec((tm, d), lambda i, *_: (i, 0)),
                  pl.BlockSpec((tm, SUBLANES), lambda i, *_: (i, 0)),
                  pl.BlockSpec((1, 1, d), lambda i, *_: (i // tiles_per_batch, 0, 0)),
                  pl.BlockSpec((1, d), lambda i, *_: (0, 0))],
        out_specs=pl.BlockSpec((tm, d), lambda i, *_: (i, 0)),
        scratch_shapes=[pltpu.VMEM((2, nslot * SUBLANES, LANES), F32), pltpu.SemaphoreType.DMA((2,))],
    )
    return pl.pallas_call(
        _combine_kernel,
        grid_spec=grid_spec,
        out_shape=jax.ShapeDtypeStruct((t, d), F32),
        compiler_params=_cparams("arbitrary"),
    )(*seg, y, x2, gates, gf, fg)


def _final_norm_kernel(x_ref, g_ref, o_ref):
    x = x_ref[...]
    o_ref[...] = x * lax.rsqrt(jnp.mean(x * x, axis=-1, keepdims=True) + EPS) * g_ref[...]


def _final_norm(x2, g):
    t, d = x2.shape
    tm = ROW_TILE
    return pl.pallas_call(
        _final_norm_kernel,
        grid=(t // tm,),
        in_specs=[pl.BlockSpec((tm, d), lambda i: (i, 0)), pl.BlockSpec((1, d), lambda i: (0, 0))],
        out_specs=pl.BlockSpec((tm, d), lambda i: (i, 0)),
        out_shape=jax.ShapeDtypeStruct((t, d), F32),
        compiler_params=_cparams("parallel"),
    )(x2, g)


def _qk_column_perm():
    half = HEAD_DIM // 2
    lane = np.arange(LANES)
    pair = (lane % HEAD_DIM) // half
    dim = lane % half + half * (lane // HEAD_DIM)
    q = np.concatenate([(i + Q_TILES * pair) * HEAD_DIM + dim for i in range(Q_TILES)])
    k = ATTN_WIDTH + pair * HEAD_DIM + dim
    return q, k


def _attn_out_row_perm():
    lane = np.arange(LANES)
    return np.concatenate([(i + Q_TILES * (lane // HEAD_DIM)) * HEAD_DIM + lane % HEAD_DIM
                           for i in range(Q_TILES)])


def _pad_lanes(v):
    return jnp.pad(v.astype(F32), (0, LANES - v.shape[0])).reshape(1, LANES)


def _moe_route_tables(counts, base, rows, n_blocks):
    cnt = counts[0, :N_EXPERTS].astype(jnp.int32)
    padded = (cnt + rows - 1) // rows * rows
    pad_end = jnp.cumsum(padded)
    pad_start = pad_end - padded
    first = base[:, 0, :N_EXPERTS].astype(jnp.int32)
    run = jnp.concatenate([first[1:], cnt[None]], axis=0) - first
    off = jnp.cumsum(run, axis=1) - run
    flat = lambda a: a.reshape(-1).astype(jnp.int32)
    seg = (flat(off - first), flat(run), flat(off), flat(pad_start[None, :] + first))
    blk_start = jnp.arange(n_blocks, dtype=jnp.int32) * rows
    bexp = jnp.minimum(jnp.sum(blk_start[:, None] >= pad_end[None, :], axis=1), N_EXPERTS - 1).astype(jnp.int32)
    nvalid = (pad_end[-1:] // rows).astype(jnp.int32)
    row_end = pad_start + cnt
    brows = jnp.clip(row_end[bexp] - blk_start, 0, rows).astype(jnp.int32)
    piece = rows // MOE_SPLIT
    piece_end = (row_end + piece - 1) // piece * piece
    zero_spans = (row_end.astype(jnp.int32), piece_end.astype(jnp.int32), pad_end.astype(jnp.int32))
    return seg, bexp, nvalid, brows, zero_spans


def kernel(x, c, positions, ada_w, ada_b, norm_mix_g, norm_ffn_g, w_in, w_out, attn_sinks, conv_w, conv_b,
           dt_bias, a_log, d_skip, ssm_norm_g, ffn_w_gate, ffn_w_up, ffn_w_down, router_w, moe_w_gate,
           moe_w_up, moe_w_down, final_norm_g):
    batch, seq, d = x.shape
    depth = w_in.shape[0]
    t = batch * seq
    tiles_per_batch = seq // ROW_TILE
    x2 = x.reshape(t, d)

    c8 = jnp.pad(c, ((0, SUBLANES - batch), (0, 0)))
    mod = _ada_mod(c8, ada_w, ada_b)[:, :batch].reshape(depth, batch, 6, 1, d)
    cos, sin = _rope_tables(positions)

    q_perm, k_perm = _qk_column_perm()
    o_perm = _attn_out_row_perm()
    for l in range(depth):
        sh_m, sc_m, g_m, sh_f, sc_f, g_f = (mod[l, :, k] for k in range(6))
        wl = w_in[l]
        w_cat = jnp.concatenate(
            [wl[:, q_perm], wl[:, k_perm], wl[:, C_V:C_DT],
             jnp.pad(wl[:, C_DT:], ((0, 0), (0, LANES - SSM_HEADS)))], axis=1).astype(BF16)
        q, kv, z, xbc, dt = _inproj(x2, norm_mix_g[l].reshape(1, d), sc_m, sh_m, cos, sin, w_cat,
                                    _pad_lanes(dt_bias[l]), seq)
        attn = _attention(q, kv, attn_sinks[l].astype(F32), seq)
        ssm = _ssd(xbc, z, dt, conv_w[l], conv_b[l].reshape(1, CONV_CH), _pad_lanes(a_log[l]),
                   jnp.repeat(d_skip[l].astype(F32), SSM_HEAD_DIM).reshape(1, SSM_WIDTH),
                   ssm_norm_g[l].reshape(1, SSM_WIDTH), batch, seq)
        wo = jnp.concatenate([w_out[l][o_perm], w_out[l][ATTN_WIDTH:]], axis=0).astype(BF16)
        ffn_g = norm_ffn_g[l].reshape(1, d)
        if l % 2 == 0:
            x2 = _outproj_ffn(attn, ssm, x2, wo, g_m, ffn_g, sc_f, sh_f, ffn_w_gate[l // 2].astype(BF16),
                              ffn_w_up[l // 2].astype(BF16), ffn_w_down[l // 2].astype(BF16), g_f, seq)
            if l == depth - 1:
                x2 = _final_norm(x2, final_norm_g.reshape(1, d))
        else:
            if l != depth - 1:
                raise NotImplementedError("the expert layer fuses the final norm and must be last")
            rw = jnp.pad(router_w[l // 2].astype(F32), ((0, 0), (0, LANES - N_EXPERTS)))
            rw_hi = rw.astype(BF16)
            rw_lo = (rw - rw_hi.astype(F32)).astype(BF16)
            x_new, h, route, gates, counts, base = _outproj_router(
                attn, ssm, x2, wo, g_m, ffn_g, sc_f, sh_f, tiles_per_batch,
                jnp.concatenate([rw_hi, rw_lo], axis=1))
            rows = min(MOE_ROWS, t)
            n_blocks = -(-(t * TOP_K) // rows) + N_EXPERTS
            seg, bexp, nvalid, brows, zero_spans = _moe_route_tables(counts, base, rows, n_blocks)
            xb = _dispatch(h, route, seg, zero_spans, nvalid, n_blocks, rows)
            y = _moe_experts(xb, bexp, nvalid, brows, moe_w_gate[l // 2], moe_w_up[l // 2], moe_w_down[l // 2],
                             n_blocks, rows)
            x2 = _combine(y, seg, x_new, gates, g_f, final_norm_g.reshape(1, d), tiles_per_batch)
    return x2.reshape(batch, seq, d)
```

```python
import functools

import numpy as np
import jax
import jax.numpy as jnp
from jax import lax
from jax.experimental import pallas as pl
from jax.experimental.pallas import tpu as pltpu

F32 = jnp.float32
BF16 = jnp.bfloat16

LANES = 128
SUBLANES = 8
VMEM_LIMIT = 48 * 1024 * 1024
BIG_VMEM_LIMIT = 58 * 1024 * 1024

EPS = 1e-6
HEAD_DIM = 64
Q_HEADS = 8
KV_HEADS = 2
GROUP = Q_HEADS // KV_HEADS
ATTN_WIDTH = Q_HEADS * HEAD_DIM
KV_WIDTH = KV_HEADS * HEAD_DIM
WINDOW = 128
ROPE_THETA = 10000.0
SSM_HEADS = 8
SSM_HEAD_DIM = 64
SSM_WIDTH = SSM_HEADS * SSM_HEAD_DIM
SSM_GROUPS = 2
SSM_STATE = 128
GROUP_WIDTH = SSM_WIDTH // SSM_GROUPS
CONV_WIDTH = 4
CONV_CH = SSM_WIDTH + 2 * SSM_GROUPS * SSM_STATE
CHUNK = 128
N_EXPERTS = 8
TOP_K = 2

Q_TILES = ATTN_WIDTH // LANES
C_Q = 0
C_K = C_Q + ATTN_WIDTH
C_V = C_K + KV_WIDTH
C_Z = C_V + KV_WIDTH
C_X = C_Z + SSM_WIDTH
C_DT = C_X + CONV_CH
C_END = C_DT + LANES

ROW_TILE = 512
INPROJ_TILE = 1024
FFN_TILE = 1024
ATTN_TILE = 1024
MOE_ROWS = 1536
MOE_FT = 512
SSD_CHUNKS_PER_STEP = 16
ROW_SPLIT = 2
MOE_SPLIT = 3


def _cparams(*sem):
    return pltpu.CompilerParams(dimension_semantics=sem, vmem_limit_bytes=VMEM_LIMIT)


def _silu(v):
    return v * (1.0 / (1.0 + jnp.exp(-v)))


def _softplus(v):
    return jnp.maximum(v, 0.0) + jnp.log1p(jnp.exp(-jnp.abs(v)))


def _rms_mod(x, g, scale, shift):
    ms = jnp.mean(x * x, axis=-1, keepdims=True)
    return (x * lax.rsqrt(ms + EPS) * g) * (1.0 + scale) + shift


def _ada_kernel(c_ref, w_ref, b_ref, o_ref):
    c = c_ref[...]
    o_ref[0] = jnp.dot(_silu(c), w_ref[0], preferred_element_type=F32,
                       precision=lax.Precision.HIGHEST) + b_ref[0]


def _ada_mod(c8, ada_w, ada_b):
    depth, d, n = ada_w.shape
    tn = 1536
    return pl.pallas_call(
        _ada_kernel,
        grid=(depth, n // tn),
        in_specs=[pl.BlockSpec((SUBLANES, d), lambda l, j: (0, 0)),
                  pl.BlockSpec((1, d, tn), lambda l, j: (l, 0, j)),
                  pl.BlockSpec((1, 1, tn), lambda l, j: (l, 0, j))],
        out_specs=pl.BlockSpec((1, SUBLANES, tn), lambda l, j: (l, 0, j)),
        out_shape=jax.ShapeDtypeStruct((depth, SUBLANES, n), F32),
        compiler_params=_cparams("parallel", "parallel"),
    )(c8, ada_w, ada_b.reshape(depth, 1, n))


ROPE_PER_ROW = LANES // (HEAD_DIM // 2)


def _rope_kernel(pos_ref, inv_ref, sign_ref, cos_ref, sin_ref):
    ang = pos_ref[...].astype(F32) * inv_ref[...]
    tr = ang.shape[0]
    group = lax.broadcasted_iota(jnp.int32, (1, LANES), 1) // (HEAD_DIM // 2)
    for table, out_ref, sign in ((jnp.cos(ang), cos_ref, None), (jnp.sin(ang), sin_ref, sign_ref[...])):
        rolled = [table] + [pltpu.roll(table, (HEAD_DIM // 2) * s, axis=1) for s in range(1, ROPE_PER_ROW)]
        for j in range(ROPE_PER_ROW):
            rep = rolled[(0 - j) % ROPE_PER_ROW]
            for g in range(1, ROPE_PER_ROW):
                rep = jnp.where(group == g, rolled[(g - j) % ROPE_PER_ROW], rep)
            out_ref[pl.ds(j, tr, stride=ROPE_PER_ROW), :] = rep if sign is None else rep * sign


def _rope_tables(positions):
    t = positions.size
    half = HEAD_DIM // 2
    pos_dense = jnp.repeat(positions.reshape(t // ROPE_PER_ROW, ROPE_PER_ROW), half, axis=1)
    inv_freq = ROPE_THETA ** (-jnp.arange(0, HEAD_DIM, 2, dtype=F32) / HEAD_DIM)
    inv_dense = jnp.tile(inv_freq, ROPE_PER_ROW).reshape(1, LANES)
    sign = jnp.repeat(jnp.array([-1.0, -1.0, 1.0, 1.0], F32), half).reshape(1, LANES)
    rows = t // ROPE_PER_ROW
    tr = min(1024, rows)
    return pl.pallas_call(
        _rope_kernel,
        grid=(rows // tr,),
        in_specs=[pl.BlockSpec((tr, LANES), lambda i: (i, 0)),
                  pl.BlockSpec((1, LANES), lambda i: (0, 0)),
                  pl.BlockSpec((1, LANES), lambda i: (0, 0))],
        out_specs=[pl.BlockSpec((tr * ROPE_PER_ROW, LANES), lambda i: (i, 0))] * 2,
        out_shape=[jax.ShapeDtypeStruct((t, LANES), F32)] * 2,
        compiler_params=_cparams("parallel"),
    )(pos_dense, inv_dense, sign)


def _inproj_kernel(x_ref, g_ref, sc_ref, sh_ref, cos_ref, sin_ref, w_ref, dtb_ref,
                   q_ref, kv_ref, z_ref, xbc_ref, dt_ref):
    h = _rms_mod(x_ref[...], g_ref[...], sc_ref[0], sh_ref[0]).astype(BF16)
    cos = cos_ref[...]
    sin = sin_ref[...]

    def rope(t):
        return t * cos + pltpu.roll(t, LANES // 2, axis=1) * sin

    qkv = jnp.dot(h, w_ref[:, C_Q:C_Z], preferred_element_type=F32)
    for i in range(Q_TILES):
        q_ref[:, i * LANES:(i + 1) * LANES] = (
            rope(qkv[:, i * LANES:(i + 1) * LANES]) * (HEAD_DIM ** -0.5)).astype(BF16)
    kv_ref[:, 0:LANES] = rope(qkv[:, C_K:C_V]).astype(BF16)
    kv_ref[:, LANES:2 * LANES] = qkv[:, C_V:C_Z].astype(BF16)
    z_ref[...] = jnp.dot(h, w_ref[:, C_Z:C_X], preferred_element_type=F32)
    xbc_ref[...] = jnp.dot(h, w_ref[:, C_X:C_DT], preferred_element_type=F32)
    dt_raw = jnp.dot(h, w_ref[:, C_DT:C_END], preferred_element_type=F32)
    dt_ref[...] = _softplus(dt_raw + dtb_ref[...])


def _inproj(x2, g, sc, sh, cos, sin, w, dtb, seq):
    t, d = x2.shape
    tm = min(INPROJ_TILE, seq)
    row = lambda i: (i, 0)
    const = lambda i: (0, 0)
    per_b = lambda i: (i // (seq // tm), 0, 0)
    return pl.pallas_call(
        _inproj_kernel,
        grid=(t // tm,),
        in_specs=[pl.BlockSpec((tm, d), row),
                  pl.BlockSpec((1, d), const),
                  pl.BlockSpec((1, 1, d), per_b),
                  pl.BlockSpec((1, 1, d), per_b),
                  pl.BlockSpec((tm, LANES), row),
                  pl.BlockSpec((tm, LANES), row),
                  pl.BlockSpec((d, C_END), const),
                  pl.BlockSpec((1, LANES), const)],
        out_specs=[pl.BlockSpec((tm, ATTN_WIDTH), row),
                   pl.BlockSpec((tm, 2 * KV_WIDTH), row),
                   pl.BlockSpec((tm, SSM_WIDTH), row),
                   pl.BlockSpec((tm, CONV_CH), row),
                   pl.BlockSpec((tm, LANES), row)],
        out_shape=[jax.ShapeDtypeStruct((t, ATTN_WIDTH), BF16),
                   jax.ShapeDtypeStruct((t, 2 * KV_WIDTH), BF16),
                   jax.ShapeDtypeStruct((t, SSM_WIDTH), F32),
                   jax.ShapeDtypeStruct((t, CONV_CH), F32),
                   jax.ShapeDtypeStruct((t, LANES), F32)],
        compiler_params=_cparams("parallel"),
    )(x2, g, sc, sh, cos, sin, w, dtb)


def _attn_kernel(sink_ref, q_ref, kv_ref, kvp_ref, o_ref, *, tiles_per_seq):
    first = (pl.program_id(0) % tiles_per_seq) == 0
    blk = WINDOW
    nsub = ATTN_TILE // blk
    lane = lax.broadcasted_iota(jnp.int32, (1, LANES), 1)
    k_lo_mask = (lane % HEAD_DIM) < (HEAD_DIM // 2)
    v_lo_mask = lane < HEAD_DIM
    zero = jnp.zeros((), BF16)

    k_all = jnp.concatenate([kvp_ref[:, 0:LANES], kv_ref[:, 0:LANES]], axis=0)
    v_all = jnp.concatenate([kvp_ref[:, LANES:2 * LANES], kv_ref[:, LANES:2 * LANES]], axis=0)
    k_sel = (jnp.where(k_lo_mask, k_all, zero), jnp.where(k_lo_mask, zero, k_all))
    v_sel = (jnp.where(v_lo_mask, v_all, zero), jnp.where(v_lo_mask, zero, v_all))

    qi = lax.broadcasted_iota(jnp.int32, (blk, blk), 0)
    col = lax.broadcasted_iota(jnp.int32, (blk, blk), 1)
    cur = col <= qi
    no_prev = (col > qi) & (col > jnp.where(first, -1, blk))

    for n in range(nsub):
        q_st = jnp.concatenate([q_ref[n * blk:(n + 1) * blk, i * LANES:(i + 1) * LANES]
                                for i in range(Q_TILES)], axis=0)
        out = None
        for hk in range(KV_HEADS):
            k_n = k_sel[hk][n * blk:(n + 2) * blk]
            v_n = v_sel[hk][n * blk:(n + 2) * blk]
            s = lax.dot_general(q_st, k_n, (((1,), (1,)), ((), ())), preferred_element_type=F32)
            probs, scales = [], []
            for i in range(GROUP):
                s_i = s[i * blk:(i + 1) * blk]
                sc = jnp.where(cur, s_i[:, blk:], s_i[:, :blk])
                if n == 0:
                    sc = jnp.where(no_prev, -jnp.inf, sc)
                sink = sink_ref[hk * GROUP + i]
                m = jnp.maximum(jnp.max(sc, axis=-1, keepdims=True), sink)
                p = jnp.exp(sc - m)
                scales.append(1.0 / (jnp.sum(p, axis=-1, keepdims=True) + jnp.exp(sink - m)))
                probs.append(jnp.concatenate([jnp.where(cur, 0.0, p), jnp.where(cur, p, 0.0)],
                                             axis=1).astype(BF16))
            o = jnp.dot(jnp.concatenate(probs, axis=0), v_n, preferred_element_type=F32)
            o = o * jnp.concatenate(scales, axis=0)
            out = o if out is None else out + o
        for i in range(Q_TILES):
            o_ref[n * blk:(n + 1) * blk, i * LANES:(i + 1) * LANES] = out[i * blk:(i + 1) * blk].astype(BF16)


def _attention(q, kv, sinks, seq):
    t = q.shape[0]
    tq = ATTN_TILE
    r = tq // WINDOW
    kern = functools.partial(_attn_kernel, tiles_per_seq=seq // tq)
    return pl.pallas_call(
        kern,
        grid=(t // tq,),
        in_specs=[pl.BlockSpec(memory_space=pltpu.SMEM),
                  pl.BlockSpec((tq, ATTN_WIDTH), lambda i: (i, 0)),
                  pl.BlockSpec((tq, 2 * KV_WIDTH), lambda i: (i, 0)),
                  pl.BlockSpec((WINDOW, 2 * KV_WIDTH), lambda i: (jnp.maximum(i * r - 1, 0), 0))],
        out_specs=pl.BlockSpec((tq, ATTN_WIDTH), lambda i: (i, 0)),
        out_shape=jax.ShapeDtypeStruct((t, ATTN_WIDTH), BF16),
        compiler_params=_cparams("parallel"),
    )(sinks, q, kv, kv)


def _split3(v):
    p0 = v.astype(BF16)
    r1 = v - p0.astype(F32)
    p1 = r1.astype(BF16)
    return p0, p1, (r1 - p1.astype(F32)).astype(BF16)


def _dot3(lhs_exact_bf16, v):
    return sum(jnp.dot(lhs_exact_bf16, p, preferred_element_type=F32) for p in _split3(v))


def _ssd_kernel(xbc_ref, z_ref, dt_ref, cw_ref, cb_ref, alog_ref, dskip_ref, ng_ref,
                o_ref, ext_ref, st_ref):
    L = CHUNK
    rows = xbc_ref.shape[0]
    halo = SUBLANES

    @pl.when(pl.program_id(1) == 0)
    def _():
        ext_ref[0:halo, :] = jnp.zeros((halo, CONV_CH), F32)
        st_ref[...] = jnp.zeros(st_ref.shape, F32)

    ext_ref[halo:halo + rows, :] = xbc_ref[...]
    acc = cb_ref[...] + cw_ref[CONV_WIDTH - 1:CONV_WIDTH, :] * ext_ref[halo:halo + rows, :]
    for k in range(CONV_WIDTH - 1):
        off = halo - (CONV_WIDTH - 1) + k
        acc = acc + cw_ref[k:k + 1, :] * ext_ref[off:off + rows, :]
    ext_ref[0:halo, :] = ext_ref[rows:rows + halo, :]
    u_all = _silu(acc)

    lane = lax.broadcasted_iota(jnp.int32, (1, LANES), 1)
    a = jnp.where(lane < SSM_HEADS, -jnp.exp(alog_ref[...]), 0.0)
    row = lax.broadcasted_iota(jnp.int32, (L, L), 0)
    col = lax.broadcasted_iota(jnp.int32, (L, L), 1)
    causal = row >= col
    tri = jnp.where(causal, 1.0, 0.0).astype(BF16)
    spread = jnp.where(lax.broadcasted_iota(jnp.int32, (LANES, SSM_WIDTH), 1) // SSM_HEAD_DIM
                       == lax.broadcasted_iota(jnp.int32, (LANES, SSM_WIDTH), 0), 1.0, 0.0).astype(BF16)
    r_heads = SSM_HEADS // SSM_GROUPS
    glane = lax.broadcasted_iota(jnp.int32, (1, GROUP_WIDTH), 1) // SSM_HEAD_DIM
    zero = jnp.zeros((), BF16)
    states = [st_ref[g] for g in range(SSM_GROUPS)]

    for c in range(rows // L):
        rs = slice(c * L, (c + 1) * L)
        u = u_all[rs]
        xs = u[:, 0:SSM_WIDTH]
        bm = u[:, SSM_WIDTH:SSM_WIDTH + SSM_GROUPS * SSM_STATE]
        cm = u[:, SSM_WIDTH + SSM_GROUPS * SSM_STATE:]
        dt = dt_ref[rs, :]
        acs = _dot3(tri, dt * a)
        acs_t = acs.T
        dt_e = sum(jnp.dot(p, spread, preferred_element_type=F32) for p in _split3(dt))
        acs_e = sum(jnp.dot(p, spread, preferred_element_type=F32) for p in _split3(acs))
        last = acs_e[L - 1:L, :]
        xd = xs * dt_e
        xd_b = xd.astype(BF16)
        xdw_b = (xd * jnp.exp(last - acs_e)).astype(BF16)
        e_acs = jnp.exp(acs_e)
        c_dec = jnp.exp(last)

        ys = []
        for g in range(SSM_GROUPS):
            gs = slice(g * GROUP_WIDTH, (g + 1) * GROUP_WIDTH)
            b_g = bm[:, g * SSM_STATE:(g + 1) * SSM_STATE]
            c_b = cm[:, g * SSM_STATE:(g + 1) * SSM_STATE].astype(BF16)
            cb = lax.dot_general(c_b, b_g.astype(BF16), (((1,), (1,)), ((), ())),
                                 preferred_element_type=F32)
            st = states[g]
            y_g = jnp.dot(c_b, st.astype(BF16), preferred_element_type=F32) * e_acs[:, gs]
            xd_g = xd_b[:, gs]
            for r in range(r_heads):
                h = g * r_heads + r
                seg = acs[:, h:h + 1] - acs_t[h:h + 1, :]
                m_h = (cb * jnp.exp(jnp.where(causal, seg, -jnp.inf))).astype(BF16)
                y_g = y_g + jnp.dot(m_h, jnp.where(glane == r, xd_g, zero), preferred_element_type=F32)
            new = jnp.dot(b_g.T.astype(BF16), xdw_b[:, gs], preferred_element_type=F32)
            states[g] = c_dec[:, gs] * st + new
            ys.append(y_g)

        y = jnp.concatenate(ys, axis=1) + dskip_ref[...] * xs
        y = y * _silu(z_ref[rs, :])
        outs = []
        for g in range(SSM_GROUPS):
            yg = y[:, g * GROUP_WIDTH:(g + 1) * GROUP_WIDTH]
            outs.append(yg * lax.rsqrt(jnp.mean(yg * yg, axis=-1, keepdims=True) + EPS))
        o_ref[rs, :] = (jnp.concatenate(outs, axis=1) * ng_ref[...]).astype(BF16)

    for g in range(SSM_GROUPS):
        st_ref[g] = states[g]


def _ssd(xbc, z, dt, cw, cb, alog, dskip, ng, batch, seq):
    t = xbc.shape[0]
    rows = SSD_CHUNKS_PER_STEP * CHUNK
    ns = seq // rows
    row = lambda b, c: (b * ns + c, 0)
    const = lambda b, c: (0, 0)
    return pl.pallas_call(
        _ssd_kernel,
        grid=(batch, ns),
        in_specs=[pl.BlockSpec((rows, CONV_CH), row),
                  pl.BlockSpec((rows, SSM_WIDTH), row),
                  pl.BlockSpec((rows, LANES), row),
                  pl.BlockSpec((CONV_WIDTH, CONV_CH), const),
                  pl.BlockSpec((1, CONV_CH), const),
                  pl.BlockSpec((1, LANES), const),
                  pl.BlockSpec((1, SSM_WIDTH), const),
                  pl.BlockSpec((1, SSM_WIDTH), const)],
        out_specs=pl.BlockSpec((rows, SSM_WIDTH), row),
        out_shape=jax.ShapeDtypeStruct((t, SSM_WIDTH), BF16),
        scratch_shapes=[pltpu.VMEM((SUBLANES + rows, CONV_CH), F32),
                        pltpu.VMEM((SSM_GROUPS, SSM_STATE, GROUP_WIDTH), F32)],
        compiler_params=_cparams("parallel", "arbitrary"),
    )(xbc, z, dt, cw, cb, alog, dskip, ng)


def _outproj_core(a_ref, s_ref, x_ref, w_ref, gm_ref, g_ref, sc_ref, sh_ref):
    half = a_ref.shape[1]
    mixed = (jnp.dot(a_ref[...], w_ref[0:half, :], preferred_element_type=F32)
             + jnp.dot(s_ref[...], w_ref[half:, :], preferred_element_type=F32))
    x_new = x_ref[...] + gm_ref[0] * mixed
    return x_new, _rms_mod(x_new, g_ref[...], sc_ref[0], sh_ref[0])


def _outproj_router_kernel(a_ref, s_ref, x_ref, w_ref, gm_ref, g_ref, sc_ref, sh_ref, rw_ref,
                           xo_ref, h_ref, route_ref, gate_ref, cnt_ref, base_ref, carry_ref):
    x_new, h = _outproj_core(a_ref, s_ref, x_ref, w_ref, gm_ref, g_ref, sc_ref, sh_ref)
    xo_ref[...] = x_new
    tm = h.shape[0]
    h_ref[...] = h.astype(BF16)

    h_hi = h.astype(BF16)
    h_lo = (h - h_hi.astype(F32)).astype(BF16)
    hi_part = jnp.dot(h_hi, rw_ref[...], preferred_element_type=F32)
    logits = (hi_part[:, 0:LANES] + hi_part[:, LANES:]
              + jnp.dot(h_lo, rw_ref[:, 0:LANES], preferred_element_type=F32))
    lane = lax.broadcasted_iota(jnp.int32, (tm, LANES), 1).astype(F32)
    logits = jnp.where(lane < N_EXPERTS, logits, -jnp.inf)
    m0 = jnp.max(logits, axis=-1, keepdims=True)
    i0 = jnp.min(jnp.where(logits == m0, lane, float(LANES)), axis=-1, keepdims=True)
    rest = jnp.where(lane == i0, -jnp.inf, logits)
    m1 = jnp.max(rest, axis=-1, keepdims=True)
    i1 = jnp.min(jnp.where(rest == m1, lane, float(LANES)), axis=-1, keepdims=True)
    e = jnp.exp(m1 - m0)
    g0 = 1.0 / (1.0 + e)
    g1 = e / (1.0 + e)

    @pl.when(pl.program_id(0) == 0)
    def _():
        carry_ref[...] = jnp.zeros(carry_ref.shape, F32)

    sel0 = lane == i0
    sel1 = lane == i1
    member = jnp.where(sel0 | sel1, 1.0, 0.0)
    r_i = lax.broadcasted_iota(jnp.int32, (tm, tm), 0)
    c_i = lax.broadcasted_iota(jnp.int32, (tm, tm), 1)
    strict = jnp.where(r_i > c_i, 1.0, 0.0).astype(BF16)
    base_ref[0] = carry_ref[...]
    rank = carry_ref[...] + jnp.dot(strict, member.astype(BF16), preferred_element_type=F32)
    r0 = jnp.sum(jnp.where(sel0, rank, 0.0), axis=-1, keepdims=True)
    r1 = jnp.sum(jnp.where(sel1, rank, 0.0), axis=-1, keepdims=True)
    carry_ref[...] = carry_ref[...] + jnp.sum(member, axis=0, keepdims=True)
    cnt_ref[...] = carry_ref[...]

    cols = r1
    for k, colv in enumerate((g0, g1, i0, i1, r0)):
        cols = jnp.where(lane == float(k), colv, cols)
    gate_ref[...] = cols[:, 0:SUBLANES]
    diag = (lax.broadcasted_iota(jnp.int32, (tm, LANES), 0) % LANES
            == lax.broadcasted_iota(jnp.int32, (tm, LANES), 1))
    nrow = tm // LANES
    for q, colv in enumerate((i0, i1, r0, r1)):
        spread = jnp.where(diag, colv, 0.0)
        rows = [jnp.sum(spread[b * LANES:(b + 1) * LANES], axis=0, keepdims=True) for b in range(nrow)]
        route_ref[0, q * nrow:(q + 1) * nrow, :] = jnp.concatenate(rows, axis=0).astype(jnp.int32)


def _outproj_router(attn, ssm, x2, w, gm, g, sc, sh, tiles_per_batch, router):
    t, d = x2.shape
    tm = ROW_TILE
    row = lambda i: (i, 0)
    const = lambda i: (0, 0)
    per_b = lambda i: (i // tiles_per_batch, 0, 0)
    half = attn.shape[1]
    return pl.pallas_call(
        _outproj_router_kernel,
        grid=(t // tm,),
        in_specs=[pl.BlockSpec((tm, half), row),
                  pl.BlockSpec((tm, half), row),
                  pl.BlockSpec((tm, d), row),
                  pl.BlockSpec((2 * half, d), const),
                  pl.BlockSpec((1, 1, d), per_b),
                  pl.BlockSpec((1, d), const),
                  pl.BlockSpec((1, 1, d), per_b),
                  pl.BlockSpec((1, 1, d), per_b),
                  pl.BlockSpec((d, 2 * LANES), const)],
        out_specs=[pl.BlockSpec((tm, d), row),
                   pl.BlockSpec((tm, d), row),
                   pl.BlockSpec((1, 4 * tm // LANES, LANES), lambda i: (i, 0, 0)),
                   pl.BlockSpec((tm, SUBLANES), row),
                   pl.BlockSpec((1, LANES), const),
                   pl.BlockSpec((1, 1, LANES), lambda i: (i, 0, 0))],
        out_shape=[jax.ShapeDtypeStruct((t, d), F32),
                   jax.ShapeDtypeStruct((t, d), BF16),
                   jax.ShapeDtypeStruct((t // tm, 4 * tm // LANES, LANES), jnp.int32),
                   jax.ShapeDtypeStruct((t, SUBLANES), F32),
                   jax.ShapeDtypeStruct((1, LANES), F32),
                   jax.ShapeDtypeStruct((t // tm, 1, LANES), F32)],
        scratch_shapes=[pltpu.VMEM((1, LANES), F32)],
        compiler_params=_cparams("arbitrary"),
    )(attn, ssm, x2, w, gm, g, sc, sh, router)


def _swiglu_rows(x, wg, wu, wd):
    gate = jnp.dot(x, wg, preferred_element_type=F32)
    up = jnp.dot(x, wu, preferred_element_type=F32)
    return jnp.dot((_silu(gate) * up).astype(BF16), wd, preferred_element_type=F32)


def _outproj_ffn_kernel(a_ref, s_ref, x_ref, w_ref, gm_ref, g_ref, sc_ref, sh_ref, wg_ref, wu_ref, wd_ref, gf_ref,
                        o_ref):
    half = a_ref.shape[1]
    piece = x_ref.shape[0] // ROW_SPLIT
    for r in range(ROW_SPLIT):
        rs = slice(r * piece, (r + 1) * piece)
        mixed = (jnp.dot(a_ref[rs, :], w_ref[0:half, :], preferred_element_type=F32)
                 + jnp.dot(s_ref[rs, :], w_ref[half:, :], preferred_element_type=F32))
        x_new = x_ref[rs, :] + gm_ref[0] * mixed
        h = _rms_mod(x_new, g_ref[...], sc_ref[0], sh_ref[0]).astype(BF16)
        o_ref[rs, :] = x_new + gf_ref[0] * _swiglu_rows(h, wg_ref[...], wu_ref[...], wd_ref[...])


def _outproj_ffn(attn, ssm, x2, w, gm, g, sc, sh, wg, wu, wd, gf, seq):
    t, d = x2.shape
    f = wg.shape[1]
    tm = min(FFN_TILE, seq)
    half = attn.shape[1]
    row = lambda i: (i, 0)
    const = lambda i: (0, 0)
    per_b = lambda i: (i // (seq // tm), 0, 0)
    resident = pl.Buffered(1)
    return pl.pallas_call(
        _outproj_ffn_kernel,
        grid=(t // tm,),
        in_specs=[pl.BlockSpec((tm, half), row),
                  pl.BlockSpec((tm, half), row),
                  pl.BlockSpec((tm, d), row),
                  pl.BlockSpec((2 * half, d), const, pipeline_mode=resident),
                  pl.BlockSpec((1, 1, d), per_b),
                  pl.BlockSpec((1, d), const),
                  pl.BlockSpec((1, 1, d), per_b),
                  pl.BlockSpec((1, 1, d), per_b),
                  pl.BlockSpec((d, f), const, pipeline_mode=resident),
                  pl.BlockSpec((d, f), const, pipeline_mode=resident),
                  pl.BlockSpec((f, d), const, pipeline_mode=resident),
                  pl.BlockSpec((1, 1, d), per_b)],
        out_specs=pl.BlockSpec((tm, d), row),
        out_shape=jax.ShapeDtypeStruct((t, d), F32),
        compiler_params=pltpu.CompilerParams(dimension_semantics=("parallel",), vmem_limit_bytes=BIG_VMEM_LIMIT),
    )(attn, ssm, x2, w, gm, g, sc, sh, wg, wu, wd, gf)


def _segment_copies(count, src, src_row, dst, dst_row, sem, max_rows):
    for k in range(max_rows.bit_length() - 1, -1, -1):
        size = (1 << k) * SUBLANES

        @pl.when(((count >> k) & 1) == 1)
        def _(k=k, size=size):
            done = (count >> (k + 1)) << (k + 1)
            pltpu.make_async_copy(
                src.at[pl.ds(pl.multiple_of((src_row + done) * SUBLANES, SUBLANES), size)],
                dst.at[pl.ds(pl.multiple_of((dst_row + done) * SUBLANES, SUBLANES), size)], sem).start()


def _lane_rows(route_ref, q, nrow):
    return jnp.concatenate([route_ref[0, q * nrow + b:q * nrow + b + 1, :] for b in range(nrow)], axis=1)


def _dispatch_kernel(delta_ref, cnt_ref, off_ref, dst_ref, zs_ref, zm_ref, ze_ref, nvalid_ref,
                     route_ref, h_ref, xb_hbm, g_ref, zero_ref, sem, zsem):
    i = pl.program_id(0)
    n = pl.num_programs(0)
    tm = h_ref.shape[0]
    nslot = TOP_K * tm
    slot = i % 2
    blk_rows = zero_ref.shape[0]
    n_blocks = xb_hbm.shape[0] // blk_rows

    def zero_row(r):
        return pltpu.make_async_copy(
            zero_ref.at[pl.ds(0, SUBLANES)],
            xb_hbm.at[pl.ds(pl.multiple_of(r * SUBLANES, SUBLANES), SUBLANES)], zsem)

    piece_rows = blk_rows // MOE_SPLIT

    def zero_piece(p):
        return pltpu.make_async_copy(
            zero_ref.at[pl.ds(0, piece_rows)],
            xb_hbm.at[pl.ds(pl.multiple_of(p * piece_rows, piece_rows), piece_rows)], zsem)

    def zero_block(b):
        return pltpu.make_async_copy(
            zero_ref, xb_hbm.at[pl.ds(pl.multiple_of(b * blk_rows, blk_rows), blk_rows)], zsem)

    @pl.when(i == 0)
    def _():
        zero_ref[...] = jnp.zeros(zero_ref.shape, F32)
        for start in (True, False):
            for e in range(N_EXPERTS):
                def rows_body(r, c):
                    zero_row(r).start() if start else zero_row(r).wait()
                    return c
                lax.fori_loop(zs_ref[e], zm_ref[e], rows_body, 0)

                def pieces_body(p, c):
                    zero_piece(p).start() if start else zero_piece(p).wait()
                    return c
                lax.fori_loop(zm_ref[e] // (piece_rows // SUBLANES), ze_ref[e] // (piece_rows // SUBLANES),
                              pieces_body, 0)

            def blocks_body(b, c):
                zero_block(b).start() if start else zero_block(b).wait()
                return c
            lax.fori_loop(nvalid_ref[0], n_blocks, blocks_body, 0)

    def wait_slot(sl):
        pltpu.make_async_copy(g_ref.at[sl], xb_hbm.at[pl.ds(0, nslot * SUBLANES)], sem.at[sl]).wait()

    @pl.when(i >= 2)
    def _():
        wait_slot(slot)

    nrow = tm // LANES
    experts = (_lane_rows(route_ref, 0, nrow), _lane_rows(route_ref, 1, nrow))
    ranks = (_lane_rows(route_ref, 2, nrow), _lane_rows(route_ref, 3, nrow))
    row_id = lax.broadcasted_iota(jnp.int32, (nslot, tm), 0)
    hit = None
    for k in range(TOP_K):
        delta = jnp.zeros((1, tm), jnp.int32)
        for e in range(N_EXPERTS):
            delta = jnp.where(experts[k] == e, delta_ref[i * N_EXPERTS + e], delta)
        mine = row_id == delta + ranks[k]
        hit = mine if hit is None else hit | mine
    grouped = jnp.dot(jnp.where(hit, 1.0, 0.0).astype(BF16), h_ref[...], preferred_element_type=F32)
    for s in range(grouped.shape[1] // LANES):
        g_ref[slot, pl.ds(s, nslot, stride=SUBLANES), :] = grouped[:, s * LANES:(s + 1) * LANES]
    for e in range(N_EXPERTS):
        _segment_copies(cnt_ref[i * N_EXPERTS + e], g_ref.at[slot], off_ref[i * N_EXPERTS + e],
                        xb_hbm, dst_ref[i * N_EXPERTS + e], sem.at[slot], tm)

    @pl.when(i == n - 1)
    def _():
        wait_slot(slot)

    @pl.when((i == n - 1) & (n >= 2))
    def _():
        wait_slot(1 - slot)


def _dispatch(h, route, seg, zero_spans, nvalid, n_blocks, rows):
    t, d = h.shape
    tm = ROW_TILE
    nslot = TOP_K * tm
    nrow = tm // LANES
    grid_spec = pltpu.PrefetchScalarGridSpec(
        num_scalar_prefetch=8,
        grid=(t // tm,),
        in_specs=[pl.BlockSpec((1, 4 * nrow, LANES), lambda i, *_: (i, 0, 0)),
                  pl.BlockSpec((tm, d), lambda i, *_: (i, 0))],
        out_specs=pl.BlockSpec(memory_space=pl.ANY),
        scratch_shapes=[pltpu.VMEM((2, nslot * SUBLANES, LANES), F32),
                        pltpu.VMEM((rows * SUBLANES, LANES), F32),
                        pltpu.SemaphoreType.DMA((2,)), pltpu.SemaphoreType.DMA(())],
    )
    return pl.pallas_call(
        _dispatch_kernel,
        grid_spec=grid_spec,
        out_shape=jax.ShapeDtypeStruct((n_blocks * rows * SUBLANES, LANES), F32),
        compiler_params=pltpu.CompilerParams(dimension_semantics=("arbitrary",), has_side_effects=True,
                                             vmem_limit_bytes=VMEM_LIMIT),
    )(*seg, *zero_spans, nvalid, route, h)


def _moe_kernel(bexp_ref, nvalid_ref, brows_ref, xb_ref, wg_ref, wu_ref, wd_ref, y_ref, x_scr, acc_ref):
    b = pl.program_id(0)
    j = pl.program_id(1)
    nf = pl.num_programs(1)
    rows = x_scr.shape[0]
    nsl = x_scr.shape[1] // LANES
    valid = b < nvalid_ref[0]
    piece = rows // MOE_SPLIT
    used = (brows_ref[b] + piece - 1) // piece

    def tile_rows(r, s):
        return pl.ds(r * piece * SUBLANES + s, piece, stride=SUBLANES)

    def body(first, last, n_used):
        wg = wg_ref[0].astype(BF16)
        wu = wu_ref[0].astype(BF16)
        wd = wd_ref[0].astype(BF16)
        for r in range(n_used):
            rs = slice(r * piece, (r + 1) * piece)
            if first:
                for s in range(nsl):
                    x_scr[rs, s * LANES:(s + 1) * LANES] = xb_ref[tile_rows(r, s), :].astype(BF16)
            part = _swiglu_rows(x_scr[rs, :], wg, wu, wd)
            total = part if first else acc_ref[rs, :] + part
            if last:
                for s in range(nsl):
                    y_ref[tile_rows(r, s), :] = total[:, s * LANES:(s + 1) * LANES]
            else:
                acc_ref[rs, :] = total
        if last:
            for r in range(n_used, MOE_SPLIT):
                for s in range(nsl):
                    y_ref[tile_rows(r, s), :] = jnp.zeros((piece, LANES), F32)

    for first, last, when_j in ((True, False, j == 0), (False, False, (j > 0) & (j < nf - 1)),
                                (False, True, j == nf - 1)):
        for n_used in range(1, MOE_SPLIT + 1):
            pl.when(valid & when_j & (used == n_used))(functools.partial(body, first, last, n_used))

    @pl.when(jnp.logical_not(valid) & (j == nf - 1))
    def _():
        y_ref[...] = jnp.zeros(y_ref.shape, F32)


def _moe_experts(xb, bexp, nvalid, brows, wg, wu, wd, n_blocks, rows):
    d = wg.shape[1]
    f = wg.shape[2]
    nf = f // MOE_FT
    assert nf >= 2

    def blk(b, j, bexp, nvalid, brows):
        return (jnp.minimum(b, nvalid[0] - 1), 0)

    def fidx(b, j, nvalid):
        return jnp.where(b < nvalid[0], j, nf - 1)

    grid_spec = pltpu.PrefetchScalarGridSpec(
        num_scalar_prefetch=3,
        grid=(n_blocks, nf),
        in_specs=[pl.BlockSpec((rows * SUBLANES, LANES), blk),
                  pl.BlockSpec((1, d, MOE_FT), lambda b, j, bexp, nvalid, brows: (bexp[b], 0, fidx(b, j, nvalid))),
                  pl.BlockSpec((1, d, MOE_FT), lambda b, j, bexp, nvalid, brows: (bexp[b], 0, fidx(b, j, nvalid))),
                  pl.BlockSpec((1, MOE_FT, d), lambda b, j, bexp, nvalid, brows: (bexp[b], fidx(b, j, nvalid), 0))],
        out_specs=pl.BlockSpec((rows * SUBLANES, LANES), lambda b, j, bexp, nvalid, brows: (b, 0)),
        scratch_shapes=[pltpu.VMEM((rows, d), BF16), pltpu.VMEM((rows, d), F32)],
    )
    return pl.pallas_call(
        _moe_kernel,
        grid_spec=grid_spec,
        out_shape=jax.ShapeDtypeStruct(xb.shape, F32),
        compiler_params=pltpu.CompilerParams(dimension_semantics=("arbitrary", "arbitrary"),
                                             vmem_limit_bytes=BIG_VMEM_LIMIT),
    )(bexp, nvalid, brows, xb, wg, wu, wd)


def _combine_kernel(delta_ref, cnt_ref, off_ref, dst_ref, y_hbm, x_ref, gate_ref, gf_ref, fg_ref, o_ref,
                    buf_ref, sem):
    i = pl.program_id(0)
    tm = x_ref.shape[0]
    nslot = TOP_K * tm
    nsl = x_ref.shape[1] // LANES
    slot = i % 2

    def fetch(tile, to_slot):
        for e in range(N_EXPERTS):
            _segment_copies(cnt_ref[tile * N_EXPERTS + e], y_hbm, dst_ref[tile * N_EXPERTS + e],
                            buf_ref.at[to_slot], off_ref[tile * N_EXPERTS + e], sem.at[to_slot], tm)

    @pl.when(i == 0)
    def _():
        fetch(0, 0)

    @pl.when(i + 1 < pl.num_programs(0))
    def _():
        fetch(i + 1, 1 - slot)

    pltpu.make_async_copy(y_hbm.at[pl.ds(0, nslot * SUBLANES)], buf_ref.at[slot], sem.at[slot]).wait()

    y_rows = jnp.concatenate([buf_ref[slot, pl.ds(s, nslot, stride=SUBLANES), :] for s in range(nsl)],
                             axis=1).astype(BF16)
    cols = gate_ref[...]
    lane = lax.broadcasted_iota(jnp.int32, (tm, nslot), 1).astype(F32)
    weights = None
    for k in range(TOP_K):
        expert = cols[:, 2 + k:3 + k]
        delta = jnp.zeros((tm, 1), F32)
        for e in range(N_EXPERTS):
            delta = jnp.where(expert == float(e), delta_ref[i * N_EXPERTS + e].astype(F32), delta)
        w_k = jnp.where(lane == delta + cols[:, 4 + k:5 + k], cols[:, k:k + 1], 0.0)
        weights = w_k if weights is None else weights + w_k
    f = jnp.dot(weights.astype(BF16), y_rows, preferred_element_type=F32)
    x = x_ref[...] + gf_ref[0] * f
    ms = jnp.mean(x * x, axis=-1, keepdims=True)
    o_ref[...] = x * lax.rsqrt(ms + EPS) * fg_ref[...]


def _combine(y, seg, x2, gates, gf, fg, tiles_per_batch):
    t, d = x2.shape
    tm = ROW_TILE
    nslot = TOP_K * tm
    grid_spec = pltpu.PrefetchScalarGridSpec(
        num_scalar_prefetch=4,
        grid=(t // tm,),
        in_specs=[pl.BlockSpec(memory_space=pl.ANY),
                  pl.BlockSpec((tm, d), lambda i, *_: (i, 0)),
                  pl.BlockSpec((tm, SUBLANES), lambda i, *_: (i, 0)),
                  pl.BlockSpec((1, 1, d), lambda i, *_: (i // tiles_per_batch, 0, 0)),
                  pl.BlockSpec((1, d), lambda i, *_: (0, 0))],
        out_specs=pl.BlockSpec((tm, d), lambda i, *_: (i, 0)),
        scratch_shapes=[pltpu.VMEM((2, nslot * SUBLANES, LANES), F32), pltpu.SemaphoreType.DMA((2,))],
    )
    return pl.pallas_call(
        _combine_kernel,
        grid_spec=grid_spec,
        out_shape=jax.ShapeDtypeStruct((t, d), F32),
        compiler_params=_cparams("arbitrary"),
    )(*seg, y, x2, gates, gf, fg)


def _final_norm_kernel(x_ref, g_ref, o_ref):
    x = x_ref[...]
    o_ref[...] = x * lax.rsqrt(jnp.mean(x * x, axis=-1, keepdims=True) + EPS) * g_ref[...]


def _final_norm(x2, g):
    t, d = x2.shape
    tm = ROW_TILE
    return pl.pallas_call(
        _final_norm_kernel,
        grid=(t // tm,),
        in_specs=[pl.BlockSpec((tm, d), lambda i: (i, 0)), pl.BlockSpec((1, d), lambda i: (0, 0))],
        out_specs=pl.BlockSpec((tm, d), lambda i: (i, 0)),
        out_shape=jax.ShapeDtypeStruct((t, d), F32),
        compiler_params=_cparams("parallel"),
    )(x2, g)


def _qk_column_perm():
    half = HEAD_DIM // 2
    lane = np.arange(LANES)
    pair = (lane % HEAD_DIM) // half
    dim = lane % half + half * (lane // HEAD_DIM)
    q = np.concatenate([(i + Q_TILES * pair) * HEAD_DIM + dim for i in range(Q_TILES)])
    k = ATTN_WIDTH + pair * HEAD_DIM + dim
    return q, k


def _attn_out_row_perm():
    lane = np.arange(LANES)
    return np.concatenate([(i + Q_TILES * (lane // HEAD_DIM)) * HEAD_DIM + lane % HEAD_DIM
                           for i in range(Q_TILES)])


def _pad_lanes(v):
    return jnp.pad(v.astype(F32), (0, LANES - v.shape[0])).reshape(1, LANES)


def _moe_route_tables(counts, base, rows, n_blocks):
    cnt = counts[0, :N_EXPERTS].astype(jnp.int32)
    padded = (cnt + rows - 1) // rows * rows
    pad_end = jnp.cumsum(padded)
    pad_start = pad_end - padded
    first = base[:, 0, :N_EXPERTS].astype(jnp.int32)
    run = jnp.concatenate([first[1:], cnt[None]], axis=0) - first
    off = jnp.cumsum(run, axis=1) - run
    flat = lambda a: a.reshape(-1).astype(jnp.int32)
    seg = (flat(off - first), flat(run), flat(off), flat(pad_start[None, :] + first))
    blk_start = jnp.arange(n_blocks, dtype=jnp.int32) * rows
    bexp = jnp.minimum(jnp.sum(blk_start[:, None] >= pad_end[None, :], axis=1), N_EXPERTS - 1).astype(jnp.int32)
    nvalid = (pad_end[-1:] // rows).astype(jnp.int32)
    row_end = pad_start + cnt
    brows = jnp.clip(row_end[bexp] - blk_start, 0, rows).astype(jnp.int32)
    piece = rows // MOE_SPLIT
    piece_end = (row_end + piece - 1) // piece * piece
    zero_spans = (row_end.astype(jnp.int32), piece_end.astype(jnp.int32), pad_end.astype(jnp.int32))
    return seg, bexp, nvalid, brows, zero_spans


def kernel(x, c, positions, ada_w, ada_b, norm_mix_g, norm_ffn_g, w_in, w_out, attn_sinks, conv_w, conv_b,
           dt_bias, a_log, d_skip, ssm_norm_g, ffn_w_gate, ffn_w_up, ffn_w_down, router_w, moe_w_gate,
           moe_w_up, moe_w_down, final_norm_g):
    batch, seq, d = x.shape
    depth = w_in.shape[0]
    t = batch * seq
    tiles_per_batch = seq // ROW_TILE
    x2 = x.reshape(t, d)

    c8 = jnp.pad(c, ((0, SUBLANES - batch), (0, 0)))
    mod = _ada_mod(c8, ada_w, ada_b)[:, :batch].reshape(depth, batch, 6, 1, d)
    cos, sin = _rope_tables(positions)

    q_perm, k_perm = _qk_column_perm()
    n_in = w_in.shape[2]
    in_cols = np.concatenate([q_perm, k_perm, np.arange(C_V, n_in), np.full(LANES - SSM_HEADS, n_in)])
    out_rows = np.concatenate([_attn_out_row_perm(), np.arange(ATTN_WIDTH, w_out.shape[1])])
    for l in range(depth):
        sh_m, sc_m, g_m, sh_f, sc_f, g_f = (mod[l, :, k] for k in range(6))
        w_cat = jnp.take(jnp.pad(w_in[l].astype(BF16), ((0, 0), (0, 1))), in_cols, axis=1)
        q, kv, z, xbc, dt = _inproj(x2, norm_mix_g[l].reshape(1, d), sc_m, sh_m, cos, sin, w_cat,
                                    _pad_lanes(dt_bias[l]), seq)
        attn = _attention(q, kv, attn_sinks[l].astype(F32), seq)
        ssm = _ssd(xbc, z, dt, conv_w[l], conv_b[l].reshape(1, CONV_CH), _pad_lanes(a_log[l]),
                   jnp.repeat(d_skip[l].astype(F32), SSM_HEAD_DIM).reshape(1, SSM_WIDTH),
                   ssm_norm_g[l].reshape(1, SSM_WIDTH), batch, seq)
        wo = jnp.take(w_out[l].astype(BF16), out_rows, axis=0)
        ffn_g = norm_ffn_g[l].reshape(1, d)
        if l % 2 == 0:
            x2 = _outproj_ffn(attn, ssm, x2, wo, g_m, ffn_g, sc_f, sh_f, ffn_w_gate[l // 2].astype(BF16),
                              ffn_w_up[l // 2].astype(BF16), ffn_w_down[l // 2].astype(BF16), g_f, seq)
            if l == depth - 1:
                x2 = _final_norm(x2, final_norm_g.reshape(1, d))
        else:
            if l != depth - 1:
                raise NotImplementedError("the expert layer fuses the final norm and must be last")
            rw = jnp.pad(router_w[l // 2].astype(F32), ((0, 0), (0, LANES - N_EXPERTS)))
            rw_hi = rw.astype(BF16)
            rw_lo = (rw - rw_hi.astype(F32)).astype(BF16)
            x_new, h, route, gates, counts, base = _outproj_router(
                attn, ssm, x2, wo, g_m, ffn_g, sc_f, sh_f, tiles_per_batch,
                jnp.concatenate([rw_hi, rw_lo], axis=1))
            rows = min(MOE_ROWS, t)
            n_blocks = -(-(t * TOP_K) // rows) + N_EXPERTS
            seg, bexp, nvalid, brows, zero_spans = _moe_route_tables(counts, base, rows, n_blocks)
            xb = _dispatch(h, route, seg, zero_spans, nvalid, n_blocks, rows)
            y = _moe_experts(xb, bexp, nvalid, brows, moe_w_gate[l // 2], moe_w_up[l // 2], moe_w_down[l // 2],
                             n_blocks, rows)
            x2 = _combine(y, seg, x_new, gates, g_f, final_norm_g.reshape(1, d), tiles_per_batch)
    return x2.reshape(batch, seq, d)
```

```python
import functools

import numpy as np
import jax
import jax.numpy as jnp
from jax import lax
from jax.experimental import pallas as pl
from jax.experimental.pallas import tpu as pltpu

F32 = jnp.float32
BF16 = jnp.bfloat16

LANES = 128
SUBLANES = 8
VMEM_LIMIT = 48 * 1024 * 1024
BIG_VMEM_LIMIT = 58 * 1024 * 1024

EPS = 1e-6
HEAD_DIM = 64
Q_HEADS = 8
KV_HEADS = 2
GROUP = Q_HEADS // KV_HEADS
ATTN_WIDTH = Q_HEADS * HEAD_DIM
KV_WIDTH = KV_HEADS * HEAD_DIM
WINDOW = 128
ROPE_THETA = 10000.0
SSM_HEADS = 8
SSM_HEAD_DIM = 64
SSM_WIDTH = SSM_HEADS * SSM_HEAD_DIM
SSM_GROUPS = 2
SSM_STATE = 128
GROUP_WIDTH = SSM_WIDTH // SSM_GROUPS
CONV_WIDTH = 4
CONV_CH = SSM_WIDTH + 2 * SSM_GROUPS * SSM_STATE
CHUNK = 128
N_EXPERTS = 8
TOP_K = 2

Q_TILES = ATTN_WIDTH // LANES
C_Q = 0
C_K = C_Q + ATTN_WIDTH
C_V = C_K + KV_WIDTH
C_Z = C_V + KV_WIDTH
C_X = C_Z + SSM_WIDTH
C_DT = C_X + CONV_CH
C_END = C_DT + LANES

ROW_TILE = 512
INPROJ_TILE = 1024
FFN_TILE = 1024
ATTN_TILE = 1024
MOE_ROWS = 1536
MOE_FT = 512
SSD_CHUNKS_PER_STEP = 16
ROW_SPLIT = 2
MOE_SPLIT = 3


def _cparams(*sem):
    return pltpu.CompilerParams(dimension_semantics=sem, vmem_limit_bytes=VMEM_LIMIT)


def _silu(v):
    return v * (1.0 / (1.0 + jnp.exp(-v)))


def _softplus(v):
    return jnp.maximum(v, 0.0) + jnp.log1p(jnp.exp(-jnp.abs(v)))


def _rms_mod(x, g, scale, shift):
    ms = jnp.mean(x * x, axis=-1, keepdims=True)
    return (x * lax.rsqrt(ms + EPS) * g) * (1.0 + scale) + shift


def _ada_kernel(c_ref, w_ref, b_ref, o_ref):
    c = c_ref[...]
    o_ref[0] = jnp.dot(_silu(c), w_ref[0], preferred_element_type=F32,
                       precision=lax.Precision.HIGHEST) + b_ref[0]


def _ada_mod(c8, ada_w, ada_b):
    depth, d, n = ada_w.shape
    tn = 1536
    return pl.pallas_call(
        _ada_kernel,
        grid=(depth, n // tn),
        in_specs=[pl.BlockSpec((SUBLANES, d), lambda l, j: (0, 0)),
                  pl.BlockSpec((1, d, tn), lambda l, j: (l, 0, j)),
                  pl.BlockSpec((1, 1, tn), lambda l, j: (l, 0, j))],
        out_specs=pl.BlockSpec((1, SUBLANES, tn), lambda l, j: (l, 0, j)),
        out_shape=jax.ShapeDtypeStruct((depth, SUBLANES, n), F32),
        compiler_params=_cparams("parallel", "parallel"),
    )(c8, ada_w, ada_b.reshape(depth, 1, n))


ROPE_PER_ROW = LANES // (HEAD_DIM // 2)


def _rope_kernel(pos_ref, inv_ref, sign_ref, cos_ref, sin_ref):
    ang = pos_ref[...].astype(F32) * inv_ref[...]
    tr = ang.shape[0]
    group = lax.broadcasted_iota(jnp.int32, (1, LANES), 1) // (HEAD_DIM // 2)
    for table, out_ref, sign in ((jnp.cos(ang), cos_ref, None), (jnp.sin(ang), sin_ref, sign_ref[...])):
        rolled = [table] + [pltpu.roll(table, (HEAD_DIM // 2) * s, axis=1) for s in range(1, ROPE_PER_ROW)]
        for j in range(ROPE_PER_ROW):
            rep = rolled[(0 - j) % ROPE_PER_ROW]
            for g in range(1, ROPE_PER_ROW):
                rep = jnp.where(group == g, rolled[(g - j) % ROPE_PER_ROW], rep)
            out_ref[pl.ds(j, tr, stride=ROPE_PER_ROW), :] = rep if sign is None else rep * sign


def _rope_tables(positions):
    t = positions.size
    half = HEAD_DIM // 2
    pos_dense = jnp.repeat(positions.reshape(t // ROPE_PER_ROW, ROPE_PER_ROW), half, axis=1)
    inv_freq = ROPE_THETA ** (-jnp.arange(0, HEAD_DIM, 2, dtype=F32) / HEAD_DIM)
    inv_dense = jnp.tile(inv_freq, ROPE_PER_ROW).reshape(1, LANES)
    sign = jnp.repeat(jnp.array([-1.0, -1.0, 1.0, 1.0], F32), half).reshape(1, LANES)
    rows = t // ROPE_PER_ROW
    tr = min(1024, rows)
    return pl.pallas_call(
        _rope_kernel,
        grid=(rows // tr,),
        in_specs=[pl.BlockSpec((tr, LANES), lambda i: (i, 0)),
                  pl.BlockSpec((1, LANES), lambda i: (0, 0)),
                  pl.BlockSpec((1, LANES), lambda i: (0, 0))],
        out_specs=[pl.BlockSpec((tr * ROPE_PER_ROW, LANES), lambda i: (i, 0))] * 2,
        out_shape=[jax.ShapeDtypeStruct((t, LANES), F32)] * 2,
        compiler_params=_cparams("parallel"),
    )(pos_dense, inv_dense, sign)


def _inproj_kernel(x_ref, g_ref, sc_ref, sh_ref, cos_ref, sin_ref, w_ref, dtb_ref,
                   q_ref, kv_ref, z_ref, xbc_ref, dt_ref):
    h = _rms_mod(x_ref[...], g_ref[...], sc_ref[0], sh_ref[0]).astype(BF16)
    cos = cos_ref[...]
    sin = sin_ref[...]

    def rope(t):
        return t * cos + pltpu.roll(t, LANES // 2, axis=1) * sin

    qkv = jnp.dot(h, w_ref[:, C_Q:C_Z], preferred_element_type=F32)
    for i in range(Q_TILES):
        q_ref[:, i * LANES:(i + 1) * LANES] = (
            rope(qkv[:, i * LANES:(i + 1) * LANES]) * (HEAD_DIM ** -0.5)).astype(BF16)
    kv_ref[:, 0:LANES] = rope(qkv[:, C_K:C_V]).astype(BF16)
    kv_ref[:, LANES:2 * LANES] = qkv[:, C_V:C_Z].astype(BF16)
    z_ref[...] = jnp.dot(h, w_ref[:, C_Z:C_X], preferred_element_type=F32)
    xbc_ref[...] = jnp.dot(h, w_ref[:, C_X:C_DT], preferred_element_type=F32)
    dt_raw = jnp.dot(h, w_ref[:, C_DT:C_END], preferred_element_type=F32)
    dt_ref[...] = _softplus(dt_raw + dtb_ref[...])


def _inproj(x2, g, sc, sh, cos, sin, w, dtb, seq):
    t, d = x2.shape
    tm = min(INPROJ_TILE, seq)
    row = lambda i: (i, 0)
    const = lambda i: (0, 0)
    per_b = lambda i: (i // (seq // tm), 0, 0)
    return pl.pallas_call(
        _inproj_kernel,
        grid=(t // tm,),
        in_specs=[pl.BlockSpec((tm, d), row),
                  pl.BlockSpec((1, d), const),
                  pl.BlockSpec((1, 1, d), per_b),
                  pl.BlockSpec((1, 1, d), per_b),
                  pl.BlockSpec((tm, LANES), row),
                  pl.BlockSpec((tm, LANES), row),
                  pl.BlockSpec((d, C_END), const),
                  pl.BlockSpec((1, LANES), const)],
        out_specs=[pl.BlockSpec((tm, ATTN_WIDTH), row),
                   pl.BlockSpec((tm, 2 * KV_WIDTH), row),
                   pl.BlockSpec((tm, SSM_WIDTH), row),
                   pl.BlockSpec((tm, CONV_CH), row),
                   pl.BlockSpec((tm, LANES), row)],
        out_shape=[jax.ShapeDtypeStruct((t, ATTN_WIDTH), BF16),
                   jax.ShapeDtypeStruct((t, 2 * KV_WIDTH), BF16),
                   jax.ShapeDtypeStruct((t, SSM_WIDTH), F32),
                   jax.ShapeDtypeStruct((t, CONV_CH), F32),
                   jax.ShapeDtypeStruct((t, LANES), F32)],
        compiler_params=_cparams("parallel"),
    )(x2, g, sc, sh, cos, sin, w, dtb)


def _attn_kernel(sink_ref, q_ref, kv_ref, kvp_ref, o_ref, *, tiles_per_seq):
    first = (pl.program_id(0) % tiles_per_seq) == 0
    blk = WINDOW
    nsub = ATTN_TILE // blk
    lane = lax.broadcasted_iota(jnp.int32, (1, LANES), 1)
    k_lo_mask = (lane % HEAD_DIM) < (HEAD_DIM // 2)
    v_lo_mask = lane < HEAD_DIM
    zero = jnp.zeros((), BF16)

    k_all = jnp.concatenate([kvp_ref[:, 0:LANES], kv_ref[:, 0:LANES]], axis=0)
    v_all = jnp.concatenate([kvp_ref[:, LANES:2 * LANES], kv_ref[:, LANES:2 * LANES]], axis=0)
    k_sel = (jnp.where(k_lo_mask, k_all, zero), jnp.where(k_lo_mask, zero, k_all))
    v_sel = (jnp.where(v_lo_mask, v_all, zero), jnp.where(v_lo_mask, zero, v_all))

    qi = lax.broadcasted_iota(jnp.int32, (blk, blk), 0)
    col = lax.broadcasted_iota(jnp.int32, (blk, blk), 1)
    cur = col <= qi
    no_prev = (col > qi) & (col > jnp.where(first, -1, blk))

    for n in range(nsub):
        q_st = jnp.concatenate([q_ref[n * blk:(n + 1) * blk, i * LANES:(i + 1) * LANES]
                                for i in range(Q_TILES)], axis=0)
        out = None
        for hk in range(KV_HEADS):
            k_n = k_sel[hk][n * blk:(n + 2) * blk]
            v_n = v_sel[hk][n * blk:(n + 2) * blk]
            s = lax.dot_general(q_st, k_n, (((1,), (1,)), ((), ())), preferred_element_type=F32)
            probs, scales = [], []
            for i in range(GROUP):
                s_i = s[i * blk:(i + 1) * blk]
                sc = jnp.where(cur, s_i[:, blk:], s_i[:, :blk])
                if n == 0:
                    sc = jnp.where(no_prev, -jnp.inf, sc)
                sink = sink_ref[hk * GROUP + i]
                m = jnp.maximum(jnp.max(sc, axis=-1, keepdims=True), sink)
                p = jnp.exp(sc - m)
                scales.append(1.0 / (jnp.sum(p, axis=-1, keepdims=True) + jnp.exp(sink - m)))
                probs.append(jnp.concatenate([jnp.where(cur, 0.0, p), jnp.where(cur, p, 0.0)],
                                             axis=1).astype(BF16))
            o = jnp.dot(jnp.concatenate(probs, axis=0), v_n, preferred_element_type=F32)
            o = o * jnp.concatenate(scales, axis=0)
            out = o if out is None else out + o
        for i in range(Q_TILES):
            o_ref[n * blk:(n + 1) * blk, i * LANES:(i + 1) * LANES] = out[i * blk:(i + 1) * blk].astype(BF16)


def _attention(q, kv, sinks, seq):
    t = q.shape[0]
    tq = ATTN_TILE
    r = tq // WINDOW
    kern = functools.partial(_attn_kernel, tiles_per_seq=seq // tq)
    return pl.pallas_call(
        kern,
        grid=(t // tq,),
        in_specs=[pl.BlockSpec(memory_space=pltpu.SMEM),
                  pl.BlockSpec((tq, ATTN_WIDTH), lambda i: (i, 0)),
                  pl.BlockSpec((tq, 2 * KV_WIDTH), lambda i: (i, 0)),
                  pl.BlockSpec((WINDOW, 2 * KV_WIDTH), lambda i: (jnp.maximum(i * r - 1, 0), 0))],
        out_specs=pl.BlockSpec((tq, ATTN_WIDTH), lambda i: (i, 0)),
        out_shape=jax.ShapeDtypeStruct((t, ATTN_WIDTH), BF16),
        compiler_params=_cparams("parallel"),
    )(sinks, q, kv, kv)


def _split3(v):
    p0 = v.astype(BF16)
    r1 = v - p0.astype(F32)
    p1 = r1.astype(BF16)
    return p0, p1, (r1 - p1.astype(F32)).astype(BF16)


def _dot3(lhs_exact_bf16, v):
    return sum(jnp.dot(lhs_exact_bf16, p, preferred_element_type=F32) for p in _split3(v))


def _ssd_kernel(xbc_ref, z_ref, dt_ref, cw_ref, cb_ref, alog_ref, dskip_ref, ng_ref,
                o_ref, ext_ref, st_ref):
    L = CHUNK
    rows = xbc_ref.shape[0]
    halo = SUBLANES

    @pl.when(pl.program_id(1) == 0)
    def _():
        ext_ref[0:halo, :] = jnp.zeros((halo, CONV_CH), F32)
        st_ref[...] = jnp.zeros(st_ref.shape, F32)

    ext_ref[halo:halo + rows, :] = xbc_ref[...]
    acc = cb_ref[...] + cw_ref[CONV_WIDTH - 1:CONV_WIDTH, :] * ext_ref[halo:halo + rows, :]
    for k in range(CONV_WIDTH - 1):
        off = halo - (CONV_WIDTH - 1) + k
        acc = acc + cw_ref[k:k + 1, :] * ext_ref[off:off + rows, :]
    ext_ref[0:halo, :] = ext_ref[rows:rows + halo, :]
    u_all = _silu(acc)

    lane = lax.broadcasted_iota(jnp.int32, (1, LANES), 1)
    a = jnp.where(lane < SSM_HEADS, -jnp.exp(alog_ref[...]), 0.0)
    row = lax.broadcasted_iota(jnp.int32, (L, L), 0)
    col = lax.broadcasted_iota(jnp.int32, (L, L), 1)
    causal = row >= col
    tri = jnp.where(causal, 1.0, 0.0).astype(BF16)
    spread = jnp.where(lax.broadcasted_iota(jnp.int32, (LANES, SSM_WIDTH), 1) // SSM_HEAD_DIM
                       == lax.broadcasted_iota(jnp.int32, (LANES, SSM_WIDTH), 0), 1.0, 0.0).astype(BF16)
    r_heads = SSM_HEADS // SSM_GROUPS
    glane = lax.broadcasted_iota(jnp.int32, (1, GROUP_WIDTH), 1) // SSM_HEAD_DIM
    zero = jnp.zeros((), BF16)
    states = [st_ref[g] for g in range(SSM_GROUPS)]

    for c in range(rows // L):
        rs = slice(c * L, (c + 1) * L)
        u = u_all[rs]
        xs = u[:, 0:SSM_WIDTH]
        bm = u[:, SSM_WIDTH:SSM_WIDTH + SSM_GROUPS * SSM_STATE]
        cm = u[:, SSM_WIDTH + SSM_GROUPS * SSM_STATE:]
        dt = dt_ref[rs, :]
        acs = _dot3(tri, dt * a)
        acs_t = acs.T
        dt_e = sum(jnp.dot(p, spread, preferred_element_type=F32) for p in _split3(dt))
        acs_e = sum(jnp.dot(p, spread, preferred_element_type=F32) for p in _split3(acs))
        last = acs_e[L - 1:L, :]
        xd = xs * dt_e
        xd_b = xd.astype(BF16)
        xdw_b = (xd * jnp.exp(last - acs_e)).astype(BF16)
        e_acs = jnp.exp(acs_e)
        c_dec = jnp.exp(last)

        ys = []
        for g in range(SSM_GROUPS):
            gs = slice(g * GROUP_WIDTH, (g + 1) * GROUP_WIDTH)
            b_g = bm[:, g * SSM_STATE:(g + 1) * SSM_STATE]
            c_b = cm[:, g * SSM_STATE:(g + 1) * SSM_STATE].astype(BF16)
            cb = lax.dot_general(c_b, b_g.astype(BF16), (((1,), (1,)), ((), ())),
                                 preferred_element_type=F32)
            st = states[g]
            y_g = jnp.dot(c_b, st.astype(BF16), preferred_element_type=F32) * e_acs[:, gs]
            xd_g = xd_b[:, gs]
            for r in range(r_heads):
                h = g * r_heads + r
                seg = acs[:, h:h + 1] - acs_t[h:h + 1, :]
                m_h = (cb * jnp.exp(jnp.where(causal, seg, -jnp.inf))).astype(BF16)
                y_g = y_g + jnp.dot(m_h, jnp.where(glane == r, xd_g, zero), preferred_element_type=F32)
            new = jnp.dot(b_g.T.astype(BF16), xdw_b[:, gs], preferred_element_type=F32)
            states[g] = c_dec[:, gs] * st + new
            ys.append(y_g)

        y = jnp.concatenate(ys, axis=1) + dskip_ref[...] * xs
        y = y * _silu(z_ref[rs, :])
        outs = []
        for g in range(SSM_GROUPS):
            yg = y[:, g * GROUP_WIDTH:(g + 1) * GROUP_WIDTH]
            outs.append(yg * lax.rsqrt(jnp.mean(yg * yg, axis=-1, keepdims=True) + EPS))
        o_ref[rs, :] = (jnp.concatenate(outs, axis=1) * ng_ref[...]).astype(BF16)

    for g in range(SSM_GROUPS):
        st_ref[g] = states[g]


def _ssd(xbc, z, dt, cw, cb, alog, dskip, ng, batch, seq):
    t = xbc.shape[0]
    rows = SSD_CHUNKS_PER_STEP * CHUNK
    ns = seq // rows
    row = lambda b, c: (b * ns + c, 0)
    const = lambda b, c: (0, 0)
    return pl.pallas_call(
        _ssd_kernel,
        grid=(batch, ns),
        in_specs=[pl.BlockSpec((rows, CONV_CH), row),
                  pl.BlockSpec((rows, SSM_WIDTH), row),
                  pl.BlockSpec((rows, LANES), row),
                  pl.BlockSpec((CONV_WIDTH, CONV_CH), const),
                  pl.BlockSpec((1, CONV_CH), const),
                  pl.BlockSpec((1, LANES), const),
                  pl.BlockSpec((1, SSM_WIDTH), const),
                  pl.BlockSpec((1, SSM_WIDTH), const)],
        out_specs=pl.BlockSpec((rows, SSM_WIDTH), row),
        out_shape=jax.ShapeDtypeStruct((t, SSM_WIDTH), BF16),
        scratch_shapes=[pltpu.VMEM((SUBLANES + rows, CONV_CH), F32),
                        pltpu.VMEM((SSM_GROUPS, SSM_STATE, GROUP_WIDTH), F32)],
        compiler_params=_cparams("parallel", "arbitrary"),
    )(xbc, z, dt, cw, cb, alog, dskip, ng)


def _outproj_core(a_ref, s_ref, x_ref, w_ref, gm_ref, g_ref, sc_ref, sh_ref):
    half = a_ref.shape[1]
    mixed = (jnp.dot(a_ref[...], w_ref[0:half, :], preferred_element_type=F32)
             + jnp.dot(s_ref[...], w_ref[half:, :], preferred_element_type=F32))
    x_new = x_ref[...] + gm_ref[0] * mixed
    return x_new, _rms_mod(x_new, g_ref[...], sc_ref[0], sh_ref[0])


def _outproj_router_kernel(a_ref, s_ref, x_ref, w_ref, gm_ref, g_ref, sc_ref, sh_ref, rw_ref,
                           xo_ref, h_ref, route_ref, gate_ref, cnt_ref, base_ref, carry_ref):
    x_new, h = _outproj_core(a_ref, s_ref, x_ref, w_ref, gm_ref, g_ref, sc_ref, sh_ref)
    xo_ref[...] = x_new
    tm = h.shape[0]
    h_ref[...] = h.astype(BF16)

    h_hi = h.astype(BF16)
    h_lo = (h - h_hi.astype(F32)).astype(BF16)
    hi_part = jnp.dot(h_hi, rw_ref[...], preferred_element_type=F32)
    logits = (hi_part[:, 0:LANES] + hi_part[:, LANES:]
              + jnp.dot(h_lo, rw_ref[:, 0:LANES], preferred_element_type=F32))
    lane = lax.broadcasted_iota(jnp.int32, (tm, LANES), 1).astype(F32)
    logits = jnp.where(lane < N_EXPERTS, logits, -jnp.inf)
    m0 = jnp.max(logits, axis=-1, keepdims=True)
    i0 = jnp.min(jnp.where(logits == m0, lane, float(LANES)), axis=-1, keepdims=True)
    rest = jnp.where(lane == i0, -jnp.inf, logits)
    m1 = jnp.max(rest, axis=-1, keepdims=True)
    i1 = jnp.min(jnp.where(rest == m1, lane, float(LANES)), axis=-1, keepdims=True)
    e = jnp.exp(m1 - m0)
    g0 = 1.0 / (1.0 + e)
    g1 = e / (1.0 + e)

    @pl.when(pl.program_id(0) == 0)
    def _():
        carry_ref[...] = jnp.zeros(carry_ref.shape, F32)

    sel0 = lane == i0
    sel1 = lane == i1
    member = jnp.where(sel0 | sel1, 1.0, 0.0)
    r_i = lax.broadcasted_iota(jnp.int32, (tm, tm), 0)
    c_i = lax.broadcasted_iota(jnp.int32, (tm, tm), 1)
    strict = jnp.where(r_i > c_i, 1.0, 0.0).astype(BF16)
    base_ref[0] = carry_ref[...]
    rank = carry_ref[...] + jnp.dot(strict, member.astype(BF16), preferred_element_type=F32)
    r0 = jnp.sum(jnp.where(sel0, rank, 0.0), axis=-1, keepdims=True)
    r1 = jnp.sum(jnp.where(sel1, rank, 0.0), axis=-1, keepdims=True)
    carry_ref[...] = carry_ref[...] + jnp.sum(member, axis=0, keepdims=True)
    cnt_ref[...] = carry_ref[...]

    cols = r1
    for k, colv in enumerate((g0, g1, i0, i1, r0)):
        cols = jnp.where(lane == float(k), colv, cols)
    gate_ref[...] = cols[:, 0:SUBLANES]
    diag = (lax.broadcasted_iota(jnp.int32, (tm, LANES), 0) % LANES
            == lax.broadcasted_iota(jnp.int32, (tm, LANES), 1))
    nrow = tm // LANES
    for q, colv in enumerate((i0, i1, r0, r1)):
        spread = jnp.where(diag, colv, 0.0)
        rows = [jnp.sum(spread[b * LANES:(b + 1) * LANES], axis=0, keepdims=True) for b in range(nrow)]
        route_ref[0, q * nrow:(q + 1) * nrow, :] = jnp.concatenate(rows, axis=0).astype(jnp.int32)


def _outproj_router(attn, ssm, x2, w, gm, g, sc, sh, tiles_per_batch, router):
    t, d = x2.shape
    tm = ROW_TILE
    row = lambda i: (i, 0)
    const = lambda i: (0, 0)
    per_b = lambda i: (i // tiles_per_batch, 0, 0)
    half = attn.shape[1]
    return pl.pallas_call(
        _outproj_router_kernel,
        grid=(t // tm,),
        in_specs=[pl.BlockSpec((tm, half), row),
                  pl.BlockSpec((tm, half), row),
                  pl.BlockSpec((tm, d), row),
                  pl.BlockSpec((2 * half, d), const),
                  pl.BlockSpec((1, 1, d), per_b),
                  pl.BlockSpec((1, d), const),
                  pl.BlockSpec((1, 1, d), per_b),
                  pl.BlockSpec((1, 1, d), per_b),
                  pl.BlockSpec((d, 2 * LANES), const)],
        out_specs=[pl.BlockSpec((tm, d), row),
                   pl.BlockSpec((tm, d), row),
                   pl.BlockSpec((1, 4 * tm // LANES, LANES), lambda i: (i, 0, 0)),
                   pl.BlockSpec((tm, SUBLANES), row),
                   pl.BlockSpec((1, LANES), const),
                   pl.BlockSpec((1, 1, LANES), lambda i: (i, 0, 0))],
        out_shape=[jax.ShapeDtypeStruct((t, d), F32),
                   jax.ShapeDtypeStruct((t, d), BF16),
                   jax.ShapeDtypeStruct((t // tm, 4 * tm // LANES, LANES), jnp.int32),
                   jax.ShapeDtypeStruct((t, SUBLANES), F32),
                   jax.ShapeDtypeStruct((1, LANES), F32),
                   jax.ShapeDtypeStruct((t // tm, 1, LANES), F32)],
        scratch_shapes=[pltpu.VMEM((1, LANES), F32)],
        compiler_params=_cparams("arbitrary"),
    )(attn, ssm, x2, w, gm, g, sc, sh, router)


def _swiglu_rows(x, wg, wu, wd):
    gate = jnp.dot(x, wg, preferred_element_type=F32)
    up = jnp.dot(x, wu, preferred_element_type=F32)
    return jnp.dot((_silu(gate) * up).astype(BF16), wd, preferred_element_type=F32)


def _outproj_ffn_kernel(a_ref, s_ref, x_ref, w_ref, gm_ref, g_ref, sc_ref, sh_ref, wg_ref, wu_ref, wd_ref, gf_ref,
                        o_ref):
    half = a_ref.shape[1]
    piece = x_ref.shape[0] // ROW_SPLIT
    for r in range(ROW_SPLIT):
        rs = slice(r * piece, (r + 1) * piece)
        mixed = (jnp.dot(a_ref[rs, :], w_ref[0:half, :], preferred_element_type=F32)
                 + jnp.dot(s_ref[rs, :], w_ref[half:, :], preferred_element_type=F32))
        x_new = x_ref[rs, :] + gm_ref[0] * mixed
        h = _rms_mod(x_new, g_ref[...], sc_ref[0], sh_ref[0]).astype(BF16)
        o_ref[rs, :] = x_new + gf_ref[0] * _swiglu_rows(h, wg_ref[...], wu_ref[...], wd_ref[...])


def _outproj_ffn(attn, ssm, x2, w, gm, g, sc, sh, wg, wu, wd, gf, seq):
    t, d = x2.shape
    f = wg.shape[1]
    tm = min(FFN_TILE, seq)
    half = attn.shape[1]
    row = lambda i: (i, 0)
    const = lambda i: (0, 0)
    per_b = lambda i: (i // (seq // tm), 0, 0)
    resident = pl.Buffered(1)
    return pl.pallas_call(
        _outproj_ffn_kernel,
        grid=(t // tm,),
        in_specs=[pl.BlockSpec((tm, half), row),
                  pl.BlockSpec((tm, half), row),
                  pl.BlockSpec((tm, d), row),
                  pl.BlockSpec((2 * half, d), const, pipeline_mode=resident),
                  pl.BlockSpec((1, 1, d), per_b),
                  pl.BlockSpec((1, d), const),
                  pl.BlockSpec((1, 1, d), per_b),
                  pl.BlockSpec((1, 1, d), per_b),
                  pl.BlockSpec((d, f), const, pipeline_mode=resident),
                  pl.BlockSpec((d, f), const, pipeline_mode=resident),
                  pl.BlockSpec((f, d), const, pipeline_mode=resident),
                  pl.BlockSpec((1, 1, d), per_b)],
        out_specs=pl.BlockSpec((tm, d), row),
        out_shape=jax.ShapeDtypeStruct((t, d), F32),
        compiler_params=pltpu.CompilerParams(dimension_semantics=("parallel",), vmem_limit_bytes=BIG_VMEM_LIMIT),
    )(attn, ssm, x2, w, gm, g, sc, sh, wg, wu, wd, gf)


def _segment_copies(count, src, src_row, dst, dst_row, sem, max_rows):
    for k in range(max_rows.bit_length() - 1, -1, -1):
        size = (1 << k) * SUBLANES

        @pl.when(((count >> k) & 1) == 1)
        def _(k=k, size=size):
            done = (count >> (k + 1)) << (k + 1)
            pltpu.make_async_copy(
                src.at[pl.ds(pl.multiple_of((src_row + done) * SUBLANES, SUBLANES), size)],
                dst.at[pl.ds(pl.multiple_of((dst_row + done) * SUBLANES, SUBLANES), size)], sem).start(
                    priority=k % 2)


def _lane_rows(route_ref, q, nrow):
    return jnp.concatenate([route_ref[0, q * nrow + b:q * nrow + b + 1, :] for b in range(nrow)], axis=1)


def _dispatch_kernel(delta_ref, cnt_ref, off_ref, dst_ref, zs_ref, zm_ref, ze_ref, nvalid_ref,
                     route_ref, h_ref, xb_hbm, g_ref, zero_ref, sem, zsem):
    i = pl.program_id(0)
    n = pl.num_programs(0)
    tm = h_ref.shape[0]
    nslot = TOP_K * tm
    slot = i % 2
    blk_rows = zero_ref.shape[0]
    n_blocks = xb_hbm.shape[0] // blk_rows

    def zero_row(r):
        return pltpu.make_async_copy(
            zero_ref.at[pl.ds(0, SUBLANES)],
            xb_hbm.at[pl.ds(pl.multiple_of(r * SUBLANES, SUBLANES), SUBLANES)], zsem)

    piece_rows = blk_rows // MOE_SPLIT

    def zero_piece(p):
        return pltpu.make_async_copy(
            zero_ref.at[pl.ds(0, piece_rows)],
            xb_hbm.at[pl.ds(pl.multiple_of(p * piece_rows, piece_rows), piece_rows)], zsem)

    def zero_block(b):
        return pltpu.make_async_copy(
            zero_ref, xb_hbm.at[pl.ds(pl.multiple_of(b * blk_rows, blk_rows), blk_rows)], zsem)

    @pl.when(i == 0)
    def _():
        zero_ref[...] = jnp.zeros(zero_ref.shape, F32)
        for start in (True, False):
            for e in range(N_EXPERTS):
                def rows_body(r, c):
                    zero_row(r).start() if start else zero_row(r).wait()
                    return c
                lax.fori_loop(zs_ref[e], zm_ref[e], rows_body, 0)

                def pieces_body(p, c):
                    zero_piece(p).start() if start else zero_piece(p).wait()
                    return c
                lax.fori_loop(zm_ref[e] // (piece_rows // SUBLANES), ze_ref[e] // (piece_rows // SUBLANES),
                              pieces_body, 0)

            def blocks_body(b, c):
                zero_block(b).start() if start else zero_block(b).wait()
                return c
            lax.fori_loop(nvalid_ref[0], n_blocks, blocks_body, 0)

    def wait_slot(sl):
        pltpu.make_async_copy(g_ref.at[sl], xb_hbm.at[pl.ds(0, nslot * SUBLANES)], sem.at[sl]).wait()

    @pl.when(i >= 2)
    def _():
        wait_slot(slot)

    nrow = tm // LANES
    experts = (_lane_rows(route_ref, 0, nrow), _lane_rows(route_ref, 1, nrow))
    ranks = (_lane_rows(route_ref, 2, nrow), _lane_rows(route_ref, 3, nrow))
    row_id = lax.broadcasted_iota(jnp.int32, (nslot, tm), 0)
    hit = None
    for k in range(TOP_K):
        delta = jnp.zeros((1, tm), jnp.int32)
        for e in range(N_EXPERTS):
            delta = jnp.where(experts[k] == e, delta_ref[i * N_EXPERTS + e], delta)
        mine = row_id == delta + ranks[k]
        hit = mine if hit is None else hit | mine
    grouped = jnp.dot(jnp.where(hit, 1.0, 0.0).astype(BF16), h_ref[...], preferred_element_type=F32)
    for s in range(grouped.shape[1] // LANES):
        g_ref[slot, pl.ds(s, nslot, stride=SUBLANES), :] = grouped[:, s * LANES:(s + 1) * LANES]
    for e in range(N_EXPERTS):
        _segment_copies(cnt_ref[i * N_EXPERTS + e], g_ref.at[slot], off_ref[i * N_EXPERTS + e],
                        xb_hbm, dst_ref[i * N_EXPERTS + e], sem.at[slot], tm)

    @pl.when(i == n - 1)
    def _():
        wait_slot(slot)

    @pl.when((i == n - 1) & (n >= 2))
    def _():
        wait_slot(1 - slot)


def _dispatch(h, route, seg, zero_spans, nvalid, n_blocks, rows):
    t, d = h.shape
    tm = ROW_TILE
    nslot = TOP_K * tm
    nrow = tm // LANES
    grid_spec = pltpu.PrefetchScalarGridSpec(
        num_scalar_prefetch=8,
        grid=(t // tm,),
        in_specs=[pl.BlockSpec((1, 4 * nrow, LANES), lambda i, *_: (i, 0, 0)),
                  pl.BlockSpec((tm, d), lambda i, *_: (i, 0))],
        out_specs=pl.BlockSpec(memory_space=pl.ANY),
        scratch_shapes=[pltpu.VMEM((2, nslot * SUBLANES, LANES), F32),
                        pltpu.VMEM((rows * SUBLANES, LANES), F32),
                        pltpu.SemaphoreType.DMA((2,)), pltpu.SemaphoreType.DMA(())],
    )
    return pl.pallas_call(
        _dispatch_kernel,
        grid_spec=grid_spec,
        out_shape=jax.ShapeDtypeStruct((n_blocks * rows * SUBLANES, LANES), F32),
        compiler_params=pltpu.CompilerParams(dimension_semantics=("arbitrary",), has_side_effects=True,
                                             vmem_limit_bytes=VMEM_LIMIT),
    )(*seg, *zero_spans, nvalid, route, h)


def _moe_kernel(bexp_ref, nvalid_ref, brows_ref, xb_ref, wg_ref, wu_ref, wd_ref, y_ref, x_scr, acc_ref):
    b = pl.program_id(0)
    j = pl.program_id(1)
    nf = pl.num_programs(1)
    rows = x_scr.shape[0]
    nsl = x_scr.shape[1] // LANES
    valid = b < nvalid_ref[0]
    piece = rows // MOE_SPLIT
    used = (brows_ref[b] + piece - 1) // piece

    def tile_rows(r, s):
        return pl.ds(r * piece * SUBLANES + s, piece, stride=SUBLANES)

    def body(first, last, n_used):
        wg = wg_ref[0].astype(BF16)
        wu = wu_ref[0].astype(BF16)
        wd = wd_ref[0].astype(BF16)
        for r in range(n_used):
            rs = slice(r * piece, (r + 1) * piece)
            if first:
                for s in range(nsl):
                    x_scr[rs, s * LANES:(s + 1) * LANES] = xb_ref[tile_rows(r, s), :].astype(BF16)
            part = _swiglu_rows(x_scr[rs, :], wg, wu, wd)
            total = part if first else acc_ref[rs, :] + part
            if last:
                for s in range(nsl):
                    y_ref[tile_rows(r, s), :] = total[:, s * LANES:(s + 1) * LANES]
            else:
                acc_ref[rs, :] = total
        if last:
            for r in range(n_used, MOE_SPLIT):
                for s in range(nsl):
                    y_ref[tile_rows(r, s), :] = jnp.zeros((piece, LANES), F32)

    for first, last, when_j in ((True, False, j == 0), (False, False, (j > 0) & (j < nf - 1)),
                                (False, True, j == nf - 1)):
        for n_used in range(1, MOE_SPLIT + 1):
            pl.when(valid & when_j & (used == n_used))(functools.partial(body, first, last, n_used))

    @pl.when(jnp.logical_not(valid) & (j == nf - 1))
    def _():
        y_ref[...] = jnp.zeros(y_ref.shape, F32)


def _moe_experts(xb, bexp, nvalid, brows, wg, wu, wd, n_blocks, rows):
    d = wg.shape[1]
    f = wg.shape[2]
    nf = f // MOE_FT
    assert nf >= 2

    def blk(b, j, bexp, nvalid, brows):
        return (jnp.minimum(b, nvalid[0] - 1), 0)

    def fidx(b, j, nvalid):
        return jnp.where(b < nvalid[0], j, nf - 1)

    grid_spec = pltpu.PrefetchScalarGridSpec(
        num_scalar_prefetch=3,
        grid=(n_blocks, nf),
        in_specs=[pl.BlockSpec((rows * SUBLANES, LANES), blk),
                  pl.BlockSpec((1, d, MOE_FT), lambda b, j, bexp, nvalid, brows: (bexp[b], 0, fidx(b, j, nvalid))),
                  pl.BlockSpec((1, d, MOE_FT), lambda b, j, bexp, nvalid, brows: (bexp[b], 0, fidx(b, j, nvalid))),
                  pl.BlockSpec((1, MOE_FT, d), lambda b, j, bexp, nvalid, brows: (bexp[b], fidx(b, j, nvalid), 0))],
        out_specs=pl.BlockSpec((rows * SUBLANES, LANES), lambda b, j, bexp, nvalid, brows: (b, 0)),
        scratch_shapes=[pltpu.VMEM((rows, d), BF16), pltpu.VMEM((rows, d), F32)],
    )
    return pl.pallas_call(
        _moe_kernel,
        grid_spec=grid_spec,
        out_shape=jax.ShapeDtypeStruct(xb.shape, F32),
        compiler_params=pltpu.CompilerParams(dimension_semantics=("arbitrary", "arbitrary"),
                                             vmem_limit_bytes=BIG_VMEM_LIMIT),
    )(bexp, nvalid, brows, xb, wg, wu, wd)


def _combine_kernel(delta_ref, cnt_ref, off_ref, dst_ref, y_hbm, x_ref, gate_ref, gf_ref, fg_ref, o_ref,
                    buf_ref, sem):
    i = pl.program_id(0)
    tm = x_ref.shape[0]
    nslot = TOP_K * tm
    nsl = x_ref.shape[1] // LANES
    slot = i % 2

    def fetch(tile, to_slot):
        for e in range(N_EXPERTS):
            _segment_copies(cnt_ref[tile * N_EXPERTS + e], y_hbm, dst_ref[tile * N_EXPERTS + e],
                            buf_ref.at[to_slot], off_ref[tile * N_EXPERTS + e], sem.at[to_slot], tm)

    @pl.when(i == 0)
    def _():
        fetch(0, 0)

    @pl.when(i + 1 < pl.num_programs(0))
    def _():
        fetch(i + 1, 1 - slot)

    pltpu.make_async_copy(y_hbm.at[pl.ds(0, nslot * SUBLANES)], buf_ref.at[slot], sem.at[slot]).wait()

    y_rows = jnp.concatenate([buf_ref[slot, pl.ds(s, nslot, stride=SUBLANES), :] for s in range(nsl)],
                             axis=1).astype(BF16)
    cols = gate_ref[...]
    lane = lax.broadcasted_iota(jnp.int32, (tm, nslot), 1).astype(F32)
    weights = None
    for k in range(TOP_K):
        expert = cols[:, 2 + k:3 + k]
        delta = jnp.zeros((tm, 1), F32)
        for e in range(N_EXPERTS):
            delta = jnp.where(expert == float(e), delta_ref[i * N_EXPERTS + e].astype(F32), delta)
        w_k = jnp.where(lane == delta + cols[:, 4 + k:5 + k], cols[:, k:k + 1], 0.0)
        weights = w_k if weights is None else weights + w_k
    f = jnp.dot(weights.astype(BF16), y_rows, preferred_element_type=F32)
    x = x_ref[...] + gf_ref[0] * f
    ms = jnp.mean(x * x, axis=-1, keepdims=True)
    o_ref[...] = x * lax.rsqrt(ms + EPS) * fg_ref[...]


def _combine(y, seg, x2, gates, gf, fg, tiles_per_batch):
    t, d = x2.shape
    tm = ROW_TILE
    nslot = TOP_K * tm
    grid_spec = pltpu.PrefetchScalarGridSpec(
        num_scalar_prefetch=4,
        grid=(t // tm,),
        in_specs=[pl.BlockSpec(memory_space=pl.ANY),
                  pl.BlockSpec((tm, d), lambda i, *_: (i, 0)),
                  pl.BlockSpec((tm, SUBLANES), lambda i, *_: (i, 0)),
                  pl.BlockSpec((1, 1, d), lambda i, *_: (i // tiles_per_batch, 0, 0)),
                  pl.BlockSpec((1, d), lambda i, *_: (0, 0))],
        out_specs=pl.BlockSpec((tm, d), lambda i, *_: (i, 0)),
        scratch_shapes=[pltpu.VMEM((2, nslot * SUBLANES, LANES), F32), pltpu.SemaphoreType.DMA((2,))],
    )
    return pl.pallas_call(
        _combine_kernel,
        grid_spec=grid_spec,
        out_shape=jax.ShapeDtypeStruct((t, d), F32),
        compiler_params=_cparams("arbitrary"),
    )(*seg, y, x2, gates, gf, fg)


def _final_norm_kernel(x_ref, g_ref, o_ref):
    x = x_ref[...]
    o_ref[...] = x * lax.rsqrt(jnp.mean(x * x, axis=-1, keepdims=True) + EPS) * g_ref[...]


def _final_norm(x2, g):
    t, d = x2.shape
    tm = ROW_TILE
    return pl.pallas_call(
        _final_norm_kernel,
        grid=(t // tm,),
        in_specs=[pl.BlockSpec((tm, d), lambda i: (i, 0)), pl.BlockSpec((1, d), lambda i: (0, 0))],
        out_specs=pl.BlockSpec((tm, d), lambda i: (i, 0)),
        out_shape=jax.ShapeDtypeStruct((t, d), F32),
        compiler_params=_cparams("parallel"),
    )(x2, g)


def _qk_column_perm():
    half = HEAD_DIM // 2
    lane = np.arange(LANES)
    pair = (lane % HEAD_DIM) // half
    dim = lane % half + half * (lane // HEAD_DIM)
    q = np.concatenate([(i + Q_TILES * pair) * HEAD_DIM + dim for i in range(Q_TILES)])
    k = ATTN_WIDTH + pair * HEAD_DIM + dim
    return q, k


def _attn_out_row_perm():
    lane = np.arange(LANES)
    return np.concatenate([(i + Q_TILES * (lane // HEAD_DIM)) * HEAD_DIM + lane % HEAD_DIM
                           for i in range(Q_TILES)])


def _pad_lanes(v):
    return jnp.pad(v.astype(F32), (0, LANES - v.shape[0])).reshape(1, LANES)


def _moe_route_tables(counts, base, rows, n_blocks):
    cnt = counts[0, :N_EXPERTS].astype(jnp.int32)
    padded = (cnt + rows - 1) // rows * rows
    pad_end = jnp.cumsum(padded)
    pad_start = pad_end - padded
    first = base[:, 0, :N_EXPERTS].astype(jnp.int32)
    run = jnp.concatenate([first[1:], cnt[None]], axis=0) - first
    off = jnp.cumsum(run, axis=1) - run
    flat = lambda a: a.reshape(-1).astype(jnp.int32)
    seg = (flat(off - first), flat(run), flat(off), flat(pad_start[None, :] + first))
    blk_start = jnp.arange(n_blocks, dtype=jnp.int32) * rows
    bexp = jnp.minimum(jnp.sum(blk_start[:, None] >= pad_end[None, :], axis=1), N_EXPERTS - 1).astype(jnp.int32)
    nvalid = (pad_end[-1:] // rows).astype(jnp.int32)
    row_end = pad_start + cnt
    brows = jnp.clip(row_end[bexp] - blk_start, 0, rows).astype(jnp.int32)
    piece = rows // MOE_SPLIT
    piece_end = (row_end + piece - 1) // piece * piece
    zero_spans = (row_end.astype(jnp.int32), piece_end.astype(jnp.int32), pad_end.astype(jnp.int32))
    return seg, bexp, nvalid, brows, zero_spans


def kernel(x, c, positions, ada_w, ada_b, norm_mix_g, norm_ffn_g, w_in, w_out, attn_sinks, conv_w, conv_b,
           dt_bias, a_log, d_skip, ssm_norm_g, ffn_w_gate, ffn_w_up, ffn_w_down, router_w, moe_w_gate,
           moe_w_up, moe_w_down, final_norm_g):
    batch, seq, d = x.shape
    depth = w_in.shape[0]
    t = batch * seq
    tiles_per_batch = seq // ROW_TILE
    x2 = x.reshape(t, d)

    c8 = jnp.pad(c, ((0, SUBLANES - batch), (0, 0)))
    mod = _ada_mod(c8, ada_w, ada_b)[:, :batch].reshape(depth, batch, 6, 1, d)
    cos, sin = _rope_tables(positions)

    q_perm, k_perm = _qk_column_perm()
    n_in = w_in.shape[2]
    in_cols = np.concatenate([q_perm, k_perm, np.arange(C_V, n_in), np.full(LANES - SSM_HEADS, n_in)])
    out_rows = np.concatenate([_attn_out_row_perm(), np.arange(ATTN_WIDTH, w_out.shape[1])])
    for l in range(depth):
        sh_m, sc_m, g_m, sh_f, sc_f, g_f = (mod[l, :, k] for k in range(6))
        w_cat = jnp.take(jnp.pad(w_in[l].astype(BF16), ((0, 0), (0, 1))), in_cols, axis=1)
        q, kv, z, xbc, dt = _inproj(x2, norm_mix_g[l].reshape(1, d), sc_m, sh_m, cos, sin, w_cat,
                                    _pad_lanes(dt_bias[l]), seq)
        attn = _attention(q, kv, attn_sinks[l].astype(F32), seq)
        ssm = _ssd(xbc, z, dt, conv_w[l], conv_b[l].reshape(1, CONV_CH), _pad_lanes(a_log[l]),
                   jnp.repeat(d_skip[l].astype(F32), SSM_HEAD_DIM).reshape(1, SSM_WIDTH),
                   ssm_norm_g[l].reshape(1, SSM_WIDTH), batch, seq)
        wo = jnp.take(w_out[l].astype(BF16), out_rows, axis=0)
        ffn_g = norm_ffn_g[l].reshape(1, d)
        if l % 2 == 0:
            x2 = _outproj_ffn(attn, ssm, x2, wo, g_m, ffn_g, sc_f, sh_f, ffn_w_gate[l // 2].astype(BF16),
                              ffn_w_up[l // 2].astype(BF16), ffn_w_down[l // 2].astype(BF16), g_f, seq)
            if l == depth - 1:
                x2 = _final_norm(x2, final_norm_g.reshape(1, d))
        else:
            if l != depth - 1:
                raise NotImplementedError("the expert layer fuses the final norm and must be last")
            rw = jnp.pad(router_w[l // 2].astype(F32), ((0, 0), (0, LANES - N_EXPERTS)))
            rw_hi = rw.astype(BF16)
            rw_lo = (rw - rw_hi.astype(F32)).astype(BF16)
            x_new, h, route, gates, counts, base = _outproj_router(
                attn, ssm, x2, wo, g_m, ffn_g, sc_f, sh_f, tiles_per_batch,
                jnp.concatenate([rw_hi, rw_lo], axis=1))
            rows = min(MOE_ROWS, t)
            n_blocks = -(-(t * TOP_K) // rows) + N_EXPERTS
            seg, bexp, nvalid, brows, zero_spans = _moe_route_tables(counts, base, rows, n_blocks)
            xb = _dispatch(h, route, seg, zero_spans, nvalid, n_blocks, rows)
            y = _moe_experts(xb, bexp, nvalid, brows, moe_w_gate[l // 2], moe_w_up[l // 2], moe_w_down[l // 2],
                             n_blocks, rows)
            x2 = _combine(y, seg, x_new, gates, g_f, final_norm_g.reshape(1, d), tiles_per_batch)
    return x2.reshape(batch, seq, d)
```
